```python
import jax, jax.numpy as jnp
from jax import lax
import numpy as np

D_MODEL = 1024
BATCH = 16
SEQ = 2048
DEPTH = 2

HEAD_DIM = 64
GROUP_HEADS = 4
GROUP_WIDTH = GROUP_HEADS * HEAD_DIM
N_GROUPS = 4
D_MIX = N_GROUPS * GROUP_WIDTH
SB_COLS = 3 * GROUP_WIDTH
CONF_COLS = 2 * GROUP_WIDTH
GMLP_COLS = 2 * GROUP_WIDTH
LRU_COLS = 2 * GROUP_WIDTH
IN_COLS = SB_COLS + CONF_COLS + GMLP_COLS + LRU_COLS
SB_BLOCK = 128
CONF_KERNEL = 31
GMLP_CHUNK = 128
LRU_CONV = 4
LRU_C = 8.0
D_FF = 2816
N_EXPERTS = 8
TOP_K = 2
D_FF_EXPERT = 3584
EXPERT_BLOCK = 512
N_DENSE = (DEPTH + 1) // 2
N_MOE = DEPTH // 2
EPS = 1e-6

kernel_name = "hybrid_sb_conformer_gmlp_rglru_moe"


def rms_norm(x, g):
    xf = x.astype(jnp.float32)
    y = xf * lax.rsqrt(jnp.mean(xf * xf, axis=-1, keepdims=True) + EPS)
    return (y * g.astype(jnp.float32)).astype(x.dtype)


def layer_norm(x, g, b):
    xf = x.astype(jnp.float32)
    xc = xf - jnp.mean(xf, axis=-1, keepdims=True)
    var = jnp.mean(xc * xc, axis=-1, keepdims=True)
    return (xc * lax.rsqrt(var + EPS) * g.astype(jnp.float32) + b.astype(jnp.float32)).astype(x.dtype)


def causal_depthwise_conv(x, w, b):
    K, C = w.shape
    y = lax.conv_general_dilated(
        x, w[:, None, :], window_strides=(1,), padding=[(K - 1, 0)],
        dimension_numbers=("NWC", "WIO", "NWC"), feature_group_count=C)
    return y + b


def stick_breaking_attention(q, k, v):
    B, S, H, Dh = q.shape
    nb = S // SB_BLOCK
    scale = Dh ** -0.5
    kf = k.astype(jnp.float32)
    vf = v.astype(jnp.float32)
    qb = q.astype(jnp.float32).reshape(B, nb, SB_BLOCK, H, Dh).transpose(1, 0, 2, 3, 4)
    key_pos = jnp.arange(S)

    def block(args):
        q_blk, start = args
        z = jnp.einsum("bthd,bshd->bhts", q_blk, kf) * scale
        q_pos = start + jnp.arange(SB_BLOCK)
        mask = key_pos[None, :] < q_pos[:, None]
        log_beta = jax.nn.log_sigmoid(z)
        log_keep = jnp.where(mask, -jax.nn.softplus(z), 0.0)
        later = lax.cumsum(log_keep, axis=3, reverse=True) - log_keep
        w = jnp.where(mask, jnp.exp(log_beta + later), 0.0)
        return jnp.einsum("bhts,bshd->bthd", w, vf)

    out = lax.map(block, (qb, jnp.arange(nb, dtype=jnp.int32) * SB_BLOCK))
    return out.transpose(1, 0, 2, 3, 4).reshape(B, S, H * Dh).astype(q.dtype)


def conformer_conv(val, gate, w_dw, b_dw, ln_g, ln_b):
    h = val * jax.nn.sigmoid(gate)
    h = causal_depthwise_conv(h, w_dw, b_dw)
    h = layer_norm(h, ln_g, ln_b)
    return jax.nn.silu(h)


def chunked_spatial_gating(uv, ln_g, ln_b, w_s, b_s):
    B, S, _ = uv.shape
    u, v = jnp.split(jax.nn.gelu(uv), 2, axis=-1)
    v = layer_norm(v, ln_g, ln_b)
    nc = S // GMLP_CHUNK
    v = v.reshape(B, nc, GMLP_CHUNK, GROUP_HEADS, HEAD_DIM)
    tri = jnp.tril(jnp.ones((GMLP_CHUNK, GMLP_CHUNK), dtype=bool))
    w = jnp.where(tri, w_s, 0.0)
    mixed = jnp.einsum("hts,bcshd->bcthd", w, v) + b_s.T[:, :, None]
    return u * mixed.reshape(B, S, GROUP_WIDTH)


def rg_lru_branch(xb, gate, conv_w, conv_b, w_a, b_a, w_x, b_x, lam):
    xb = causal_depthwise_conv(xb, conv_w, conv_b)
    B, S, C = xb.shape
    xh = xb.reshape(B, S, GROUP_HEADS, HEAD_DIM)
    r = jax.nn.sigmoid(jnp.einsum("bshi,hij->bshj", xh, w_a).reshape(B, S, C) + b_a)
    i = jax.nn.sigmoid(jnp.einsum("bshi,hij->bshj", xh, w_x).reshape(B, S, C) + b_x)
    log_a = -LRU_C * r.astype(jnp.float32) * jax.nn.softplus(-lam.astype(jnp.float32))
    a = jnp.exp(log_a)
    b_in = jnp.sqrt(-jnp.expm1(2.0 * log_a)) * (i * xb).astype(jnp.float32)

    def combine(left, right):
        a1, b1 = left
        a2, b2 = right
        return a1 * a2, a2 * b1 + b2

    _, h = lax.associative_scan(combine, (a, b_in), axis=1)
    return (h * jax.nn.gelu(gate.astype(jnp.float32))).astype(xb.dtype)


def hybrid_mixer(h, w_in, q_norm_g, k_norm_g, conf_dw_w, conf_dw_b, conf_ln_g, conf_ln_b,
                 gmlp_ln_g, gmlp_ln_b, gmlp_ws, gmlp_bs, lru_conv_w, lru_conv_b,
                 lru_wa, lru_ba, lru_wx, lru_bx, lru_lambda, group_norm_g, w_out):
    B, S, _ = h.shape
    proj = jnp.einsum("bsd,dc->bsc", h, w_in)
    c1 = SB_COLS
    c2 = c1 + CONF_COLS
    c3 = c2 + GMLP_COLS
    sb, conf, gm, lru = jnp.split(proj, [c1, c2, c3], axis=-1)
    q, k, v = jnp.split(sb, 3, axis=-1)
    q = rms_norm(q.reshape(B, S, GROUP_HEADS, HEAD_DIM), q_norm_g)
    k = rms_norm(k.reshape(B, S, GROUP_HEADS, HEAD_DIM), k_norm_g)
    v = v.reshape(B, S, GROUP_HEADS, HEAD_DIM)
    y_a = stick_breaking_attention(q, k, v)
    conf_val, conf_gate = jnp.split(conf, 2, axis=-1)
    y_b = conformer_conv(conf_val, conf_gate, conf_dw_w, conf_dw_b, conf_ln_g, conf_ln_b)
    y_c = chunked_spatial_gating(gm, gmlp_ln_g, gmlp_ln_b, gmlp_ws, gmlp_bs)
    lru_x, lru_gate = jnp.split(lru, 2, axis=-1)
    y_d = rg_lru_branch(lru_x, lru_gate, lru_conv_w, lru_conv_b, lru_wa, lru_ba, lru_wx, lru_bx, lru_lambda)
    y = jnp.stack([y_a, y_b, y_c, y_d], axis=2)
    y = rms_norm(y, group_norm_g.reshape(N_GROUPS, GROUP_WIDTH)).reshape(B, S, D_MIX)
    return jnp.einsum("bsc,cd->bsd", y, w_out)


def swiglu(h, wg, wu, wd):
    return (jax.nn.silu(h @ wg) * (h @ wu)) @ wd


def top2_moe(h, router, wg, wu, wd):
    B, S, D = h.shape
    N = B * S
    A = N * TOP_K
    hf = h.reshape(N, D)
    logits = (hf @ router).astype(jnp.float32)
    top_logits, top_idx = lax.top_k(logits, TOP_K)
    gates = jax.nn.softmax(top_logits, axis=-1).astype(h.dtype)
    exp_ids = top_idx.reshape(A)
    tok_ids = jnp.repeat(jnp.arange(N, dtype=jnp.int32), TOP_K)
    gate_flat = gates.reshape(A)
    order = jnp.argsort(exp_ids)
    exp_sorted = exp_ids[order]
    tok_sorted = tok_ids[order]
    gate_sorted = gate_flat[order]
    counts = jnp.bincount(exp_ids, length=N_EXPERTS)
    starts = jnp.cumsum(counts) - counts
    padded = (counts + EXPERT_BLOCK - 1) // EXPERT_BLOCK * EXPERT_BLOCK
    pad_ends = jnp.cumsum(padded)
    pad_starts = pad_ends - padded
    rank = jnp.arange(A, dtype=jnp.int32) - starts[exp_sorted]
    dest = pad_starts[exp_sorted] + rank
    n_blocks = -(-A // EXPERT_BLOCK) + N_EXPERTS
    n_rows = n_blocks * EXPERT_BLOCK
    row_tok = jnp.full((n_rows,), N, dtype=jnp.int32).at[dest].set(tok_sorted)
    row_gate = jnp.zeros((n_rows,), h.dtype).at[dest].set(gate_sorted)
    block_start = jnp.arange(n_blocks, dtype=jnp.int32) * EXPERT_BLOCK
    block_exp = jnp.minimum(jnp.searchsorted(pad_ends, block_start, side="right"), N_EXPERTS - 1)
    h_pad = jnp.concatenate([hf, jnp.zeros((1, D), hf.dtype)], axis=0)

    def expert_block(args):
        tok, gate, e = args
        y = swiglu(h_pad[tok], wg[e], wu[e], wd[e])
        return y * gate[:, None]

    ys = lax.map(expert_block, (row_tok.reshape(n_blocks, EXPERT_BLOCK),
                                row_gate.reshape(n_blocks, EXPERT_BLOCK), block_exp))
    out = jnp.zeros((N + 1, D), h.dtype).at[row_tok].add(ys.reshape(n_rows, D))
    return out[:N].reshape(B, S, D)


def setup_inputs(seed: int = 0) -> dict:
    key = jax.random.key(seed)
    ks = jax.random.split(key, 40)
    f32 = jnp.float32

    def nrm(k, shape, scale):
        return jax.random.normal(k, shape, f32) * scale

    def gain(k, shape):
        return 1.0 + 0.02 * jax.random.normal(k, shape, f32)

    def bias(k, shape):
        return 0.02 * jax.random.normal(k, shape, f32)

    u = jax.random.uniform(ks[20], (DEPTH, GROUP_WIDTH), f32, minval=0.9, maxval=0.999)
    s = u ** (1.0 / LRU_C)
    lru_lambda = jnp.log(s) - jnp.log1p(-s)
    return {
        "x": jax.random.normal(ks[0], (BATCH, SEQ, D_MODEL), f32),
        "norm1_g": gain(ks[1], (DEPTH, D_MODEL)),
        "w_in": nrm(ks[2], (DEPTH, D_MODEL, IN_COLS), D_MODEL ** -0.5),
        "q_norm_g": gain(ks[3], (DEPTH, HEAD_DIM)),
        "k_norm_g": gain(ks[4], (DEPTH, HEAD_DIM)),
        "conf_dw_w": nrm(ks[5], (DEPTH, CONF_KERNEL, GROUP_WIDTH), CONF_KERNEL ** -0.5),
        "conf_dw_b": bias(ks[6], (DEPTH, GROUP_WIDTH)),
        "conf_ln_g": gain(ks[7], (DEPTH, GROUP_WIDTH)),
        "conf_ln_b": bias(ks[8], (DEPTH, GROUP_WIDTH)),
        "gmlp_ln_g": gain(ks[9], (DEPTH, GROUP_WIDTH)),
        "gmlp_ln_b": bias(ks[10], (DEPTH, GROUP_WIDTH)),
        "gmlp_ws": nrm(ks[11], (DEPTH, GROUP_HEADS, GMLP_CHUNK, GMLP_CHUNK), GMLP_CHUNK ** -0.5),
        "gmlp_bs": gain(ks[12], (DEPTH, GROUP_HEADS, GMLP_CHUNK)),
        "lru_conv_w": nrm(ks[13], (DEPTH, LRU_CONV, GROUP_WIDTH), LRU_CONV ** -0.5),
        "lru_conv_b": bias(ks[14], (DEPTH, GROUP_WIDTH)),
        "lru_wa": nrm(ks[15], (DEPTH, GROUP_HEADS, HEAD_DIM, HEAD_DIM), HEAD_DIM ** -0.5),
        "lru_ba": bias(ks[16], (DEPTH, GROUP_WIDTH)),
        "lru_wx": nrm(ks[17], (DEPTH, GROUP_HEADS, HEAD_DIM, HEAD_DIM), HEAD_DIM ** -0.5),
        "lru_bx": bias(ks[18], (DEPTH, GROUP_WIDTH)),
        "lru_lambda": lru_lambda,
        "group_norm_g": gain(ks[21], (DEPTH, D_MIX)),
        "w_out": nrm(ks[22], (DEPTH, D_MIX, D_MODEL), D_MIX ** -0.5),
        "norm2_g": gain(ks[23], (DEPTH, D_MODEL)),
        "ffn_w_gate": nrm(ks[24], (N_DENSE, D_MODEL, D_FF), D_MODEL ** -0.5),
        "ffn_w_up": nrm(ks[25], (N_DENSE, D_MODEL, D_FF), D_MODEL ** -0.5),
        "ffn_w_down": nrm(ks[26], (N_DENSE, D_FF, D_MODEL), D_FF ** -0.5),
        "moe_router": nrm(ks[27], (N_MOE, D_MODEL, N_EXPERTS), D_MODEL ** -0.5),
        "moe_w_gate": nrm(ks[28], (N_MOE, N_EXPERTS, D_MODEL, D_FF_EXPERT), D_MODEL ** -0.5),
        "moe_w_up": nrm(ks[29], (N_MOE, N_EXPERTS, D_MODEL, D_FF_EXPERT), D_MODEL ** -0.5),
        "moe_w_down": nrm(ks[30], (N_MOE, N_EXPERTS, D_FF_EXPERT, D_MODEL), D_FF_EXPERT ** -0.5),
    }


def reference(x, norm1_g, w_in, q_norm_g, k_norm_g, conf_dw_w, conf_dw_b, conf_ln_g, conf_ln_b,
              gmlp_ln_g, gmlp_ln_b, gmlp_ws, gmlp_bs, lru_conv_w, lru_conv_b, lru_wa, lru_ba,
              lru_wx, lru_bx, lru_lambda, group_norm_g, w_out, norm2_g,
              ffn_w_gate, ffn_w_up, ffn_w_down, moe_router, moe_w_gate, moe_w_up, moe_w_down):
    for l in range(DEPTH):
        h = rms_norm(x, norm1_g[l])
        x = x + hybrid_mixer(h, w_in[l], q_norm_g[l], k_norm_g[l], conf_dw_w[l], conf_dw_b[l],
                             conf_ln_g[l], conf_ln_b[l], gmlp_ln_g[l], gmlp_ln_b[l], gmlp_ws[l],
                             gmlp_bs[l], lru_conv_w[l], lru_conv_b[l], lru_wa[l], lru_ba[l],
                             lru_wx[l], lru_bx[l], lru_lambda[l], group_norm_g[l], w_out[l])
        h = rms_norm(x, norm2_g[l])
        if l % 2 == 0:
            j = l // 2
            x = x + swiglu(h, ffn_w_gate[j], ffn_w_up[j], ffn_w_down[j])
        else:
            j = l // 2
            x = x + top2_moe(h, moe_router[j], moe_w_gate[j], moe_w_up[j], moe_w_down[j])
    return x
```

```python
import functools

import jax
import jax.numpy as jnp
from jax import lax
from jax.experimental import pallas as pl
from jax.experimental.pallas import tpu as pltpu

F32 = jnp.float32
BF16 = jnp.bfloat16

HEAD_DIM = 64
GROUP_HEADS = 4
GROUP_WIDTH = GROUP_HEADS * HEAD_DIM
N_GROUPS = 4
CONF_KERNEL = 31
GMLP_CHUNK = 128
LRU_CONV = 4
LRU_C = 8.0
N_EXPERTS = 8
EPS = 1e-6

LANES = 128
VMEM_LIMIT = 48 * 1024 * 1024

IN_TM = 512
ATT_T = 256
CONF_T = 256
CONF_R = 64
CONF_HALO = 32
GMLP_T = 512
LRU_T = 256
LRU_HALO = 8
OUT_TM = 512
FFN_TM = 512
FFN_TF = 1408
ROUTE_TM = 512
MOE_TM = 512
MOE_TF = 512
COMB_TM = 256


def _params(*sem):
    return pltpu.CompilerParams(dimension_semantics=sem, vmem_limit_bytes=VMEM_LIMIT)


def _split_hi_lo(x):
    hi = x.astype(BF16)
    lo = (x - hi.astype(F32)).astype(BF16)
    return hi, lo


def _softplus(z):
    return jnp.maximum(z, 0.0) + jnp.log1p(jnp.exp(-jnp.abs(z)))


def _gelu_tanh(x):
    c = 0.7978845608028654
    return 0.5 * x * (1.0 + jnp.tanh(c * (x + 0.044715 * (x * x * x))))


def _group_rms(y, g):
    ms = jnp.mean(y * y, axis=-1, keepdims=True)
    return y * lax.rsqrt(ms + EPS) * g


def _in_proj_kernel(x_ref, g_ref, w_ref, qg_ref, kg_ref, bd_ref, q_ref, k_ref, v_ref, rest_ref):
    xf = x_ref[...]
    ms = jnp.mean(xf * xf, axis=-1, keepdims=True)
    h = (xf * lax.rsqrt(ms + EPS) * g_ref[...]).astype(BF16)
    W = GROUP_WIDTH

    def proj(c):
        return jnp.dot(h, w_ref[:, c * W:(c + 1) * W], preferred_element_type=F32)

    def head_norm(p, gain):
        hi, lo = _split_hi_lo(p * p)
        ss = jnp.dot(jnp.concatenate([hi, lo], axis=1), bd_ref[...], preferred_element_type=F32)
        return p * lax.rsqrt(ss * (1.0 / HEAD_DIM) + EPS) * gain

    q_ref[...] = head_norm(proj(0), qg_ref[...]).astype(BF16)
    k_ref[...] = head_norm(proj(1), kg_ref[...]).astype(BF16)
    v_ref[...] = proj(2).astype(BF16)
    n_rest = rest_ref.shape[1] // W
    for c in range(n_rest):
        rest_ref[:, c * W:(c + 1) * W] = proj(3 + c)


def _in_proj(x2, g, w_bf, qg, kg, bd2):
    n, d = x2.shape
    cols = w_bf.shape[1]
    W = GROUP_WIDTH
    rest_cols = cols - 3 * W
    tm = min(IN_TM, n)
    row = lambda i: (i, 0)
    const = lambda i: (0, 0)
    return pl.pallas_call(
        _in_proj_kernel,
        grid=(n // tm,),
        in_specs=[
            pl.BlockSpec((tm, d), row),
            pl.BlockSpec((1, d), const),
            pl.BlockSpec((d, cols), const),
            pl.BlockSpec((1, W), const),
            pl.BlockSpec((1, W), const),
            pl.BlockSpec((2 * W, W), const),
        ],
        out_specs=[
            pl.BlockSpec((tm, W), row),
            pl.BlockSpec((tm, W), row),
            pl.BlockSpec((tm, W), row),
            pl.BlockSpec((tm, rest_cols), row),
        ],
        out_shape=[
            jax.ShapeDtypeStruct((n, W), BF16),
            jax.ShapeDtypeStruct((n, W), BF16),
            jax.ShapeDtypeStruct((n, W), BF16),
            jax.ShapeDtypeStruct((n, rest_cols), F32),
        ],
        compiler_params=_params("arbitrary"),
        name="in_proj",
    )(x2, g, w_bf, qg, kg, bd2)


def _sb_attn_kernel(q_ref, k_ref, v_ref, uu_ref, gn_ref, o_ref):
    T = q_ref.shape[1]
    i = pl.program_id(1)
    lane = lax.broadcasted_iota(jnp.int32, (T, LANES), 1)
    rows = lax.broadcasted_iota(jnp.int32, (T, T), 0)
    cols = lax.broadcasted_iota(jnp.int32, (T, T), 1)
    causal = cols < rows
    uu = uu_ref[...]

    def tile(qh, k2, v2, carry, mask):
        z = lax.dot_general(qh, k2, (((1,), (1,)), ((), ())), preferred_element_type=F32)
        lk = -_softplus(z)
        if mask is not None:
            lk = jnp.where(mask, lk, 0.0)
        hi, lo = _split_hi_lo(lk)
        cum = jnp.dot(jnp.concatenate([hi, lo], axis=1), uu, preferred_element_type=F32)
        w = jnp.exp(z + cum + carry)
        if mask is not None:
            w = jnp.where(mask, w, 0.0)
        pv = jnp.dot(w.astype(BF16), v2, preferred_element_type=F32)
        return pv, carry + jnp.sum(lk, axis=1, keepdims=True)

    pair_out = []
    for pair in range(GROUP_WIDTH // LANES):
        cs = slice(pair * LANES, (pair + 1) * LANES)
        q2 = q_ref[0, :, cs]
        accs = []
        for hh in range(LANES // HEAD_DIM):
            in_head = (lane >= hh * HEAD_DIM) & (lane < (hh + 1) * HEAD_DIM)
            qh = jnp.where(in_head, q2, jnp.zeros_like(q2))
            row0 = pl.multiple_of(i * T, T)
            acc, carry = tile(qh, k_ref[0, pl.ds(row0, T), cs], v_ref[0, pl.ds(row0, T), cs],
                              jnp.zeros((T, 1), F32), causal)

            def body(step, st, qh=qh, cs=cs):
                acc, carry = st
                r0 = pl.multiple_of((i - 1 - step) * T, T)
                pv, carry = tile(qh, k_ref[0, pl.ds(r0, T), cs], v_ref[0, pl.ds(r0, T), cs], carry, None)
                return acc + pv, carry

            acc, carry = lax.fori_loop(0, i, body, (acc, carry))
            accs.append(acc)
        pair_out.append(jnp.where(lane < HEAD_DIM, accs[0], accs[1]))
    y = jnp.concatenate(pair_out, axis=1)
    o_ref[0] = _group_rms(y, gn_ref[...]).astype(o_ref.dtype)


def _sb_attention(q3, k3, v3, uu, gn):
    b, s, w = q3.shape
    t = min(ATT_T, s)
    return pl.pallas_call(
        _sb_attn_kernel,
        grid=(b, s // t),
        in_specs=[
            pl.BlockSpec((1, t, w), lambda bi, i: (bi, i, 0)),
            pl.BlockSpec((1, s, w), lambda bi, i: (bi, 0, 0)),
            pl.BlockSpec((1, s, w), lambda bi, i: (bi, 0, 0)),
            pl.BlockSpec((2 * t, t), lambda bi, i: (0, 0)),
            pl.BlockSpec((1, w), lambda bi, i: (0, 0)),
        ],
        out_specs=pl.BlockSpec((1, t, w), lambda bi, i: (bi, i, 0)),
        out_shape=jax.ShapeDtypeStruct((b, s, w), BF16),
        compiler_params=_params("arbitrary", "arbitrary"),
        name="sb_attn",
    )(q3, k3, v3, uu, gn)


def _conf_kernel(val_ref, gate_ref, w_ref, b_ref, lng_ref, lnb_ref, gn_ref, o_ref, hs_ref):
    T = val_ref.shape[1]
    H = CONF_HALO

    @pl.when(pl.program_id(1) == 0)
    def _():
        hs_ref[0:H, :] = jnp.zeros((H, GROUP_WIDTH), F32)

    hs_ref[H:H + T, :] = val_ref[0] * jax.nn.sigmoid(gate_ref[0])
    off = H - (CONF_KERNEL - 1)
    R = min(CONF_R, T)
    for r0 in range(0, T, R):
        acc = jnp.broadcast_to(b_ref[...], (R, GROUP_WIDTH))
        for k in range(CONF_KERNEL):
            acc = acc + w_ref[k:k + 1, :] * hs_ref[off + k + r0:off + k + r0 + R, :]
        mu = jnp.mean(acc, axis=-1, keepdims=True)
        xc = acc - mu
        var = jnp.mean(xc * xc, axis=-1, keepdims=True)
        y = xc * lax.rsqrt(var + EPS) * lng_ref[...] + lnb_ref[...]
        y = y * jax.nn.sigmoid(y)
        o_ref[0, r0:r0 + R, :] = _group_rms(y, gn_ref[...]).astype(o_ref.dtype)
    hs_ref[0:H, :] = hs_ref[T:T + H, :]


def _conformer(rest3, col0, w_pad, b, lng, lnb, gn):
    bsz, s, _ = rest3.shape
    W = GROUP_WIDTH
    t = min(CONF_T, s)
    vec = pl.BlockSpec((1, W), lambda bi, i: (0, 0))
    return pl.pallas_call(
        _conf_kernel,
        grid=(bsz, s // t),
        in_specs=[
            pl.BlockSpec((1, t, W), lambda bi, i: (bi, i, col0)),
            pl.BlockSpec((1, t, W), lambda bi, i: (bi, i, col0 + 1)),
            pl.BlockSpec(w_pad.shape, lambda bi, i: (0, 0)),
            vec, vec, vec, vec,
        ],
        out_specs=pl.BlockSpec((1, t, W), lambda bi, i: (bi, i, 0)),
        out_shape=jax.ShapeDtypeStruct((bsz, s, W), BF16),
        scratch_shapes=[pltpu.VMEM((t + CONF_HALO, W), F32)],
        compiler_params=_params("arbitrary", "arbitrary"),
        name="conformer",
    )(rest3, rest3, w_pad, b, lng, lnb, gn)


def _gmlp_kernel(u_ref, v_ref, lng_ref, lnb_ref, ws_ref, bs_ref, gn_ref, o_ref):
    T = u_ref.shape[1]
    C = GMLP_CHUNK
    lane = lax.broadcasted_iota(jnp.int32, (C, GROUP_WIDTH), 1)
    for c0 in range(0, T, C):
        u = _gelu_tanh(u_ref[0, c0:c0 + C, :])
        v = _gelu_tanh(v_ref[0, c0:c0 + C, :])
        mu = jnp.mean(v, axis=-1, keepdims=True)
        xc = v - mu
        var = jnp.mean(xc * xc, axis=-1, keepdims=True)
        vb = (xc * lax.rsqrt(var + EPS) * lng_ref[...] + lnb_ref[...]).astype(BF16)
        mixed = bs_ref[...]
        for h in range(GROUP_HEADS):
            m = jnp.dot(ws_ref[h], vb, preferred_element_type=F32)
            in_head = (lane >= h * HEAD_DIM) & (lane < (h + 1) * HEAD_DIM)
            mixed = mixed + jnp.where(in_head, m, 0.0)
        o_ref[0, c0:c0 + C, :] = _group_rms(u * mixed, gn_ref[...]).astype(o_ref.dtype)


def _gmlp(rest3, col0, lng, lnb, ws_bf, bs_mat, gn):
    bsz, s, _ = rest3.shape
    W = GROUP_WIDTH
    t = min(GMLP_T, s)
    vec = pl.BlockSpec((1, W), lambda bi, i: (0, 0))
    return pl.pallas_call(
        _gmlp_kernel,
        grid=(bsz, s // t),
        in_specs=[
            pl.BlockSpec((1, t, W), lambda bi, i: (bi, i, col0)),
            pl.BlockSpec((1, t, W), lambda bi, i: (bi, i, col0 + 1)),
            vec, vec,
            pl.BlockSpec(ws_bf.shape, lambda bi, i: (0, 0, 0)),
            pl.BlockSpec(bs_mat.shape, lambda bi, i: (0, 0)),
            vec,
        ],
        out_specs=pl.BlockSpec((1, t, W), lambda bi, i: (bi, i, 0)),
        out_shape=jax.ShapeDtypeStruct((bsz, s, W), BF16),
        compiler_params=_params("arbitrary", "arbitrary"),
        name="gmlp",
    )(rest3, rest3, lng, lnb, ws_bf, bs_mat, gn)


def _lru_kernel(x_ref, gate_ref, cw_ref, cb_ref, wax_ref, ba_ref, bx_ref, lam_ref, gn_ref, o_ref,
                xs_ref, hprev_ref):
    T = x_ref.shape[1]
    W = GROUP_WIDTH
    H = LRU_HALO

    @pl.when(pl.program_id(1) == 0)
    def _():
        xs_ref[0:H, :] = jnp.zeros((H, W), F32)
        hprev_ref[...] = jnp.zeros(hprev_ref.shape, F32)

    xs_ref[H:H + T, :] = x_ref[0]
    off = H - (LRU_CONV - 1)
    xb = jnp.broadcast_to(cb_ref[...], (T, W))
    for k in range(LRU_CONV):
        xb = xb + cw_ref[k:k + 1, :] * xs_ref[off + k:off + k + T, :]
    xs_ref[0:H, :] = xs_ref[T:T + H, :]

    pre = jnp.dot(xb.astype(BF16), wax_ref[...], preferred_element_type=F32)
    r = jax.nn.sigmoid(pre[:, :W] + ba_ref[...])
    ig = jax.nn.sigmoid(pre[:, W:] + bx_ref[...])
    log_a = (-LRU_C) * r * _softplus(-lam_ref[...])
    a = jnp.exp(log_a)
    th = jnp.tanh(log_a)
    one_minus_a2 = 2.0 * th / (th - 1.0)
    b = jnp.sqrt(one_minus_a2) * (ig * xb)

    row = lax.broadcasted_iota(jnp.int32, (T, W), 0)
    d = 1
    while d < T:
        keep = row >= d
        a_sh = jnp.where(keep, pltpu.roll(a, d, 0), 1.0)
        b_sh = jnp.where(keep, pltpu.roll(b, d, 0), 0.0)
        b = a * b_sh + b
        a = a * a_sh
        d *= 2
    h = b + a * hprev_ref[0:1, :]
    hprev_ref[...] = jnp.broadcast_to(h[T - 1:T, :], hprev_ref.shape)
    y = h * _gelu_tanh(gate_ref[0])
    o_ref[0] = _group_rms(y, gn_ref[...]).astype(o_ref.dtype)


def _rg_lru(rest3, col0, cw_pad, cb, wax_bf, ba, bx, lam, gn):
    bsz, s, _ = rest3.shape
    W = GROUP_WIDTH
    t = min(LRU_T, s)
    vec = pl.BlockSpec((1, W), lambda bi, i: (0, 0))
    return pl.pallas_call(
        _lru_kernel,
        grid=(bsz, s // t),
        in_specs=[
            pl.BlockSpec((1, t, W), lambda bi, i: (bi, i, col0)),
            pl.BlockSpec((1, t, W), lambda bi, i: (bi, i, col0 + 1)),
            pl.BlockSpec(cw_pad.shape, lambda bi, i: (0, 0)),
            vec,
            pl.BlockSpec(wax_bf.shape, lambda bi, i: (0, 0)),
            vec, vec, vec, vec,
        ],
        out_specs=pl.BlockSpec((1, t, W), lambda bi, i: (bi, i, 0)),
        out_shape=jax.ShapeDtypeStruct((bsz, s, W), BF16),
        scratch_shapes=[pltpu.VMEM((t + LRU_HALO, W), F32), pltpu.VMEM((8, W), F32)],
        compiler_params=_params("arbitrary", "arbitrary"),
        name="rg_lru",
    )(rest3, rest3, cw_pad, cb, wax_bf, ba, bx, lam, gn)


def _out_proj_kernel(ya_ref, yb_ref, yc_ref, yd_ref, w_ref, x_ref, g_ref, xo_ref, h_ref):
    W = GROUP_WIDTH
    acc = x_ref[...]
    for gi, y_ref in enumerate((ya_ref, yb_ref, yc_ref, yd_ref)):
        acc = acc + jnp.dot(y_ref[...], w_ref[gi * W:(gi + 1) * W, :], preferred_element_type=F32)
    xo_ref[...] = acc
    ms = jnp.mean(acc * acc, axis=-1, keepdims=True)
    h_ref[...] = (acc * lax.rsqrt(ms + EPS) * g_ref[...]).astype(h_ref.dtype)


def _out_proj(ys, w_bf, x2, g, h_dtype):
    n, d = x2.shape
    W = GROUP_WIDTH
    tm = min(OUT_TM, n)
    row = lambda i: (i, 0)
    const = lambda i: (0, 0)
    ysp = pl.BlockSpec((tm, W), row)
    return pl.pallas_call(
        _out_proj_kernel,
        grid=(n // tm,),
        in_specs=[ysp, ysp, ysp, ysp,
                  pl.BlockSpec(w_bf.shape, const),
                  pl.BlockSpec((tm, d), row),
                  pl.BlockSpec((1, d), const)],
        out_specs=[pl.BlockSpec((tm, d), row), pl.BlockSpec((tm, d), row)],
        out_shape=[jax.ShapeDtypeStruct((n, d), F32), jax.ShapeDtypeStruct((n, d), h_dtype)],
        compiler_params=_params("arbitrary"),
        name="out_proj",
    )(*ys, w_bf, x2, g)


def _ffn_kernel(h_ref, x_ref, wg_ref, wu_ref, wd_ref, o_ref):
    @pl.when(pl.program_id(1) == 0)
    def _():
        o_ref[...] = x_ref[...]

    h = h_ref[...]
    g = jnp.dot(h, wg_ref[...], preferred_element_type=F32)
    u = jnp.dot(h, wu_ref[...], preferred_element_type=F32)
    act = (g * jax.nn.sigmoid(g) * u).astype(BF16)
    o_ref[...] += jnp.dot(act, wd_ref[...], preferred_element_type=F32)


def _dense_ffn(hn, x2, wg, wu, wd):
    n, d = x2.shape
    f = wg.shape[1]
    tm = min(FFN_TM, n)
    tf = FFN_TF if f % FFN_TF == 0 else f
    return pl.pallas_call(
        _ffn_kernel,
        grid=(n // tm, f // tf),
        in_specs=[
            pl.BlockSpec((tm, d), lambda i, j: (i, 0)),
            pl.BlockSpec((tm, d), lambda i, j: (i, 0)),
            pl.BlockSpec((d, tf), lambda i, j: (0, j)),
            pl.BlockSpec((d, tf), lambda i, j: (0, j)),
            pl.BlockSpec((tf, d), lambda i, j: (j, 0)),
        ],
        out_specs=pl.BlockSpec((tm, d), lambda i, j: (i, 0)),
        out_shape=jax.ShapeDtypeStruct((n, d), F32),
        compiler_params=_params("arbitrary", "arbitrary"),
        name="dense_ffn",
    )(hn, x2, wg, wu, wd)


def _router_kernel(x_ref, g_ref, rh_ref, rl_ref, tri_ref, route_ref, cnt_ref, base_ref):
    tm = x_ref.shape[0]

    @pl.when(pl.program_id(0) == 0)
    def _():
        base_ref[...] = jnp.zeros(base_ref.shape, F32)

    xf = x_ref[...]
    ms = jnp.mean(xf * xf, axis=-1, keepdims=True)
    hh, hl = _split_hi_lo(xf * lax.rsqrt(ms + EPS) * g_ref[...])
    rh = rh_ref[...]
    logits = (jnp.dot(hh, rh, preferred_element_type=F32)
              + jnp.dot(hl, rh, preferred_element_type=F32)
              + jnp.dot(hh, rl_ref[...], preferred_element_type=F32))
    lane = lax.broadcasted_iota(jnp.int32, (tm, LANES), 1).astype(F32)
    neg = jnp.float32(-jnp.inf)
    logits = jnp.where(lane < N_EXPERTS, logits, neg)
    m1 = jnp.max(logits, axis=1, keepdims=True)
    i1 = jnp.min(jnp.where(logits == m1, lane, float(LANES)), axis=1, keepdims=True)
    l2 = jnp.where(lane == i1, neg, logits)
    m2 = jnp.max(l2, axis=1, keepdims=True)
    i2 = jnp.min(jnp.where(l2 == m2, lane, float(LANES)), axis=1, keepdims=True)
    e = jnp.exp(m2 - m1)
    g1 = 1.0 / (1.0 + e)
    g2 = e / (1.0 + e)
    oh1 = jnp.where(lane == i1, 1.0, 0.0)
    oh2 = jnp.where(lane == i2, 1.0, 0.0)
    oh = oh1 + oh2
    before = jnp.dot(tri_ref[...], oh.astype(BF16), preferred_element_type=F32) + base_ref[0:1, :]
    r1 = jnp.sum(oh1 * before, axis=1, keepdims=True)
    r2 = jnp.sum(oh2 * before, axis=1, keepdims=True)
    base = base_ref[0:1, :] + jnp.sum(oh, axis=0, keepdims=True)
    base_ref[...] = jnp.broadcast_to(base, base_ref.shape)
    cnt_ref[...] = jnp.broadcast_to(base, cnt_ref.shape)
    out = jnp.where(lane == 0, i1, 0.0)
    out = jnp.where(lane == 1, i2, out)
    out = jnp.where(lane == 2, g1, out)
    out = jnp.where(lane == 3, g2, out)
    out = jnp.where(lane == 4, r1, out)
    out = jnp.where(lane == 5, r2, out)
    route_ref[...] = out


def _router(x2, g, rh, rl, tri):
    n, d = x2.shape
    tm = min(ROUTE_TM, n)
    return pl.pallas_call(
        _router_kernel,
        grid=(n // tm,),
        in_specs=[
            pl.BlockSpec((tm, d), lambda i: (i, 0)),
            pl.BlockSpec((1, d), lambda i: (0, 0)),
            pl.BlockSpec((d, LANES), lambda i: (0, 0)),
            pl.BlockSpec((d, LANES), lambda i: (0, 0)),
            pl.BlockSpec((tm, tm), lambda i: (0, 0)),
        ],
        out_specs=[pl.BlockSpec((tm, LANES), lambda i: (i, 0)),
                   pl.BlockSpec((8, LANES), lambda i: (0, 0))],
        out_shape=[jax.ShapeDtypeStruct((n, LANES), F32), jax.ShapeDtypeStruct((8, LANES), F32)],
        scratch_shapes=[pltpu.VMEM((8, LANES), F32)],
        compiler_params=_params("arbitrary"),
        name="router",
    )(x2, g, rh, rl, tri)


def _moe_kernel(bexp_ref, nused_ref, tok_ref, h_hbm, wg_ref, wu_ref, wd_ref, o_ref, xbuf, xb, sem):
    i = pl.program_id(0)
    j = pl.program_id(1)
    tm = xbuf.shape[0]
    used = i < nused_ref[0]

    def row_copy(r):
        return pltpu.make_async_copy(h_hbm.at[pl.ds(tok_ref[0, 0, r], 1)], xbuf.at[pl.ds(r, 1)], sem.at[0])

    @pl.when(used & (j == 0))
    def _():
        def start(r, c):
            row_copy(r).start()
            return c

        lax.fori_loop(0, tm, start, 0)

        def wait(r, c):
            row_copy(r).wait()
            return c

        lax.fori_loop(0, tm, wait, 0)
        xb[...] = xbuf[...].astype(BF16)

    @pl.when(used)
    def _():
        x = xb[...]
        g = jnp.dot(x, wg_ref[0], preferred_element_type=F32)
        u = jnp.dot(x, wu_ref[0], preferred_element_type=F32)
        act = (g * jax.nn.sigmoid(g) * u).astype(BF16)
        y = jnp.dot(act, wd_ref[0], preferred_element_type=F32)

        @pl.when(j == 0)
        def _():
            o_ref[...] = y

        @pl.when(j > 0)
        def _():
            o_ref[...] += y

    @pl.when(jnp.logical_not(used) & (j == 0))
    def _():
        o_ref[...] = jnp.zeros(o_ref.shape, o_ref.dtype)


def _moe_experts(block_exp, n_used, row_tok3, hn, wg, wu, wd):
    n_blocks, _, tm = row_tok3.shape
    d = hn.shape[1]
    f = wg.shape[2]
    tf = MOE_TF if f % MOE_TF == 0 else f
    nj = f // tf

    def live(i, j, be, nu):
        u = i < nu[0]
        return jnp.where(u, i, nu[0] - 1), jnp.where(u, j, nj - 1)

    def w_in_map(i, j, be, nu):
        ii, jj = live(i, j, be, nu)
        return (be[ii], 0, jj)

    def w_down_map(i, j, be, nu):
        ii, jj = live(i, j, be, nu)
        return (be[ii], jj, 0)

    def tok_map(i, j, be, nu):
        return (live(i, j, be, nu)[0], 0, 0)

    def out_map(i, j, be, nu):
        return (i, 0)

    grid_spec = pltpu.PrefetchScalarGridSpec(
        num_scalar_prefetch=2,
        grid=(n_blocks, nj),
        in_specs=[
            pl.BlockSpec((1, 1, tm), tok_map, memory_space=pltpu.SMEM),
            pl.BlockSpec(memory_space=pl.ANY),
            pl.BlockSpec((1, d, tf), w_in_map),
            pl.BlockSpec((1, d, tf), w_in_map),
            pl.BlockSpec((1, tf, d), w_down_map),
        ],
        out_specs=pl.BlockSpec((tm, d), out_map),
        scratch_shapes=[pltpu.VMEM((tm, d), F32), pltpu.VMEM((tm, d), BF16), pltpu.SemaphoreType.DMA((1,))],
    )
    return pl.pallas_call(
        _moe_kernel,
        grid_spec=grid_spec,
        out_shape=jax.ShapeDtypeStruct((n_blocks * tm, d), F32),
        compiler_params=_params("arbitrary", "arbitrary"),
        name="moe_experts",
    )(block_exp, n_used, row_tok3, hn, wg, wu, wd)


def _combine_kernel(d1_ref, d2_ref, ys_hbm, x_ref, route_ref, o_ref, abuf, bbuf, sem):
    tm = x_ref.shape[0]

    def copies(r):
        return (pltpu.make_async_copy(ys_hbm.at[pl.ds(d1_ref[0, 0, r], 1)], abuf.at[pl.ds(r, 1)], sem.at[0]),
                pltpu.make_async_copy(ys_hbm.at[pl.ds(d2_ref[0, 0, r], 1)], bbuf.at[pl.ds(r, 1)], sem.at[1]))

    def start(r, c):
        ca, cb = copies(r)
        ca.start()
        cb.start()
        return c

    lax.fori_loop(0, tm, start, 0)

    def wait(r, c):
        ca, cb = copies(r)
        ca.wait()
        cb.wait()
        return c

    lax.fori_loop(0, tm, wait, 0)
    route = route_ref[...]
    g1 = route[:, 2:3]
    g2 = route[:, 3:4]
    o_ref[...] = x_ref[...] + g1 * abuf[...] + g2 * bbuf[...]


def _combine(dest1_3, dest2_3, ys, x2, route):
    n, d = x2.shape
    tm = dest1_3.shape[2]
    idx = pl.BlockSpec((1, 1, tm), lambda i: (i, 0, 0), memory_space=pltpu.SMEM)
    return pl.pallas_call(
        _combine_kernel,
        grid=(n // tm,),
        in_specs=[idx, idx,
                  pl.BlockSpec(memory_space=pl.ANY),
                  pl.BlockSpec((tm, d), lambda i: (i, 0)),
                  pl.BlockSpec((tm, LANES), lambda i: (i, 0))],
        out_specs=pl.BlockSpec((tm, d), lambda i: (i, 0)),
        out_shape=jax.ShapeDtypeStruct((n, d), F32),
        scratch_shapes=[pltpu.VMEM((tm, d), F32), pltpu.VMEM((tm, d), F32), pltpu.SemaphoreType.DMA((2,))],
        compiler_params=_params("arbitrary"),
        name="moe_combine",
    )(dest1_3, dest2_3, ys, x2, route)


def _top2_moe(hn, x2, g2, router, wg, wu, wd):
    n, d = x2.shape
    a = 2 * n
    tm = min(MOE_TM, n)
    r_pad = jnp.zeros((d, LANES), F32).at[:, :N_EXPERTS].set(router)
    rh, rl = _split_hi_lo(r_pad)
    rt = min(ROUTE_TM, n)
    tri = (lax.broadcasted_iota(jnp.int32, (rt, rt), 1) < lax.broadcasted_iota(jnp.int32, (rt, rt), 0)).astype(BF16)
    route, cnt = _router(x2, g2, rh, rl, tri)

    counts = cnt[0, :N_EXPERTS].astype(jnp.int32)
    padded = (counts + tm - 1) // tm * tm
    pad_ends = jnp.cumsum(padded)
    pad_starts = pad_ends - padded
    e1 = route[:, 0].astype(jnp.int32)
    e2 = route[:, 1].astype(jnp.int32)
    dest1 = pad_starts[e1] + route[:, 4].astype(jnp.int32)
    dest2 = pad_starts[e2] + route[:, 5].astype(jnp.int32)
    n_blocks = a // tm + N_EXPERTS
    n_rows = n_blocks * tm
    tok = jnp.arange(n, dtype=jnp.int32)
    row_tok = jnp.zeros((n_rows,), jnp.int32).at[dest1].set(tok).at[dest2].set(tok)
    block_start = jnp.arange(n_blocks, dtype=jnp.int32) * tm
    block_exp = jnp.minimum(jnp.searchsorted(pad_ends, block_start, side="right"), N_EXPERTS - 1).astype(jnp.int32)
    n_used = (pad_ends[-1] // tm).astype(jnp.int32).reshape(1)

    ys = _moe_experts(block_exp, n_used, row_tok.reshape(n_blocks, 1, tm), hn, wg, wu, wd)
    ct = min(COMB_TM, n)
    return _combine(dest1.reshape(n // ct, 1, ct), dest2.reshape(n // ct, 1, ct), ys, x2, route)


def _block_diag(w):
    h, dh, _ = w.shape
    eye = jnp.eye(h, dtype=w.dtype)
    return jnp.einsum("hij,hg->higj", w, eye).reshape(h * dh, h * dh)


def kernel(x, norm1_g, w_in, q_norm_g, k_norm_g, conf_dw_w, conf_dw_b, conf_ln_g, conf_ln_b, gmlp_ln_g, gmlp_ln_b, gmlp_ws, gmlp_bs, lru_conv_w, lru_conv_b, lru_wa, lru_ba, lru_wx, lru_bx, lru_lambda, group_norm_g, w_out, norm2_g, ffn_w_gate, ffn_w_up, ffn_w_down, moe_router, moe_w_gate, moe_w_up, moe_w_down):
    bsz, s, d = x.shape
    n = bsz * s
    depth = w_in.shape[0]
    W = GROUP_WIDTH
    row = lambda v: v.reshape(1, -1).astype(F32)

    head_id = jnp.arange(W) // HEAD_DIM
    bd = (head_id[:, None] == head_id[None, :]).astype(BF16)
    bd2 = jnp.concatenate([bd, bd], axis=0)
    t_att = min(ATT_T, s)
    u_incl = (jnp.arange(t_att)[:, None] >= jnp.arange(t_att)[None, :]).astype(BF16)
    uu = jnp.concatenate([u_incl, u_incl], axis=0)
    tril = jnp.tril(jnp.ones((GMLP_CHUNK, GMLP_CHUNK), dtype=bool))

    x2 = x.reshape(n, d)
    for l in range(depth):
        qg = row(jnp.tile(q_norm_g[l], GROUP_HEADS) * (HEAD_DIM ** -0.5))
        kg = row(jnp.tile(k_norm_g[l], GROUP_HEADS))
        q, k, v, rest = _in_proj(x2, row(norm1_g[l]), w_in[l].astype(BF16), qg, kg, bd2)
        gn = group_norm_g[l].reshape(N_GROUPS, 1, W)
        to3 = lambda t: t.reshape(bsz, s, t.shape[-1])
        rest3 = to3(rest)
        y_a = _sb_attention(to3(q), to3(k), to3(v), uu, gn[0])
        cw = jnp.zeros((CONF_HALO, W), F32).at[:CONF_KERNEL].set(conf_dw_w[l])
        y_b = _conformer(rest3, 0, cw, row(conf_dw_b[l]), row(conf_ln_g[l]), row(conf_ln_b[l]), gn[1])
        ws = jnp.where(tril, gmlp_ws[l], 0.0).astype(BF16)
        bs_mat = jnp.repeat(gmlp_bs[l].T, HEAD_DIM, axis=1)
        y_c = _gmlp(rest3, 2, row(gmlp_ln_g[l]), row(gmlp_ln_b[l]), ws, bs_mat, gn[2])
        lw = jnp.zeros((8, W), F32).at[:LRU_CONV].set(lru_conv_w[l])
        wax = jnp.concatenate([_block_diag(lru_wa[l]), _block_diag(lru_wx[l])], axis=1).astype(BF16)
        y_d = _rg_lru(rest3, 4, lw, row(lru_conv_b[l]), wax, row(lru_ba[l]), row(lru_bx[l]),
                      row(lru_lambda[l]), gn[3])
        ys = [t.reshape(n, W) for t in (y_a, y_b, y_c, y_d)]
        x2, hn = _out_proj(ys, w_out[l].astype(BF16), x2, row(norm2_g[l]), BF16 if l % 2 == 0 else F32)
        j = l // 2
        if l % 2 == 0:
            x2 = _dense_ffn(hn, x2, ffn_w_gate[j].astype(BF16), ffn_w_up[j].astype(BF16),
                            ffn_w_down[j].astype(BF16))
        else:
            x2 = _top2_moe(hn, x2, row(norm2_g[l]), moe_router[j], moe_w_gate[j].astype(BF16),
                           moe_w_up[j].astype(BF16), moe_w_down[j].astype(BF16))
    return x2.reshape(bsz, s, d)
```

```python
import functools

import jax
import jax.numpy as jnp
from jax import lax
from jax.experimental import pallas as pl
from jax.experimental.pallas import tpu as pltpu
from jax.experimental.pallas import tpu_sc as plsc

F32 = jnp.float32
BF16 = jnp.bfloat16

HEAD_DIM = 64
GROUP_HEADS = 4
GROUP_WIDTH = GROUP_HEADS * HEAD_DIM
N_GROUPS = 4
CONF_KERNEL = 31
GMLP_CHUNK = 128
LRU_CONV = 4
LRU_C = 8.0
N_EXPERTS = 8
EPS = 1e-6

LANES = 128
VMEM_LIMIT = 56 * 1024 * 1024

IN_TM = 512
ATT_T = 256
CONF_T = 256
CONF_R = 64
CONF_HALO = 32
GMLP_T = 512
LRU_T = 256
LRU_HALO = 8
OUT_TM = 512
FFN_TM = 512
FFN_TF = 1408
ROUTE_TM = 512
MOE_TM = 512
MOE_TF = 1792
COMB_TM = 512
SC_GATHER_ROWS = 32


def _params(*sem):
    return pltpu.CompilerParams(dimension_semantics=sem, vmem_limit_bytes=VMEM_LIMIT)


def _split_hi_lo(x):
    hi = x.astype(BF16)
    lo = (x - hi.astype(F32)).astype(BF16)
    return hi, lo


def _softplus(z):
    return jnp.maximum(z, 0.0) + jnp.log(1.0 + jnp.exp(-jnp.abs(z)))


def _gelu_tanh(x):
    c = 0.7978845608028654
    return 0.5 * x * (1.0 + jnp.tanh(c * (x + 0.044715 * (x * x * x))))


def _group_rms(y, g):
    ms = jnp.mean(y * y, axis=-1, keepdims=True)
    return y * lax.rsqrt(ms + EPS) * g


def _in_proj_kernel(x_ref, g_ref, w_ref, qg_ref, kg_ref, bd_ref, q_ref, k_ref, v_ref, rest_ref):
    xf = x_ref[...]
    ms = jnp.mean(xf * xf, axis=-1, keepdims=True)
    h = (xf * lax.rsqrt(ms + EPS) * g_ref[...]).astype(BF16)
    W = GROUP_WIDTH

    def proj(c):
        return jnp.dot(h, w_ref[:, c * W:(c + 1) * W], preferred_element_type=F32)

    def head_norm(p, gain):
        hi, lo = _split_hi_lo(p * p)
        ss = jnp.dot(jnp.concatenate([hi, lo], axis=1), bd_ref[...], preferred_element_type=F32)
        return p * lax.rsqrt(ss * (1.0 / HEAD_DIM) + EPS) * gain

    q_ref[...] = head_norm(proj(0), qg_ref[...]).astype(BF16)
    k_ref[...] = head_norm(proj(1), kg_ref[...]).astype(BF16)
    v_ref[...] = proj(2).astype(BF16)
    n_rest = rest_ref.shape[1] // W
    for c in range(n_rest):
        rest_ref[:, c * W:(c + 1) * W] = proj(3 + c)


def _in_proj(x2, g, w_bf, qg, kg, bd2):
    n, d = x2.shape
    cols = w_bf.shape[1]
    W = GROUP_WIDTH
    rest_cols = cols - 3 * W
    tm = min(IN_TM, n)
    row = lambda i: (i, 0)
    const = lambda i: (0, 0)
    return pl.pallas_call(
        _in_proj_kernel,
        grid=(n // tm,),
        in_specs=[
            pl.BlockSpec((tm, d), row),
            pl.BlockSpec((1, d), const),
            pl.BlockSpec((d, cols), const),
            pl.BlockSpec((1, W), const),
            pl.BlockSpec((1, W), const),
            pl.BlockSpec((2 * W, W), const),
        ],
        out_specs=[
            pl.BlockSpec((tm, W), row),
            pl.BlockSpec((tm, W), row),
            pl.BlockSpec((tm, W), row),
            pl.BlockSpec((tm, rest_cols), row),
        ],
        out_shape=[
            jax.ShapeDtypeStruct((n, W), BF16),
            jax.ShapeDtypeStruct((n, W), BF16),
            jax.ShapeDtypeStruct((n, W), BF16),
            jax.ShapeDtypeStruct((n, rest_cols), F32),
        ],
        compiler_params=_params("arbitrary"),
        name="in_proj",
    )(x2, g, w_bf, qg, kg, bd2)


def _sb_attn_kernel(q_ref, k_ref, v_ref, uu_ref, gn_ref, o_ref):
    T = q_ref.shape[1]
    n_pairs = GROUP_WIDTH // LANES
    i = pl.program_id(1)
    lane = lax.broadcasted_iota(jnp.int32, (T, LANES), 1)
    rows = lax.broadcasted_iota(jnp.int32, (2 * T, T), 0)
    cols = lax.broadcasted_iota(jnp.int32, (2 * T, T), 1)
    causal = cols < jnp.where(rows >= T, rows - T, rows)
    uu = uu_ref[...]

    def tile(qs, k2, v2, carry, mask):
        z = lax.dot_general(qs, k2, (((1,), (1,)), ((), ())), preferred_element_type=F32)
        lk = -_softplus(z)
        if mask is not None:
            lk = jnp.where(mask, lk, 0.0)
        hi, lo = _split_hi_lo(lk)
        cum = jnp.dot(jnp.concatenate([hi, lo], axis=1), uu, preferred_element_type=F32)
        w = jnp.exp(z + cum + carry)
        if mask is not None:
            w = jnp.where(mask, w, 0.0)
        pv = jnp.dot(w.astype(BF16), v2, preferred_element_type=F32)
        return pv, carry + jnp.sum(lk, axis=1, keepdims=True)

    def kv(r0, pair):
        cs = slice(pair * LANES, (pair + 1) * LANES)
        return k_ref[0, pl.ds(r0, T), cs], v_ref[0, pl.ds(r0, T), cs]

    qs = []
    for pair in range(n_pairs):
        q2 = q_ref[0, :, pair * LANES:(pair + 1) * LANES]
        zero = jnp.zeros_like(q2)
        qs.append(jnp.concatenate([jnp.where(lane < HEAD_DIM, q2, zero),
                                   jnp.where(lane >= HEAD_DIM, q2, zero)], axis=0))
    row0 = pl.multiple_of(i * T, T)
    state = tuple(tile(qs[p], *kv(row0, p), jnp.zeros((2 * T, 1), F32), causal) for p in range(n_pairs))

    def body(step, st):
        r0 = pl.multiple_of((i - 1 - step) * T, T)
        new = []
        for p in range(n_pairs):
            acc, carry = st[p]
            pv, carry = tile(qs[p], *kv(r0, p), carry, None)
            new.append((acc + pv, carry))
        return tuple(new)

    state = lax.fori_loop(0, i, body, state)
    y = jnp.concatenate([jnp.where(lane < HEAD_DIM, acc[:T], acc[T:]) for acc, _ in state], axis=1)
    o_ref[0] = _group_rms(y, gn_ref[...]).astype(o_ref.dtype)


def _sb_attention(q3, k3, v3, uu, gn):
    b, s, w = q3.shape
    t = min(ATT_T, s)
    return pl.pallas_call(
        _sb_attn_kernel,
        grid=(b, s // t),
        in_specs=[
            pl.BlockSpec((1, t, w), lambda bi, i: (bi, i, 0)),
            pl.BlockSpec((1, s, w), lambda bi, i: (bi, 0, 0)),
            pl.BlockSpec((1, s, w), lambda bi, i: (bi, 0, 0)),
            pl.BlockSpec((2 * t, t), lambda bi, i: (0, 0)),
            pl.BlockSpec((1, w), lambda bi, i: (0, 0)),
        ],
        out_specs=pl.BlockSpec((1, t, w), lambda bi, i: (bi, i, 0)),
        out_shape=jax.ShapeDtypeStruct((b, s, w), BF16),
        compiler_params=_params("arbitrary", "arbitrary"),
        name="sb_attn",
    )(q3, k3, v3, uu, gn)


def _conf_kernel(val_ref, gate_ref, w_ref, b_ref, lng_ref, lnb_ref, gn_ref, o_ref, hs_ref):
    T = val_ref.shape[1]
    H = CONF_HALO

    @pl.when(pl.program_id(1) == 0)
    def _():
        hs_ref[0:H, :] = jnp.zeros((H, GROUP_WIDTH), F32)

    hs_ref[H:H + T, :] = val_ref[0] * jax.nn.sigmoid(gate_ref[0])
    off = H - (CONF_KERNEL - 1)
    R = min(CONF_R, T)
    for r0 in range(0, T, R):
        acc = jnp.broadcast_to(b_ref[...], (R, GROUP_WIDTH))
        for k in range(CONF_KERNEL):
            acc = acc + w_ref[k:k + 1, :] * hs_ref[off + k + r0:off + k + r0 + R, :]
        mu = jnp.mean(acc, axis=-1, keepdims=True)
        xc = acc - mu
        var = jnp.mean(xc * xc, axis=-1, keepdims=True)
        y = xc * lax.rsqrt(var + EPS) * lng_ref[...] + lnb_ref[...]
        y = y * jax.nn.sigmoid(y)
        o_ref[0, r0:r0 + R, :] = _group_rms(y, gn_ref[...]).astype(o_ref.dtype)
    hs_ref[0:H, :] = hs_ref[T:T + H, :]


def _conformer(rest3, col0, w_pad, b, lng, lnb, gn):
    bsz, s, _ = rest3.shape
    W = GROUP_WIDTH
    t = min(CONF_T, s)
    vec = pl.BlockSpec((1, W), lambda bi, i: (0, 0))
    return pl.pallas_call(
        _conf_kernel,
        grid=(bsz, s // t),
        in_specs=[
            pl.BlockSpec((1, t, W), lambda bi, i: (bi, i, col0)),
            pl.BlockSpec((1, t, W), lambda bi, i: (bi, i, col0 + 1)),
            pl.BlockSpec(w_pad.shape, lambda bi, i: (0, 0)),
            vec, vec, vec, vec,
        ],
        out_specs=pl.BlockSpec((1, t, W), lambda bi, i: (bi, i, 0)),
        out_shape=jax.ShapeDtypeStruct((bsz, s, W), BF16),
        scratch_shapes=[pltpu.VMEM((t + CONF_HALO, W), F32)],
        compiler_params=_params("arbitrary", "arbitrary"),
        name="conformer",
    )(rest3, rest3, w_pad, b, lng, lnb, gn)


def _gmlp_kernel(u_ref, v_ref, lng_ref, lnb_ref, ws_ref, bs_ref, gn_ref, o_ref):
    T = u_ref.shape[1]
    C = GMLP_CHUNK
    lane = lax.broadcasted_iota(jnp.int32, (C, GROUP_WIDTH), 1)
    for c0 in range(0, T, C):
        u = _gelu_tanh(u_ref[0, c0:c0 + C, :])
        v = _gelu_tanh(v_ref[0, c0:c0 + C, :])
        mu = jnp.mean(v, axis=-1, keepdims=True)
        xc = v - mu
        var = jnp.mean(xc * xc, axis=-1, keepdims=True)
        vb = (xc * lax.rsqrt(var + EPS) * lng_ref[...] + lnb_ref[...]).astype(BF16)
        mixed = bs_ref[...]
        for h in range(GROUP_HEADS):
            m = jnp.dot(ws_ref[h], vb, preferred_element_type=F32)
            in_head = (lane >= h * HEAD_DIM) & (lane < (h + 1) * HEAD_DIM)
            mixed = mixed + jnp.where(in_head, m, 0.0)
        o_ref[0, c0:c0 + C, :] = _group_rms(u * mixed, gn_ref[...]).astype(o_ref.dtype)


def _gmlp(rest3, col0, lng, lnb, ws_bf, bs_mat, gn):
    bsz, s, _ = rest3.shape
    W = GROUP_WIDTH
    t = min(GMLP_T, s)
    vec = pl.BlockSpec((1, W), lambda bi, i: (0, 0))
    return pl.pallas_call(
        _gmlp_kernel,
        grid=(bsz, s // t),
        in_specs=[
            pl.BlockSpec((1, t, W), lambda bi, i: (bi, i, col0)),
            pl.BlockSpec((1, t, W), lambda bi, i: (bi, i, col0 + 1)),
            vec, vec,
            pl.BlockSpec(ws_bf.shape, lambda bi, i: (0, 0, 0)),
            pl.BlockSpec(bs_mat.shape, lambda bi, i: (0, 0)),
            vec,
        ],
        out_specs=pl.BlockSpec((1, t, W), lambda bi, i: (bi, i, 0)),
        out_shape=jax.ShapeDtypeStruct((bsz, s, W), BF16),
        compiler_params=_params("arbitrary", "arbitrary"),
        name="gmlp",
    )(rest3, rest3, lng, lnb, ws_bf, bs_mat, gn)


def _lru_kernel(x_ref, gate_ref, cw_ref, cb_ref, wax_ref, ba_ref, bx_ref, lam_ref, gn_ref, o_ref,
                xs_ref, hprev_ref):
    T = x_ref.shape[1]
    W = GROUP_WIDTH
    H = LRU_HALO

    @pl.when(pl.program_id(1) == 0)
    def _():
        xs_ref[0:H, :] = jnp.zeros((H, W), F32)
        hprev_ref[...] = jnp.zeros(hprev_ref.shape, F32)

    xs_ref[H:H + T, :] = x_ref[0]
    off = H - (LRU_CONV - 1)
    xb = jnp.broadcast_to(cb_ref[...], (T, W))
    for k in range(LRU_CONV):
        xb = xb + cw_ref[k:k + 1, :] * xs_ref[off + k:off + k + T, :]
    xs_ref[0:H, :] = xs_ref[T:T + H, :]

    pre = jnp.dot(xb.astype(BF16), wax_ref[...], preferred_element_type=F32)
    r = jax.nn.sigmoid(pre[:, :W] + ba_ref[...])
    ig = jax.nn.sigmoid(pre[:, W:] + bx_ref[...])
    log_a = (-LRU_C) * r * _softplus(-lam_ref[...])
    a = jnp.exp(log_a)
    th = jnp.tanh(log_a)
    one_minus_a2 = 2.0 * th / (th - 1.0)
    b = jnp.sqrt(one_minus_a2) * (ig * xb)

    row = lax.broadcasted_iota(jnp.int32, (T, W), 0)
    d = 1
    while d < T:
        keep = row >= d
        a_sh = jnp.where(keep, pltpu.roll(a, d, 0), 1.0)
        b_sh = jnp.where(keep, pltpu.roll(b, d, 0), 0.0)
        b = a * b_sh + b
        a = a * a_sh
        d *= 2
    h = b + a * hprev_ref[0:1, :]
    hprev_ref[...] = jnp.broadcast_to(h[T - 1:T, :], hprev_ref.shape)
    y = h * _gelu_tanh(gate_ref[0])
    o_ref[0] = _group_rms(y, gn_ref[...]).astype(o_ref.dtype)


def _rg_lru(rest3, col0, cw_pad, cb, wax_bf, ba, bx, lam, gn):
    bsz, s, _ = rest3.shape
    W = GROUP_WIDTH
    t = min(LRU_T, s)
    vec = pl.BlockSpec((1, W), lambda bi, i: (0, 0))
    return pl.pallas_call(
        _lru_kernel,
        grid=(bsz, s // t),
        in_specs=[
            pl.BlockSpec((1, t, W), lambda bi, i: (bi, i, col0)),
            pl.BlockSpec((1, t, W), lambda bi, i: (bi, i, col0 + 1)),
            pl.BlockSpec(cw_pad.shape, lambda bi, i: (0, 0)),
            vec,
            pl.BlockSpec(wax_bf.shape, lambda bi, i: (0, 0)),
            vec, vec, vec, vec,
        ],
        out_specs=pl.BlockSpec((1, t, W), lambda bi, i: (bi, i, 0)),
        out_shape=jax.ShapeDtypeStruct((bsz, s, W), BF16),
        scratch_shapes=[pltpu.VMEM((t + LRU_HALO, W), F32), pltpu.VMEM((8, W), F32)],
        compiler_params=_params("arbitrary", "arbitrary"),
        name="rg_lru",
    )(rest3, rest3, cw_pad, cb, wax_bf, ba, bx, lam, gn)


def _out_proj_kernel(ya_ref, yb_ref, yc_ref, yd_ref, w_ref, x_ref, g_ref, xo_ref, h_ref):
    W = GROUP_WIDTH
    acc = x_ref[...]
    for gi, y_ref in enumerate((ya_ref, yb_ref, yc_ref, yd_ref)):
        acc = acc + jnp.dot(y_ref[...], w_ref[gi * W:(gi + 1) * W, :], preferred_element_type=F32)
    xo_ref[...] = acc
    ms = jnp.mean(acc * acc, axis=-1, keepdims=True)
    h_ref[...] = (acc * lax.rsqrt(ms + EPS) * g_ref[...]).astype(h_ref.dtype)


def _out_proj(ys, w_bf, x2, g, h_dtype):
    n, d = x2.shape
    W = GROUP_WIDTH
    tm = min(OUT_TM, n)
    row = lambda i: (i, 0)
    const = lambda i: (0, 0)
    ysp = pl.BlockSpec((tm, W), row)
    return pl.pallas_call(
        _out_proj_kernel,
        grid=(n // tm,),
        in_specs=[ysp, ysp, ysp, ysp,
                  pl.BlockSpec(w_bf.shape, const),
                  pl.BlockSpec((tm, d), row),
                  pl.BlockSpec((1, d), const)],
        out_specs=[pl.BlockSpec((tm, d), row), pl.BlockSpec((tm, d), row)],
        out_shape=[jax.ShapeDtypeStruct((n, d), F32), jax.ShapeDtypeStruct((n, d), h_dtype)],
        compiler_params=_params("arbitrary"),
        name="out_proj",
    )(*ys, w_bf, x2, g)


def _ffn_kernel(h_ref, x_ref, wg_ref, wu_ref, wd_ref, o_ref):
    @pl.when(pl.program_id(1) == 0)
    def _():
        o_ref[...] = x_ref[...]

    h = h_ref[...]
    g = jnp.dot(h, wg_ref[...], preferred_element_type=F32)
    u = jnp.dot(h, wu_ref[...], preferred_element_type=F32)
    act = (g * jax.nn.sigmoid(g) * u).astype(BF16)
    o_ref[...] += jnp.dot(act, wd_ref[...], preferred_element_type=F32)


def _dense_ffn(hn, x2, wg, wu, wd):
    n, d = x2.shape
    f = wg.shape[1]
    tm = min(FFN_TM, n)
    tf = FFN_TF if f % FFN_TF == 0 else f
    return pl.pallas_call(
        _ffn_kernel,
        grid=(n // tm, f // tf),
        in_specs=[
            pl.BlockSpec((tm, d), lambda i, j: (i, 0)),
            pl.BlockSpec((tm, d), lambda i, j: (i, 0)),
            pl.BlockSpec((d, tf), lambda i, j: (0, j)),
            pl.BlockSpec((d, tf), lambda i, j: (0, j)),
            pl.BlockSpec((tf, d), lambda i, j: (j, 0)),
        ],
        out_specs=pl.BlockSpec((tm, d), lambda i, j: (i, 0)),
        out_shape=jax.ShapeDtypeStruct((n, d), F32),
        compiler_params=_params("arbitrary", "arbitrary"),
        name="dense_ffn",
    )(hn, x2, wg, wu, wd)


def _router_kernel(x_ref, g_ref, rh_ref, rl_ref, tri_ref, route_ref, cnt_ref, base_ref):
    tm = x_ref.shape[0]

    @pl.when(pl.program_id(0) == 0)
    def _():
        base_ref[...] = jnp.zeros(base_ref.shape, F32)

    xf = x_ref[...]
    ms = jnp.mean(xf * xf, axis=-1, keepdims=True)
    hh, hl = _split_hi_lo(xf * lax.rsqrt(ms + EPS) * g_ref[...])
    rh = rh_ref[...]
    logits = (jnp.dot(hh, rh, preferred_element_type=F32)
              + jnp.dot(hl, rh, preferred_element_type=F32)
              + jnp.dot(hh, rl_ref[...], preferred_element_type=F32))
    lane = lax.broadcasted_iota(jnp.int32, (tm, LANES), 1).astype(F32)
    neg = jnp.float32(-jnp.inf)
    logits = jnp.where(lane < N_EXPERTS, logits, neg)
    m1 = jnp.max(logits, axis=1, keepdims=True)
    i1 = jnp.min(jnp.where(logits == m1, lane, float(LANES)), axis=1, keepdims=True)
    l2 = jnp.where(lane == i1, neg, logits)
    m2 = jnp.max(l2, axis=1, keepdims=True)
    i2 = jnp.min(jnp.where(l2 == m2, lane, float(LANES)), axis=1, keepdims=True)
    e = jnp.exp(m2 - m1)
    g1 = 1.0 / (1.0 + e)
    g2 = e / (1.0 + e)
    oh1 = jnp.where(lane == i1, 1.0, 0.0)
    oh2 = jnp.where(lane == i2, 1.0, 0.0)
    oh = oh1 + oh2
    before = jnp.dot(tri_ref[...], oh.astype(BF16), preferred_element_type=F32) + base_ref[0:1, :]
    r1 = jnp.sum(oh1 * before, axis=1, keepdims=True)
    r2 = jnp.sum(oh2 * before, axis=1, keepdims=True)
    base = base_ref[0:1, :] + jnp.sum(oh, axis=0, keepdims=True)
    base_ref[...] = jnp.broadcast_to(base, base_ref.shape)
    cnt_ref[...] = jnp.broadcast_to(base, cnt_ref.shape)
    out = jnp.where(lane == 0, i1, 0.0)
    out = jnp.where(lane == 1, i2, out)
    out = jnp.where(lane == 2, g1, out)
    out = jnp.where(lane == 3, g2, out)
    out = jnp.where(lane == 4, r1, out)
    out = jnp.where(lane == 5, r2, out)
    route_ref[...] = out


def _router(x2, g, rh, rl, tri):
    n, d = x2.shape
    tm = min(ROUTE_TM, n)
    return pl.pallas_call(
        _router_kernel,
        grid=(n // tm,),
        in_specs=[
            pl.BlockSpec((tm, d), lambda i: (i, 0)),
            pl.BlockSpec((1, d), lambda i: (0, 0)),
            pl.BlockSpec((d, LANES), lambda i: (0, 0)),
            pl.BlockSpec((d, LANES), lambda i: (0, 0)),
            pl.BlockSpec((tm, tm), lambda i: (0, 0)),
        ],
        out_specs=[pl.BlockSpec((tm, LANES), lambda i: (i, 0)),
                   pl.BlockSpec((8, LANES), lambda i: (0, 0))],
        out_shape=[jax.ShapeDtypeStruct((n, LANES), F32), jax.ShapeDtypeStruct((8, LANES), F32)],
        scratch_shapes=[pltpu.VMEM((8, LANES), F32)],
        compiler_params=_params("arbitrary"),
        name="router",
    )(x2, g, rh, rl, tri)


def _sc_gather_rows(table, idx):
    _, d = table.shape
    b = idx.shape[0]
    win = SC_GATHER_ROWS
    sc = plsc.get_sparse_core_info()
    n_workers = sc.num_cores * sc.num_subcores
    per_w = b // n_workers
    n_win = per_w // win
    assert per_w * n_workers == b and n_win * win == per_w and n_win % 2 == 0, (b, n_workers, win)
    mesh = plsc.VectorSubcoreMesh(core_axis_name="c", subcore_axis_name="s")
    dma = pltpu.SemaphoreType.DMA

    @functools.partial(
        pl.kernel, mesh=mesh, out_type=jax.ShapeDtypeStruct((b, d), table.dtype), name="sc_gather_rows",
        scratch_types=[pltpu.VMEM((win,), jnp.int32), pltpu.VMEM((win,), jnp.int32),
                       pltpu.VMEM((win, d), table.dtype), pltpu.VMEM((win, d), table.dtype),
                       dma, dma, dma, dma])
    def gather_kernel(table_hbm, idx_hbm, out_hbm, i0, i1, r0, r1, g0, g1, w0, w1):
        wid = lax.axis_index("s") * sc.num_cores + lax.axis_index("c")
        base = wid * per_w
        idxb, rows, gsem, wsem = (i0, i1), (r0, r1), (g0, g1), (w0, w1)

        def off(c):
            return pl.multiple_of(base + c * win, win)

        def gather(s):
            return pltpu.make_async_copy(table_hbm.at[idxb[s]], rows[s], gsem[s])

        def write(c, s):
            return pltpu.make_async_copy(rows[s], out_hbm.at[pl.ds(off(c), win)], wsem[s])

        pltpu.sync_copy(idx_hbm.at[pl.ds(off(0), win)], idxb[0])
        gather(0).start()

        @pl.loop(0, n_win, step=2)
        def _(c):
            for s in (0, 1):
                cc = c + s

                @pl.when(cc + 1 < n_win)
                def _():
                    @pl.when(cc >= 1)
                    def _():
                        write(cc - 1, 1 - s).wait()

                    pltpu.sync_copy(idx_hbm.at[pl.ds(off(cc + 1), win)], idxb[1 - s])
                    gather(1 - s).start()

                gather(s).wait()
                write(cc, s).start()

        write(n_win - 2, 0).wait()
        write(n_win - 1, 1).wait()

    return gather_kernel(table, idx)


def _moe_kernel(bexp_ref, nused_ref, x_ref, wg_ref, wu_ref, wd_ref, o_ref, xb):
    i = pl.program_id(0)
    j = pl.program_id(1)
    used = i < nused_ref[0]

    @pl.when(used & (j == 0))
    def _():
        xb[...] = x_ref[...].astype(BF16)

    @pl.when(used)
    def _():
        x = xb[...]
        g = jnp.dot(x, wg_ref[0], preferred_element_type=F32)
        u = jnp.dot(x, wu_ref[0], preferred_element_type=F32)
        act = (g * jax.nn.sigmoid(g) * u).astype(BF16)
        y = jnp.dot(act, wd_ref[0], preferred_element_type=F32)

        @pl.when(j == 0)
        def _():
            o_ref[...] = y

        @pl.when(j > 0)
        def _():
            o_ref[...] += y

    @pl.when(jnp.logical_not(used) & (j == 0))
    def _():
        o_ref[...] = jnp.zeros(o_ref.shape, o_ref.dtype)


def _moe_experts(block_exp, n_used, xs, wg, wu, wd, tm):
    n_rows, d = xs.shape
    n_blocks = n_rows // tm
    f = wg.shape[2]
    tf = MOE_TF if f % MOE_TF == 0 else f
    nj = f // tf

    def live(i, j, be, nu):
        u = i < nu[0]
        return jnp.where(u, i, nu[0] - 1), jnp.where(u, j, nj - 1)

    def w_in_map(i, j, be, nu):
        ii, jj = live(i, j, be, nu)
        return (be[ii], 0, jj)

    def w_down_map(i, j, be, nu):
        ii, jj = live(i, j, be, nu)
        return (be[ii], jj, 0)

    def x_map(i, j, be, nu):
        return (live(i, j, be, nu)[0], 0)

    def out_map(i, j, be, nu):
        return (i, 0)

    grid_spec = pltpu.PrefetchScalarGridSpec(
        num_scalar_prefetch=2,
        grid=(n_blocks, nj),
        in_specs=[
            pl.BlockSpec((tm, d), x_map),
            pl.BlockSpec((1, d, tf), w_in_map),
            pl.BlockSpec((1, d, tf), w_in_map),
            pl.BlockSpec((1, tf, d), w_down_map),
        ],
        out_specs=pl.BlockSpec((tm, d), out_map),
        scratch_shapes=[pltpu.VMEM((tm, d), BF16)],
    )
    return pl.pallas_call(
        _moe_kernel,
        grid_spec=grid_spec,
        out_shape=jax.ShapeDtypeStruct((n_rows, d), F32),
        compiler_params=_params("arbitrary", "arbitrary"),
        name="moe_experts",
    )(block_exp, n_used, xs, wg, wu, wd)


def _combine_kernel(x_ref, route_ref, a_ref, b_ref, o_ref):
    route = route_ref[...]
    o_ref[...] = x_ref[...] + route[:, 2:3] * a_ref[...] + route[:, 3:4] * b_ref[...]


def _combine(yg, x2, route):
    n, d = x2.shape
    tm = min(COMB_TM, n)
    nb = n // tm
    return pl.pallas_call(
        _combine_kernel,
        grid=(nb,),
        in_specs=[pl.BlockSpec((tm, d), lambda i: (i, 0)),
                  pl.BlockSpec((tm, LANES), lambda i: (i, 0)),
                  pl.BlockSpec((tm, d), lambda i: (i, 0)),
                  pl.BlockSpec((tm, d), lambda i: (i + nb, 0))],
        out_specs=pl.BlockSpec((tm, d), lambda i: (i, 0)),
        out_shape=jax.ShapeDtypeStruct((n, d), F32),
        compiler_params=_params("arbitrary"),
        name="moe_combine",
    )(x2, route, yg, yg)


def _top2_moe(hn, x2, g2, router, wg, wu, wd):
    n, d = x2.shape
    a = 2 * n
    tm = min(MOE_TM, n)
    r_pad = jnp.zeros((d, LANES), F32).at[:, :N_EXPERTS].set(router)
    rh, rl = _split_hi_lo(r_pad)
    rt = min(ROUTE_TM, n)
    tri = (lax.broadcasted_iota(jnp.int32, (rt, rt), 1) < lax.broadcasted_iota(jnp.int32, (rt, rt), 0)).astype(BF16)
    route, cnt = _router(x2, g2, rh, rl, tri)

    counts = cnt[0, :N_EXPERTS].astype(jnp.int32)
    padded = (counts + tm - 1) // tm * tm
    pad_ends = jnp.cumsum(padded)
    pad_starts = pad_ends - padded
    e1 = route[:, 0].astype(jnp.int32)
    e2 = route[:, 1].astype(jnp.int32)
    dest1 = pad_starts[e1] + route[:, 4].astype(jnp.int32)
    dest2 = pad_starts[e2] + route[:, 5].astype(jnp.int32)
    n_blocks = a // tm + N_EXPERTS
    n_rows = n_blocks * tm
    tok = jnp.arange(n, dtype=jnp.int32)
    row_tok = jnp.zeros((n_rows,), jnp.int32).at[dest1].set(tok).at[dest2].set(tok)
    block_start = jnp.arange(n_blocks, dtype=jnp.int32) * tm
    block_exp = jnp.minimum(jnp.searchsorted(pad_ends, block_start, side="right"), N_EXPERTS - 1).astype(jnp.int32)
    n_used = (pad_ends[-1] // tm).astype(jnp.int32).reshape(1)

    xs = _sc_gather_rows(hn, row_tok)
    ys = _moe_experts(block_exp, n_used, xs, wg, wu, wd, tm)
    yg = _sc_gather_rows(ys, jnp.concatenate([dest1, dest2]))
    return _combine(yg, x2, route)


def _block_diag(w):
    h, dh, _ = w.shape
    eye = jnp.eye(h, dtype=w.dtype)
    return jnp.einsum("hij,hg->higj", w, eye).reshape(h * dh, h * dh)


def kernel(x, norm1_g, w_in, q_norm_g, k_norm_g, conf_dw_w, conf_dw_b, conf_ln_g, conf_ln_b, gmlp_ln_g, gmlp_ln_b, gmlp_ws, gmlp_bs, lru_conv_w, lru_conv_b, lru_wa, lru_ba, lru_wx, lru_bx, lru_lambda, group_norm_g, w_out, norm2_g, ffn_w_gate, ffn_w_up, ffn_w_down, moe_router, moe_w_gate, moe_w_up, moe_w_down):
    bsz, s, d = x.shape
    n = bsz * s
    depth = w_in.shape[0]
    W = GROUP_WIDTH
    row = lambda v: v.reshape(1, -1).astype(F32)

    head_id = jnp.arange(W) // HEAD_DIM
    bd = (head_id[:, None] == head_id[None, :]).astype(BF16)
    bd2 = jnp.concatenate([bd, bd], axis=0)
    t_att = min(ATT_T, s)
    u_incl = (jnp.arange(t_att)[:, None] >= jnp.arange(t_att)[None, :]).astype(BF16)
    uu = jnp.concatenate([u_incl, u_incl], axis=0)
    tril = jnp.tril(jnp.ones((GMLP_CHUNK, GMLP_CHUNK), dtype=bool))

    x2 = x.reshape(n, d)
    for l in range(depth):
        qg = row(jnp.tile(q_norm_g[l], GROUP_HEADS) * (HEAD_DIM ** -0.5))
        kg = row(jnp.tile(k_norm_g[l], GROUP_HEADS))
        q, k, v, rest = _in_proj(x2, row(norm1_g[l]), w_in[l].astype(BF16), qg, kg, bd2)
        gn = group_norm_g[l].reshape(N_GROUPS, 1, W)
        to3 = lambda t: t.reshape(bsz, s, t.shape[-1])
        rest3 = to3(rest)
        y_a = _sb_attention(to3(q), to3(k), to3(v), uu, gn[0])
        cw = jnp.zeros((CONF_HALO, W), F32).at[:CONF_KERNEL].set(conf_dw_w[l])
        y_b = _conformer(rest3, 0, cw, row(conf_dw_b[l]), row(conf_ln_g[l]), row(conf_ln_b[l]), gn[1])
        ws = jnp.where(tril, gmlp_ws[l], 0.0).astype(BF16)
        bs_mat = jnp.repeat(gmlp_bs[l].T, HEAD_DIM, axis=1)
        y_c = _gmlp(rest3, 2, row(gmlp_ln_g[l]), row(gmlp_ln_b[l]), ws, bs_mat, gn[2])
        lw = jnp.zeros((8, W), F32).at[:LRU_CONV].set(lru_conv_w[l])
        wax = jnp.concatenate([_block_diag(lru_wa[l]), _block_diag(lru_wx[l])], axis=1).astype(BF16)
        y_d = _rg_lru(rest3, 4, lw, row(lru_conv_b[l]), wax, row(lru_ba[l]), row(lru_bx[l]),
                      row(lru_lambda[l]), gn[3])
        ys = [t.reshape(n, W) for t in (y_a, y_b, y_c, y_d)]
        x2, hn = _out_proj(ys, w_out[l].astype(BF16), x2, row(norm2_g[l]), BF16 if l % 2 == 0 else F32)
        j = l // 2
        if l % 2 == 0:
            x2 = _dense_ffn(hn, x2, ffn_w_gate[j].astype(BF16), ffn_w_up[j].astype(BF16),
                            ffn_w_down[j].astype(BF16))
        else:
            x2 = _top2_moe(hn, x2, row(norm2_g[l]), moe_router[j], moe_w_gate[j].astype(BF16),
                           moe_w_up[j].astype(BF16), moe_w_down[j].astype(BF16))
    return x2.reshape(bsz, s, d)
```

```python
import functools

import jax
import jax.numpy as jnp
from jax import lax
from jax.experimental import pallas as pl
from jax.experimental.pallas import tpu as pltpu
from jax.experimental.pallas import tpu_sc as plsc

F32 = jnp.float32
BF16 = jnp.bfloat16

HEAD_DIM = 64
GROUP_HEADS = 4
GROUP_WIDTH = GROUP_HEADS * HEAD_DIM
N_GROUPS = 4
CONF_KERNEL = 31
GMLP_CHUNK = 128
LRU_CONV = 4
LRU_C = 8.0
N_EXPERTS = 8
EPS = 1e-6

LANES = 128
VMEM_LIMIT = 56 * 1024 * 1024

IN_TM = 512
ATT_T = 256
CONF_T = 256
CONF_R = 64
CONF_HALO = 32
GMLP_T = 512
LRU_T = 256
LRU_HALO = 8
OUT_TM = 512
FFN_TM = 1024
FFN_TF = 1408
ROUTE_TM = 512
MOE_TM = 1024
MOE_TF = 512
COMB_TM = 512
SC_GATHER_ROWS = 32


def _params(*sem):
    return pltpu.CompilerParams(dimension_semantics=sem, vmem_limit_bytes=VMEM_LIMIT)


def _split_hi_lo(x):
    hi = x.astype(BF16)
    lo = (x - hi.astype(F32)).astype(BF16)
    return hi, lo


SOFTPLUS_CLAMP = 60.0


def _softplus(z):
    return jnp.maximum(jnp.log(1.0 + jnp.exp(jnp.minimum(z, SOFTPLUS_CLAMP))), z)


def _gelu_tanh(x):
    c = 0.7978845608028654
    return 0.5 * x * (1.0 + jnp.tanh(c * (x + 0.044715 * (x * x * x))))


def _group_rms(y, g):
    ms = jnp.mean(y * y, axis=-1, keepdims=True)
    return y * lax.rsqrt(ms + EPS) * g


def _in_proj_kernel(x_ref, g_ref, w_ref, qg_ref, kg_ref, bd_ref, q_ref, k_ref, v_ref, rest_ref):
    xf = x_ref[...]
    ms = jnp.mean(xf * xf, axis=-1, keepdims=True)
    h = (xf * lax.rsqrt(ms + EPS) * g_ref[...]).astype(BF16)
    W = GROUP_WIDTH

    def proj(c):
        return jnp.dot(h, w_ref[:, c * W:(c + 1) * W], preferred_element_type=F32)

    def head_norm(p, gain):
        hi, lo = _split_hi_lo(p * p)
        ss = jnp.dot(jnp.concatenate([hi, lo], axis=1), bd_ref[...], preferred_element_type=F32)
        return p * lax.rsqrt(ss * (1.0 / HEAD_DIM) + EPS) * gain

    q_ref[...] = head_norm(proj(0), qg_ref[...]).astype(BF16)
    k_ref[...] = head_norm(proj(1), kg_ref[...]).astype(BF16)
    v_ref[...] = proj(2).astype(BF16)
    n_rest = rest_ref.shape[1] // W
    for c in range(n_rest):
        rest_ref[:, c * W:(c + 1) * W] = proj(3 + c)


def _in_proj(x2, g, w_bf, qg, kg, bd2):
    n, d = x2.shape
    cols = w_bf.shape[1]
    W = GROUP_WIDTH
    rest_cols = cols - 3 * W
    tm = min(IN_TM, n)
    row = lambda i: (i, 0)
    const = lambda i: (0, 0)
    return pl.pallas_call(
        _in_proj_kernel,
        grid=(n // tm,),
        in_specs=[
            pl.BlockSpec((tm, d), row),
            pl.BlockSpec((1, d), const),
            pl.BlockSpec((d, cols), const),
            pl.BlockSpec((1, W), const),
            pl.BlockSpec((1, W), const),
            pl.BlockSpec((2 * W, W), const),
        ],
        out_specs=[
            pl.BlockSpec((tm, W), row),
            pl.BlockSpec((tm, W), row),
            pl.BlockSpec((tm, W), row),
            pl.BlockSpec((tm, rest_cols), row),
        ],
        out_shape=[
            jax.ShapeDtypeStruct((n, W), BF16),
            jax.ShapeDtypeStruct((n, W), BF16),
            jax.ShapeDtypeStruct((n, W), BF16),
            jax.ShapeDtypeStruct((n, rest_cols), F32),
        ],
        compiler_params=_params("arbitrary"),
        name="in_proj",
    )(x2, g, w_bf, qg, kg, bd2)


def _sb_attn_kernel(q_ref, k_ref, v_ref, uu_ref, gn_ref, o_ref):
    T = q_ref.shape[1]
    n_pairs = GROUP_WIDTH // LANES
    i = pl.program_id(1)
    lane = lax.broadcasted_iota(jnp.int32, (T, LANES), 1)
    rows = lax.broadcasted_iota(jnp.int32, (2 * T, T), 0)
    cols = lax.broadcasted_iota(jnp.int32, (2 * T, T), 1)
    causal = cols < jnp.where(rows >= T, rows - T, rows)
    uu = uu_ref[...]

    def tile(qs, k2, v2, carry, mask):
        z = lax.dot_general(qs, k2, (((1,), (1,)), ((), ())), preferred_element_type=F32)
        sp = _softplus(z)
        if mask is not None:
            sp = jnp.where(mask, sp, 0.0)
        cum = jnp.dot(sp.astype(BF16), uu, preferred_element_type=F32)
        w = jnp.exp(z - cum - carry)
        if mask is not None:
            w = jnp.where(mask, w, 0.0)
        pv = jnp.dot(w.astype(BF16), v2, preferred_element_type=F32)
        return pv, carry + jnp.sum(sp, axis=1, keepdims=True)

    def kv(r0, pair):
        cs = slice(pair * LANES, (pair + 1) * LANES)
        return k_ref[0, pl.ds(r0, T), cs], v_ref[0, pl.ds(r0, T), cs]

    qs = []
    for pair in range(n_pairs):
        q2 = q_ref[0, :, pair * LANES:(pair + 1) * LANES]
        zero = jnp.zeros_like(q2)
        qs.append(jnp.concatenate([jnp.where(lane < HEAD_DIM, q2, zero),
                                   jnp.where(lane >= HEAD_DIM, q2, zero)], axis=0))
    row0 = pl.multiple_of(i * T, T)
    state = tuple(tile(qs[p], *kv(row0, p), jnp.zeros((2 * T, 1), F32), causal) for p in range(n_pairs))

    def body(step, st):
        r0 = pl.multiple_of((i - 1 - step) * T, T)
        new = []
        for p in range(n_pairs):
            acc, carry = st[p]
            pv, carry = tile(qs[p], *kv(r0, p), carry, None)
            new.append((acc + pv, carry))
        return tuple(new)

    state = lax.fori_loop(0, i, body, state)
    y = jnp.concatenate([jnp.where(lane < HEAD_DIM, acc[:T], acc[T:]) for acc, _ in state], axis=1)
    o_ref[0] = _group_rms(y, gn_ref[...]).astype(o_ref.dtype)


def _sb_attention(q3, k3, v3, uu, gn):
    b, s, w = q3.shape
    t = min(ATT_T, s)
    return pl.pallas_call(
        _sb_attn_kernel,
        grid=(b, s // t),
        in_specs=[
            pl.BlockSpec((1, t, w), lambda bi, i: (bi, i, 0)),
            pl.BlockSpec((1, s, w), lambda bi, i: (bi, 0, 0)),
            pl.BlockSpec((1, s, w), lambda bi, i: (bi, 0, 0)),
            pl.BlockSpec((t, t), lambda bi, i: (0, 0)),
            pl.BlockSpec((1, w), lambda bi, i: (0, 0)),
        ],
        out_specs=pl.BlockSpec((1, t, w), lambda bi, i: (bi, i, 0)),
        out_shape=jax.ShapeDtypeStruct((b, s, w), BF16),
        compiler_params=_params("arbitrary", "arbitrary"),
        name="sb_attn",
    )(q3, k3, v3, uu, gn)


def _conf_kernel(val_ref, gate_ref, w_ref, b_ref, lng_ref, lnb_ref, gn_ref, o_ref, hs_ref):
    T = val_ref.shape[1]
    H = CONF_HALO

    @pl.when(pl.program_id(1) == 0)
    def _():
        hs_ref[0:H, :] = jnp.zeros((H, GROUP_WIDTH), F32)

    hs_ref[H:H + T, :] = val_ref[0] * jax.nn.sigmoid(gate_ref[0])
    off = H - (CONF_KERNEL - 1)
    R = min(CONF_R, T)
    for r0 in range(0, T, R):
        acc = jnp.broadcast_to(b_ref[...], (R, GROUP_WIDTH))
        for k in range(CONF_KERNEL):
            acc = acc + w_ref[k:k + 1, :] * hs_ref[off + k + r0:off + k + r0 + R, :]
        mu = jnp.mean(acc, axis=-1, keepdims=True)
        xc = acc - mu
        var = jnp.mean(xc * xc, axis=-1, keepdims=True)
        y = xc * lax.rsqrt(var + EPS) * lng_ref[...] + lnb_ref[...]
        y = y * jax.nn.sigmoid(y)
        o_ref[0, r0:r0 + R, :] = _group_rms(y, gn_ref[...]).astype(o_ref.dtype)
    hs_ref[0:H, :] = hs_ref[T:T + H, :]


def _conformer(rest3, col0, w_pad, b, lng, lnb, gn):
    bsz, s, _ = rest3.shape
    W = GROUP_WIDTH
    t = min(CONF_T, s)
    vec = pl.BlockSpec((1, W), lambda bi, i: (0, 0))
    return pl.pallas_call(
        _conf_kernel,
        grid=(bsz, s // t),
        in_specs=[
            pl.BlockSpec((1, t, W), lambda bi, i: (bi, i, col0)),
            pl.BlockSpec((1, t, W), lambda bi, i: (bi, i, col0 + 1)),
            pl.BlockSpec(w_pad.shape, lambda bi, i: (0, 0)),
            vec, vec, vec, vec,
        ],
        out_specs=pl.BlockSpec((1, t, W), lambda bi, i: (bi, i, 0)),
        out_shape=jax.ShapeDtypeStruct((bsz, s, W), BF16),
        scratch_shapes=[pltpu.VMEM((t + CONF_HALO, W), F32)],
        compiler_params=_params("arbitrary", "arbitrary"),
        name="conformer",
    )(rest3, rest3, w_pad, b, lng, lnb, gn)


def _gmlp_kernel(u_ref, v_ref, lng_ref, lnb_ref, ws_ref, bs_ref, gn_ref, o_ref):
    T = u_ref.shape[1]
    C = GMLP_CHUNK
    lane = lax.broadcasted_iota(jnp.int32, (C, GROUP_WIDTH), 1)
    for c0 in range(0, T, C):
        u = _gelu_tanh(u_ref[0, c0:c0 + C, :])
        v = _gelu_tanh(v_ref[0, c0:c0 + C, :])
        mu = jnp.mean(v, axis=-1, keepdims=True)
        xc = v - mu
        var = jnp.mean(xc * xc, axis=-1, keepdims=True)
        vb = (xc * lax.rsqrt(var + EPS) * lng_ref[...] + lnb_ref[...]).astype(BF16)
        mixed = bs_ref[...]
        for h in range(GROUP_HEADS):
            m = jnp.dot(ws_ref[h], vb, preferred_element_type=F32)
            in_head = (lane >= h * HEAD_DIM) & (lane < (h + 1) * HEAD_DIM)
            mixed = mixed + jnp.where(in_head, m, 0.0)
        o_ref[0, c0:c0 + C, :] = _group_rms(u * mixed, gn_ref[...]).astype(o_ref.dtype)


def _gmlp(rest3, col0, lng, lnb, ws_bf, bs_mat, gn):
    bsz, s, _ = rest3.shape
    W = GROUP_WIDTH
    t = min(GMLP_T, s)
    vec = pl.BlockSpec((1, W), lambda bi, i: (0, 0))
    return pl.pallas_call(
        _gmlp_kernel,
        grid=(bsz, s // t),
        in_specs=[
            pl.BlockSpec((1, t, W), lambda bi, i: (bi, i, col0)),
            pl.BlockSpec((1, t, W), lambda bi, i: (bi, i, col0 + 1)),
            vec, vec,
            pl.BlockSpec(ws_bf.shape, lambda bi, i: (0, 0, 0)),
            pl.BlockSpec(bs_mat.shape, lambda bi, i: (0, 0)),
            vec,
        ],
        out_specs=pl.BlockSpec((1, t, W), lambda bi, i: (bi, i, 0)),
        out_shape=jax.ShapeDtypeStruct((bsz, s, W), BF16),
        compiler_params=_params("arbitrary", "arbitrary"),
        name="gmlp",
    )(rest3, rest3, lng, lnb, ws_bf, bs_mat, gn)


def _lru_kernel(x_ref, gate_ref, cw_ref, cb_ref, wax_ref, ba_ref, bx_ref, lam_ref, gn_ref, o_ref,
                xs_ref, hprev_ref):
    T = x_ref.shape[1]
    W = GROUP_WIDTH
    H = LRU_HALO

    @pl.when(pl.program_id(1) == 0)
    def _():
        xs_ref[0:H, :] = jnp.zeros((H, W), F32)
        hprev_ref[...] = jnp.zeros(hprev_ref.shape, F32)

    xs_ref[H:H + T, :] = x_ref[0]
    off = H - (LRU_CONV - 1)
    xb = jnp.broadcast_to(cb_ref[...], (T, W))
    for k in range(LRU_CONV):
        xb = xb + cw_ref[k:k + 1, :] * xs_ref[off + k:off + k + T, :]
    xs_ref[0:H, :] = xs_ref[T:T + H, :]

    pre = jnp.dot(xb.astype(BF16), wax_ref[...], preferred_element_type=F32)
    r = jax.nn.sigmoid(pre[:, :W] + ba_ref[...])
    ig = jax.nn.sigmoid(pre[:, W:] + bx_ref[...])
    nlam = -lam_ref[...]
    log_a = (-LRU_C) * r * (jnp.maximum(nlam, 0.0) + jnp.log1p(jnp.exp(-jnp.abs(nlam))))
    a = jnp.exp(log_a)
    th = jnp.tanh(log_a)
    one_minus_a2 = 2.0 * th / (th - 1.0)
    b = jnp.sqrt(one_minus_a2) * (ig * xb)

    row = lax.broadcasted_iota(jnp.int32, (T, W), 0)
    d = 1
    while d < T:
        keep = row >= d
        a_sh = jnp.where(keep, pltpu.roll(a, d, 0), 1.0)
        b_sh = jnp.where(keep, pltpu.roll(b, d, 0), 0.0)
        b = a * b_sh + b
        a = a * a_sh
        d *= 2
    h = b + a * hprev_ref[0:1, :]
    hprev_ref[...] = jnp.broadcast_to(h[T - 1:T, :], hprev_ref.shape)
    y = h * _gelu_tanh(gate_ref[0])
    o_ref[0] = _group_rms(y, gn_ref[...]).astype(o_ref.dtype)


def _rg_lru(rest3, col0, cw_pad, cb, wax_bf, ba, bx, lam, gn):
    bsz, s, _ = rest3.shape
    W = GROUP_WIDTH
    t = min(LRU_T, s)
    vec = pl.BlockSpec((1, W), lambda bi, i: (0, 0))
    return pl.pallas_call(
        _lru_kernel,
        grid=(bsz, s // t),
        in_specs=[
            pl.BlockSpec((1, t, W), lambda bi, i: (bi, i, col0)),
            pl.BlockSpec((1, t, W), lambda bi, i: (bi, i, col0 + 1)),
            pl.BlockSpec(cw_pad.shape, lambda bi, i: (0, 0)),
            vec,
            pl.BlockSpec(wax_bf.shape, lambda bi, i: (0, 0)),
            vec, vec, vec, vec,
        ],
        out_specs=pl.BlockSpec((1, t, W), lambda bi, i: (bi, i, 0)),
        out_shape=jax.ShapeDtypeStruct((bsz, s, W), BF16),
        scratch_shapes=[pltpu.VMEM((t + LRU_HALO, W), F32), pltpu.VMEM((8, W), F32)],
        compiler_params=_params("arbitrary", "arbitrary"),
        name="rg_lru",
    )(rest3, rest3, cw_pad, cb, wax_bf, ba, bx, lam, gn)


def _out_proj_kernel(ya_ref, yb_ref, yc_ref, yd_ref, w_ref, x_ref, g_ref, xo_ref, h_ref):
    W = GROUP_WIDTH
    acc = x_ref[...]
    for gi, y_ref in enumerate((ya_ref, yb_ref, yc_ref, yd_ref)):
        acc = acc + jnp.dot(y_ref[...], w_ref[gi * W:(gi + 1) * W, :], preferred_element_type=F32)
    xo_ref[...] = acc
    ms = jnp.mean(acc * acc, axis=-1, keepdims=True)
    h_ref[...] = (acc * lax.rsqrt(ms + EPS) * g_ref[...]).astype(h_ref.dtype)


def _out_proj(ys, w_bf, x2, g, h_dtype):
    n, d = x2.shape
    W = GROUP_WIDTH
    tm = min(OUT_TM, n)
    row = lambda i: (i, 0)
    const = lambda i: (0, 0)
    ysp = pl.BlockSpec((tm, W), row)
    return pl.pallas_call(
        _out_proj_kernel,
        grid=(n // tm,),
        in_specs=[ysp, ysp, ysp, ysp,
                  pl.BlockSpec(w_bf.shape, const),
                  pl.BlockSpec((tm, d), row),
                  pl.BlockSpec((1, d), const)],
        out_specs=[pl.BlockSpec((tm, d), row), pl.BlockSpec((tm, d), row)],
        out_shape=[jax.ShapeDtypeStruct((n, d), F32), jax.ShapeDtypeStruct((n, d), h_dtype)],
        compiler_params=_params("arbitrary"),
        name="out_proj",
    )(*ys, w_bf, x2, g)


def _ffn_kernel(h_ref, x_ref, wg_ref, wu_ref, wd_ref, o_ref):
    @pl.when(pl.program_id(1) == 0)
    def _():
        o_ref[...] = x_ref[...]

    h = h_ref[...]
    g = jnp.dot(h, wg_ref[...], preferred_element_type=F32)
    u = jnp.dot(h, wu_ref[...], preferred_element_type=F32)
    act = (g * jax.nn.sigmoid(g) * u).astype(BF16)
    o_ref[...] += jnp.dot(act, wd_ref[...], preferred_element_type=F32)


def _dense_ffn(hn, x2, wg, wu, wd):
    n, d = x2.shape
    f = wg.shape[1]
    tm = min(FFN_TM, n)
    tf = FFN_TF if f % FFN_TF == 0 else f
    return pl.pallas_call(
        _ffn_kernel,
        grid=(n // tm, f // tf),
        in_specs=[
            pl.BlockSpec((tm, d), lambda i, j: (i, 0)),
            pl.BlockSpec((tm, d), lambda i, j: (i, 0)),
            pl.BlockSpec((d, tf), lambda i, j: (0, j)),
            pl.BlockSpec((d, tf), lambda i, j: (0, j)),
            pl.BlockSpec((tf, d), lambda i, j: (j, 0)),
        ],
        out_specs=pl.BlockSpec((tm, d), lambda i, j: (i, 0)),
        out_shape=jax.ShapeDtypeStruct((n, d), F32),
        compiler_params=_params("arbitrary", "arbitrary"),
        name="dense_ffn",
    )(hn, x2, wg, wu, wd)


def _router_kernel(x_ref, g_ref, rh_ref, rl_ref, tri_ref, route_ref, cnt_ref, base_ref):
    tm = x_ref.shape[0]

    @pl.when(pl.program_id(0) == 0)
    def _():
        base_ref[...] = jnp.zeros(base_ref.shape, F32)

    xf = x_ref[...]
    ms = jnp.mean(xf * xf, axis=-1, keepdims=True)
    hh, hl = _split_hi_lo(xf * lax.rsqrt(ms + EPS) * g_ref[...])
    rh = rh_ref[...]
    logits = (jnp.dot(hh, rh, preferred_element_type=F32)
              + jnp.dot(hl, rh, preferred_element_type=F32)
              + jnp.dot(hh, rl_ref[...], preferred_element_type=F32))
    lane = lax.broadcasted_iota(jnp.int32, (tm, LANES), 1).astype(F32)
    neg = jnp.float32(-jnp.inf)
    logits = jnp.where(lane < N_EXPERTS, logits, neg)
    m1 = jnp.max(logits, axis=1, keepdims=True)
    i1 = jnp.min(jnp.where(logits == m1, lane, float(LANES)), axis=1, keepdims=True)
    l2 = jnp.where(lane == i1, neg, logits)
    m2 = jnp.max(l2, axis=1, keepdims=True)
    i2 = jnp.min(jnp.where(l2 == m2, lane, float(LANES)), axis=1, keepdims=True)
    e = jnp.exp(m2 - m1)
    g1 = 1.0 / (1.0 + e)
    g2 = e / (1.0 + e)
    oh1 = jnp.where(lane == i1, 1.0, 0.0)
    oh2 = jnp.where(lane == i2, 1.0, 0.0)
    oh = oh1 + oh2
    before = jnp.dot(tri_ref[...], oh.astype(BF16), preferred_element_type=F32) + base_ref[0:1, :]
    r1 = jnp.sum(oh1 * before, axis=1, keepdims=True)
    r2 = jnp.sum(oh2 * before, axis=1, keepdims=True)
    base = base_ref[0:1, :] + jnp.sum(oh, axis=0, keepdims=True)
    base_ref[...] = jnp.broadcast_to(base, base_ref.shape)
    cnt_ref[...] = jnp.broadcast_to(base, cnt_ref.shape)
    out = jnp.where(lane == 0, i1, 0.0)
    out = jnp.where(lane == 1, i2, out)
    out = jnp.where(lane == 2, g1, out)
    out = jnp.where(lane == 3, g2, out)
    out = jnp.where(lane == 4, r1, out)
    out = jnp.where(lane == 5, r2, out)
    route_ref[...] = out


def _router(x2, g, rh, rl, tri):
    n, d = x2.shape
    tm = min(ROUTE_TM, n)
    return pl.pallas_call(
        _router_kernel,
        grid=(n // tm,),
        in_specs=[
            pl.BlockSpec((tm, d), lambda i: (i, 0)),
            pl.BlockSpec((1, d), lambda i: (0, 0)),
            pl.BlockSpec((d, LANES), lambda i: (0, 0)),
            pl.BlockSpec((d, LANES), lambda i: (0, 0)),
            pl.BlockSpec((tm, tm), lambda i: (0, 0)),
        ],
        out_specs=[pl.BlockSpec((tm, LANES), lambda i: (i, 0)),
                   pl.BlockSpec((8, LANES), lambda i: (0, 0))],
        out_shape=[jax.ShapeDtypeStruct((n, LANES), F32), jax.ShapeDtypeStruct((8, LANES), F32)],
        scratch_shapes=[pltpu.VMEM((8, LANES), F32)],
        compiler_params=_params("arbitrary"),
        name="router",
    )(x2, g, rh, rl, tri)


def _sc_gather_rows(table, idx):
    _, d = table.shape
    b = idx.shape[0]
    win = SC_GATHER_ROWS
    sc = plsc.get_sparse_core_info()
    n_workers = sc.num_cores * sc.num_subcores
    per_w = b // n_workers
    n_win = per_w // win
    assert per_w * n_workers == b and n_win * win == per_w and n_win % 2 == 0, (b, n_workers, win)
    mesh = plsc.VectorSubcoreMesh(core_axis_name="c", subcore_axis_name="s")
    dma = pltpu.SemaphoreType.DMA

    @functools.partial(
        pl.kernel, mesh=mesh, out_type=jax.ShapeDtypeStruct((b, d), table.dtype), name="sc_gather_rows",
        scratch_types=[pltpu.VMEM((win,), jnp.int32), pltpu.VMEM((win,), jnp.int32),
                       pltpu.VMEM((win, d), table.dtype), pltpu.VMEM((win, d), table.dtype),
                       dma, dma, dma, dma])
    def gather_kernel(table_hbm, idx_hbm, out_hbm, i0, i1, r0, r1, g0, g1, w0, w1):
        wid = lax.axis_index("s") * sc.num_cores + lax.axis_index("c")
        base = wid * per_w
        idxb, rows, gsem, wsem = (i0, i1), (r0, r1), (g0, g1), (w0, w1)

        def off(c):
            return pl.multiple_of(base + c * win, win)

        def gather(s):
            return pltpu.make_async_copy(table_hbm.at[idxb[s]], rows[s], gsem[s])

        def write(c, s):
            return pltpu.make_async_copy(rows[s], out_hbm.at[pl.ds(off(c), win)], wsem[s])

        pltpu.sync_copy(idx_hbm.at[pl.ds(off(0), win)], idxb[0])
        gather(0).start()

        @pl.loop(0, n_win, step=2)
        def _(c):
            for s in (0, 1):
                cc = c + s

                @pl.when(cc + 1 < n_win)
                def _():
                    @pl.when(cc >= 1)
                    def _():
                        write(cc - 1, 1 - s).wait()

                    pltpu.sync_copy(idx_hbm.at[pl.ds(off(cc + 1), win)], idxb[1 - s])
                    gather(1 - s).start()

                gather(s).wait()
                write(cc, s).start()

        write(n_win - 2, 0).wait()
        write(n_win - 1, 1).wait()

    return gather_kernel(table, idx)


def _sc_dispatch_rows(src, dest1, dest2, n_rows):
    n, d = src.shape
    win = SC_GATHER_ROWS
    sc = plsc.get_sparse_core_info()
    n_workers = sc.num_cores * sc.num_subcores
    per_w = n // n_workers
    n_win = per_w // win
    assert per_w * n_workers == n and n_win * win == per_w and n_win % 2 == 0, (n, n_workers, win)
    mesh = plsc.VectorSubcoreMesh(core_axis_name="c", subcore_axis_name="s")
    dma = pltpu.SemaphoreType.DMA
    ivec = pltpu.VMEM((win,), jnp.int32)
    rbuf = pltpu.VMEM((win, d), src.dtype)

    @functools.partial(
        pl.kernel, mesh=mesh, out_type=jax.ShapeDtypeStruct((n_rows, d), src.dtype), name="sc_dispatch_rows",
        scratch_types=[ivec, ivec, ivec, ivec, rbuf, rbuf, dma, dma, dma, dma, dma, dma])
    def dispatch_kernel(src_hbm, d1_hbm, d2_hbm, out_hbm, a0, a1, b0, b1, r0, r1, l0, l1, p0, p1, q0, q1):
        wid = lax.axis_index("s") * sc.num_cores + lax.axis_index("c")
        base = wid * per_w
        ia, ib, rows, lsem, psem, qsem = (a0, a1), (b0, b1), (r0, r1), (l0, l1), (p0, p1), (q0, q1)

        def off(c):
            return pl.multiple_of(base + c * win, win)

        def load(c, s):
            return pltpu.make_async_copy(src_hbm.at[pl.ds(off(c), win)], rows[s], lsem[s])

        def scatters(s):
            return (pltpu.make_async_copy(rows[s], out_hbm.at[ia[s]], psem[s]),
                    pltpu.make_async_copy(rows[s], out_hbm.at[ib[s]], qsem[s]))

        def fetch(c, s):
            pltpu.sync_copy(d1_hbm.at[pl.ds(off(c), win)], ia[s])
            pltpu.sync_copy(d2_hbm.at[pl.ds(off(c), win)], ib[s])
            load(c, s).start()

        fetch(0, 0)

        @pl.loop(0, n_win, step=2)
        def _(c):
            for s in (0, 1):
                cc = c + s

                @pl.when(cc + 1 < n_win)
                def _():
                    @pl.when(cc >= 1)
                    def _():
                        for cp in scatters(1 - s):
                            cp.wait()

                    fetch(cc + 1, 1 - s)

                load(cc, s).wait()
                for cp in scatters(s):
                    cp.start()

        for s in (0, 1):
            for cp in scatters(s):
                cp.wait()

    return dispatch_kernel(src, dest1, dest2)


def _moe_kernel(bexp_ref, nused_ref, nvalid_ref, x_ref, wg_ref, wu_ref, wd_ref, o_ref, xb):
    i = pl.program_id(0)
    j = pl.program_id(1)
    used = i < nused_ref[0]

    @pl.when(used & (j == 0))
    def _():
        row = lax.broadcasted_iota(jnp.int32, x_ref.shape, 0)
        xb[...] = jnp.where(row < nvalid_ref[i], x_ref[...], 0.0).astype(BF16)

    @pl.when(used)
    def _():
        x = xb[...]
        g = jnp.dot(x, wg_ref[0].astype(BF16), preferred_element_type=F32)
        u = jnp.dot(x, wu_ref[0].astype(BF16), preferred_element_type=F32)
        act = (g * jax.nn.sigmoid(g) * u).astype(BF16)
        y = jnp.dot(act, wd_ref[0].astype(BF16), preferred_element_type=F32)

        @pl.when(j == 0)
        def _():
            o_ref[...] = y

        @pl.when(j > 0)
        def _():
            o_ref[...] += y

    @pl.when(jnp.logical_not(used) & (j == 0))
    def _():
        o_ref[...] = jnp.zeros(o_ref.shape, o_ref.dtype)


def _moe_experts(block_exp, n_used, n_valid, xs, wg, wu, wd, tm):
    n_rows, d = xs.shape
    n_blocks = n_rows // tm
    f = wg.shape[2]
    tf = MOE_TF if f % MOE_TF == 0 else f
    nj = f // tf

    def live(i, j, be, nu):
        u = i < nu[0]
        return jnp.where(u, i, nu[0] - 1), jnp.where(u, j, nj - 1)

    def w_in_map(i, j, be, nu, nv):
        ii, jj = live(i, j, be, nu)
        return (be[ii], 0, jj)

    def w_down_map(i, j, be, nu, nv):
        ii, jj = live(i, j, be, nu)
        return (be[ii], jj, 0)

    def x_map(i, j, be, nu, nv):
        return (live(i, j, be, nu)[0], 0)

    def out_map(i, j, be, nu, nv):
        return (i, 0)

    grid_spec = pltpu.PrefetchScalarGridSpec(
        num_scalar_prefetch=3,
        grid=(n_blocks, nj),
        in_specs=[
            pl.BlockSpec((tm, d), x_map),
            pl.BlockSpec((1, d, tf), w_in_map),
            pl.BlockSpec((1, d, tf), w_in_map),
            pl.BlockSpec((1, tf, d), w_down_map),
        ],
        out_specs=pl.BlockSpec((tm, d), out_map),
        scratch_shapes=[pltpu.VMEM((tm, d), BF16)],
    )
    return pl.pallas_call(
        _moe_kernel,
        grid_spec=grid_spec,
        out_shape=jax.ShapeDtypeStruct((n_rows, d), F32),
        compiler_params=_params("arbitrary", "arbitrary"),
        name="moe_experts",
    )(block_exp, n_used, n_valid, xs, wg, wu, wd)


def _combine_kernel(x_ref, route_ref, a_ref, b_ref, o_ref):
    route = route_ref[...]
    o_ref[...] = x_ref[...] + route[:, 2:3] * a_ref[...] + route[:, 3:4] * b_ref[...]


def _combine(yg, x2, route):
    n, d = x2.shape
    tm = min(COMB_TM, n)
    nb = n // tm
    return pl.pallas_call(
        _combine_kernel,
        grid=(nb,),
        in_specs=[pl.BlockSpec((tm, d), lambda i: (i, 0)),
                  pl.BlockSpec((tm, LANES), lambda i: (i, 0)),
                  pl.BlockSpec((tm, d), lambda i: (i, 0)),
                  pl.BlockSpec((tm, d), lambda i: (i + nb, 0))],
        out_specs=pl.BlockSpec((tm, d), lambda i: (i, 0)),
        out_shape=jax.ShapeDtypeStruct((n, d), F32),
        compiler_params=_params("arbitrary"),
        name="moe_combine",
    )(x2, route, yg, yg)


def _top2_moe(hn, x2, g2, router, wg, wu, wd):
    n, d = x2.shape
    a = 2 * n
    tm = min(MOE_TM, n)
    r_pad = jnp.zeros((d, LANES), F32).at[:, :N_EXPERTS].set(router)
    rh, rl = _split_hi_lo(r_pad)
    rt = min(ROUTE_TM, n)
    tri = (lax.broadcasted_iota(jnp.int32, (rt, rt), 1) < lax.broadcasted_iota(jnp.int32, (rt, rt), 0)).astype(BF16)
    route, cnt = _router(x2, g2, rh, rl, tri)

    counts = cnt[0, :N_EXPERTS].astype(jnp.int32)
    padded = (counts + tm - 1) // tm * tm
    pad_ends = jnp.cumsum(padded)
    pad_starts = pad_ends - padded
    e1 = route[:, 0].astype(jnp.int32)
    e2 = route[:, 1].astype(jnp.int32)
    dest1 = pad_starts[e1] + route[:, 4].astype(jnp.int32)
    dest2 = pad_starts[e2] + route[:, 5].astype(jnp.int32)
    n_blocks = a // tm + N_EXPERTS
    n_rows = n_blocks * tm
    block_start = jnp.arange(n_blocks, dtype=jnp.int32) * tm
    block_exp = jnp.minimum(jnp.searchsorted(pad_ends, block_start, side="right"), N_EXPERTS - 1).astype(jnp.int32)
    n_used = (pad_ends[-1] // tm).astype(jnp.int32).reshape(1)
    n_valid = jnp.clip(counts[block_exp] - (block_start - pad_starts[block_exp]), 0, tm).astype(jnp.int32)

    xs = _sc_dispatch_rows(hn, dest1, dest2, n_rows)
    ys = _moe_experts(block_exp, n_used, n_valid, xs, wg, wu, wd, tm)
    yg = _sc_gather_rows(ys, jnp.concatenate([dest1, dest2]))
    return _combine(yg, x2, route)


def _block_diag(w):
    h, dh, _ = w.shape
    eye = jnp.eye(h, dtype=w.dtype)
    return jnp.einsum("hij,hg->higj", w, eye).reshape(h * dh, h * dh)


def kernel(x, norm1_g, w_in, q_norm_g, k_norm_g, conf_dw_w, conf_dw_b, conf_ln_g, conf_ln_b, gmlp_ln_g, gmlp_ln_b, gmlp_ws, gmlp_bs, lru_conv_w, lru_conv_b, lru_wa, lru_ba, lru_wx, lru_bx, lru_lambda, group_norm_g, w_out, norm2_g, ffn_w_gate, ffn_w_up, ffn_w_down, moe_router, moe_w_gate, moe_w_up, moe_w_down):
    bsz, s, d = x.shape
    n = bsz * s
    depth = w_in.shape[0]
    W = GROUP_WIDTH
    row = lambda v: v.reshape(1, -1).astype(F32)

    head_id = jnp.arange(W) // HEAD_DIM
    bd = (head_id[:, None] == head_id[None, :]).astype(BF16)
    bd2 = jnp.concatenate([bd, bd], axis=0)
    t_att = min(ATT_T, s)
    uu = (jnp.arange(t_att)[:, None] >= jnp.arange(t_att)[None, :]).astype(BF16)
    tril = jnp.tril(jnp.ones((GMLP_CHUNK, GMLP_CHUNK), dtype=bool))

    x2 = x.reshape(n, d)
    for l in range(depth):
        qg = row(jnp.tile(q_norm_g[l], GROUP_HEADS) * (HEAD_DIM ** -0.5))
        kg = row(jnp.tile(k_norm_g[l], GROUP_HEADS))
        q, k, v, rest = _in_proj(x2, row(norm1_g[l]), w_in[l].astype(BF16), qg, kg, bd2)
        gn = group_norm_g[l].reshape(N_GROUPS, 1, W)
        to3 = lambda t: t.reshape(bsz, s, t.shape[-1])
        rest3 = to3(rest)
        y_a = _sb_attention(to3(q), to3(k), to3(v), uu, gn[0])
        cw = jnp.zeros((CONF_HALO, W), F32).at[:CONF_KERNEL].set(conf_dw_w[l])
        y_b = _conformer(rest3, 0, cw, row(conf_dw_b[l]), row(conf_ln_g[l]), row(conf_ln_b[l]), gn[1])
        ws = jnp.where(tril, gmlp_ws[l], 0.0).astype(BF16)
        bs_mat = jnp.repeat(gmlp_bs[l].T, HEAD_DIM, axis=1)
        y_c = _gmlp(rest3, 2, row(gmlp_ln_g[l]), row(gmlp_ln_b[l]), ws, bs_mat, gn[2])
        lw = jnp.zeros((8, W), F32).at[:LRU_CONV].set(lru_conv_w[l])
        wax = jnp.concatenate([_block_diag(lru_wa[l]), _block_diag(lru_wx[l])], axis=1).astype(BF16)
        y_d = _rg_lru(rest3, 4, lw, row(lru_conv_b[l]), wax, row(lru_ba[l]), row(lru_bx[l]),
                      row(lru_lambda[l]), gn[3])
        ys = [t.reshape(n, W) for t in (y_a, y_b, y_c, y_d)]
        x2, hn = _out_proj(ys, w_out[l].astype(BF16), x2, row(norm2_g[l]), BF16 if l % 2 == 0 else F32)
        j = l // 2
        if l % 2 == 0:
            x2 = _dense_ffn(hn, x2, ffn_w_gate[j].astype(BF16), ffn_w_up[j].astype(BF16),
                            ffn_w_down[j].astype(BF16))
        else:
            x2 = _top2_moe(hn, x2, row(norm2_g[l]), moe_router[j], moe_w_gate[j], moe_w_up[j], moe_w_down[j])
    return x2.reshape(bsz, s, d)
```

```python
import functools

import jax
import jax.numpy as jnp
from jax import lax
from jax.experimental import pallas as pl
from jax.experimental.pallas import tpu as pltpu
from jax.experimental.pallas import tpu_sc as plsc

F32 = jnp.float32
BF16 = jnp.bfloat16

HEAD_DIM = 64
GROUP_HEADS = 4
GROUP_WIDTH = GROUP_HEADS * HEAD_DIM
N_GROUPS = 4
CONF_KERNEL = 31
GMLP_CHUNK = 128
LRU_CONV = 4
LRU_C = 8.0
N_EXPERTS = 8
EPS = 1e-6

LANES = 128
VMEM_LIMIT = 56 * 1024 * 1024

IN_TM = 1024
ATT_T = 256
CONF_T = 256
CONF_R = 64
CONF_HALO = 32
GMLP_T = 512
LRU_T = 256
LRU_HALO = 8
OUT_TM = 512
FFN_TM = 1024
FFN_TF = 1408
ROUTE_TM = 512
MOE_TM = 1024
MOE_TF = 512
COMB_TM = 512
SC_GATHER_ROWS = 32


def _params(*sem):
    return pltpu.CompilerParams(dimension_semantics=sem, vmem_limit_bytes=VMEM_LIMIT)


def _split_hi_lo(x):
    hi = x.astype(BF16)
    lo = (x - hi.astype(F32)).astype(BF16)
    return hi, lo


SOFTPLUS_CLAMP = 60.0


def _softplus(z):
    return jnp.maximum(jnp.log(1.0 + jnp.exp(jnp.minimum(z, SOFTPLUS_CLAMP))), z)


def _gelu_tanh(x):
    c = 0.7978845608028654
    return 0.5 * x * (1.0 + jnp.tanh(c * (x + 0.044715 * (x * x * x))))


def _group_rms(y, g):
    ms = jnp.mean(y * y, axis=-1, keepdims=True)
    return y * lax.rsqrt(ms + EPS) * g


def _in_proj_kernel(x_ref, g_ref, w_ref, qg_ref, kg_ref, bd_ref, q_ref, k_ref, v_ref, rest_ref):
    xf = x_ref[...]
    ms = jnp.mean(xf * xf, axis=-1, keepdims=True)
    h = (xf * lax.rsqrt(ms + EPS) * g_ref[...]).astype(BF16)
    W = GROUP_WIDTH

    def proj(c):
        return jnp.dot(h, w_ref[:, c * W:(c + 1) * W], preferred_element_type=F32)

    def head_norm(p, gain):
        hi, lo = _split_hi_lo(p * p)
        ss = jnp.dot(jnp.concatenate([hi, lo], axis=1), bd_ref[...], preferred_element_type=F32)
        return p * lax.rsqrt(ss * (1.0 / HEAD_DIM) + EPS) * gain

    q_ref[...] = head_norm(proj(0), qg_ref[...]).astype(BF16)
    k_ref[...] = head_norm(proj(1), kg_ref[...]).astype(BF16)
    v_ref[...] = proj(2).astype(BF16)
    n_rest = rest_ref.shape[1] // W
    for c in range(n_rest):
        rest_ref[:, c * W:(c + 1) * W] = proj(3 + c)


def _in_proj(x2, g, w_bf, qg, kg, bd2):
    n, d = x2.shape
    cols = w_bf.shape[1]
    W = GROUP_WIDTH
    rest_cols = cols - 3 * W
    tm = min(IN_TM, n)
    row = lambda i: (i, 0)
    const = lambda i: (0, 0)
    return pl.pallas_call(
        _in_proj_kernel,
        grid=(n // tm,),
        in_specs=[
            pl.BlockSpec((tm, d), row),
            pl.BlockSpec((1, d), const),
            pl.BlockSpec((d, cols), const),
            pl.BlockSpec((1, W), const),
            pl.BlockSpec((1, W), const),
            pl.BlockSpec((2 * W, W), const),
        ],
        out_specs=[
            pl.BlockSpec((tm, W), row),
            pl.BlockSpec((tm, W), row),
            pl.BlockSpec((tm, W), row),
            pl.BlockSpec((tm, rest_cols), row),
        ],
        out_shape=[
            jax.ShapeDtypeStruct((n, W), BF16),
            jax.ShapeDtypeStruct((n, W), BF16),
            jax.ShapeDtypeStruct((n, W), BF16),
            jax.ShapeDtypeStruct((n, rest_cols), F32),
        ],
        compiler_params=_params("arbitrary"),
        name="in_proj",
    )(x2, g, w_bf, qg, kg, bd2)


def _sb_attn_kernel(q_ref, k_ref, v_ref, uu_ref, gn_ref, o_ref):
    T = q_ref.shape[1]
    n_pairs = GROUP_WIDTH // LANES
    i = pl.program_id(1)
    lane = lax.broadcasted_iota(jnp.int32, (T, LANES), 1)
    rows = lax.broadcasted_iota(jnp.int32, (2 * T, T), 0)
    cols = lax.broadcasted_iota(jnp.int32, (2 * T, T), 1)
    causal = cols < jnp.where(rows >= T, rows - T, rows)
    uu = uu_ref[...]

    def tile(qs, k2, v2, carry, mask):
        z = lax.dot_general(qs, k2, (((1,), (1,)), ((), ())), preferred_element_type=F32)
        sp = _softplus(z)
        if mask is not None:
            sp = jnp.where(mask, sp, 0.0)
        cum = jnp.dot(sp.astype(BF16), uu, preferred_element_type=F32)
        w = jnp.exp(z - cum - carry)
        if mask is not None:
            w = jnp.where(mask, w, 0.0)
        pv = jnp.dot(w.astype(BF16), v2, preferred_element_type=F32)
        return pv, carry + jnp.sum(sp, axis=1, keepdims=True)

    def kv(r0, pair):
        cs = slice(pair * LANES, (pair + 1) * LANES)
        return k_ref[0, pl.ds(r0, T), cs], v_ref[0, pl.ds(r0, T), cs]

    qs = []
    for pair in range(n_pairs):
        q2 = q_ref[0, :, pair * LANES:(pair + 1) * LANES]
        zero = jnp.zeros_like(q2)
        qs.append(jnp.concatenate([jnp.where(lane < HEAD_DIM, q2, zero),
                                   jnp.where(lane >= HEAD_DIM, q2, zero)], axis=0))
    row0 = pl.multiple_of(i * T, T)
    state = tuple(tile(qs[p], *kv(row0, p), jnp.zeros((2 * T, 1), F32), causal) for p in range(n_pairs))

    def body(step, st):
        r0 = pl.multiple_of((i - 1 - step) * T, T)
        new = []
        for p in range(n_pairs):
            acc, carry = st[p]
            pv, carry = tile(qs[p], *kv(r0, p), carry, None)
            new.append((acc + pv, carry))
        return tuple(new)

    state = lax.fori_loop(0, i, body, state)
    y = jnp.concatenate([jnp.where(lane < HEAD_DIM, acc[:T], acc[T:]) for acc, _ in state], axis=1)
    o_ref[0] = _group_rms(y, gn_ref[...]).astype(o_ref.dtype)


def _sb_attention(q3, k3, v3, uu, gn):
    b, s, w = q3.shape
    t = min(ATT_T, s)
    return pl.pallas_call(
        _sb_attn_kernel,
        grid=(b, s // t),
        in_specs=[
            pl.BlockSpec((1, t, w), lambda bi, i: (bi, i, 0)),
            pl.BlockSpec((1, s, w), lambda bi, i: (bi, 0, 0)),
            pl.BlockSpec((1, s, w), lambda bi, i: (bi, 0, 0)),
            pl.BlockSpec((t, t), lambda bi, i: (0, 0)),
            pl.BlockSpec((1, w), lambda bi, i: (0, 0)),
        ],
        out_specs=pl.BlockSpec((1, t, w), lambda bi, i: (bi, i, 0)),
        out_shape=jax.ShapeDtypeStruct((b, s, w), BF16),
        compiler_params=_params("arbitrary", "arbitrary"),
        name="sb_attn",
    )(q3, k3, v3, uu, gn)


def _conf_kernel(val_ref, gate_ref, w_ref, b_ref, lng_ref, lnb_ref, gn_ref, o_ref, hs_ref, sh_ref):
    T = val_ref.shape[1]
    H = CONF_HALO
    SUB = 8

    @pl.when(pl.program_id(1) == 0)
    def _():
        hs_ref[0:H, :] = jnp.zeros((H, GROUP_WIDTH), F32)

    hs_ref[H:H + T, :] = val_ref[0] * jax.nn.sigmoid(gate_ref[0])
    off = H - (CONF_KERNEL - 1)
    L = sh_ref.shape[1]
    for s in range(1, SUB):
        sh_ref[s - 1] = hs_ref[s:s + L, :]
    R = min(CONF_R, T)
    for r0 in range(0, T, R):
        acc = jnp.broadcast_to(b_ref[...], (R, GROUP_WIDTH))
        for k in range(CONF_KERNEL):
            s = (off + k) % SUB
            a = off + k - s + r0
            tap = hs_ref[a:a + R, :] if s == 0 else sh_ref[s - 1, a:a + R, :]
            acc = acc + w_ref[k:k + 1, :] * tap
        mu = jnp.mean(acc, axis=-1, keepdims=True)
        xc = acc - mu
        var = jnp.mean(xc * xc, axis=-1, keepdims=True)
        y = xc * lax.rsqrt(var + EPS) * lng_ref[...] + lnb_ref[...]
        y = y * jax.nn.sigmoid(y)
        o_ref[0, r0:r0 + R, :] = _group_rms(y, gn_ref[...]).astype(o_ref.dtype)
    hs_ref[0:H, :] = hs_ref[T:T + H, :]


def _conformer(rest3, col0, w_pad, b, lng, lnb, gn):
    bsz, s, _ = rest3.shape
    W = GROUP_WIDTH
    t = min(CONF_T, s)
    vec = pl.BlockSpec((1, W), lambda bi, i: (0, 0))
    return pl.pallas_call(
        _conf_kernel,
        grid=(bsz, s // t),
        in_specs=[
            pl.BlockSpec((1, t, W), lambda bi, i: (bi, i, col0)),
            pl.BlockSpec((1, t, W), lambda bi, i: (bi, i, col0 + 1)),
            pl.BlockSpec(w_pad.shape, lambda bi, i: (0, 0)),
            vec, vec, vec, vec,
        ],
        out_specs=pl.BlockSpec((1, t, W), lambda bi, i: (bi, i, 0)),
        out_shape=jax.ShapeDtypeStruct((bsz, s, W), BF16),
        scratch_shapes=[pltpu.VMEM((t + CONF_HALO, W), F32),
                        pltpu.VMEM((7, t + CONF_HALO - 8, W), F32)],
        compiler_params=_params("arbitrary", "arbitrary"),
        name="conformer",
    )(rest3, rest3, w_pad, b, lng, lnb, gn)


def _gmlp_kernel(u_ref, v_ref, lng_ref, lnb_ref, ws_ref, bs_ref, gn_ref, o_ref):
    T = u_ref.shape[1]
    C = GMLP_CHUNK
    lane = lax.broadcasted_iota(jnp.int32, (C, GROUP_WIDTH), 1)
    for c0 in range(0, T, C):
        u = _gelu_tanh(u_ref[0, c0:c0 + C, :])
        v = _gelu_tanh(v_ref[0, c0:c0 + C, :])
        mu = jnp.mean(v, axis=-1, keepdims=True)
        xc = v - mu
        var = jnp.mean(xc * xc, axis=-1, keepdims=True)
        vb = (xc * lax.rsqrt(var + EPS) * lng_ref[...] + lnb_ref[...]).astype(BF16)
        mixed = bs_ref[...]
        for h in range(GROUP_HEADS):
            m = jnp.dot(ws_ref[h], vb, preferred_element_type=F32)
            in_head = (lane >= h * HEAD_DIM) & (lane < (h + 1) * HEAD_DIM)
            mixed = mixed + jnp.where(in_head, m, 0.0)
        o_ref[0, c0:c0 + C, :] = _group_rms(u * mixed, gn_ref[...]).astype(o_ref.dtype)


def _gmlp(rest3, col0, lng, lnb, ws_bf, bs_mat, gn):
    bsz, s, _ = rest3.shape
    W = GROUP_WIDTH
    t = min(GMLP_T, s)
    vec = pl.BlockSpec((1, W), lambda bi, i: (0, 0))
    return pl.pallas_call(
        _gmlp_kernel,
        grid=(bsz, s // t),
        in_specs=[
            pl.BlockSpec((1, t, W), lambda bi, i: (bi, i, col0)),
            pl.BlockSpec((1, t, W), lambda bi, i: (bi, i, col0 + 1)),
            vec, vec,
            pl.BlockSpec(ws_bf.shape, lambda bi, i: (0, 0, 0)),
            pl.BlockSpec(bs_mat.shape, lambda bi, i: (0, 0)),
            vec,
        ],
        out_specs=pl.BlockSpec((1, t, W), lambda bi, i: (bi, i, 0)),
        out_shape=jax.ShapeDtypeStruct((bsz, s, W), BF16),
        compiler_params=_params("arbitrary", "arbitrary"),
        name="gmlp",
    )(rest3, rest3, lng, lnb, ws_bf, bs_mat, gn)


def _lru_kernel(x_ref, gate_ref, cw_ref, cb_ref, wax_ref, ba_ref, bx_ref, lam_ref, gn_ref, o_ref,
                xs_ref, hprev_ref):
    T = x_ref.shape[1]
    W = GROUP_WIDTH
    H = LRU_HALO

    @pl.when(pl.program_id(1) == 0)
    def _():
        xs_ref[0:H, :] = jnp.zeros((H, W), F32)
        hprev_ref[...] = jnp.zeros(hprev_ref.shape, F32)

    xs_ref[H:H + T, :] = x_ref[0]
    off = H - (LRU_CONV - 1)
    xb = jnp.broadcast_to(cb_ref[...], (T, W))
    for k in range(LRU_CONV):
        xb = xb + cw_ref[k:k + 1, :] * xs_ref[off + k:off + k + T, :]
    xs_ref[0:H, :] = xs_ref[T:T + H, :]

    pre = jnp.dot(xb.astype(BF16), wax_ref[...], preferred_element_type=F32)
    r = jax.nn.sigmoid(pre[:, :W] + ba_ref[...])
    ig = jax.nn.sigmoid(pre[:, W:] + bx_ref[...])
    nlam = -lam_ref[...]
    log_a = (-LRU_C) * r * (jnp.maximum(nlam, 0.0) + jnp.log1p(jnp.exp(-jnp.abs(nlam))))
    a = jnp.exp(log_a)
    th = jnp.tanh(log_a)
    one_minus_a2 = 2.0 * th / (th - 1.0)
    b = jnp.sqrt(one_minus_a2) * (ig * xb)

    SUB = 8
    a = a.reshape(T // SUB, SUB, W)
    b = b.reshape(T // SUB, SUB, W)
    sub = lax.broadcasted_iota(jnp.int32, a.shape, 1)
    d = 1
    while d < SUB:
        keep = sub >= d
        a_sh = jnp.where(keep, pltpu.roll(a, d, 1), 1.0)
        b_sh = jnp.where(keep, pltpu.roll(b, d, 1), 0.0)
        b = a * b_sh + b
        a = a * a_sh
        d *= 2
    carry = hprev_ref[0:1, :]
    tiles = []
    for g in range(T // SUB):
        hg = b[g] + a[g] * carry
        carry = hg[SUB - 1:SUB]
        tiles.append(hg)
    h = jnp.concatenate(tiles, axis=0)
    hprev_ref[...] = jnp.broadcast_to(carry, hprev_ref.shape)
    y = h * _gelu_tanh(gate_ref[0])
    o_ref[0] = _group_rms(y, gn_ref[...]).astype(o_ref.dtype)


def _rg_lru(rest3, col0, cw_pad, cb, wax_bf, ba, bx, lam, gn):
    bsz, s, _ = rest3.shape
    W = GROUP_WIDTH
    t = min(LRU_T, s)
    vec = pl.BlockSpec((1, W), lambda bi, i: (0, 0))
    return pl.pallas_call(
        _lru_kernel,
        grid=(bsz, s // t),
        in_specs=[
            pl.BlockSpec((1, t, W), lambda bi, i: (bi, i, col0)),
            pl.BlockSpec((1, t, W), lambda bi, i: (bi, i, col0 + 1)),
            pl.BlockSpec(cw_pad.shape, lambda bi, i: (0, 0)),
            vec,
            pl.BlockSpec(wax_bf.shape, lambda bi, i: (0, 0)),
            vec, vec, vec, vec,
        ],
        out_specs=pl.BlockSpec((1, t, W), lambda bi, i: (bi, i, 0)),
        out_shape=jax.ShapeDtypeStruct((bsz, s, W), BF16),
        scratch_shapes=[pltpu.VMEM((t + LRU_HALO, W), F32), pltpu.VMEM((8, W), F32)],
        compiler_params=_params("arbitrary", "arbitrary"),
        name="rg_lru",
    )(rest3, rest3, cw_pad, cb, wax_bf, ba, bx, lam, gn)


def _out_proj_kernel(ya_ref, yb_ref, yc_ref, yd_ref, w_ref, x_ref, g_ref, xo_ref, h_ref):
    W = GROUP_WIDTH
    acc = x_ref[...]
    for gi, y_ref in enumerate((ya_ref, yb_ref, yc_ref, yd_ref)):
        acc = acc + jnp.dot(y_ref[...], w_ref[gi * W:(gi + 1) * W, :], preferred_element_type=F32)
    xo_ref[...] = acc
    ms = jnp.mean(acc * acc, axis=-1, keepdims=True)
    h_ref[...] = (acc * lax.rsqrt(ms + EPS) * g_ref[...]).astype(h_ref.dtype)


def _out_proj(ys, w_bf, x2, g, h_dtype):
    n, d = x2.shape
    W = GROUP_WIDTH
    tm = min(OUT_TM, n)
    row = lambda i: (i, 0)
    const = lambda i: (0, 0)
    ysp = pl.BlockSpec((tm, W), row)
    return pl.pallas_call(
        _out_proj_kernel,
        grid=(n // tm,),
        in_specs=[ysp, ysp, ysp, ysp,
                  pl.BlockSpec(w_bf.shape, const),
                  pl.BlockSpec((tm, d), row),
                  pl.BlockSpec((1, d), const)],
        out_specs=[pl.BlockSpec((tm, d), row), pl.BlockSpec((tm, d), row)],
        out_shape=[jax.ShapeDtypeStruct((n, d), F32), jax.ShapeDtypeStruct((n, d), h_dtype)],
        compiler_params=_params("arbitrary"),
        name="out_proj",
    )(*ys, w_bf, x2, g)


def _ffn_kernel(h_ref, x_ref, wg_ref, wu_ref, wd_ref, o_ref):
    @pl.when(pl.program_id(1) == 0)
    def _():
        o_ref[...] = x_ref[...]

    h = h_ref[...]
    g = jnp.dot(h, wg_ref[...], preferred_element_type=F32)
    u = jnp.dot(h, wu_ref[...], preferred_element_type=F32)
    act = (g * jax.nn.sigmoid(g) * u).astype(BF16)
    o_ref[...] += jnp.dot(act, wd_ref[...], preferred_element_type=F32)


def _dense_ffn(hn, x2, wg, wu, wd):
    n, d = x2.shape
    f = wg.shape[1]
    tm = min(FFN_TM, n)
    tf = FFN_TF if f % FFN_TF == 0 else f
    return pl.pallas_call(
        _ffn_kernel,
        grid=(n // tm, f // tf),
        in_specs=[
            pl.BlockSpec((tm, d), lambda i, j: (i, 0)),
            pl.BlockSpec((tm, d), lambda i, j: (i, 0)),
            pl.BlockSpec((d, tf), lambda i, j: (0, j)),
            pl.BlockSpec((d, tf), lambda i, j: (0, j)),
            pl.BlockSpec((tf, d), lambda i, j: (j, 0)),
        ],
        out_specs=pl.BlockSpec((tm, d), lambda i, j: (i, 0)),
        out_shape=jax.ShapeDtypeStruct((n, d), F32),
        compiler_params=_params("arbitrary", "arbitrary"),
        name="dense_ffn",
    )(hn, x2, wg, wu, wd)


def _router_kernel(x_ref, g_ref, rh_ref, rl_ref, tri_ref, route_ref, cnt_ref, base_ref):
    tm = x_ref.shape[0]

    @pl.when(pl.program_id(0) == 0)
    def _():
        base_ref[...] = jnp.zeros(base_ref.shape, F32)

    xf = x_ref[...]
    ms = jnp.mean(xf * xf, axis=-1, keepdims=True)
    hh, hl = _split_hi_lo(xf * lax.rsqrt(ms + EPS) * g_ref[...])
    rh = rh_ref[...]
    logits = (jnp.dot(hh, rh, preferred_element_type=F32)
              + jnp.dot(hl, rh, preferred_element_type=F32)
              + jnp.dot(hh, rl_ref[...], preferred_element_type=F32))
    lane = lax.broadcasted_iota(jnp.int32, (tm, LANES), 1).astype(F32)
    neg = jnp.float32(-jnp.inf)
    logits = jnp.where(lane < N_EXPERTS, logits, neg)
    m1 = jnp.max(logits, axis=1, keepdims=True)
    i1 = jnp.min(jnp.where(logits == m1, lane, float(LANES)), axis=1, keepdims=True)
    l2 = jnp.where(lane == i1, neg, logits)
    m2 = jnp.max(l2, axis=1, keepdims=True)
    i2 = jnp.min(jnp.where(l2 == m2, lane, float(LANES)), axis=1, keepdims=True)
    e = jnp.exp(m2 - m1)
    g1 = 1.0 / (1.0 + e)
    g2 = e / (1.0 + e)
    oh1 = jnp.where(lane == i1, 1.0, 0.0)
    oh2 = jnp.where(lane == i2, 1.0, 0.0)
    oh = oh1 + oh2
    before = jnp.dot(tri_ref[...], oh.astype(BF16), preferred_element_type=F32) + base_ref[0:1, :]
    r1 = jnp.sum(oh1 * before, axis=1, keepdims=True)
    r2 = jnp.sum(oh2 * before, axis=1, keepdims=True)
    base = base_ref[0:1, :] + jnp.sum(oh, axis=0, keepdims=True)
    base_ref[...] = jnp.broadcast_to(base, base_ref.shape)
    cnt_ref[...] = jnp.broadcast_to(base, cnt_ref.shape)
    out = jnp.where(lane == 0, i1, 0.0)
    out = jnp.where(lane == 1, i2, out)
    out = jnp.where(lane == 2, g1, out)
    out = jnp.where(lane == 3, g2, out)
    out = jnp.where(lane == 4, r1, out)
    out = jnp.where(lane == 5, r2, out)
    route_ref[...] = out


def _router(x2, g, rh, rl, tri):
    n, d = x2.shape
    tm = min(ROUTE_TM, n)
    return pl.pallas_call(
        _router_kernel,
        grid=(n // tm,),
        in_specs=[
            pl.BlockSpec((tm, d), lambda i: (i, 0)),
            pl.BlockSpec((1, d), lambda i: (0, 0)),
            pl.BlockSpec((d, LANES), lambda i: (0, 0)),
            pl.BlockSpec((d, LANES), lambda i: (0, 0)),
            pl.BlockSpec((tm, tm), lambda i: (0, 0)),
        ],
        out_specs=[pl.BlockSpec((tm, LANES), lambda i: (i, 0)),
                   pl.BlockSpec((8, LANES), lambda i: (0, 0))],
        out_shape=[jax.ShapeDtypeStruct((n, LANES), F32), jax.ShapeDtypeStruct((8, LANES), F32)],
        scratch_shapes=[pltpu.VMEM((8, LANES), F32)],
        compiler_params=_params("arbitrary"),
        name="router",
    )(x2, g, rh, rl, tri)


def _sc_gather_rows(table, idx):
    _, d = table.shape
    b = idx.shape[0]
    win = SC_GATHER_ROWS
    sc = plsc.get_sparse_core_info()
    n_workers = sc.num_cores * sc.num_subcores
    per_w = b // n_workers
    n_win = per_w // win
    assert per_w * n_workers == b and n_win * win == per_w and n_win % 2 == 0, (b, n_workers, win)
    mesh = plsc.VectorSubcoreMesh(core_axis_name="c", subcore_axis_name="s")
    dma = pltpu.SemaphoreType.DMA

    @functools.partial(
        pl.kernel, mesh=mesh, out_type=jax.ShapeDtypeStruct((b, d), table.dtype), name="sc_gather_rows",
        scratch_types=[pltpu.VMEM((win,), jnp.int32), pltpu.VMEM((win,), jnp.int32),
                       pltpu.VMEM((win, d), table.dtype), pltpu.VMEM((win, d), table.dtype),
                       dma, dma, dma, dma])
    def gather_kernel(table_hbm, idx_hbm, out_hbm, i0, i1, r0, r1, g0, g1, w0, w1):
        wid = lax.axis_index("s") * sc.num_cores + lax.axis_index("c")
        base = wid * per_w
        idxb, rows, gsem, wsem = (i0, i1), (r0, r1), (g0, g1), (w0, w1)

        def off(c):
            return pl.multiple_of(base + c * win, win)

        def gather(s):
            return pltpu.make_async_copy(table_hbm.at[idxb[s]], rows[s], gsem[s])

        def write(c, s):
            return pltpu.make_async_copy(rows[s], out_hbm.at[pl.ds(off(c), win)], wsem[s])

        pltpu.sync_copy(idx_hbm.at[pl.ds(off(0), win)], idxb[0])
        gather(0).start()

        @pl.loop(0, n_win, step=2)
        def _(c):
            for s in (0, 1):
                cc = c + s

                @pl.when(cc + 1 < n_win)
                def _():
                    @pl.when(cc >= 1)
                    def _():
                        write(cc - 1, 1 - s).wait()

                    pltpu.sync_copy(idx_hbm.at[pl.ds(off(cc + 1), win)], idxb[1 - s])
                    gather(1 - s).start()

                gather(s).wait()
                write(cc, s).start()

        write(n_win - 2, 0).wait()
        write(n_win - 1, 1).wait()

    return gather_kernel(table, idx)


def _sc_dispatch_rows(src, dest1, dest2, n_rows):
    n, d = src.shape
    win = SC_GATHER_ROWS
    sc = plsc.get_sparse_core_info()
    n_workers = sc.num_cores * sc.num_subcores
    per_w = n // n_workers
    n_win = per_w // win
    assert per_w * n_workers == n and n_win * win == per_w and n_win % 2 == 0, (n, n_workers, win)
    mesh = plsc.VectorSubcoreMesh(core_axis_name="c", subcore_axis_name="s")
    dma = pltpu.SemaphoreType.DMA
    ivec = pltpu.VMEM((win,), jnp.int32)
    rbuf = pltpu.VMEM((win, d), src.dtype)

    @functools.partial(
        pl.kernel, mesh=mesh, out_type=jax.ShapeDtypeStruct((n_rows, d), src.dtype), name="sc_dispatch_rows",
        scratch_types=[ivec, ivec, ivec, ivec, rbuf, rbuf, dma, dma, dma, dma, dma, dma])
    def dispatch_kernel(src_hbm, d1_hbm, d2_hbm, out_hbm, a0, a1, b0, b1, r0, r1, l0, l1, p0, p1, q0, q1):
        wid = lax.axis_index("s") * sc.num_cores + lax.axis_index("c")
        base = wid * per_w
        ia, ib, rows, lsem, psem, qsem = (a0, a1), (b0, b1), (r0, r1), (l0, l1), (p0, p1), (q0, q1)

        def off(c):
            return pl.multiple_of(base + c * win, win)

        def load(c, s):
            return pltpu.make_async_copy(src_hbm.at[pl.ds(off(c), win)], rows[s], lsem[s])

        def scatters(s):
            return (pltpu.make_async_copy(rows[s], out_hbm.at[ia[s]], psem[s]),
                    pltpu.make_async_copy(rows[s], out_hbm.at[ib[s]], qsem[s]))

        def fetch(c, s):
            pltpu.sync_copy(d1_hbm.at[pl.ds(off(c), win)], ia[s])
            pltpu.sync_copy(d2_hbm.at[pl.ds(off(c), win)], ib[s])
            load(c, s).start()

        fetch(0, 0)

        @pl.loop(0, n_win, step=2)
        def _(c):
            for s in (0, 1):
                cc = c + s

                @pl.when(cc + 1 < n_win)
                def _():
                    @pl.when(cc >= 1)
                    def _():
                        for cp in scatters(1 - s):
                            cp.wait()

                    fetch(cc + 1, 1 - s)

                load(cc, s).wait()
                for cp in scatters(s):
                    cp.start()

        for s in (0, 1):
            for cp in scatters(s):
                cp.wait()

    return dispatch_kernel(src, dest1, dest2)


def _moe_kernel(bexp_ref, nused_ref, nvalid_ref, x_ref, wg_ref, wu_ref, wd_ref, o_ref, xb):
    i = pl.program_id(0)
    j = pl.program_id(1)
    used = i < nused_ref[0]

    @pl.when(used & (j == 0))
    def _():
        row = lax.broadcasted_iota(jnp.int32, x_ref.shape, 0)
        xb[...] = jnp.where(row < nvalid_ref[i], x_ref[...], 0.0).astype(BF16)

    @pl.when(used)
    def _():
        x = xb[...]
        g = jnp.dot(x, wg_ref[0].astype(BF16), preferred_element_type=F32)
        u = jnp.dot(x, wu_ref[0].astype(BF16), preferred_element_type=F32)
        act = (g * jax.nn.sigmoid(g) * u).astype(BF16)
        y = jnp.dot(act, wd_ref[0].astype(BF16), preferred_element_type=F32)

        @pl.when(j == 0)
        def _():
            o_ref[...] = y

        @pl.when(j > 0)
        def _():
            o_ref[...] += y

    @pl.when(jnp.logical_not(used) & (j == 0))
    def _():
        o_ref[...] = jnp.zeros(o_ref.shape, o_ref.dtype)


def _moe_experts(block_exp, n_used, n_valid, xs, wg, wu, wd, tm):
    n_rows, d = xs.shape
    n_blocks = n_rows // tm
    f = wg.shape[2]
    tf = MOE_TF if f % MOE_TF == 0 else f
    nj = f // tf

    def live(i, j, be, nu):
        u = i < nu[0]
        return jnp.where(u, i, nu[0] - 1), jnp.where(u, j, nj - 1)

    def w_in_map(i, j, be, nu, nv):
        ii, jj = live(i, j, be, nu)
        return (be[ii], 0, jj)

    def w_down_map(i, j, be, nu, nv):
        ii, jj = live(i, j, be, nu)
        return (be[ii], jj, 0)

    def x_map(i, j, be, nu, nv):
        return (live(i, j, be, nu)[0], 0)

    def out_map(i, j, be, nu, nv):
        return (i, 0)

    grid_spec = pltpu.PrefetchScalarGridSpec(
        num_scalar_prefetch=3,
        grid=(n_blocks, nj),
        in_specs=[
            pl.BlockSpec((tm, d), x_map),
            pl.BlockSpec((1, d, tf), w_in_map),
            pl.BlockSpec((1, d, tf), w_in_map),
            pl.BlockSpec((1, tf, d), w_down_map),
        ],
        out_specs=pl.BlockSpec((tm, d), out_map),
        scratch_shapes=[pltpu.VMEM((tm, d), BF16)],
    )
    return pl.pallas_call(
        _moe_kernel,
        grid_spec=grid_spec,
        out_shape=jax.ShapeDtypeStruct((n_rows, d), F32),
        compiler_params=_params("arbitrary", "arbitrary"),
        name="moe_experts",
    )(block_exp, n_used, n_valid, xs, wg, wu, wd)


def _combine_kernel(x_ref, route_ref, a_ref, b_ref, o_ref):
    route = route_ref[...]
    o_ref[...] = x_ref[...] + route[:, 2:3] * a_ref[...] + route[:, 3:4] * b_ref[...]


def _combine(yg, x2, route):
    n, d = x2.shape
    tm = min(COMB_TM, n)
    nb = n // tm
    return pl.pallas_call(
        _combine_kernel,
        grid=(nb,),
        in_specs=[pl.BlockSpec((tm, d), lambda i: (i, 0)),
                  pl.BlockSpec((tm, LANES), lambda i: (i, 0)),
                  pl.BlockSpec((tm, d), lambda i: (i, 0)),
                  pl.BlockSpec((tm, d), lambda i: (i + nb, 0))],
        out_specs=pl.BlockSpec((tm, d), lambda i: (i, 0)),
        out_shape=jax.ShapeDtypeStruct((n, d), F32),
        compiler_params=_params("arbitrary"),
        name="moe_combine",
    )(x2, route, yg, yg)


def _top2_moe(hn, x2, g2, router, wg, wu, wd):
    n, d = x2.shape
    a = 2 * n
    tm = min(MOE_TM, n)
    r_pad = jnp.zeros((d, LANES), F32).at[:, :N_EXPERTS].set(router)
    rh, rl = _split_hi_lo(r_pad)
    rt = min(ROUTE_TM, n)
    tri = (lax.broadcasted_iota(jnp.int32, (rt, rt), 1) < lax.broadcasted_iota(jnp.int32, (rt, rt), 0)).astype(BF16)
    route, cnt = _router(x2, g2, rh, rl, tri)

    counts = cnt[0, :N_EXPERTS].astype(jnp.int32)
    padded = (counts + tm - 1) // tm * tm
    pad_ends = jnp.cumsum(padded)
    pad_starts = pad_ends - padded
    e1 = route[:, 0].astype(jnp.int32)
    e2 = route[:, 1].astype(jnp.int32)
    dest1 = pad_starts[e1] + route[:, 4].astype(jnp.int32)
    dest2 = pad_starts[e2] + route[:, 5].astype(jnp.int32)
    n_blocks = a // tm + N_EXPERTS
    n_rows = n_blocks * tm
    block_start = jnp.arange(n_blocks, dtype=jnp.int32) * tm
    block_exp = jnp.minimum(jnp.searchsorted(pad_ends, block_start, side="right"), N_EXPERTS - 1).astype(jnp.int32)
    n_used = (pad_ends[-1] // tm).astype(jnp.int32).reshape(1)
    n_valid = jnp.clip(counts[block_exp] - (block_start - pad_starts[block_exp]), 0, tm).astype(jnp.int32)

    xs = _sc_dispatch_rows(hn, dest1, dest2, n_rows)
    ys = _moe_experts(block_exp, n_used, n_valid, xs, wg, wu, wd, tm)
    yg = _sc_gather_rows(ys, jnp.concatenate([dest1, dest2]))
    return _combine(yg, x2, route)


def _block_diag(w):
    h, dh, _ = w.shape
    eye = jnp.eye(h, dtype=w.dtype)
    return jnp.einsum("hij,hg->higj", w, eye).reshape(h * dh, h * dh)


def kernel(x, norm1_g, w_in, q_norm_g, k_norm_g, conf_dw_w, conf_dw_b, conf_ln_g, conf_ln_b, gmlp_ln_g, gmlp_ln_b, gmlp_ws, gmlp_bs, lru_conv_w, lru_conv_b, lru_wa, lru_ba, lru_wx, lru_bx, lru_lambda, group_norm_g, w_out, norm2_g, ffn_w_gate, ffn_w_up, ffn_w_down, moe_router, moe_w_gate, moe_w_up, moe_w_down):
    bsz, s, d = x.shape
    n = bsz * s
    depth = w_in.shape[0]
    W = GROUP_WIDTH
    row = lambda v: v.reshape(1, -1).astype(F32)

    head_id = jnp.arange(W) // HEAD_DIM
    bd = (head_id[:, None] == head_id[None, :]).astype(BF16)
    bd2 = jnp.concatenate([bd, bd], axis=0)
    t_att = min(ATT_T, s)
    uu = (jnp.arange(t_att)[:, None] >= jnp.arange(t_att)[None, :]).astype(BF16)
    tril = jnp.tril(jnp.ones((GMLP_CHUNK, GMLP_CHUNK), dtype=bool))

    x2 = x.reshape(n, d)
    for l in range(depth):
        qg = row(jnp.tile(q_norm_g[l], GROUP_HEADS) * (HEAD_DIM ** -0.5))
        kg = row(jnp.tile(k_norm_g[l], GROUP_HEADS))
        q, k, v, rest = _in_proj(x2, row(norm1_g[l]), w_in[l].astype(BF16), qg, kg, bd2)
        gn = group_norm_g[l].reshape(N_GROUPS, 1, W)
        to3 = lambda t: t.reshape(bsz, s, t.shape[-1])
        rest3 = to3(rest)
        y_a = _sb_attention(to3(q), to3(k), to3(v), uu, gn[0])
        cw = jnp.zeros((CONF_HALO, W), F32).at[:CONF_KERNEL].set(conf_dw_w[l])
        y_b = _conformer(rest3, 0, cw, row(conf_dw_b[l]), row(conf_ln_g[l]), row(conf_ln_b[l]), gn[1])
        ws = jnp.where(tril, gmlp_ws[l], 0.0).astype(BF16)
        bs_mat = jnp.repeat(gmlp_bs[l].T, HEAD_DIM, axis=1)
        y_c = _gmlp(rest3, 2, row(gmlp_ln_g[l]), row(gmlp_ln_b[l]), ws, bs_mat, gn[2])
        lw = jnp.zeros((8, W), F32).at[:LRU_CONV].set(lru_conv_w[l])
        wax = jnp.concatenate([_block_diag(lru_wa[l]), _block_diag(lru_wx[l])], axis=1).astype(BF16)
        y_d = _rg_lru(rest3, 4, lw, row(lru_conv_b[l]), wax, row(lru_ba[l]), row(lru_bx[l]),
                      row(lru_lambda[l]), gn[3])
        ys = [t.reshape(n, W) for t in (y_a, y_b, y_c, y_d)]
        x2, hn = _out_proj(ys, w_out[l].astype(BF16), x2, row(norm2_g[l]), BF16 if l % 2 == 0 else F32)
        j = l // 2
        if l % 2 == 0:
            x2 = _dense_ffn(hn, x2, ffn_w_gate[j].astype(BF16), ffn_w_up[j].astype(BF16),
                            ffn_w_down[j].astype(BF16))
        else:
            x2 = _top2_moe(hn, x2, row(norm2_g[l]), moe_router[j], moe_w_gate[j], moe_w_up[j], moe_w_down[j])
    return x2.reshape(bsz, s, d)
```

```python
import functools

import jax
import jax.numpy as jnp
from jax import lax
from jax.experimental import pallas as pl
from jax.experimental.pallas import tpu as pltpu
from jax.experimental.pallas import tpu_sc as plsc

F32 = jnp.float32
BF16 = jnp.bfloat16

HEAD_DIM = 64
GROUP_HEADS = 4
GROUP_WIDTH = GROUP_HEADS * HEAD_DIM
N_GROUPS = 4
CONF_KERNEL = 31
GMLP_CHUNK = 128
LRU_CONV = 4
LRU_C = 8.0
N_EXPERTS = 8
EPS = 1e-6

LANES = 128
VMEM_LIMIT = 56 * 1024 * 1024

IN_TM = 1024
ATT_T = 256
ATT_UNDERFLOW = 110.0
ATT_ZMAX_SLACK = 1.05
CONF_T = 256
CONF_R = 64
CONF_HALO = 32
GMLP_T = 512
LRU_T = 256
LRU_HALO = 8
OUT_TM = 512
FFN_TM = 1024
FFN_TF = 1408
ROUTE_TM = 512
MOE_TM = 1024
MOE_TF = 512
COMB_TM = 512
SC_GATHER_ROWS = 32


def _params(*sem):
    return pltpu.CompilerParams(dimension_semantics=sem, vmem_limit_bytes=VMEM_LIMIT)


def _split_hi_lo(x):
    hi = x.astype(BF16)
    lo = (x - hi.astype(F32)).astype(BF16)
    return hi, lo


SOFTPLUS_CLAMP = 60.0


def _softplus(z):
    return jnp.maximum(jnp.log(1.0 + jnp.exp(jnp.minimum(z, SOFTPLUS_CLAMP))), z)


def _gelu_tanh(x):
    c = 0.7978845608028654
    return 0.5 * x * (1.0 + jnp.tanh(c * (x + 0.044715 * (x * x * x))))


def _group_rms(y, g):
    ms = jnp.mean(y * y, axis=-1, keepdims=True)
    return y * lax.rsqrt(ms + EPS) * g


def _in_proj_kernel(x_ref, g_ref, w_ref, qg_ref, kg_ref, bd_ref, q_ref, k_ref, v_ref, rest_ref):
    xf = x_ref[...]
    ms = jnp.mean(xf * xf, axis=-1, keepdims=True)
    h = (xf * lax.rsqrt(ms + EPS) * g_ref[...]).astype(BF16)
    W = GROUP_WIDTH

    def proj(c):
        return jnp.dot(h, w_ref[:, c * W:(c + 1) * W], preferred_element_type=F32)

    def head_norm(p, gain):
        hi, lo = _split_hi_lo(p * p)
        ss = jnp.dot(jnp.concatenate([hi, lo], axis=1), bd_ref[...], preferred_element_type=F32)
        return p * lax.rsqrt(ss * (1.0 / HEAD_DIM) + EPS) * gain

    q_ref[...] = head_norm(proj(0), qg_ref[...]).astype(BF16)
    k_ref[...] = head_norm(proj(1), kg_ref[...]).astype(BF16)
    v_ref[...] = proj(2).astype(BF16)
    n_rest = rest_ref.shape[1] // W
    for c in range(n_rest):
        rest_ref[:, c * W:(c + 1) * W] = proj(3 + c)


def _in_proj(x2, g, w_bf, qg, kg, bd2):
    n, d = x2.shape
    cols = w_bf.shape[1]
    W = GROUP_WIDTH
    rest_cols = cols - 3 * W
    tm = min(IN_TM, n)
    row = lambda i: (i, 0)
    const = lambda i: (0, 0)
    return pl.pallas_call(
        _in_proj_kernel,
        grid=(n // tm,),
        in_specs=[
            pl.BlockSpec((tm, d), row),
            pl.BlockSpec((1, d), const),
            pl.BlockSpec((d, cols), const),
            pl.BlockSpec((1, W), const),
            pl.BlockSpec((1, W), const),
            pl.BlockSpec((2 * W, W), const),
        ],
        out_specs=[
            pl.BlockSpec((tm, W), row),
            pl.BlockSpec((tm, W), row),
            pl.BlockSpec((tm, W), row),
            pl.BlockSpec((tm, rest_cols), row),
        ],
        out_shape=[
            jax.ShapeDtypeStruct((n, W), BF16),
            jax.ShapeDtypeStruct((n, W), BF16),
            jax.ShapeDtypeStruct((n, W), BF16),
            jax.ShapeDtypeStruct((n, rest_cols), F32),
        ],
        compiler_params=_params("arbitrary"),
        name="in_proj",
    )(x2, g, w_bf, qg, kg, bd2)


def _sb_attn_kernel(zmax_ref, q_ref, k_ref, v_ref, uu_ref, gn_ref, o_ref):
    T = q_ref.shape[1]
    n_pairs = GROUP_WIDTH // LANES
    i = pl.program_id(1)
    lane = lax.broadcasted_iota(jnp.int32, (T, LANES), 1)
    rows = lax.broadcasted_iota(jnp.int32, (2 * T, T), 0)
    cols = lax.broadcasted_iota(jnp.int32, (2 * T, T), 1)
    causal = cols < jnp.where(rows >= T, rows - T, rows)
    uu = uu_ref[...]

    def tile(qs, k2, v2, carry, mask):
        z = lax.dot_general(qs, k2, (((1,), (1,)), ((), ())), preferred_element_type=F32)
        sp = _softplus(z)
        if mask is not None:
            sp = jnp.where(mask, sp, 0.0)
        cum = jnp.dot(sp.astype(BF16), uu, preferred_element_type=F32)
        w = jnp.exp(z - cum - carry)
        if mask is not None:
            w = jnp.where(mask, w, 0.0)
        pv = jnp.dot(w.astype(BF16), v2, preferred_element_type=F32)
        return pv, carry + jnp.sum(sp, axis=1, keepdims=True)

    def kv(r0, pair):
        cs = slice(pair * LANES, (pair + 1) * LANES)
        return k_ref[0, pl.ds(r0, T), cs], v_ref[0, pl.ds(r0, T), cs]

    qs = []
    for pair in range(n_pairs):
        q2 = q_ref[0, :, pair * LANES:(pair + 1) * LANES]
        zero = jnp.zeros_like(q2)
        qs.append(jnp.concatenate([jnp.where(lane < HEAD_DIM, q2, zero),
                                   jnp.where(lane >= HEAD_DIM, q2, zero)], axis=0))
    row0 = pl.multiple_of(i * T, T)
    state = tuple(tile(qs[p], *kv(row0, p), jnp.zeros((2 * T, 1), F32), causal) for p in range(n_pairs))

    dead_at = zmax_ref[0] + ATT_UNDERFLOW

    def alive(st):
        return functools.reduce(jnp.minimum, [jnp.min(carry) for _, carry in st]) <= dead_at

    def cond(c):
        step, live, _ = c
        return (step < i) & live

    def body(c):
        step, _, st = c
        r0 = pl.multiple_of((i - 1 - step) * T, T)
        new = []
        for p in range(n_pairs):
            acc, carry = st[p]
            pv, carry = tile(qs[p], *kv(r0, p), carry, None)
            new.append((acc + pv, carry))
        new = tuple(new)
        return step + 1, alive(new), new

    _, _, state = lax.while_loop(cond, body, (jnp.int32(0), alive(state), state))
    y = jnp.concatenate([jnp.where(lane < HEAD_DIM, acc[:T], acc[T:]) for acc, _ in state], axis=1)
    o_ref[0] = _group_rms(y, gn_ref[...]).astype(o_ref.dtype)


def _sb_attention(zmax, q3, k3, v3, uu, gn):
    b, s, w = q3.shape
    t = min(ATT_T, s)
    return pl.pallas_call(
        _sb_attn_kernel,
        grid=(b, s // t),
        in_specs=[
            pl.BlockSpec(memory_space=pltpu.SMEM),
            pl.BlockSpec((1, t, w), lambda bi, i: (bi, i, 0)),
            pl.BlockSpec((1, s, w), lambda bi, i: (bi, 0, 0)),
            pl.BlockSpec((1, s, w), lambda bi, i: (bi, 0, 0)),
            pl.BlockSpec((t, t), lambda bi, i: (0, 0)),
            pl.BlockSpec((1, w), lambda bi, i: (0, 0)),
        ],
        out_specs=pl.BlockSpec((1, t, w), lambda bi, i: (bi, i, 0)),
        out_shape=jax.ShapeDtypeStruct((b, s, w), BF16),
        compiler_params=_params("arbitrary", "arbitrary"),
        name="sb_attn",
    )(zmax, q3, k3, v3, uu, gn)


def _conf_kernel(val_ref, gate_ref, w_ref, b_ref, lng_ref, lnb_ref, gn_ref, o_ref, hs_ref, sh_ref):
    T = val_ref.shape[1]
    H = CONF_HALO
    SUB = 8

    @pl.when(pl.program_id(1) == 0)
    def _():
        hs_ref[0:H, :] = jnp.zeros((H, GROUP_WIDTH), F32)

    hs_ref[H:H + T, :] = val_ref[0] * jax.nn.sigmoid(gate_ref[0])
    off = H - (CONF_KERNEL - 1)
    L = sh_ref.shape[1]
    for s in range(1, SUB):
        sh_ref[s - 1] = hs_ref[s:s + L, :]
    R = min(CONF_R, T)
    for r0 in range(0, T, R):
        acc = jnp.broadcast_to(b_ref[...], (R, GROUP_WIDTH))
        for k in range(CONF_KERNEL):
            s = (off + k) % SUB
            a = off + k - s + r0
            tap = hs_ref[a:a + R, :] if s == 0 else sh_ref[s - 1, a:a + R, :]
            acc = acc + w_ref[k:k + 1, :] * tap
        mu = jnp.mean(acc, axis=-1, keepdims=True)
        xc = acc - mu
        var = jnp.mean(xc * xc, axis=-1, keepdims=True)
        y = xc * lax.rsqrt(var + EPS) * lng_ref[...] + lnb_ref[...]
        y = y * jax.nn.sigmoid(y)
        o_ref[0, r0:r0 + R, :] = _group_rms(y, gn_ref[...]).astype(o_ref.dtype)
    hs_ref[0:H, :] = hs_ref[T:T + H, :]


def _conformer(rest3, col0, w_pad, b, lng, lnb, gn):
    bsz, s, _ = rest3.shape
    W = GROUP_WIDTH
    t = min(CONF_T, s)
    vec = pl.BlockSpec((1, W), lambda bi, i: (0, 0))
    return pl.pallas_call(
        _conf_kernel,
        grid=(bsz, s // t),
        in_specs=[
            pl.BlockSpec((1, t, W), lambda bi, i: (bi, i, col0)),
            pl.BlockSpec((1, t, W), lambda bi, i: (bi, i, col0 + 1)),
            pl.BlockSpec(w_pad.shape, lambda bi, i: (0, 0)),
            vec, vec, vec, vec,
        ],
        out_specs=pl.BlockSpec((1, t, W), lambda bi, i: (bi, i, 0)),
        out_shape=jax.ShapeDtypeStruct((bsz, s, W), BF16),
        scratch_shapes=[pltpu.VMEM((t + CONF_HALO, W), F32),
                        pltpu.VMEM((7, t + CONF_HALO - 8, W), F32)],
        compiler_params=_params("arbitrary", "arbitrary"),
        name="conformer",
    )(rest3, rest3, w_pad, b, lng, lnb, gn)


def _gmlp_kernel(u_ref, v_ref, lng_ref, lnb_ref, ws_ref, bs_ref, gn_ref, o_ref):
    T = u_ref.shape[1]
    C = GMLP_CHUNK
    lane = lax.broadcasted_iota(jnp.int32, (C, GROUP_WIDTH), 1)
    for c0 in range(0, T, C):
        u = _gelu_tanh(u_ref[0, c0:c0 + C, :])
        v = _gelu_tanh(v_ref[0, c0:c0 + C, :])
        mu = jnp.mean(v, axis=-1, keepdims=True)
        xc = v - mu
        var = jnp.mean(xc * xc, axis=-1, keepdims=True)
        vb = (xc * lax.rsqrt(var + EPS) * lng_ref[...] + lnb_ref[...]).astype(BF16)
        mixed = bs_ref[...]
        for h in range(GROUP_HEADS):
            m = jnp.dot(ws_ref[h], vb, preferred_element_type=F32)
            in_head = (lane >= h * HEAD_DIM) & (lane < (h + 1) * HEAD_DIM)
            mixed = mixed + jnp.where(in_head, m, 0.0)
        o_ref[0, c0:c0 + C, :] = _group_rms(u * mixed, gn_ref[...]).astype(o_ref.dtype)


def _gmlp(rest3, col0, lng, lnb, ws_bf, bs_mat, gn):
    bsz, s, _ = rest3.shape
    W = GROUP_WIDTH
    t = min(GMLP_T, s)
    vec = pl.BlockSpec((1, W), lambda bi, i: (0, 0))
    return pl.pallas_call(
        _gmlp_kernel,
        grid=(bsz, s // t),
        in_specs=[
            pl.BlockSpec((1, t, W), lambda bi, i: (bi, i, col0)),
            pl.BlockSpec((1, t, W), lambda bi, i: (bi, i, col0 + 1)),
            vec, vec,
            pl.BlockSpec(ws_bf.shape, lambda bi, i: (0, 0, 0)),
            pl.BlockSpec(bs_mat.shape, lambda bi, i: (0, 0)),
            vec,
        ],
        out_specs=pl.BlockSpec((1, t, W), lambda bi, i: (bi, i, 0)),
        out_shape=jax.ShapeDtypeStruct((bsz, s, W), BF16),
        compiler_params=_params("arbitrary", "arbitrary"),
        name="gmlp",
    )(rest3, rest3, lng, lnb, ws_bf, bs_mat, gn)


def _lru_kernel(x_ref, gate_ref, cw_ref, cb_ref, wax_ref, ba_ref, bx_ref, lam_ref, gn_ref, o_ref,
                xs_ref, hprev_ref):
    T = x_ref.shape[1]
    W = GROUP_WIDTH
    H = LRU_HALO

    @pl.when(pl.program_id(1) == 0)
    def _():
        xs_ref[0:H, :] = jnp.zeros((H, W), F32)
        hprev_ref[...] = jnp.zeros(hprev_ref.shape, F32)

    xs_ref[H:H + T, :] = x_ref[0]
    off = H - (LRU_CONV - 1)
    xb = jnp.broadcast_to(cb_ref[...], (T, W))
    for k in range(LRU_CONV):
        xb = xb + cw_ref[k:k + 1, :] * xs_ref[off + k:off + k + T, :]
    xs_ref[0:H, :] = xs_ref[T:T + H, :]

    pre = jnp.dot(xb.astype(BF16), wax_ref[...], preferred_element_type=F32)
    r = jax.nn.sigmoid(pre[:, :W] + ba_ref[...])
    ig = jax.nn.sigmoid(pre[:, W:] + bx_ref[...])
    nlam = -lam_ref[...]
    log_a = (-LRU_C) * r * (jnp.maximum(nlam, 0.0) + jnp.log1p(jnp.exp(-jnp.abs(nlam))))
    a = jnp.exp(log_a)
    th = jnp.tanh(log_a)
    one_minus_a2 = 2.0 * th / (th - 1.0)
    b = jnp.sqrt(one_minus_a2) * (ig * xb)

    SUB = 8
    a = a.reshape(T // SUB, SUB, W)
    b = b.reshape(T // SUB, SUB, W)
    sub = lax.broadcasted_iota(jnp.int32, a.shape, 1)
    d = 1
    while d < SUB:
        keep = sub >= d
        a_sh = jnp.where(keep, pltpu.roll(a, d, 1), 1.0)
        b_sh = jnp.where(keep, pltpu.roll(b, d, 1), 0.0)
        b = a * b_sh + b
        a = a * a_sh
        d *= 2
    carry = hprev_ref[0:1, :]
    tiles = []
    for g in range(T // SUB):
        hg = b[g] + a[g] * carry
        carry = hg[SUB - 1:SUB]
        tiles.append(hg)
    h = jnp.concatenate(tiles, axis=0)
    hprev_ref[...] = jnp.broadcast_to(carry, hprev_ref.shape)
    y = h * _gelu_tanh(gate_ref[0])
    o_ref[0] = _group_rms(y, gn_ref[...]).astype(o_ref.dtype)


def _rg_lru(rest3, col0, cw_pad, cb, wax_bf, ba, bx, lam, gn):
    bsz, s, _ = rest3.shape
    W = GROUP_WIDTH
    t = min(LRU_T, s)
    vec = pl.BlockSpec((1, W), lambda bi, i: (0, 0))
    return pl.pallas_call(
        _lru_kernel,
        grid=(bsz, s // t),
        in_specs=[
            pl.BlockSpec((1, t, W), lambda bi, i: (bi, i, col0)),
            pl.BlockSpec((1, t, W), lambda bi, i: (bi, i, col0 + 1)),
            pl.BlockSpec(cw_pad.shape, lambda bi, i: (0, 0)),
            vec,
            pl.BlockSpec(wax_bf.shape, lambda bi, i: (0, 0)),
            vec, vec, vec, vec,
        ],
        out_specs=pl.BlockSpec((1, t, W), lambda bi, i: (bi, i, 0)),
        out_shape=jax.ShapeDtypeStruct((bsz, s, W), BF16),
        scratch_shapes=[pltpu.VMEM((t + LRU_HALO, W), F32), pltpu.VMEM((8, W), F32)],
        compiler_params=_params("arbitrary", "arbitrary"),
        name="rg_lru",
    )(rest3, rest3, cw_pad, cb, wax_bf, ba, bx, lam, gn)


def _out_proj_kernel(ya_ref, yb_ref, yc_ref, yd_ref, w_ref, x_ref, g_ref, xo_ref, h_ref):
    W = GROUP_WIDTH
    acc = x_ref[...]
    for gi, y_ref in enumerate((ya_ref, yb_ref, yc_ref, yd_ref)):
        acc = acc + jnp.dot(y_ref[...], w_ref[gi * W:(gi + 1) * W, :], preferred_element_type=F32)
    xo_ref[...] = acc
    ms = jnp.mean(acc * acc, axis=-1, keepdims=True)
    h_ref[...] = (acc * lax.rsqrt(ms + EPS) * g_ref[...]).astype(h_ref.dtype)


def _out_proj(ys, w_bf, x2, g, h_dtype):
    n, d = x2.shape
    W = GROUP_WIDTH
    tm = min(OUT_TM, n)
    row = lambda i: (i, 0)
    const = lambda i: (0, 0)
    ysp = pl.BlockSpec((tm, W), row)
    return pl.pallas_call(
        _out_proj_kernel,
        grid=(n // tm,),
        in_specs=[ysp, ysp, ysp, ysp,
                  pl.BlockSpec(w_bf.shape, const),
                  pl.BlockSpec((tm, d), row),
                  pl.BlockSpec((1, d), const)],
        out_specs=[pl.BlockSpec((tm, d), row), pl.BlockSpec((tm, d), row)],
        out_shape=[jax.ShapeDtypeStruct((n, d), F32), jax.ShapeDtypeStruct((n, d), h_dtype)],
        compiler_params=_params("arbitrary"),
        name="out_proj",
    )(*ys, w_bf, x2, g)


def _ffn_kernel(h_ref, x_ref, wg_ref, wu_ref, wd_ref, o_ref):
    @pl.when(pl.program_id(1) == 0)
    def _():
        o_ref[...] = x_ref[...]

    h = h_ref[...]
    g = jnp.dot(h, wg_ref[...], preferred_element_type=F32)
    u = jnp.dot(h, wu_ref[...], preferred_element_type=F32)
    act = (g * jax.nn.sigmoid(g) * u).astype(BF16)
    o_ref[...] += jnp.dot(act, wd_ref[...], preferred_element_type=F32)


def _dense_ffn(hn, x2, wg, wu, wd):
    n, d = x2.shape
    f = wg.shape[1]
    tm = min(FFN_TM, n)
    tf = FFN_TF if f % FFN_TF == 0 else f
    return pl.pallas_call(
        _ffn_kernel,
        grid=(n // tm, f // tf),
        in_specs=[
            pl.BlockSpec((tm, d), lambda i, j: (i, 0)),
            pl.BlockSpec((tm, d), lambda i, j: (i, 0)),
            pl.BlockSpec((d, tf), lambda i, j: (0, j)),
            pl.BlockSpec((d, tf), lambda i, j: (0, j)),
            pl.BlockSpec((tf, d), lambda i, j: (j, 0)),
        ],
        out_specs=pl.BlockSpec((tm, d), lambda i, j: (i, 0)),
        out_shape=jax.ShapeDtypeStruct((n, d), F32),
        compiler_params=_params("arbitrary", "arbitrary"),
        name="dense_ffn",
    )(hn, x2, wg, wu, wd)


def _router_kernel(x_ref, g_ref, rh_ref, rl_ref, tri_ref, route_ref, cnt_ref, base_ref):
    tm = x_ref.shape[0]

    @pl.when(pl.program_id(0) == 0)
    def _():
        base_ref[...] = jnp.zeros(base_ref.shape, F32)

    xf = x_ref[...]
    ms = jnp.mean(xf * xf, axis=-1, keepdims=True)
    hh, hl = _split_hi_lo(xf * lax.rsqrt(ms + EPS) * g_ref[...])
    rh = rh_ref[...]
    logits = (jnp.dot(hh, rh, preferred_element_type=F32)
              + jnp.dot(hl, rh, preferred_element_type=F32)
              + jnp.dot(hh, rl_ref[...], preferred_element_type=F32))
    lane = lax.broadcasted_iota(jnp.int32, (tm, LANES), 1).astype(F32)
    neg = jnp.float32(-jnp.inf)
    logits = jnp.where(lane < N_EXPERTS, logits, neg)
    m1 = jnp.max(logits, axis=1, keepdims=True)
    i1 = jnp.min(jnp.where(logits == m1, lane, float(LANES)), axis=1, keepdims=True)
    l2 = jnp.where(lane == i1, neg, logits)
    m2 = jnp.max(l2, axis=1, keepdims=True)
    i2 = jnp.min(jnp.where(l2 == m2, lane, float(LANES)), axis=1, keepdims=True)
    e = jnp.exp(m2 - m1)
    g1 = 1.0 / (1.0 + e)
    g2 = e / (1.0 + e)
    oh1 = jnp.where(lane == i1, 1.0, 0.0)
    oh2 = jnp.where(lane == i2, 1.0, 0.0)
    oh = oh1 + oh2
    before = jnp.dot(tri_ref[...], oh.astype(BF16), preferred_element_type=F32) + base_ref[0:1, :]
    r1 = jnp.sum(oh1 * before, axis=1, keepdims=True)
    r2 = jnp.sum(oh2 * before, axis=1, keepdims=True)
    base = base_ref[0:1, :] + jnp.sum(oh, axis=0, keepdims=True)
    base_ref[...] = jnp.broadcast_to(base, base_ref.shape)
    cnt_ref[...] = jnp.broadcast_to(base, cnt_ref.shape)
    out = jnp.where(lane == 0, i1, 0.0)
    out = jnp.where(lane == 1, i2, out)
    out = jnp.where(lane == 2, g1, out)
    out = jnp.where(lane == 3, g2, out)
    out = jnp.where(lane == 4, r1, out)
    out = jnp.where(lane == 5, r2, out)
    route_ref[...] = out


def _router(x2, g, rh, rl, tri):
    n, d = x2.shape
    tm = min(ROUTE_TM, n)
    return pl.pallas_call(
        _router_kernel,
        grid=(n // tm,),
        in_specs=[
            pl.BlockSpec((tm, d), lambda i: (i, 0)),
            pl.BlockSpec((1, d), lambda i: (0, 0)),
            pl.BlockSpec((d, LANES), lambda i: (0, 0)),
            pl.BlockSpec((d, LANES), lambda i: (0, 0)),
            pl.BlockSpec((tm, tm), lambda i: (0, 0)),
        ],
        out_specs=[pl.BlockSpec((tm, LANES), lambda i: (i, 0)),
                   pl.BlockSpec((8, LANES), lambda i: (0, 0))],
        out_shape=[jax.ShapeDtypeStruct((n, LANES), F32), jax.ShapeDtypeStruct((8, LANES), F32)],
        scratch_shapes=[pltpu.VMEM((8, LANES), F32)],
        compiler_params=_params("arbitrary"),
        name="router",
    )(x2, g, rh, rl, tri)


def _sc_gather_rows(table, idx):
    _, d = table.shape
    b = idx.shape[0]
    win = SC_GATHER_ROWS
    sc = plsc.get_sparse_core_info()
    n_workers = sc.num_cores * sc.num_subcores
    per_w = b // n_workers
    n_win = per_w // win
    assert per_w * n_workers == b and n_win * win == per_w and n_win % 2 == 0, (b, n_workers, win)
    mesh = plsc.VectorSubcoreMesh(core_axis_name="c", subcore_axis_name="s")
    dma = pltpu.SemaphoreType.DMA

    @functools.partial(
        pl.kernel, mesh=mesh, out_type=jax.ShapeDtypeStruct((b, d), table.dtype), name="sc_gather_rows",
        scratch_types=[pltpu.VMEM((win,), jnp.int32), pltpu.VMEM((win,), jnp.int32),
                       pltpu.VMEM((win, d), table.dtype), pltpu.VMEM((win, d), table.dtype),
                       dma, dma, dma, dma])
    def gather_kernel(table_hbm, idx_hbm, out_hbm, i0, i1, r0, r1, g0, g1, w0, w1):
        wid = lax.axis_index("s") * sc.num_cores + lax.axis_index("c")
        base = wid * per_w
        idxb, rows, gsem, wsem = (i0, i1), (r0, r1), (g0, g1), (w0, w1)

        def off(c):
            return pl.multiple_of(base + c * win, win)

        def gather(s):
            return pltpu.make_async_copy(table_hbm.at[idxb[s]], rows[s], gsem[s])

        def write(c, s):
            return pltpu.make_async_copy(rows[s], out_hbm.at[pl.ds(off(c), win)], wsem[s])

        pltpu.sync_copy(idx_hbm.at[pl.ds(off(0), win)], idxb[0])
        gather(0).start()

        @pl.loop(0, n_win, step=2)
        def _(c):
            for s in (0, 1):
                cc = c + s

                @pl.when(cc + 1 < n_win)
                def _():
                    @pl.when(cc >= 1)
                    def _():
                        write(cc - 1, 1 - s).wait()

                    pltpu.sync_copy(idx_hbm.at[pl.ds(off(cc + 1), win)], idxb[1 - s])
                    gather(1 - s).start()

                gather(s).wait()
                write(cc, s).start()

        write(n_win - 2, 0).wait()
        write(n_win - 1, 1).wait()

    return gather_kernel(table, idx)


def _sc_dispatch_rows(src, dest1, dest2, n_rows):
    n, d = src.shape
    win = SC_GATHER_ROWS
    sc = plsc.get_sparse_core_info()
    n_workers = sc.num_cores * sc.num_subcores
    per_w = n // n_workers
    n_win = per_w // win
    assert per_w * n_workers == n and n_win * win == per_w and n_win % 2 == 0, (n, n_workers, win)
    mesh = plsc.VectorSubcoreMesh(core_axis_name="c", subcore_axis_name="s")
    dma = pltpu.SemaphoreType.DMA
    ivec = pltpu.VMEM((win,), jnp.int32)
    rbuf = pltpu.VMEM((win, d), src.dtype)

    @functools.partial(
        pl.kernel, mesh=mesh, out_type=jax.ShapeDtypeStruct((n_rows, d), src.dtype), name="sc_dispatch_rows",
        scratch_types=[ivec, ivec, ivec, ivec, rbuf, rbuf, dma, dma, dma, dma, dma, dma])
    def dispatch_kernel(src_hbm, d1_hbm, d2_hbm, out_hbm, a0, a1, b0, b1, r0, r1, l0, l1, p0, p1, q0, q1):
        wid = lax.axis_index("s") * sc.num_cores + lax.axis_index("c")
        base = wid * per_w
        ia, ib, rows, lsem, psem, qsem = (a0, a1), (b0, b1), (r0, r1), (l0, l1), (p0, p1), (q0, q1)

        def off(c):
            return pl.multiple_of(base + c * win, win)

        def load(c, s):
            return pltpu.make_async_copy(src_hbm.at[pl.ds(off(c), win)], rows[s], lsem[s])

        def scatters(s):
            return (pltpu.make_async_copy(rows[s], out_hbm.at[ia[s]], psem[s]),
                    pltpu.make_async_copy(rows[s], out_hbm.at[ib[s]], qsem[s]))

        def fetch(c, s):
            pltpu.sync_copy(d1_hbm.at[pl.ds(off(c), win)], ia[s])
            pltpu.sync_copy(d2_hbm.at[pl.ds(off(c), win)], ib[s])
            load(c, s).start()

        fetch(0, 0)

        @pl.loop(0, n_win, step=2)
        def _(c):
            for s in (0, 1):
                cc = c + s

                @pl.when(cc + 1 < n_win)
                def _():
                    @pl.when(cc >= 1)
                    def _():
                        for cp in scatters(1 - s):
                            cp.wait()

                    fetch(cc + 1, 1 - s)

                load(cc, s).wait()
                for cp in scatters(s):
                    cp.start()

        for s in (0, 1):
            for cp in scatters(s):
                cp.wait()

    return dispatch_kernel(src, dest1, dest2)


def _moe_kernel(bexp_ref, nused_ref, nvalid_ref, x_ref, wg_ref, wu_ref, wd_ref, o_ref, xb):
    i = pl.program_id(0)
    j = pl.program_id(1)
    used = i < nused_ref[0]

    @pl.when(used & (j == 0))
    def _():
        row = lax.broadcasted_iota(jnp.int32, x_ref.shape, 0)
        xb[...] = jnp.where(row < nvalid_ref[i], x_ref[...], 0.0).astype(BF16)

    @pl.when(used)
    def _():
        x = xb[...]
        g = jnp.dot(x, wg_ref[0].astype(BF16), preferred_element_type=F32)
        u = jnp.dot(x, wu_ref[0].astype(BF16), preferred_element_type=F32)
        act = (g * jax.nn.sigmoid(g) * u).astype(BF16)
        y = jnp.dot(act, wd_ref[0].astype(BF16), preferred_element_type=F32)

        @pl.when(j == 0)
        def _():
            o_ref[...] = y

        @pl.when(j > 0)
        def _():
            o_ref[...] += y

    @pl.when(jnp.logical_not(used) & (j == 0))
    def _():
        o_ref[...] = jnp.zeros(o_ref.shape, o_ref.dtype)


def _moe_experts(block_exp, n_used, n_valid, xs, wg, wu, wd, tm):
    n_rows, d = xs.shape
    n_blocks = n_rows // tm
    f = wg.shape[2]
    tf = MOE_TF if f % MOE_TF == 0 else f
    nj = f // tf

    def live(i, j, be, nu):
        u = i < nu[0]
        return jnp.where(u, i, nu[0] - 1), jnp.where(u, j, nj - 1)

    def w_in_map(i, j, be, nu, nv):
        ii, jj = live(i, j, be, nu)
        return (be[ii], 0, jj)

    def w_down_map(i, j, be, nu, nv):
        ii, jj = live(i, j, be, nu)
        return (be[ii], jj, 0)

    def x_map(i, j, be, nu, nv):
        return (live(i, j, be, nu)[0], 0)

    def out_map(i, j, be, nu, nv):
        return (i, 0)

    grid_spec = pltpu.PrefetchScalarGridSpec(
        num_scalar_prefetch=3,
        grid=(n_blocks, nj),
        in_specs=[
            pl.BlockSpec((tm, d), x_map),
            pl.BlockSpec((1, d, tf), w_in_map),
            pl.BlockSpec((1, d, tf), w_in_map),
            pl.BlockSpec((1, tf, d), w_down_map),
        ],
        out_specs=pl.BlockSpec((tm, d), out_map),
        scratch_shapes=[pltpu.VMEM((tm, d), BF16)],
    )
    return pl.pallas_call(
        _moe_kernel,
        grid_spec=grid_spec,
        out_shape=jax.ShapeDtypeStruct((n_rows, d), F32),
        compiler_params=_params("arbitrary", "arbitrary"),
        name="moe_experts",
    )(block_exp, n_used, n_valid, xs, wg, wu, wd)


def _combine_kernel(x_ref, route_ref, a_ref, b_ref, o_ref):
    route = route_ref[...]
    o_ref[...] = x_ref[...] + route[:, 2:3] * a_ref[...] + route[:, 3:4] * b_ref[...]


def _combine(yg, x2, route):
    n, d = x2.shape
    tm = min(COMB_TM, n)
    nb = n // tm
    return pl.pallas_call(
        _combine_kernel,
        grid=(nb,),
        in_specs=[pl.BlockSpec((tm, d), lambda i: (i, 0)),
                  pl.BlockSpec((tm, LANES), lambda i: (i, 0)),
                  pl.BlockSpec((tm, d), lambda i: (i, 0)),
                  pl.BlockSpec((tm, d), lambda i: (i + nb, 0))],
        out_specs=pl.BlockSpec((tm, d), lambda i: (i, 0)),
        out_shape=jax.ShapeDtypeStruct((n, d), F32),
        compiler_params=_params("arbitrary"),
        name="moe_combine",
    )(x2, route, yg, yg)


def _top2_moe(hn, x2, g2, router, wg, wu, wd):
    n, d = x2.shape
    a = 2 * n
    tm = min(MOE_TM, n)
    r_pad = jnp.zeros((d, LANES), F32).at[:, :N_EXPERTS].set(router)
    rh, rl = _split_hi_lo(r_pad)
    rt = min(ROUTE_TM, n)
    tri = (lax.broadcasted_iota(jnp.int32, (rt, rt), 1) < lax.broadcasted_iota(jnp.int32, (rt, rt), 0)).astype(BF16)
    route, cnt = _router(x2, g2, rh, rl, tri)

    counts = cnt[0, :N_EXPERTS].astype(jnp.int32)
    padded = (counts + tm - 1) // tm * tm
    pad_ends = jnp.cumsum(padded)
    pad_starts = pad_ends - padded
    e1 = route[:, 0].astype(jnp.int32)
    e2 = route[:, 1].astype(jnp.int32)
    dest1 = pad_starts[e1] + route[:, 4].astype(jnp.int32)
    dest2 = pad_starts[e2] + route[:, 5].astype(jnp.int32)
    n_blocks = a // tm + N_EXPERTS
    n_rows = n_blocks * tm
    block_start = jnp.arange(n_blocks, dtype=jnp.int32) * tm
    block_exp = jnp.minimum(jnp.searchsorted(pad_ends, block_start, side="right"), N_EXPERTS - 1).astype(jnp.int32)
    n_used = (pad_ends[-1] // tm).astype(jnp.int32).reshape(1)
    n_valid = jnp.clip(counts[block_exp] - (block_start - pad_starts[block_exp]), 0, tm).astype(jnp.int32)

    xs = _sc_dispatch_rows(hn, dest1, dest2, n_rows)
    ys = _moe_experts(block_exp, n_used, n_valid, xs, wg, wu, wd, tm)
    yg = _sc_gather_rows(ys, jnp.concatenate([dest1, dest2]))
    return _combine(yg, x2, route)


def _block_diag(w):
    h, dh, _ = w.shape
    eye = jnp.eye(h, dtype=w.dtype)
    return jnp.einsum("hij,hg->higj", w, eye).reshape(h * dh, h * dh)


def kernel(x, norm1_g, w_in, q_norm_g, k_norm_g, conf_dw_w, conf_dw_b, conf_ln_g, conf_ln_b, gmlp_ln_g, gmlp_ln_b, gmlp_ws, gmlp_bs, lru_conv_w, lru_conv_b, lru_wa, lru_ba, lru_wx, lru_bx, lru_lambda, group_norm_g, w_out, norm2_g, ffn_w_gate, ffn_w_up, ffn_w_down, moe_router, moe_w_gate, moe_w_up, moe_w_down):
    bsz, s, d = x.shape
    n = bsz * s
    depth = w_in.shape[0]
    W = GROUP_WIDTH
    row = lambda v: v.reshape(1, -1).astype(F32)

    head_id = jnp.arange(W) // HEAD_DIM
    bd = (head_id[:, None] == head_id[None, :]).astype(BF16)
    bd2 = jnp.concatenate([bd, bd], axis=0)
    t_att = min(ATT_T, s)
    uu = (jnp.arange(t_att)[:, None] >= jnp.arange(t_att)[None, :]).astype(BF16)
    tril = jnp.tril(jnp.ones((GMLP_CHUNK, GMLP_CHUNK), dtype=bool))

    x2 = x.reshape(n, d)
    for l in range(depth):
        qg = row(jnp.tile(q_norm_g[l], GROUP_HEADS) * (HEAD_DIM ** -0.5))
        kg = row(jnp.tile(k_norm_g[l], GROUP_HEADS))
        q, k, v, rest = _in_proj(x2, row(norm1_g[l]), w_in[l].astype(BF16), qg, kg, bd2)
        gn = group_norm_g[l].reshape(N_GROUPS, 1, W)
        to3 = lambda t: t.reshape(bsz, s, t.shape[-1])
        rest3 = to3(rest)
        zmax = (ATT_ZMAX_SLACK * HEAD_DIM ** 0.5 * jnp.max(jnp.abs(q_norm_g[l] * k_norm_g[l]))).reshape(1)
        y_a = _sb_attention(zmax.astype(F32), to3(q), to3(k), to3(v), uu, gn[0])
        cw = jnp.zeros((CONF_HALO, W), F32).at[:CONF_KERNEL].set(conf_dw_w[l])
        y_b = _conformer(rest3, 0, cw, row(conf_dw_b[l]), row(conf_ln_g[l]), row(conf_ln_b[l]), gn[1])
        ws = jnp.where(tril, gmlp_ws[l], 0.0).astype(BF16)
        bs_mat = jnp.repeat(gmlp_bs[l].T, HEAD_DIM, axis=1)
        y_c = _gmlp(rest3, 2, row(gmlp_ln_g[l]), row(gmlp_ln_b[l]), ws, bs_mat, gn[2])
        lw = jnp.zeros((8, W), F32).at[:LRU_CONV].set(lru_conv_w[l])
        wax = jnp.concatenate([_block_diag(lru_wa[l]), _block_diag(lru_wx[l])], axis=1).astype(BF16)
        y_d = _rg_lru(rest3, 4, lw, row(lru_conv_b[l]), wax, row(lru_ba[l]), row(lru_bx[l]),
                      row(lru_lambda[l]), gn[3])
        ys = [t.reshape(n, W) for t in (y_a, y_b, y_c, y_d)]
        x2, hn = _out_proj(ys, w_out[l].astype(BF16), x2, row(norm2_g[l]), BF16 if l % 2 == 0 else F32)
        j = l // 2
        if l % 2 == 0:
            x2 = _dense_ffn(hn, x2, ffn_w_gate[j].astype(BF16), ffn_w_up[j].astype(BF16),
                            ffn_w_down[j].astype(BF16))
        else:
            x2 = _top2_moe(hn, x2, row(norm2_g[l]), moe_router[j], moe_w_gate[j], moe_w_up[j], moe_w_down[j])
    return x2.reshape(bsz, s, d)
```

```python
import functools

import jax
import jax.numpy as jnp
from jax import lax
from jax.experimental import pallas as pl
from jax.experimental.pallas import tpu as pltpu
from jax.experimental.pallas import tpu_sc as plsc

F32 = jnp.float32
BF16 = jnp.bfloat16

HEAD_DIM = 64
GROUP_HEADS = 4
GROUP_WIDTH = GROUP_HEADS * HEAD_DIM
N_GROUPS = 4
CONF_KERNEL = 31
GMLP_CHUNK = 128
LRU_CONV = 4
LRU_C = 8.0
N_EXPERTS = 8
EPS = 1e-6

LANES = 128
VMEM_LIMIT = 56 * 1024 * 1024

IN_TM = 1024
ATT_T = 256
ATT_UNDERFLOW = 110.0
ATT_ZMAX_SLACK = 1.05
CONF_T = 256
CONF_R = 64
CONF_HALO = 32
GMLP_T = 512
LRU_T = 256
LRU_HALO = 8
OUT_TM = 512
FFN_TM = 1024
FFN_TF = 1408
ROUTE_TM = 512
MOE_TM = 1024
MOE_TF = 512
COMB_TM = 512
SC_WINDOW_BYTES = 128 * 1024
SC_MAX_INDEX_VECTOR = 128


def _params(*sem):
    return pltpu.CompilerParams(dimension_semantics=sem, vmem_limit_bytes=VMEM_LIMIT)


def _split_hi_lo(x):
    hi = x.astype(BF16)
    lo = (x - hi.astype(F32)).astype(BF16)
    return hi, lo


SOFTPLUS_CLAMP = 60.0


def _softplus(z):
    return jnp.maximum(jnp.log(1.0 + jnp.exp(jnp.minimum(z, SOFTPLUS_CLAMP))), z)


def _gelu_tanh(x):
    c = 0.7978845608028654
    return 0.5 * x * (1.0 + jnp.tanh(c * (x + 0.044715 * (x * x * x))))


def _pack_bf16_pairs(y):
    c = y.shape[1] // 2
    bits = lax.bitcast_convert_type(y.astype(BF16).astype(F32), jnp.uint32)
    word = (bits[:, :c] >> 16) | bits[:, c:]
    return lax.bitcast_convert_type(word, jnp.int32)


def _unpack_bf16_pairs(w):
    bits = lax.bitcast_convert_type(w, jnp.uint32)
    lo = lax.bitcast_convert_type(bits << 16, F32)
    hi = lax.bitcast_convert_type(bits & jnp.uint32(0xFFFF0000), F32)
    return jnp.concatenate([lo, hi], axis=1)


def _group_rms(y, g):
    ms = jnp.mean(y * y, axis=-1, keepdims=True)
    return y * lax.rsqrt(ms + EPS) * g


def _in_proj_kernel(x_ref, g_ref, w_ref, qg_ref, kg_ref, bd_ref, q_ref, k_ref, v_ref, rest_ref):
    xf = x_ref[...]
    ms = jnp.mean(xf * xf, axis=-1, keepdims=True)
    h = (xf * lax.rsqrt(ms + EPS) * g_ref[...]).astype(BF16)
    W = GROUP_WIDTH

    def proj(c):
        return jnp.dot(h, w_ref[:, c * W:(c + 1) * W], preferred_element_type=F32)

    def head_norm(p, gain):
        hi, lo = _split_hi_lo(p * p)
        ss = jnp.dot(jnp.concatenate([hi, lo], axis=1), bd_ref[...], preferred_element_type=F32)
        return p * lax.rsqrt(ss * (1.0 / HEAD_DIM) + EPS) * gain

    q_ref[...] = head_norm(proj(0), qg_ref[...]).astype(BF16)
    k_ref[...] = head_norm(proj(1), kg_ref[...]).astype(BF16)
    v_ref[...] = proj(2).astype(BF16)
    n_rest = rest_ref.shape[1] // W
    for c in range(n_rest):
        rest_ref[:, c * W:(c + 1) * W] = proj(3 + c)


def _in_proj(x2, g, w_bf, qg, kg, bd2):
    n, d = x2.shape
    cols = w_bf.shape[1]
    W = GROUP_WIDTH
    rest_cols = cols - 3 * W
    tm = min(IN_TM, n)
    row = lambda i: (i, 0)
    const = lambda i: (0, 0)
    return pl.pallas_call(
        _in_proj_kernel,
        grid=(n // tm,),
        in_specs=[
            pl.BlockSpec((tm, d), row),
            pl.BlockSpec((1, d), const),
            pl.BlockSpec((d, cols), const),
            pl.BlockSpec((1, W), const),
            pl.BlockSpec((1, W), const),
            pl.BlockSpec((2 * W, W), const),
        ],
        out_specs=[
            pl.BlockSpec((tm, W), row),
            pl.BlockSpec((tm, W), row),
            pl.BlockSpec((tm, W), row),
            pl.BlockSpec((tm, rest_cols), row),
        ],
        out_shape=[
            jax.ShapeDtypeStruct((n, W), BF16),
            jax.ShapeDtypeStruct((n, W), BF16),
            jax.ShapeDtypeStruct((n, W), BF16),
            jax.ShapeDtypeStruct((n, rest_cols), F32),
        ],
        compiler_params=_params("arbitrary"),
        name="in_proj",
    )(x2, g, w_bf, qg, kg, bd2)


def _sb_attn_kernel(zmax_ref, q_ref, k_ref, v_ref, uu_ref, gn_ref, o_ref):
    T = q_ref.shape[1]
    n_pairs = GROUP_WIDTH // LANES
    i = pl.program_id(1)
    lane = lax.broadcasted_iota(jnp.int32, (T, LANES), 1)
    rows = lax.broadcasted_iota(jnp.int32, (2 * T, T), 0)
    cols = lax.broadcasted_iota(jnp.int32, (2 * T, T), 1)
    causal = cols < jnp.where(rows >= T, rows - T, rows)
    uu = uu_ref[...]

    def tile(qs, k2, v2, carry, mask):
        z = lax.dot_general(qs, k2, (((1,), (1,)), ((), ())), preferred_element_type=F32)
        sp = _softplus(z)
        if mask is not None:
            sp = jnp.where(mask, sp, 0.0)
        cum = jnp.dot(sp.astype(BF16), uu, preferred_element_type=F32)
        w = jnp.exp(z - cum - carry)
        if mask is not None:
            w = jnp.where(mask, w, 0.0)
        pv = jnp.dot(w.astype(BF16), v2, preferred_element_type=F32)
        return pv, carry + jnp.sum(sp, axis=1, keepdims=True)

    def kv(r0, pair):
        cs = slice(pair * LANES, (pair + 1) * LANES)
        return k_ref[0, pl.ds(r0, T), cs], v_ref[0, pl.ds(r0, T), cs]

    qs = []
    for pair in range(n_pairs):
        q2 = q_ref[0, :, pair * LANES:(pair + 1) * LANES]
        zero = jnp.zeros_like(q2)
        qs.append(jnp.concatenate([jnp.where(lane < HEAD_DIM, q2, zero),
                                   jnp.where(lane >= HEAD_DIM, q2, zero)], axis=0))
    row0 = pl.multiple_of(i * T, T)
    state = tuple(tile(qs[p], *kv(row0, p), jnp.zeros((2 * T, 1), F32), causal) for p in range(n_pairs))

    dead_at = zmax_ref[0] + ATT_UNDERFLOW

    def alive(st):
        return functools.reduce(jnp.minimum, [jnp.min(carry) for _, carry in st]) <= dead_at

    def cond(c):
        step, live, _ = c
        return (step < i) & live

    def body(c):
        step, _, st = c
        r0 = pl.multiple_of((i - 1 - step) * T, T)
        new = []
        for p in range(n_pairs):
            acc, carry = st[p]
            pv, carry = tile(qs[p], *kv(r0, p), carry, None)
            new.append((acc + pv, carry))
        new = tuple(new)
        return step + 1, alive(new), new

    _, _, state = lax.while_loop(cond, body, (jnp.int32(0), alive(state), state))
    y = jnp.concatenate([jnp.where(lane < HEAD_DIM, acc[:T], acc[T:]) for acc, _ in state], axis=1)
    o_ref[0] = _group_rms(y, gn_ref[...]).astype(o_ref.dtype)


def _sb_attention(zmax, q3, k3, v3, uu, gn):
    b, s, w = q3.shape
    t = min(ATT_T, s)
    return pl.pallas_call(
        _sb_attn_kernel,
        grid=(b, s // t),
        in_specs=[
            pl.BlockSpec(memory_space=pltpu.SMEM),
            pl.BlockSpec((1, t, w), lambda bi, i: (bi, i, 0)),
            pl.BlockSpec((1, s, w), lambda bi, i: (bi, 0, 0)),
            pl.BlockSpec((1, s, w), lambda bi, i: (bi, 0, 0)),
            pl.BlockSpec((t, t), lambda bi, i: (0, 0)),
            pl.BlockSpec((1, w), lambda bi, i: (0, 0)),
        ],
        out_specs=pl.BlockSpec((1, t, w), lambda bi, i: (bi, i, 0)),
        out_shape=jax.ShapeDtypeStruct((b, s, w), BF16),
        compiler_params=_params("arbitrary", "arbitrary"),
        name="sb_attn",
    )(zmax, q3, k3, v3, uu, gn)


def _conf_kernel(val_ref, gate_ref, w_ref, b_ref, lng_ref, lnb_ref, gn_ref, o_ref, hs_ref, sh_ref):
    T = val_ref.shape[1]
    H = CONF_HALO
    SUB = 8

    @pl.when(pl.program_id(1) == 0)
    def _():
        hs_ref[0:H, :] = jnp.zeros((H, GROUP_WIDTH), F32)

    hs_ref[H:H + T, :] = val_ref[0] * jax.nn.sigmoid(gate_ref[0])
    off = H - (CONF_KERNEL - 1)
    L = sh_ref.shape[1]
    for s in range(1, SUB):
        sh_ref[s - 1] = hs_ref[s:s + L, :]
    R = min(CONF_R, T)
    for r0 in range(0, T, R):
        acc = jnp.broadcast_to(b_ref[...], (R, GROUP_WIDTH))
        for k in range(CONF_KERNEL):
            s = (off + k) % SUB
            a = off + k - s + r0
            tap = hs_ref[a:a + R, :] if s == 0 else sh_ref[s - 1, a:a + R, :]
            acc = acc + w_ref[k:k + 1, :] * tap
        mu = jnp.mean(acc, axis=-1, keepdims=True)
        xc = acc - mu
        var = jnp.mean(xc * xc, axis=-1, keepdims=True)
        y = xc * lax.rsqrt(var + EPS) * lng_ref[...] + lnb_ref[...]
        y = y * jax.nn.sigmoid(y)
        o_ref[0, r0:r0 + R, :] = _group_rms(y, gn_ref[...]).astype(o_ref.dtype)
    hs_ref[0:H, :] = hs_ref[T:T + H, :]


def _conformer(rest3, col0, w_pad, b, lng, lnb, gn):
    bsz, s, _ = rest3.shape
    W = GROUP_WIDTH
    t = min(CONF_T, s)
    vec = pl.BlockSpec((1, W), lambda bi, i: (0, 0))
    return pl.pallas_call(
        _conf_kernel,
        grid=(bsz, s // t),
        in_specs=[
            pl.BlockSpec((1, t, W), lambda bi, i: (bi, i, col0)),
            pl.BlockSpec((1, t, W), lambda bi, i: (bi, i, col0 + 1)),
            pl.BlockSpec(w_pad.shape, lambda bi, i: (0, 0)),
            vec, vec, vec, vec,
        ],
        out_specs=pl.BlockSpec((1, t, W), lambda bi, i: (bi, i, 0)),
        out_shape=jax.ShapeDtypeStruct((bsz, s, W), BF16),
        scratch_shapes=[pltpu.VMEM((t + CONF_HALO, W), F32),
                        pltpu.VMEM((7, t + CONF_HALO - 8, W), F32)],
        compiler_params=_params("arbitrary", "arbitrary"),
        name="conformer",
    )(rest3, rest3, w_pad, b, lng, lnb, gn)


def _gmlp_kernel(u_ref, v_ref, lng_ref, lnb_ref, ws_ref, bs_ref, gn_ref, o_ref):
    T = u_ref.shape[1]
    C = GMLP_CHUNK
    lane = lax.broadcasted_iota(jnp.int32, (C, GROUP_WIDTH), 1)
    for c0 in range(0, T, C):
        u = _gelu_tanh(u_ref[0, c0:c0 + C, :])
        v = _gelu_tanh(v_ref[0, c0:c0 + C, :])
        mu = jnp.mean(v, axis=-1, keepdims=True)
        xc = v - mu
        var = jnp.mean(xc * xc, axis=-1, keepdims=True)
        vb = (xc * lax.rsqrt(var + EPS) * lng_ref[...] + lnb_ref[...]).astype(BF16)
        mixed = bs_ref[...]
        for h in range(GROUP_HEADS):
            m = jnp.dot(ws_ref[h], vb, preferred_element_type=F32)
            in_head = (lane >= h * HEAD_DIM) & (lane < (h + 1) * HEAD_DIM)
            mixed = mixed + jnp.where(in_head, m, 0.0)
        o_ref[0, c0:c0 + C, :] = _group_rms(u * mixed, gn_ref[...]).astype(o_ref.dtype)


def _gmlp(rest3, col0, lng, lnb, ws_bf, bs_mat, gn):
    bsz, s, _ = rest3.shape
    W = GROUP_WIDTH
    t = min(GMLP_T, s)
    vec = pl.BlockSpec((1, W), lambda bi, i: (0, 0))
    return pl.pallas_call(
        _gmlp_kernel,
        grid=(bsz, s // t),
        in_specs=[
            pl.BlockSpec((1, t, W), lambda bi, i: (bi, i, col0)),
            pl.BlockSpec((1, t, W), lambda bi, i: (bi, i, col0 + 1)),
            vec, vec,
            pl.BlockSpec(ws_bf.shape, lambda bi, i: (0, 0, 0)),
            pl.BlockSpec(bs_mat.shape, lambda bi, i: (0, 0)),
            vec,
        ],
        out_specs=pl.BlockSpec((1, t, W), lambda bi, i: (bi, i, 0)),
        out_shape=jax.ShapeDtypeStruct((bsz, s, W), BF16),
        compiler_params=_params("arbitrary", "arbitrary"),
        name="gmlp",
    )(rest3, rest3, lng, lnb, ws_bf, bs_mat, gn)


def _lru_kernel(x_ref, gate_ref, cw_ref, cb_ref, wax_ref, ba_ref, bx_ref, lam_ref, gn_ref, o_ref,
                xs_ref, hprev_ref):
    T = x_ref.shape[1]
    W = GROUP_WIDTH
    H = LRU_HALO

    @pl.when(pl.program_id(1) == 0)
    def _():
        xs_ref[0:H, :] = jnp.zeros((H, W), F32)
        hprev_ref[...] = jnp.zeros(hprev_ref.shape, F32)

    xs_ref[H:H + T, :] = x_ref[0]
    off = H - (LRU_CONV - 1)
    xb = jnp.broadcast_to(cb_ref[...], (T, W))
    for k in range(LRU_CONV):
        xb = xb + cw_ref[k:k + 1, :] * xs_ref[off + k:off + k + T, :]
    xs_ref[0:H, :] = xs_ref[T:T + H, :]

    pre = jnp.dot(xb.astype(BF16), wax_ref[...], preferred_element_type=F32)
    r = jax.nn.sigmoid(pre[:, :W] + ba_ref[...])
    ig = jax.nn.sigmoid(pre[:, W:] + bx_ref[...])
    nlam = -lam_ref[...]
    log_a = (-LRU_C) * r * (jnp.maximum(nlam, 0.0) + jnp.log1p(jnp.exp(-jnp.abs(nlam))))
    a = jnp.exp(log_a)
    th = jnp.tanh(log_a)
    one_minus_a2 = 2.0 * th / (th - 1.0)
    b = jnp.sqrt(one_minus_a2) * (ig * xb)

    SUB = 8
    a = a.reshape(T // SUB, SUB, W)
    b = b.reshape(T // SUB, SUB, W)
    sub = lax.broadcasted_iota(jnp.int32, a.shape, 1)
    d = 1
    while d < SUB:
        keep = sub >= d
        a_sh = jnp.where(keep, pltpu.roll(a, d, 1), 1.0)
        b_sh = jnp.where(keep, pltpu.roll(b, d, 1), 0.0)
        b = a * b_sh + b
        a = a * a_sh
        d *= 2
    carry = hprev_ref[0:1, :]
    tiles = []
    for g in range(T // SUB):
        hg = b[g] + a[g] * carry
        carry = hg[SUB - 1:SUB]
        tiles.append(hg)
    h = jnp.concatenate(tiles, axis=0)
    hprev_ref[...] = jnp.broadcast_to(carry, hprev_ref.shape)
    y = h * _gelu_tanh(gate_ref[0])
    o_ref[0] = _group_rms(y, gn_ref[...]).astype(o_ref.dtype)


def _rg_lru(rest3, col0, cw_pad, cb, wax_bf, ba, bx, lam, gn):
    bsz, s, _ = rest3.shape
    W = GROUP_WIDTH
    t = min(LRU_T, s)
    vec = pl.BlockSpec((1, W), lambda bi, i: (0, 0))
    return pl.pallas_call(
        _lru_kernel,
        grid=(bsz, s // t),
        in_specs=[
            pl.BlockSpec((1, t, W), lambda bi, i: (bi, i, col0)),
            pl.BlockSpec((1, t, W), lambda bi, i: (bi, i, col0 + 1)),
            pl.BlockSpec(cw_pad.shape, lambda bi, i: (0, 0)),
            vec,
            pl.BlockSpec(wax_bf.shape, lambda bi, i: (0, 0)),
            vec, vec, vec, vec,
        ],
        out_specs=pl.BlockSpec((1, t, W), lambda bi, i: (bi, i, 0)),
        out_shape=jax.ShapeDtypeStruct((bsz, s, W), BF16),
        scratch_shapes=[pltpu.VMEM((t + LRU_HALO, W), F32), pltpu.VMEM((8, W), F32)],
        compiler_params=_params("arbitrary", "arbitrary"),
        name="rg_lru",
    )(rest3, rest3, cw_pad, cb, wax_bf, ba, bx, lam, gn)


def _out_proj_kernel(ya_ref, yb_ref, yc_ref, yd_ref, w_ref, x_ref, g_ref, xo_ref, h_ref):
    W = GROUP_WIDTH
    acc = x_ref[...]
    for gi, y_ref in enumerate((ya_ref, yb_ref, yc_ref, yd_ref)):
        acc = acc + jnp.dot(y_ref[...], w_ref[gi * W:(gi + 1) * W, :], preferred_element_type=F32)
    xo_ref[...] = acc
    ms = jnp.mean(acc * acc, axis=-1, keepdims=True)
    h = acc * lax.rsqrt(ms + EPS) * g_ref[...]
    if h_ref.dtype == jnp.int32:
        h_ref[...] = _pack_bf16_pairs(h)
    else:
        h_ref[...] = h.astype(h_ref.dtype)


def _out_proj(ys, w_bf, x2, g, packed):
    n, d = x2.shape
    h_shape = jax.ShapeDtypeStruct((n, d // 2), jnp.int32) if packed else jax.ShapeDtypeStruct((n, d), BF16)
    W = GROUP_WIDTH
    tm = min(OUT_TM, n)
    row = lambda i: (i, 0)
    const = lambda i: (0, 0)
    ysp = pl.BlockSpec((tm, W), row)
    return pl.pallas_call(
        _out_proj_kernel,
        grid=(n // tm,),
        in_specs=[ysp, ysp, ysp, ysp,
                  pl.BlockSpec(w_bf.shape, const),
                  pl.BlockSpec((tm, d), row),
                  pl.BlockSpec((1, d), const)],
        out_specs=[pl.BlockSpec((tm, d), row), pl.BlockSpec((tm, h_shape.shape[1]), row)],
        out_shape=[jax.ShapeDtypeStruct((n, d), F32), h_shape],
        compiler_params=_params("arbitrary"),
        name="out_proj",
    )(*ys, w_bf, x2, g)


def _ffn_kernel(h_ref, x_ref, wg_ref, wu_ref, wd_ref, o_ref):
    @pl.when(pl.program_id(1) == 0)
    def _():
        o_ref[...] = x_ref[...]

    h = h_ref[...]
    g = jnp.dot(h, wg_ref[...], preferred_element_type=F32)
    u = jnp.dot(h, wu_ref[...], preferred_element_type=F32)
    act = (g * jax.nn.sigmoid(g) * u).astype(BF16)
    o_ref[...] += jnp.dot(act, wd_ref[...], preferred_element_type=F32)


def _dense_ffn(hn, x2, wg, wu, wd):
    n, d = x2.shape
    f = wg.shape[1]
    tm = min(FFN_TM, n)
    tf = FFN_TF if f % FFN_TF == 0 else f
    return pl.pallas_call(
        _ffn_kernel,
        grid=(n // tm, f // tf),
        in_specs=[
            pl.BlockSpec((tm, d), lambda i, j: (i, 0)),
            pl.BlockSpec((tm, d), lambda i, j: (i, 0)),
            pl.BlockSpec((d, tf), lambda i, j: (0, j)),
            pl.BlockSpec((d, tf), lambda i, j: (0, j)),
            pl.BlockSpec((tf, d), lambda i, j: (j, 0)),
        ],
        out_specs=pl.BlockSpec((tm, d), lambda i, j: (i, 0)),
        out_shape=jax.ShapeDtypeStruct((n, d), F32),
        compiler_params=_params("arbitrary", "arbitrary"),
        name="dense_ffn",
    )(hn, x2, wg, wu, wd)


def _router_kernel(x_ref, g_ref, rh_ref, rl_ref, tri_ref, route_ref, cnt_ref, base_ref):
    tm = x_ref.shape[0]

    @pl.when(pl.program_id(0) == 0)
    def _():
        base_ref[...] = jnp.zeros(base_ref.shape, F32)

    xf = x_ref[...]
    ms = jnp.mean(xf * xf, axis=-1, keepdims=True)
    hh, hl = _split_hi_lo(xf * lax.rsqrt(ms + EPS) * g_ref[...])
    rh = rh_ref[...]
    logits = (jnp.dot(hh, rh, preferred_element_type=F32)
              + jnp.dot(hl, rh, preferred_element_type=F32)
              + jnp.dot(hh, rl_ref[...], preferred_element_type=F32))
    lane = lax.broadcasted_iota(jnp.int32, (tm, LANES), 1).astype(F32)
    neg = jnp.float32(-jnp.inf)
    logits = jnp.where(lane < N_EXPERTS, logits, neg)
    m1 = jnp.max(logits, axis=1, keepdims=True)
    i1 = jnp.min(jnp.where(logits == m1, lane, float(LANES)), axis=1, keepdims=True)
    l2 = jnp.where(lane == i1, neg, logits)
    m2 = jnp.max(l2, axis=1, keepdims=True)
    i2 = jnp.min(jnp.where(l2 == m2, lane, float(LANES)), axis=1, keepdims=True)
    e = jnp.exp(m2 - m1)
    g1 = 1.0 / (1.0 + e)
    g2 = e / (1.0 + e)
    oh1 = jnp.where(lane == i1, 1.0, 0.0)
    oh2 = jnp.where(lane == i2, 1.0, 0.0)
    oh = oh1 + oh2
    before = jnp.dot(tri_ref[...], oh.astype(BF16), preferred_element_type=F32) + base_ref[0:1, :]
    r1 = jnp.sum(oh1 * before, axis=1, keepdims=True)
    r2 = jnp.sum(oh2 * before, axis=1, keepdims=True)
    base = base_ref[0:1, :] + jnp.sum(oh, axis=0, keepdims=True)
    base_ref[...] = jnp.broadcast_to(base, base_ref.shape)
    cnt_ref[...] = jnp.broadcast_to(base, cnt_ref.shape)
    out = jnp.where(lane == 0, i1, 0.0)
    out = jnp.where(lane == 1, i2, out)
    out = jnp.where(lane == 2, g1, out)
    out = jnp.where(lane == 3, g2, out)
    out = jnp.where(lane == 4, r1, out)
    out = jnp.where(lane == 5, r2, out)
    route_ref[...] = out


def _router(x2, g, rh, rl, tri):
    n, d = x2.shape
    tm = min(ROUTE_TM, n)
    return pl.pallas_call(
        _router_kernel,
        grid=(n // tm,),
        in_specs=[
            pl.BlockSpec((tm, d), lambda i: (i, 0)),
            pl.BlockSpec((1, d), lambda i: (0, 0)),
            pl.BlockSpec((d, LANES), lambda i: (0, 0)),
            pl.BlockSpec((d, LANES), lambda i: (0, 0)),
            pl.BlockSpec((tm, tm), lambda i: (0, 0)),
        ],
        out_specs=[pl.BlockSpec((tm, LANES), lambda i: (i, 0)),
                   pl.BlockSpec((8, LANES), lambda i: (0, 0))],
        out_shape=[jax.ShapeDtypeStruct((n, LANES), F32), jax.ShapeDtypeStruct((8, LANES), F32)],
        scratch_shapes=[pltpu.VMEM((8, LANES), F32)],
        compiler_params=_params("arbitrary"),
        name="router",
    )(x2, g, rh, rl, tri)


def _sc_window_rows(table):
    row_bytes = table.shape[1] * table.dtype.itemsize
    return min(SC_MAX_INDEX_VECTOR, SC_WINDOW_BYTES // row_bytes)


def _sc_gather_rows(table, idx):
    _, d = table.shape
    b = idx.shape[0]
    win = _sc_window_rows(table)
    sc = plsc.get_sparse_core_info()
    n_workers = sc.num_cores * sc.num_subcores
    per_w = b // n_workers
    n_win = per_w // win
    assert per_w * n_workers == b and n_win * win == per_w and n_win % 2 == 0, (b, n_workers, win)
    mesh = plsc.VectorSubcoreMesh(core_axis_name="c", subcore_axis_name="s")
    dma = pltpu.SemaphoreType.DMA

    @functools.partial(
        pl.kernel, mesh=mesh, out_type=jax.ShapeDtypeStruct((b, d), table.dtype), name="sc_gather_rows",
        scratch_types=[pltpu.VMEM((win,), jnp.int32), pltpu.VMEM((win,), jnp.int32),
                       pltpu.VMEM((win, d), table.dtype), pltpu.VMEM((win, d), table.dtype),
                       dma, dma, dma, dma])
    def gather_kernel(table_hbm, idx_hbm, out_hbm, i0, i1, r0, r1, g0, g1, w0, w1):
        wid = lax.axis_index("s") * sc.num_cores + lax.axis_index("c")
        base = wid * per_w
        idxb, rows, gsem, wsem = (i0, i1), (r0, r1), (g0, g1), (w0, w1)

        def off(c):
            return pl.multiple_of(base + c * win, win)

        def gather(s):
            return pltpu.make_async_copy(table_hbm.at[idxb[s]], rows[s], gsem[s])

        def write(c, s):
            return pltpu.make_async_copy(rows[s], out_hbm.at[pl.ds(off(c), win)], wsem[s])

        pltpu.sync_copy(idx_hbm.at[pl.ds(off(0), win)], idxb[0])
        gather(0).start()

        @pl.loop(0, n_win, step=2)
        def _(c):
            for s in (0, 1):
                cc = c + s

                @pl.when(cc + 1 < n_win)
                def _():
                    @pl.when(cc >= 1)
                    def _():
                        write(cc - 1, 1 - s).wait()

                    pltpu.sync_copy(idx_hbm.at[pl.ds(off(cc + 1), win)], idxb[1 - s])
                    gather(1 - s).start()

                gather(s).wait()
                write(cc, s).start()

        write(n_win - 2, 0).wait()
        write(n_win - 1, 1).wait()

    return gather_kernel(table, idx)


def _sc_dispatch_rows(src, dest1, dest2, n_rows):
    n, d = src.shape
    win = _sc_window_rows(src)
    sc = plsc.get_sparse_core_info()
    n_workers = sc.num_cores * sc.num_subcores
    per_w = n // n_workers
    n_win = per_w // win
    assert per_w * n_workers == n and n_win * win == per_w and n_win % 2 == 0, (n, n_workers, win)
    mesh = plsc.VectorSubcoreMesh(core_axis_name="c", subcore_axis_name="s")
    dma = pltpu.SemaphoreType.DMA
    ivec = pltpu.VMEM((win,), jnp.int32)
    rbuf = pltpu.VMEM((win, d), src.dtype)

    @functools.partial(
        pl.kernel, mesh=mesh, out_type=jax.ShapeDtypeStruct((n_rows, d), src.dtype), name="sc_dispatch_rows",
        scratch_types=[ivec, ivec, ivec, ivec, rbuf, rbuf, dma, dma, dma, dma, dma, dma])
    def dispatch_kernel(src_hbm, d1_hbm, d2_hbm, out_hbm, a0, a1, b0, b1, r0, r1, l0, l1, p0, p1, q0, q1):
        wid = lax.axis_index("s") * sc.num_cores + lax.axis_index("c")
        base = wid * per_w
        ia, ib, rows, lsem, psem, qsem = (a0, a1), (b0, b1), (r0, r1), (l0, l1), (p0, p1), (q0, q1)

        def off(c):
            return pl.multiple_of(base + c * win, win)

        def load(c, s):
            return pltpu.make_async_copy(src_hbm.at[pl.ds(off(c), win)], rows[s], lsem[s])

        def scatters(s):
            return (pltpu.make_async_copy(rows[s], out_hbm.at[ia[s]], psem[s]),
                    pltpu.make_async_copy(rows[s], out_hbm.at[ib[s]], qsem[s]))

        def fetch(c, s):
            pltpu.sync_copy(d1_hbm.at[pl.ds(off(c), win)], ia[s])
            pltpu.sync_copy(d2_hbm.at[pl.ds(off(c), win)], ib[s])
            load(c, s).start()

        fetch(0, 0)

        @pl.loop(0, n_win, step=2)
        def _(c):
            for s in (0, 1):
                cc = c + s

                @pl.when(cc + 1 < n_win)
                def _():
                    @pl.when(cc >= 1)
                    def _():
                        for cp in scatters(1 - s):
                            cp.wait()

                    fetch(cc + 1, 1 - s)

                load(cc, s).wait()
                for cp in scatters(s):
                    cp.start()

        for s in (0, 1):
            for cp in scatters(s):
                cp.wait()

    return dispatch_kernel(src, dest1, dest2)


def _moe_kernel(bexp_ref, nused_ref, nvalid_ref, x_ref, wg_ref, wu_ref, wd_ref, o_ref, xb, acc):
    i = pl.program_id(0)
    j = pl.program_id(1)
    nj = pl.num_programs(1)
    used = i < nused_ref[0]

    @pl.when(used & (j == 0))
    def _():
        x = _unpack_bf16_pairs(x_ref[...])
        row = lax.broadcasted_iota(jnp.int32, x.shape, 0)
        xb[...] = jnp.where(row < nvalid_ref[i], x, 0.0).astype(BF16)

    @pl.when(used)
    def _():
        x = xb[...]
        g = jnp.dot(x, wg_ref[0].astype(BF16), preferred_element_type=F32)
        u = jnp.dot(x, wu_ref[0].astype(BF16), preferred_element_type=F32)
        act = (g * jax.nn.sigmoid(g) * u).astype(BF16)
        y = jnp.dot(act, wd_ref[0].astype(BF16), preferred_element_type=F32)

        @pl.when(j == 0)
        def _():
            acc[...] = y

        @pl.when((j > 0) & (j < nj - 1))
        def _():
            acc[...] += y

        @pl.when(j == nj - 1)
        def _():
            o_ref[...] = _pack_bf16_pairs(acc[...] + y)

    @pl.when(jnp.logical_not(used) & (j == 0))
    def _():
        o_ref[...] = jnp.zeros(o_ref.shape, o_ref.dtype)


def _moe_experts(block_exp, n_used, n_valid, xs, wg, wu, wd, tm):
    n_rows, dp = xs.shape
    d = 2 * dp
    n_blocks = n_rows // tm
    f = wg.shape[2]
    tf = MOE_TF if f % MOE_TF == 0 else f
    nj = f // tf
    assert nj >= 2

    def live(i, j, be, nu):
        u = i < nu[0]
        return jnp.where(u, i, nu[0] - 1), jnp.where(u, j, nj - 1)

    def w_in_map(i, j, be, nu, nv):
        ii, jj = live(i, j, be, nu)
        return (be[ii], 0, jj)

    def w_down_map(i, j, be, nu, nv):
        ii, jj = live(i, j, be, nu)
        return (be[ii], jj, 0)

    def x_map(i, j, be, nu, nv):
        return (live(i, j, be, nu)[0], 0)

    def out_map(i, j, be, nu, nv):
        return (i, 0)

    grid_spec = pltpu.PrefetchScalarGridSpec(
        num_scalar_prefetch=3,
        grid=(n_blocks, nj),
        in_specs=[
            pl.BlockSpec((tm, dp), x_map),
            pl.BlockSpec((1, d, tf), w_in_map),
            pl.BlockSpec((1, d, tf), w_in_map),
            pl.BlockSpec((1, tf, d), w_down_map),
        ],
        out_specs=pl.BlockSpec((tm, dp), out_map),
        scratch_shapes=[pltpu.VMEM((tm, d), BF16), pltpu.VMEM((tm, d), F32)],
    )
    return pl.pallas_call(
        _moe_kernel,
        grid_spec=grid_spec,
        out_shape=jax.ShapeDtypeStruct((n_rows, dp), jnp.int32),
        compiler_params=_params("arbitrary", "arbitrary"),
        name="moe_experts",
    )(block_exp, n_used, n_valid, xs, wg, wu, wd)


def _combine_kernel(x_ref, route_ref, a_ref, b_ref, o_ref):
    route = route_ref[...]
    ya = _unpack_bf16_pairs(a_ref[...])
    yb = _unpack_bf16_pairs(b_ref[...])
    o_ref[...] = x_ref[...] + route[:, 2:3] * ya + route[:, 3:4] * yb


def _combine(yg, x2, route):
    n, d = x2.shape
    dp = yg.shape[1]
    tm = min(COMB_TM, n)
    nb = n // tm
    return pl.pallas_call(
        _combine_kernel,
        grid=(nb,),
        in_specs=[pl.BlockSpec((tm, d), lambda i: (i, 0)),
                  pl.BlockSpec((tm, LANES), lambda i: (i, 0)),
                  pl.BlockSpec((tm, dp), lambda i: (i, 0)),
                  pl.BlockSpec((tm, dp), lambda i: (i + nb, 0))],
        out_specs=pl.BlockSpec((tm, d), lambda i: (i, 0)),
        out_shape=jax.ShapeDtypeStruct((n, d), F32),
        compiler_params=_params("arbitrary"),
        name="moe_combine",
    )(x2, route, yg, yg)


def _top2_moe(hn, x2, g2, router, wg, wu, wd):
    n, d = x2.shape
    a = 2 * n
    tm = min(MOE_TM, n)
    r_pad = jnp.zeros((d, LANES), F32).at[:, :N_EXPERTS].set(router)
    rh, rl = _split_hi_lo(r_pad)
    rt = min(ROUTE_TM, n)
    tri = (lax.broadcasted_iota(jnp.int32, (rt, rt), 1) < lax.broadcasted_iota(jnp.int32, (rt, rt), 0)).astype(BF16)
    route, cnt = _router(x2, g2, rh, rl, tri)

    counts = cnt[0, :N_EXPERTS].astype(jnp.int32)
    padded = (counts + tm - 1) // tm * tm
    pad_ends = jnp.cumsum(padded)
    pad_starts = pad_ends - padded
    def lookup(table, idx):
        hit = idx[:, None] == jnp.arange(N_EXPERTS, dtype=jnp.int32)[None, :]
        return jnp.sum(jnp.where(hit, table[None, :], 0), axis=1)

    cols = route[:, :8].astype(jnp.int32)
    dest1 = lookup(pad_starts, cols[:, 0]) + cols[:, 4]
    dest2 = lookup(pad_starts, cols[:, 1]) + cols[:, 5]
    n_blocks = a // tm + N_EXPERTS
    n_rows = n_blocks * tm
    block_start = jnp.arange(n_blocks, dtype=jnp.int32) * tm
    block_exp = jnp.minimum(jnp.sum(block_start[:, None] >= pad_ends[None, :], axis=1), N_EXPERTS - 1).astype(jnp.int32)
    n_used = (pad_ends[-1] // tm).astype(jnp.int32).reshape(1)
    n_valid = jnp.clip(lookup(counts, block_exp) - (block_start - lookup(pad_starts, block_exp)), 0, tm).astype(jnp.int32)

    xs = _sc_dispatch_rows(hn, dest1, dest2, n_rows)
    ys = _moe_experts(block_exp, n_used, n_valid, xs, wg, wu, wd, tm)
    yg = _sc_gather_rows(ys, jnp.concatenate([dest1, dest2]))
    return _combine(yg, x2, route)


def _block_diag(w):
    h, dh, _ = w.shape
    eye = jnp.eye(h, dtype=w.dtype)
    return jnp.einsum("hij,hg->higj", w, eye).reshape(h * dh, h * dh)


def kernel(x, norm1_g, w_in, q_norm_g, k_norm_g, conf_dw_w, conf_dw_b, conf_ln_g, conf_ln_b, gmlp_ln_g, gmlp_ln_b, gmlp_ws, gmlp_bs, lru_conv_w, lru_conv_b, lru_wa, lru_ba, lru_wx, lru_bx, lru_lambda, group_norm_g, w_out, norm2_g, ffn_w_gate, ffn_w_up, ffn_w_down, moe_router, moe_w_gate, moe_w_up, moe_w_down):
    bsz, s, d = x.shape
    n = bsz * s
    depth = w_in.shape[0]
    W = GROUP_WIDTH
    row = lambda v: v.reshape(1, -1).astype(F32)

    head_id = jnp.arange(W) // HEAD_DIM
    bd = (head_id[:, None] == head_id[None, :]).astype(BF16)
    bd2 = jnp.concatenate([bd, bd], axis=0)
    t_att = min(ATT_T, s)
    uu = (jnp.arange(t_att)[:, None] >= jnp.arange(t_att)[None, :]).astype(BF16)
    tril = jnp.tril(jnp.ones((GMLP_CHUNK, GMLP_CHUNK), dtype=bool))

    x2 = x.reshape(n, d)
    for l in range(depth):
        qg = row(jnp.tile(q_norm_g[l], GROUP_HEADS) * (HEAD_DIM ** -0.5))
        kg = row(jnp.tile(k_norm_g[l], GROUP_HEADS))
        q, k, v, rest = _in_proj(x2, row(norm1_g[l]), w_in[l].astype(BF16), qg, kg, bd2)
        gn = group_norm_g[l].reshape(N_GROUPS, 1, W)
        to3 = lambda t: t.reshape(bsz, s, t.shape[-1])
        rest3 = to3(rest)
        zmax = (ATT_ZMAX_SLACK * HEAD_DIM ** 0.5 * jnp.max(jnp.abs(q_norm_g[l] * k_norm_g[l]))).reshape(1)
        y_a = _sb_attention(zmax.astype(F32), to3(q), to3(k), to3(v), uu, gn[0])
        cw = jnp.zeros((CONF_HALO, W), F32).at[:CONF_KERNEL].set(conf_dw_w[l])
        y_b = _conformer(rest3, 0, cw, row(conf_dw_b[l]), row(conf_ln_g[l]), row(conf_ln_b[l]), gn[1])
        ws = jnp.where(tril, gmlp_ws[l], 0.0).astype(BF16)
        bs_mat = jnp.repeat(gmlp_bs[l].T, HEAD_DIM, axis=1)
        y_c = _gmlp(rest3, 2, row(gmlp_ln_g[l]), row(gmlp_ln_b[l]), ws, bs_mat, gn[2])
        lw = jnp.zeros((8, W), F32).at[:LRU_CONV].set(lru_conv_w[l])
        wax = jnp.concatenate([_block_diag(lru_wa[l]), _block_diag(lru_wx[l])], axis=1).astype(BF16)
        y_d = _rg_lru(rest3, 4, lw, row(lru_conv_b[l]), wax, row(lru_ba[l]), row(lru_bx[l]),
                      row(lru_lambda[l]), gn[3])
        ys = [t.reshape(n, W) for t in (y_a, y_b, y_c, y_d)]
        x2, hn = _out_proj(ys, w_out[l].astype(BF16), x2, row(norm2_g[l]), packed=(l % 2 == 1))
        j = l // 2
        if l % 2 == 0:
            x2 = _dense_ffn(hn, x2, ffn_w_gate[j].astype(BF16), ffn_w_up[j].astype(BF16),
                            ffn_w_down[j].astype(BF16))
        else:
            x2 = _top2_moe(hn, x2, row(norm2_g[l]), moe_router[j], moe_w_gate[j], moe_w_up[j], moe_w_down[j])
    return x2.reshape(bsz, s, d)
```

```python
import functools

import jax
import jax.numpy as jnp
from jax import lax
from jax.experimental import pallas as pl
from jax.experimental.pallas import tpu as pltpu
from jax.experimental.pallas import tpu_sc as plsc

F32 = jnp.float32
BF16 = jnp.bfloat16

HEAD_DIM = 64
GROUP_HEADS = 4
GROUP_WIDTH = GROUP_HEADS * HEAD_DIM
N_GROUPS = 4
CONF_KERNEL = 31
GMLP_CHUNK = 128
LRU_CONV = 4
LRU_C = 8.0
N_EXPERTS = 8
EPS = 1e-6

LANES = 128
VMEM_LIMIT = 56 * 1024 * 1024

IN_TM = 1024
ATT_T = 256
ATT_UNDERFLOW = 110.0
ATT_ZMAX_SLACK = 1.05
CONF_T = 256
CONF_R = 64
CONF_HALO = 32
GMLP_T = 512
LRU_T = 256
LRU_HALO = 8
OUT_TM = 512
FFN_TM = 1024
FFN_TF = 256
ROUTE_TM = 512
ROUTE_IDX_COLS = 8
MOE_TM = 1024
MOE_TF = 512
COMB_TM = 512
COMBINE_PARTS = 2
SC_WINDOW_BYTES = 128 * 1024
SC_MAX_INDEX_VECTOR = 128


def _params(*sem):
    return pltpu.CompilerParams(dimension_semantics=sem, vmem_limit_bytes=VMEM_LIMIT)


def _split_hi_lo(x):
    hi = x.astype(BF16)
    lo = (x - hi.astype(F32)).astype(BF16)
    return hi, lo


SOFTPLUS_CLAMP = 60.0


def _softplus(z):
    return jnp.maximum(jnp.log(1.0 + jnp.exp(jnp.minimum(z, SOFTPLUS_CLAMP))), z)


def _gelu_tanh(x):
    c = 0.7978845608028654
    return 0.5 * x * (1.0 + jnp.tanh(c * (x + 0.044715 * (x * x * x))))


def _pack_bf16_pairs(y):
    c = y.shape[1] // 2
    bits = lax.bitcast_convert_type(y.astype(BF16).astype(F32), jnp.uint32)
    word = (bits[:, :c] >> 16) | bits[:, c:]
    return lax.bitcast_convert_type(word, jnp.int32)


def _unpack_bf16_pairs(w):
    bits = lax.bitcast_convert_type(w, jnp.uint32)
    lo = lax.bitcast_convert_type(bits << 16, F32)
    hi = lax.bitcast_convert_type(bits & jnp.uint32(0xFFFF0000), F32)
    return jnp.concatenate([lo, hi], axis=1)


def _group_rms(y, g):
    ms = jnp.mean(y * y, axis=-1, keepdims=True)
    return y * lax.rsqrt(ms + EPS) * g


def _in_proj_kernel(x_ref, g_ref, w_ref, qg_ref, kg_ref, bd_ref, q_ref, k_ref, v_ref, rest_ref):
    xf = x_ref[...]
    ms = jnp.mean(xf * xf, axis=-1, keepdims=True)
    h = (xf * lax.rsqrt(ms + EPS) * g_ref[...]).astype(BF16)
    W = GROUP_WIDTH

    def proj(c):
        return jnp.dot(h, w_ref[:, c * W:(c + 1) * W], preferred_element_type=F32)

    def head_norm(p, gain):
        hi, lo = _split_hi_lo(p * p)
        ss = jnp.dot(jnp.concatenate([hi, lo], axis=1), bd_ref[...], preferred_element_type=F32)
        return p * lax.rsqrt(ss * (1.0 / HEAD_DIM) + EPS) * gain

    q_ref[...] = head_norm(proj(0), qg_ref[...]).astype(BF16)
    k_ref[...] = head_norm(proj(1), kg_ref[...]).astype(BF16)
    v_ref[...] = proj(2).astype(BF16)
    n_rest = rest_ref.shape[1] // W
    for c in range(n_rest):
        rest_ref[:, c * W:(c + 1) * W] = proj(3 + c)


def _in_proj(x2, g, w_bf, qg, kg, bd2):
    n, d = x2.shape
    cols = w_bf.shape[1]
    W = GROUP_WIDTH
    rest_cols = cols - 3 * W
    tm = min(IN_TM, n)
    row = lambda i: (i, 0)
    const = lambda i: (0, 0)
    return pl.pallas_call(
        _in_proj_kernel,
        grid=(n // tm,),
        in_specs=[
            pl.BlockSpec((tm, d), row),
            pl.BlockSpec((1, d), const),
            pl.BlockSpec((d, cols), const),
            pl.BlockSpec((1, W), const),
            pl.BlockSpec((1, W), const),
            pl.BlockSpec((2 * W, W), const),
        ],
        out_specs=[
            pl.BlockSpec((tm, W), row),
            pl.BlockSpec((tm, W), row),
            pl.BlockSpec((tm, W), row),
            pl.BlockSpec((tm, rest_cols), row),
        ],
        out_shape=[
            jax.ShapeDtypeStruct((n, W), BF16),
            jax.ShapeDtypeStruct((n, W), BF16),
            jax.ShapeDtypeStruct((n, W), BF16),
            jax.ShapeDtypeStruct((n, rest_cols), F32),
        ],
        compiler_params=_params("arbitrary"),
        name="in_proj",
    )(x2, g, w_bf, qg, kg, bd2)


def _sb_attn_kernel(zmax_ref, q_ref, k_ref, v_ref, uu_ref, gn_ref, o_ref):
    T = q_ref.shape[1]
    n_pairs = GROUP_WIDTH // LANES
    i = pl.program_id(1)
    lane = lax.broadcasted_iota(jnp.int32, (T, LANES), 1)
    rows = lax.broadcasted_iota(jnp.int32, (2 * T, T), 0)
    cols = lax.broadcasted_iota(jnp.int32, (2 * T, T), 1)
    causal = cols < jnp.where(rows >= T, rows - T, rows)
    uu = uu_ref[...]

    def tile(qs, k2, v2, carry, mask):
        z = lax.dot_general(qs, k2, (((1,), (1,)), ((), ())), preferred_element_type=F32)
        sp = _softplus(z)
        if mask is not None:
            sp = jnp.where(mask, sp, 0.0)
        cum = jnp.dot(sp.astype(BF16), uu, preferred_element_type=F32)
        w = jnp.exp(z - cum - carry)
        if mask is not None:
            w = jnp.where(mask, w, 0.0)
        pv = jnp.dot(w.astype(BF16), v2, preferred_element_type=F32)
        return pv, carry + jnp.sum(sp, axis=1, keepdims=True)

    def kv(r0, pair):
        cs = slice(pair * LANES, (pair + 1) * LANES)
        return k_ref[0, pl.ds(r0, T), cs], v_ref[0, pl.ds(r0, T), cs]

    qs = []
    for pair in range(n_pairs):
        q2 = q_ref[0, :, pair * LANES:(pair + 1) * LANES]
        zero = jnp.zeros_like(q2)
        qs.append(jnp.concatenate([jnp.where(lane < HEAD_DIM, q2, zero),
                                   jnp.where(lane >= HEAD_DIM, q2, zero)], axis=0))
    row0 = pl.multiple_of(i * T, T)
    state = tuple(tile(qs[p], *kv(row0, p), jnp.zeros((2 * T, 1), F32), causal) for p in range(n_pairs))

    dead_at = zmax_ref[0] + ATT_UNDERFLOW

    def alive(st):
        return functools.reduce(jnp.minimum, [jnp.min(carry) for _, carry in st]) <= dead_at

    def cond(c):
        step, live, _ = c
        return (step < i) & live

    def body(c):
        step, _, st = c
        r0 = pl.multiple_of((i - 1 - step) * T, T)
        new = []
        for p in range(n_pairs):
            acc, carry = st[p]
            pv, carry = tile(qs[p], *kv(r0, p), carry, None)
            new.append((acc + pv, carry))
        new = tuple(new)
        return step + 1, alive(new), new

    _, _, state = lax.while_loop(cond, body, (jnp.int32(0), alive(state), state))
    y = jnp.concatenate([jnp.where(lane < HEAD_DIM, acc[:T], acc[T:]) for acc, _ in state], axis=1)
    o_ref[0] = _group_rms(y, gn_ref[...]).astype(o_ref.dtype)


def _sb_attention(zmax, q3, k3, v3, uu, gn):
    b, s, w = q3.shape
    t = min(ATT_T, s)
    return pl.pallas_call(
        _sb_attn_kernel,
        grid=(b, s // t),
        in_specs=[
            pl.BlockSpec(memory_space=pltpu.SMEM),
            pl.BlockSpec((1, t, w), lambda bi, i: (bi, i, 0)),
            pl.BlockSpec((1, s, w), lambda bi, i: (bi, 0, 0)),
            pl.BlockSpec((1, s, w), lambda bi, i: (bi, 0, 0)),
            pl.BlockSpec((t, t), lambda bi, i: (0, 0)),
            pl.BlockSpec((1, w), lambda bi, i: (0, 0)),
        ],
        out_specs=pl.BlockSpec((1, t, w), lambda bi, i: (bi, i, 0)),
        out_shape=jax.ShapeDtypeStruct((b, s, w), BF16),
        compiler_params=_params("arbitrary", "arbitrary"),
        name="sb_attn",
    )(zmax, q3, k3, v3, uu, gn)


def _conf_kernel(val_ref, gate_ref, w_ref, b_ref, lng_ref, lnb_ref, gn_ref, o_ref, hs_ref, sh_ref):
    T = val_ref.shape[1]
    H = CONF_HALO
    SUB = 8

    @pl.when(pl.program_id(1) == 0)
    def _():
        hs_ref[0:H, :] = jnp.zeros((H, GROUP_WIDTH), F32)

    hs_ref[H:H + T, :] = val_ref[0] * jax.nn.sigmoid(gate_ref[0])
    off = H - (CONF_KERNEL - 1)
    L = sh_ref.shape[1]
    for s in range(1, SUB):
        sh_ref[s - 1] = hs_ref[s:s + L, :]
    R = min(CONF_R, T)
    for r0 in range(0, T, R):
        acc = jnp.broadcast_to(b_ref[...], (R, GROUP_WIDTH))
        for k in range(CONF_KERNEL):
            s = (off + k) % SUB
            a = off + k - s + r0
            tap = hs_ref[a:a + R, :] if s == 0 else sh_ref[s - 1, a:a + R, :]
            acc = acc + w_ref[k:k + 1, :] * tap
        mu = jnp.mean(acc, axis=-1, keepdims=True)
        xc = acc - mu
        var = jnp.mean(xc * xc, axis=-1, keepdims=True)
        y = xc * lax.rsqrt(var + EPS) * lng_ref[...] + lnb_ref[...]
        y = y * jax.nn.sigmoid(y)
        o_ref[0, r0:r0 + R, :] = _group_rms(y, gn_ref[...]).astype(o_ref.dtype)
    hs_ref[0:H, :] = hs_ref[T:T + H, :]


def _conformer(rest3, col0, w_pad, b, lng, lnb, gn):
    bsz, s, _ = rest3.shape
    W = GROUP_WIDTH
    t = min(CONF_T, s)
    vec = pl.BlockSpec((1, W), lambda bi, i: (0, 0))
    return pl.pallas_call(
        _conf_kernel,
        grid=(bsz, s // t),
        in_specs=[
            pl.BlockSpec((1, t, W), lambda bi, i: (bi, i, col0)),
            pl.BlockSpec((1, t, W), lambda bi, i: (bi, i, col0 + 1)),
            pl.BlockSpec(w_pad.shape, lambda bi, i: (0, 0)),
            vec, vec, vec, vec,
        ],
        out_specs=pl.BlockSpec((1, t, W), lambda bi, i: (bi, i, 0)),
        out_shape=jax.ShapeDtypeStruct((bsz, s, W), BF16),
        scratch_shapes=[pltpu.VMEM((t + CONF_HALO, W), F32),
                        pltpu.VMEM((7, t + CONF_HALO - 8, W), F32)],
        compiler_params=_params("arbitrary", "arbitrary"),
        name="conformer",
    )(rest3, rest3, w_pad, b, lng, lnb, gn)


def _gmlp_kernel(u_ref, v_ref, lng_ref, lnb_ref, ws_ref, bs_ref, gn_ref, o_ref):
    T = u_ref.shape[1]
    C = GMLP_CHUNK
    lane = lax.broadcasted_iota(jnp.int32, (C, GROUP_WIDTH), 1)
    for c0 in range(0, T, C):
        u = _gelu_tanh(u_ref[0, c0:c0 + C, :])
        v = _gelu_tanh(v_ref[0, c0:c0 + C, :])
        mu = jnp.mean(v, axis=-1, keepdims=True)
        xc = v - mu
        var = jnp.mean(xc * xc, axis=-1, keepdims=True)
        vb = (xc * lax.rsqrt(var + EPS) * lng_ref[...] + lnb_ref[...]).astype(BF16)
        mixed = bs_ref[...]
        for h in range(GROUP_HEADS):
            m = jnp.dot(ws_ref[h], vb, preferred_element_type=F32)
            in_head = (lane >= h * HEAD_DIM) & (lane < (h + 1) * HEAD_DIM)
            mixed = mixed + jnp.where(in_head, m, 0.0)
        o_ref[0, c0:c0 + C, :] = _group_rms(u * mixed, gn_ref[...]).astype(o_ref.dtype)


def _gmlp(rest3, col0, lng, lnb, ws_bf, bs_mat, gn):
    bsz, s, _ = rest3.shape
    W = GROUP_WIDTH
    t = min(GMLP_T, s)
    vec = pl.BlockSpec((1, W), lambda bi, i: (0, 0))
    return pl.pallas_call(
        _gmlp_kernel,
        grid=(bsz, s // t),
        in_specs=[
            pl.BlockSpec((1, t, W), lambda bi, i: (bi, i, col0)),
            pl.BlockSpec((1, t, W), lambda bi, i: (bi, i, col0 + 1)),
            vec, vec,
            pl.BlockSpec(ws_bf.shape, lambda bi, i: (0, 0, 0)),
            pl.BlockSpec(bs_mat.shape, lambda bi, i: (0, 0)),
            vec,
        ],
        out_specs=pl.BlockSpec((1, t, W), lambda bi, i: (bi, i, 0)),
        out_shape=jax.ShapeDtypeStruct((bsz, s, W), BF16),
        compiler_params=_params("arbitrary", "arbitrary"),
        name="gmlp",
    )(rest3, rest3, lng, lnb, ws_bf, bs_mat, gn)


def _lru_kernel(x_ref, gate_ref, cw_ref, cb_ref, wax_ref, ba_ref, bx_ref, lam_ref, gn_ref, o_ref,
                xs_ref, hprev_ref):
    T = x_ref.shape[1]
    W = GROUP_WIDTH
    H = LRU_HALO

    @pl.when(pl.program_id(1) == 0)
    def _():
        xs_ref[0:H, :] = jnp.zeros((H, W), F32)
        hprev_ref[...] = jnp.zeros(hprev_ref.shape, F32)

    xs_ref[H:H + T, :] = x_ref[0]
    off = H - (LRU_CONV - 1)
    xb = jnp.broadcast_to(cb_ref[...], (T, W))
    for k in range(LRU_CONV):
        xb = xb + cw_ref[k:k + 1, :] * xs_ref[off + k:off + k + T, :]
    xs_ref[0:H, :] = xs_ref[T:T + H, :]

    pre = jnp.dot(xb.astype(BF16), wax_ref[...], preferred_element_type=F32)
    r = jax.nn.sigmoid(pre[:, :W] + ba_ref[...])
    ig = jax.nn.sigmoid(pre[:, W:] + bx_ref[...])
    nlam = -lam_ref[...]
    log_a = (-LRU_C) * r * (jnp.maximum(nlam, 0.0) + jnp.log1p(jnp.exp(-jnp.abs(nlam))))
    a = jnp.exp(log_a)
    th = jnp.tanh(log_a)
    one_minus_a2 = 2.0 * th / (th - 1.0)
    b = jnp.sqrt(one_minus_a2) * (ig * xb)

    SUB = 8
    a = a.reshape(T // SUB, SUB, W)
    b = b.reshape(T // SUB, SUB, W)
    sub = lax.broadcasted_iota(jnp.int32, a.shape, 1)
    d = 1
    while d < SUB:
        keep = sub >= d
        a_sh = jnp.where(keep, pltpu.roll(a, d, 1), 1.0)
        b_sh = jnp.where(keep, pltpu.roll(b, d, 1), 0.0)
        b = a * b_sh + b
        a = a * a_sh
        d *= 2
    carry = hprev_ref[0:1, :]
    tiles = []
    for g in range(T // SUB):
        hg = b[g] + a[g] * carry
        carry = hg[SUB - 1:SUB]
        tiles.append(hg)
    h = jnp.concatenate(tiles, axis=0)
    hprev_ref[...] = jnp.broadcast_to(carry, hprev_ref.shape)
    y = h * _gelu_tanh(gate_ref[0])
    o_ref[0] = _group_rms(y, gn_ref[...]).astype(o_ref.dtype)


def _rg_lru(rest3, col0, cw_pad, cb, wax_bf, ba, bx, lam, gn):
    bsz, s, _ = rest3.shape
    W = GROUP_WIDTH
    t = min(LRU_T, s)
    vec = pl.BlockSpec((1, W), lambda bi, i: (0, 0))
    return pl.pallas_call(
        _lru_kernel,
        grid=(bsz, s // t),
        in_specs=[
            pl.BlockSpec((1, t, W), lambda bi, i: (bi, i, col0)),
            pl.BlockSpec((1, t, W), lambda bi, i: (bi, i, col0 + 1)),
            pl.BlockSpec(cw_pad.shape, lambda bi, i: (0, 0)),
            vec,
            pl.BlockSpec(wax_bf.shape, lambda bi, i: (0, 0)),
            vec, vec, vec, vec,
        ],
        out_specs=pl.BlockSpec((1, t, W), lambda bi, i: (bi, i, 0)),
        out_shape=jax.ShapeDtypeStruct((bsz, s, W), BF16),
        scratch_shapes=[pltpu.VMEM((t + LRU_HALO, W), F32), pltpu.VMEM((8, W), F32)],
        compiler_params=_params("arbitrary", "arbitrary"),
        name="rg_lru",
    )(rest3, rest3, cw_pad, cb, wax_bf, ba, bx, lam, gn)


def _out_proj_kernel(ya_ref, yb_ref, yc_ref, yd_ref, w_ref, x_ref, g_ref, xo_ref, h_ref):
    W = GROUP_WIDTH
    acc = x_ref[...]
    for gi, y_ref in enumerate((ya_ref, yb_ref, yc_ref, yd_ref)):
        acc = acc + jnp.dot(y_ref[...], w_ref[gi * W:(gi + 1) * W, :], preferred_element_type=F32)
    xo_ref[...] = acc
    ms = jnp.mean(acc * acc, axis=-1, keepdims=True)
    h = acc * lax.rsqrt(ms + EPS) * g_ref[...]
    if h_ref.dtype == jnp.int32:
        h_ref[...] = _pack_bf16_pairs(h)
    else:
        h_ref[...] = h.astype(h_ref.dtype)


def _out_proj(ys, w_bf, x2, g, packed):
    n, d = x2.shape
    h_shape = jax.ShapeDtypeStruct((n, d // 2), jnp.int32) if packed else jax.ShapeDtypeStruct((n, d), BF16)
    W = GROUP_WIDTH
    tm = min(OUT_TM, n)
    row = lambda i: (i, 0)
    const = lambda i: (0, 0)
    ysp = pl.BlockSpec((tm, W), row)
    return pl.pallas_call(
        _out_proj_kernel,
        grid=(n // tm,),
        in_specs=[ysp, ysp, ysp, ysp,
                  pl.BlockSpec(w_bf.shape, const),
                  pl.BlockSpec((tm, d), row),
                  pl.BlockSpec((1, d), const)],
        out_specs=[pl.BlockSpec((tm, d), row), pl.BlockSpec((tm, h_shape.shape[1]), row)],
        out_shape=[jax.ShapeDtypeStruct((n, d), F32), h_shape],
        compiler_params=_params("arbitrary"),
        name="out_proj",
    )(*ys, w_bf, x2, g)


def _ffn_kernel(h_ref, x_ref, wg_ref, wu_ref, wd_ref, o_ref, act_ref):
    f = wg_ref.shape[1]
    h = h_ref[...]
    for c0 in range(0, f, FFN_TF):
        g = jnp.dot(h, wg_ref[:, c0:c0 + FFN_TF], preferred_element_type=F32)
        u = jnp.dot(h, wu_ref[:, c0:c0 + FFN_TF], preferred_element_type=F32)
        act_ref[:, c0:c0 + FFN_TF] = (g * jax.nn.sigmoid(g) * u).astype(BF16)
    o_ref[...] = x_ref[...] + jnp.dot(act_ref[...], wd_ref[...], preferred_element_type=F32)


def _dense_ffn(hn, x2, wg, wu, wd):
    n, d = x2.shape
    f = wg.shape[1]
    assert f % FFN_TF == 0
    tm = min(FFN_TM, n)
    resident = dict(pipeline_mode=pl.Buffered(1))
    return pl.pallas_call(
        _ffn_kernel,
        grid=(n // tm,),
        in_specs=[
            pl.BlockSpec((tm, d), lambda i: (i, 0)),
            pl.BlockSpec((tm, d), lambda i: (i, 0)),
            pl.BlockSpec((d, f), lambda i: (0, 0), **resident),
            pl.BlockSpec((d, f), lambda i: (0, 0), **resident),
            pl.BlockSpec((f, d), lambda i: (0, 0), **resident),
        ],
        out_specs=pl.BlockSpec((tm, d), lambda i: (i, 0)),
        out_shape=jax.ShapeDtypeStruct((n, d), F32),
        scratch_shapes=[pltpu.VMEM((tm, f), BF16)],
        compiler_params=_params("arbitrary"),
        name="dense_ffn",
    )(hn, x2, wg, wu, wd)


def _router_kernel(x_ref, g_ref, rh_ref, rl_ref, tri_ref, route_ref, idx_ref, cnt_ref, base_ref):
    tm = x_ref.shape[0]

    @pl.when(pl.program_id(0) == 0)
    def _():
        base_ref[...] = jnp.zeros(base_ref.shape, F32)

    xf = x_ref[...]
    ms = jnp.mean(xf * xf, axis=-1, keepdims=True)
    hh, hl = _split_hi_lo(xf * lax.rsqrt(ms + EPS) * g_ref[...])
    rh = rh_ref[...]
    logits = (jnp.dot(hh, rh, preferred_element_type=F32)
              + jnp.dot(hl, rh, preferred_element_type=F32)
              + jnp.dot(hh, rl_ref[...], preferred_element_type=F32))
    lane = lax.broadcasted_iota(jnp.int32, (tm, LANES), 1).astype(F32)
    neg = jnp.float32(-jnp.inf)
    logits = jnp.where(lane < N_EXPERTS, logits, neg)
    m1 = jnp.max(logits, axis=1, keepdims=True)
    i1 = jnp.min(jnp.where(logits == m1, lane, float(LANES)), axis=1, keepdims=True)
    l2 = jnp.where(lane == i1, neg, logits)
    m2 = jnp.max(l2, axis=1, keepdims=True)
    i2 = jnp.min(jnp.where(l2 == m2, lane, float(LANES)), axis=1, keepdims=True)
    e = jnp.exp(m2 - m1)
    g1 = 1.0 / (1.0 + e)
    g2 = e / (1.0 + e)
    oh1 = jnp.where(lane == i1, 1.0, 0.0)
    oh2 = jnp.where(lane == i2, 1.0, 0.0)
    oh = oh1 + oh2
    before = jnp.dot(tri_ref[...], oh.astype(BF16), preferred_element_type=F32) + base_ref[0:1, :]
    r1 = jnp.sum(oh1 * before, axis=1, keepdims=True)
    r2 = jnp.sum(oh2 * before, axis=1, keepdims=True)
    base = base_ref[0:1, :] + jnp.sum(oh, axis=0, keepdims=True)
    base_ref[...] = jnp.broadcast_to(base, base_ref.shape)
    cnt_ref[...] = jnp.broadcast_to(base, cnt_ref.shape)
    out = jnp.where(lane == 0, i1, 0.0)
    out = jnp.where(lane == 1, i2, out)
    out = jnp.where(lane == 2, g1, out)
    out = jnp.where(lane == 3, g2, out)
    out = jnp.where(lane == 4, r1, out)
    out = jnp.where(lane == 5, r2, out)
    route_ref[...] = out
    idx_ref[...] = out[:, :ROUTE_IDX_COLS].astype(jnp.int32)


def _router(x2, g, rh, rl, tri):
    n, d = x2.shape
    tm = min(ROUTE_TM, n)
    return pl.pallas_call(
        _router_kernel,
        grid=(n // tm,),
        in_specs=[
            pl.BlockSpec((tm, d), lambda i: (i, 0)),
            pl.BlockSpec((1, d), lambda i: (0, 0)),
            pl.BlockSpec((d, LANES), lambda i: (0, 0)),
            pl.BlockSpec((d, LANES), lambda i: (0, 0)),
            pl.BlockSpec((tm, tm), lambda i: (0, 0)),
        ],
        out_specs=[pl.BlockSpec((tm, LANES), lambda i: (i, 0)),
                   pl.BlockSpec((tm, ROUTE_IDX_COLS), lambda i: (i, 0)),
                   pl.BlockSpec((8, LANES), lambda i: (0, 0))],
        out_shape=[jax.ShapeDtypeStruct((n, LANES), F32), jax.ShapeDtypeStruct((n, ROUTE_IDX_COLS), jnp.int32),
                   jax.ShapeDtypeStruct((8, LANES), F32)],
        scratch_shapes=[pltpu.VMEM((8, LANES), F32)],
        compiler_params=_params("arbitrary"),
        name="router",
    )(x2, g, rh, rl, tri)


def _sc_window_rows(table):
    row_bytes = table.shape[1] * table.dtype.itemsize
    return min(SC_MAX_INDEX_VECTOR, SC_WINDOW_BYTES // row_bytes)


def _sc_gather_rows(table, idx):
    _, d = table.shape
    b = idx.shape[0]
    win = _sc_window_rows(table)
    sc = plsc.get_sparse_core_info()
    n_workers = sc.num_cores * sc.num_subcores
    per_w = b // n_workers
    n_win = per_w // win
    assert per_w * n_workers == b and n_win * win == per_w and n_win % 2 == 0, (b, n_workers, win)
    mesh = plsc.VectorSubcoreMesh(core_axis_name="c", subcore_axis_name="s")
    dma = pltpu.SemaphoreType.DMA

    @functools.partial(
        pl.kernel, mesh=mesh, out_type=jax.ShapeDtypeStruct((b, d), table.dtype), name="sc_gather_rows",
        scratch_types=[pltpu.VMEM((win,), jnp.int32), pltpu.VMEM((win,), jnp.int32),
                       pltpu.VMEM((win, d), table.dtype), pltpu.VMEM((win, d), table.dtype),
                       dma, dma, dma, dma])
    def gather_kernel(table_hbm, idx_hbm, out_hbm, i0, i1, r0, r1, g0, g1, w0, w1):
        wid = lax.axis_index("s") * sc.num_cores + lax.axis_index("c")
        base = wid * per_w
        idxb, rows, gsem, wsem = (i0, i1), (r0, r1), (g0, g1), (w0, w1)

        def off(c):
            return pl.multiple_of(base + c * win, win)

        def gather(s):
            return pltpu.make_async_copy(table_hbm.at[idxb[s]], rows[s], gsem[s])

        def write(c, s):
            return pltpu.make_async_copy(rows[s], out_hbm.at[pl.ds(off(c), win)], wsem[s])

        pltpu.sync_copy(idx_hbm.at[pl.ds(off(0), win)], idxb[0])
        gather(0).start()

        @pl.loop(0, n_win, step=2)
        def _(c):
            for s in (0, 1):
                cc = c + s

                @pl.when(cc + 1 < n_win)
                def _():
                    @pl.when(cc >= 1)
                    def _():
                        write(cc - 1, 1 - s).wait()

                    pltpu.sync_copy(idx_hbm.at[pl.ds(off(cc + 1), win)], idxb[1 - s])
                    gather(1 - s).start()

                gather(s).wait()
                write(cc, s).start()

        write(n_win - 2, 0).wait()
        write(n_win - 1, 1).wait()

    return gather_kernel(table, idx)


def _sc_dispatch_rows(src, dest1, dest2, n_rows):
    n, d = src.shape
    win = _sc_window_rows(src)
    sc = plsc.get_sparse_core_info()
    n_workers = sc.num_cores * sc.num_subcores
    per_w = n // n_workers
    n_win = per_w // win
    assert per_w * n_workers == n and n_win * win == per_w and n_win % 2 == 0, (n, n_workers, win)
    mesh = plsc.VectorSubcoreMesh(core_axis_name="c", subcore_axis_name="s")
    dma = pltpu.SemaphoreType.DMA
    ivec = pltpu.VMEM((win,), jnp.int32)
    rbuf = pltpu.VMEM((win, d), src.dtype)

    @functools.partial(
        pl.kernel, mesh=mesh, out_type=jax.ShapeDtypeStruct((n_rows, d), src.dtype), name="sc_dispatch_rows",
        scratch_types=[ivec, ivec, ivec, ivec, rbuf, rbuf, dma, dma, dma, dma, dma, dma])
    def dispatch_kernel(src_hbm, d1_hbm, d2_hbm, out_hbm, a0, a1, b0, b1, r0, r1, l0, l1, p0, p1, q0, q1):
        wid = lax.axis_index("s") * sc.num_cores + lax.axis_index("c")
        base = wid * per_w
        ia, ib, rows, lsem, psem, qsem = (a0, a1), (b0, b1), (r0, r1), (l0, l1), (p0, p1), (q0, q1)

        def off(c):
            return pl.multiple_of(base + c * win, win)

        def load(c, s):
            return pltpu.make_async_copy(src_hbm.at[pl.ds(off(c), win)], rows[s], lsem[s])

        def scatters(s):
            return (pltpu.make_async_copy(rows[s], out_hbm.at[ia[s]], psem[s]),
                    pltpu.make_async_copy(rows[s], out_hbm.at[ib[s]], qsem[s]))

        def fetch(c, s):
            pltpu.sync_copy(d1_hbm.at[pl.ds(off(c), win)], ia[s])
            pltpu.sync_copy(d2_hbm.at[pl.ds(off(c), win)], ib[s])
            load(c, s).start()

        fetch(0, 0)

        @pl.loop(0, n_win, step=2)
        def _(c):
            for s in (0, 1):
                cc = c + s

                @pl.when(cc + 1 < n_win)
                def _():
                    @pl.when(cc >= 1)
                    def _():
                        for cp in scatters(1 - s):
                            cp.wait()

                    fetch(cc + 1, 1 - s)

                load(cc, s).wait()
                for cp in scatters(s):
                    cp.start()

        for s in (0, 1):
            for cp in scatters(s):
                cp.wait()

    return dispatch_kernel(src, dest1, dest2)


def _moe_kernel(bexp_ref, nused_ref, nvalid_ref, x_ref, wg_ref, wu_ref, wd_ref, o_ref, xb, acc):
    i = pl.program_id(0)
    j = pl.program_id(1)
    nj = pl.num_programs(1)
    used = i < nused_ref[0]

    @pl.when(used & (j == 0))
    def _():
        x = _unpack_bf16_pairs(x_ref[...])
        row = lax.broadcasted_iota(jnp.int32, x.shape, 0)
        xb[...] = jnp.where(row < nvalid_ref[i], x, 0.0).astype(BF16)
        acc[...] = jnp.zeros(acc.shape, F32)

    @pl.when(used)
    def _():
        x = xb[...]
        g = jnp.dot(x, wg_ref[0].astype(BF16), preferred_element_type=F32)
        u = jnp.dot(x, wu_ref[0].astype(BF16), preferred_element_type=F32)
        act = (g * jax.nn.sigmoid(g) * u).astype(BF16)
        acc[...] += jnp.dot(act, wd_ref[0].astype(BF16), preferred_element_type=F32)

        @pl.when(j == nj - 1)
        def _():
            o_ref[...] = _pack_bf16_pairs(acc[...])

    @pl.when(jnp.logical_not(used) & (j == 0))
    def _():
        o_ref[...] = jnp.zeros(o_ref.shape, o_ref.dtype)


def _moe_experts(block_exp, n_used, n_valid, xs, wg, wu, wd, tm):
    n_rows, dp = xs.shape
    d = 2 * dp
    n_blocks = n_rows // tm
    f = wg.shape[2]
    tf = MOE_TF if f % MOE_TF == 0 else f
    nj = f // tf
    assert nj >= 2

    def live(i, j, be, nu):
        u = i < nu[0]
        return jnp.where(u, i, nu[0] - 1), jnp.where(u, j, nj - 1)

    def w_in_map(i, j, be, nu, nv):
        ii, jj = live(i, j, be, nu)
        return (be[ii], 0, jj)

    def w_down_map(i, j, be, nu, nv):
        ii, jj = live(i, j, be, nu)
        return (be[ii], jj, 0)

    def x_map(i, j, be, nu, nv):
        return (live(i, j, be, nu)[0], 0)

    def out_map(i, j, be, nu, nv):
        return (i, 0)

    grid_spec = pltpu.PrefetchScalarGridSpec(
        num_scalar_prefetch=3,
        grid=(n_blocks, nj),
        in_specs=[
            pl.BlockSpec((tm, dp), x_map),
            pl.BlockSpec((1, d, tf), w_in_map),
            pl.BlockSpec((1, d, tf), w_in_map),
            pl.BlockSpec((1, tf, d), w_down_map),
        ],
        out_specs=pl.BlockSpec((tm, dp), out_map),
        scratch_shapes=[pltpu.VMEM((tm, d), BF16), pltpu.VMEM((tm, d), F32)],
    )
    return pl.pallas_call(
        _moe_kernel,
        grid_spec=grid_spec,
        out_shape=jax.ShapeDtypeStruct((n_rows, dp), jnp.int32),
        compiler_params=_params("arbitrary", "arbitrary"),
        name="moe_experts",
    )(block_exp, n_used, n_valid, xs, wg, wu, wd)


def _combine_kernel(x_ref, route_ref, a_ref, b_ref, o_ref):
    route = route_ref[...]
    ya = _unpack_bf16_pairs(a_ref[...])
    yb = _unpack_bf16_pairs(b_ref[...])
    o_ref[...] = x_ref[...] + route[:, 2:3] * ya + route[:, 3:4] * yb


def _combine(yg, x2, route, part, n_parts):
    n, d = x2.shape
    dp = yg.shape[1]
    tm = min(COMB_TM, n // n_parts)
    nb = n // n_parts // tm
    first = part * nb
    return pl.pallas_call(
        _combine_kernel,
        grid=(nb,),
        in_specs=[pl.BlockSpec((tm, d), lambda i: (i + first, 0)),
                  pl.BlockSpec((tm, LANES), lambda i: (i + first, 0)),
                  pl.BlockSpec((tm, dp), lambda i: (i, 0)),
                  pl.BlockSpec((tm, dp), lambda i: (i + nb, 0))],
        out_specs=pl.BlockSpec((tm, d), lambda i: (i + first, 0)),
        out_shape=jax.ShapeDtypeStruct((n, d), F32),
        input_output_aliases={0: 0},
        compiler_params=_params("arbitrary"),
        name="moe_combine",
    )(x2, route, yg, yg)


def _top2_moe(hn, x2, g2, router, wg, wu, wd):
    n, d = x2.shape
    a = 2 * n
    tm = min(MOE_TM, n)
    r_pad = jnp.zeros((d, LANES), F32).at[:, :N_EXPERTS].set(router)
    rh, rl = _split_hi_lo(r_pad)
    rt = min(ROUTE_TM, n)
    tri = (lax.broadcasted_iota(jnp.int32, (rt, rt), 1) < lax.broadcasted_iota(jnp.int32, (rt, rt), 0)).astype(BF16)
    route, cols, cnt = _router(x2, g2, rh, rl, tri)

    counts = cnt[0, :N_EXPERTS].astype(jnp.int32)
    padded = (counts + tm - 1) // tm * tm
    pad_ends = jnp.cumsum(padded)
    pad_starts = pad_ends - padded
    def lookup(table, idx):
        hit = idx[:, None] == jnp.arange(N_EXPERTS, dtype=jnp.int32)[None, :]
        return jnp.sum(jnp.where(hit, table[None, :], 0), axis=1)

    dest1 = lookup(pad_starts, cols[:, 0]) + cols[:, 4]
    dest2 = lookup(pad_starts, cols[:, 1]) + cols[:, 5]
    n_blocks = a // tm + N_EXPERTS
    n_rows = n_blocks * tm
    block_start = jnp.arange(n_blocks, dtype=jnp.int32) * tm
    block_exp = jnp.minimum(jnp.sum(block_start[:, None] >= pad_ends[None, :], axis=1), N_EXPERTS - 1).astype(jnp.int32)
    n_used = (pad_ends[-1] // tm).astype(jnp.int32).reshape(1)
    n_valid = jnp.clip(lookup(counts, block_exp) - (block_start - lookup(pad_starts, block_exp)), 0, tm).astype(jnp.int32)

    xs = _sc_dispatch_rows(hn, dest1, dest2, n_rows)
    ys = _moe_experts(block_exp, n_used, n_valid, xs, wg, wu, wd, tm)
    n_parts = COMBINE_PARTS if n % (COMBINE_PARTS * COMB_TM) == 0 else 1
    step = n // n_parts
    gathered = [_sc_gather_rows(ys, jnp.concatenate([dest1[p * step:(p + 1) * step], dest2[p * step:(p + 1) * step]]))
                for p in range(n_parts)]
    for p in range(n_parts):
        x2 = _combine(gathered[p], x2, route, p, n_parts)
    return x2


def _block_diag(w):
    h, dh, _ = w.shape
    eye = jnp.eye(h, dtype=w.dtype)
    return jnp.einsum("hij,hg->higj", w, eye).reshape(h * dh, h * dh)


def kernel(x, norm1_g, w_in, q_norm_g, k_norm_g, conf_dw_w, conf_dw_b, conf_ln_g, conf_ln_b, gmlp_ln_g, gmlp_ln_b, gmlp_ws, gmlp_bs, lru_conv_w, lru_conv_b, lru_wa, lru_ba, lru_wx, lru_bx, lru_lambda, group_norm_g, w_out, norm2_g, ffn_w_gate, ffn_w_up, ffn_w_down, moe_router, moe_w_gate, moe_w_up, moe_w_down):
    bsz, s, d = x.shape
    n = bsz * s
    depth = w_in.shape[0]
    W = GROUP_WIDTH
    row = lambda v: v.reshape(1, -1).astype(F32)

    head_id = jnp.arange(W) // HEAD_DIM
    bd = (head_id[:, None] == head_id[None, :]).astype(BF16)
    bd2 = jnp.concatenate([bd, bd], axis=0)
    t_att = min(ATT_T, s)
    uu = (jnp.arange(t_att)[:, None] >= jnp.arange(t_att)[None, :]).astype(BF16)
    tril = jnp.tril(jnp.ones((GMLP_CHUNK, GMLP_CHUNK), dtype=bool))

    x2 = x.reshape(n, d)
    for l in range(depth):
        qg = row(jnp.tile(q_norm_g[l], GROUP_HEADS) * (HEAD_DIM ** -0.5))
        kg = row(jnp.tile(k_norm_g[l], GROUP_HEADS))
        q, k, v, rest = _in_proj(x2, row(norm1_g[l]), w_in[l].astype(BF16), qg, kg, bd2)
        gn = group_norm_g[l].reshape(N_GROUPS, 1, W)
        to3 = lambda t: t.reshape(bsz, s, t.shape[-1])
        rest3 = to3(rest)
        zmax = (ATT_ZMAX_SLACK * HEAD_DIM ** 0.5 * jnp.max(jnp.abs(q_norm_g[l] * k_norm_g[l]))).reshape(1)
        y_a = _sb_attention(zmax.astype(F32), to3(q), to3(k), to3(v), uu, gn[0])
        cw = jnp.zeros((CONF_HALO, W), F32).at[:CONF_KERNEL].set(conf_dw_w[l])
        y_b = _conformer(rest3, 0, cw, row(conf_dw_b[l]), row(conf_ln_g[l]), row(conf_ln_b[l]), gn[1])
        ws = jnp.where(tril, gmlp_ws[l], 0.0).astype(BF16)
        bs_mat = jnp.repeat(gmlp_bs[l].T, HEAD_DIM, axis=1)
        y_c = _gmlp(rest3, 2, row(gmlp_ln_g[l]), row(gmlp_ln_b[l]), ws, bs_mat, gn[2])
        lw = jnp.zeros((8, W), F32).at[:LRU_CONV].set(lru_conv_w[l])
        wax = jnp.concatenate([_block_diag(lru_wa[l]), _block_diag(lru_wx[l])], axis=1).astype(BF16)
        y_d = _rg_lru(rest3, 4, lw, row(lru_conv_b[l]), wax, row(lru_ba[l]), row(lru_bx[l]),
                      row(lru_lambda[l]), gn[3])
        ys = [t.reshape(n, W) for t in (y_a, y_b, y_c, y_d)]
        x2, hn = _out_proj(ys, w_out[l].astype(BF16), x2, row(norm2_g[l]), packed=(l % 2 == 1))
        j = l // 2
        if l % 2 == 0:
            x2 = _dense_ffn(hn, x2, ffn_w_gate[j].astype(BF16), ffn_w_up[j].astype(BF16),
                            ffn_w_down[j].astype(BF16))
        else:
            x2 = _top2_moe(hn, x2, row(norm2_g[l]), moe_router[j], moe_w_gate[j], moe_w_up[j], moe_w_down[j])
    return x2.reshape(bsz, s, d)
```

```python
import functools

import jax
import jax.numpy as jnp
from jax import lax
from jax.experimental import pallas as pl
from jax.experimental.pallas import tpu as pltpu
from jax.experimental.pallas import tpu_sc as plsc

F32 = jnp.float32
BF16 = jnp.bfloat16

HEAD_DIM = 64
GROUP_HEADS = 4
GROUP_WIDTH = GROUP_HEADS * HEAD_DIM
N_GROUPS = 4
CONF_KERNEL = 31
GMLP_CHUNK = 128
LRU_CONV = 4
LRU_C = 8.0
N_EXPERTS = 8
EPS = 1e-6

LANES = 128
VMEM_LIMIT = 56 * 1024 * 1024

IN_TM = 1024
MIX_T = 256
ATT_T = 256
ATT_UNDERFLOW = 110.0
ATT_ZMAX_SLACK = 1.05
CONF_R = 64
CONF_HALO = 32
LRU_HALO = 8
OUT_TM = 512
FFN_TM = 1024
FFN_TF = 256
ROUTE_TM = 512
ROUTE_IDX_COLS = 8
MOE_TM = 1024
MOE_TF = 512
COMB_TM = 512
COMBINE_PARTS = 2
SC_WINDOW_BYTES = 128 * 1024
SC_MAX_INDEX_VECTOR = 128


def _params(*sem):
    return pltpu.CompilerParams(dimension_semantics=sem, vmem_limit_bytes=VMEM_LIMIT)


def _split_hi_lo(x):
    hi = x.astype(BF16)
    lo = (x - hi.astype(F32)).astype(BF16)
    return hi, lo


SOFTPLUS_CLAMP = 60.0


def _softplus(z):
    return jnp.maximum(jnp.log(1.0 + jnp.exp(jnp.minimum(z, SOFTPLUS_CLAMP))), z)


def _gelu_tanh(x):
    c = 0.7978845608028654
    return 0.5 * x * (1.0 + jnp.tanh(c * (x + 0.044715 * (x * x * x))))


def _pack_bf16_pairs(y):
    c = y.shape[1] // 2
    bits = lax.bitcast_convert_type(y.astype(BF16).astype(F32), jnp.uint32)
    word = (bits[:, :c] >> 16) | bits[:, c:]
    return lax.bitcast_convert_type(word, jnp.int32)


def _unpack_bf16_pairs(w):
    bits = lax.bitcast_convert_type(w, jnp.uint32)
    lo = lax.bitcast_convert_type(bits << 16, F32)
    hi = lax.bitcast_convert_type(bits & jnp.uint32(0xFFFF0000), F32)
    return jnp.concatenate([lo, hi], axis=1)


def _group_rms(y, g):
    ms = jnp.mean(y * y, axis=-1, keepdims=True)
    return y * lax.rsqrt(ms + EPS) * g


def _sb_attn_kernel(zmax_ref, q_ref, k_ref, v_ref, uu_ref, gn_ref, o_ref):
    T = q_ref.shape[1]
    n_pairs = GROUP_WIDTH // LANES
    i = pl.program_id(1)
    lane = lax.broadcasted_iota(jnp.int32, (T, LANES), 1)
    rows = lax.broadcasted_iota(jnp.int32, (2 * T, T), 0)
    cols = lax.broadcasted_iota(jnp.int32, (2 * T, T), 1)
    causal = cols < jnp.where(rows >= T, rows - T, rows)
    uu = uu_ref[...]

    def tile(qs, k2, v2, carry, mask):
        z = lax.dot_general(qs, k2, (((1,), (1,)), ((), ())), preferred_element_type=F32)
        sp = _softplus(z)
        if mask is not None:
            sp = jnp.where(mask, sp, 0.0)
        cum = jnp.dot(sp.astype(BF16), uu, preferred_element_type=F32)
        w = jnp.exp(z - cum - carry)
        if mask is not None:
            w = jnp.where(mask, w, 0.0)
        pv = jnp.dot(w.astype(BF16), v2, preferred_element_type=F32)
        return pv, carry + jnp.sum(sp, axis=1, keepdims=True)

    def kv(r0, pair):
        cs = slice(pair * LANES, (pair + 1) * LANES)
        return k_ref[0, pl.ds(r0, T), cs], v_ref[0, pl.ds(r0, T), cs]

    qs = []
    for pair in range(n_pairs):
        q2 = q_ref[0, :, pair * LANES:(pair + 1) * LANES]
        zero = jnp.zeros_like(q2)
        qs.append(jnp.concatenate([jnp.where(lane < HEAD_DIM, q2, zero),
                                   jnp.where(lane >= HEAD_DIM, q2, zero)], axis=0))
    row0 = pl.multiple_of(i * T, T)
    state = tuple(tile(qs[p], *kv(row0, p), jnp.zeros((2 * T, 1), F32), causal) for p in range(n_pairs))

    dead_at = zmax_ref[0] + ATT_UNDERFLOW

    def alive(st):
        return functools.reduce(jnp.minimum, [jnp.min(carry) for _, carry in st]) <= dead_at

    def cond(c):
        step, live, _ = c
        return (step < i) & live

    def body(c):
        step, _, st = c
        r0 = pl.multiple_of((i - 1 - step) * T, T)
        new = []
        for p in range(n_pairs):
            acc, carry = st[p]
            pv, carry = tile(qs[p], *kv(r0, p), carry, None)
            new.append((acc + pv, carry))
        new = tuple(new)
        return step + 1, alive(new), new

    _, _, state = lax.while_loop(cond, body, (jnp.int32(0), alive(state), state))
    y = jnp.concatenate([jnp.where(lane < HEAD_DIM, acc[:T], acc[T:]) for acc, _ in state], axis=1)
    o_ref[0] = _group_rms(y, gn_ref[...]).astype(o_ref.dtype)


def _sb_attention(zmax, q3, k3, v3, uu, gn):
    b, s, w = q3.shape
    t = min(ATT_T, s)
    return pl.pallas_call(
        _sb_attn_kernel,
        grid=(b, s // t),
        in_specs=[
            pl.BlockSpec(memory_space=pltpu.SMEM),
            pl.BlockSpec((1, t, w), lambda bi, i: (bi, i, 0)),
            pl.BlockSpec((1, s, w), lambda bi, i: (bi, 0, 0)),
            pl.BlockSpec((1, s, w), lambda bi, i: (bi, 0, 0)),
            pl.BlockSpec((t, t), lambda bi, i: (0, 0)),
            pl.BlockSpec((1, w), lambda bi, i: (0, 0)),
        ],
        out_specs=pl.BlockSpec((1, t, w), lambda bi, i: (bi, i, 0)),
        out_shape=jax.ShapeDtypeStruct((b, s, w), BF16),
        compiler_params=_params("arbitrary", "arbitrary"),
        name="sb_attn",
    )(zmax, q3, k3, v3, uu, gn)


def _conformer_rows(val, gate, w_ref, b_ref, lng_ref, lnb_ref, gn, o_ref, row0, hs_ref, sh_ref):
    T = val.shape[0]
    H = CONF_HALO
    SUB = 8
    hs_ref[H:H + T, :] = val * jax.nn.sigmoid(gate)
    off = H - (CONF_KERNEL - 1)
    L = sh_ref.shape[1]
    for s in range(1, SUB):
        sh_ref[s - 1] = hs_ref[s:s + L, :]
    R = min(CONF_R, T)
    for r0 in range(0, T, R):
        acc = jnp.broadcast_to(b_ref[...], (R, GROUP_WIDTH))
        for k in range(CONF_KERNEL):
            s = (off + k) % SUB
            a = off + k - s + r0
            tap = hs_ref[a:a + R, :] if s == 0 else sh_ref[s - 1, a:a + R, :]
            acc = acc + w_ref[k:k + 1, :] * tap
        mu = jnp.mean(acc, axis=-1, keepdims=True)
        xc = acc - mu
        var = jnp.mean(xc * xc, axis=-1, keepdims=True)
        y = xc * lax.rsqrt(var + EPS) * lng_ref[...] + lnb_ref[...]
        y = y * jax.nn.sigmoid(y)
        o_ref[row0 + r0:row0 + r0 + R, :] = _group_rms(y, gn).astype(o_ref.dtype)
    hs_ref[0:H, :] = hs_ref[T:T + H, :]


def _gmlp_rows(u_in, v_in, lng_ref, lnb_ref, ws_ref, bs_ref, gn, o_ref, row0):
    C = GMLP_CHUNK
    lane = lax.broadcasted_iota(jnp.int32, (C, GROUP_WIDTH), 1)
    u = _gelu_tanh(u_in)
    v = _gelu_tanh(v_in)
    mu = jnp.mean(v, axis=-1, keepdims=True)
    xc = v - mu
    var = jnp.mean(xc * xc, axis=-1, keepdims=True)
    vb = (xc * lax.rsqrt(var + EPS) * lng_ref[...] + lnb_ref[...]).astype(BF16)
    mixed = bs_ref[...]
    for h in range(GROUP_HEADS):
        m = jnp.dot(ws_ref[h], vb, preferred_element_type=F32)
        in_head = (lane >= h * HEAD_DIM) & (lane < (h + 1) * HEAD_DIM)
        mixed = mixed + jnp.where(in_head, m, 0.0)
    o_ref[row0:row0 + C, :] = _group_rms(u * mixed, gn).astype(o_ref.dtype)


def _lru_rows(x_in, gate, cw_ref, cb_ref, wax_ref, ba_ref, bx_ref, lam_ref, gn, o_ref, row0, xs_ref, hprev_ref):
    T = x_in.shape[0]
    W = GROUP_WIDTH
    H = LRU_HALO
    xs_ref[H:H + T, :] = x_in
    off = H - (LRU_CONV - 1)
    xb = jnp.broadcast_to(cb_ref[...], (T, W))
    for k in range(LRU_CONV):
        xb = xb + cw_ref[k:k + 1, :] * xs_ref[off + k:off + k + T, :]
    xs_ref[0:H, :] = xs_ref[T:T + H, :]

    pre = jnp.dot(xb.astype(BF16), wax_ref[...], preferred_element_type=F32)
    r = jax.nn.sigmoid(pre[:, :W] + ba_ref[...])
    ig = jax.nn.sigmoid(pre[:, W:] + bx_ref[...])
    nlam = -lam_ref[...]
    log_a = (-LRU_C) * r * (jnp.maximum(nlam, 0.0) + jnp.log1p(jnp.exp(-jnp.abs(nlam))))
    a = jnp.exp(log_a)
    th = jnp.tanh(log_a)
    one_minus_a2 = 2.0 * th / (th - 1.0)
    b = jnp.sqrt(one_minus_a2) * (ig * xb)

    SUB = 8
    a = a.reshape(T // SUB, SUB, W)
    b = b.reshape(T // SUB, SUB, W)
    sub = lax.broadcasted_iota(jnp.int32, a.shape, 1)
    d = 1
    while d < SUB:
        keep = sub >= d
        a_sh = jnp.where(keep, pltpu.roll(a, d, 1), 1.0)
        b_sh = jnp.where(keep, pltpu.roll(b, d, 1), 0.0)
        b = a * b_sh + b
        a = a * a_sh
        d *= 2
    carry = hprev_ref[0:1, :]
    tiles = []
    for g in range(T // SUB):
        hg = b[g] + a[g] * carry
        carry = hg[SUB - 1:SUB]
        tiles.append(hg)
    h = jnp.concatenate(tiles, axis=0)
    hprev_ref[...] = jnp.broadcast_to(carry, hprev_ref.shape)
    y = h * _gelu_tanh(gate)
    o_ref[row0:row0 + T, :] = _group_rms(y, gn).astype(o_ref.dtype)


def _in_mix_kernel(blocks_per_seq, x_ref, g_ref, w_ref, qg_ref, kg_ref, bd_ref,
                   cw_ref, cb_ref, clg_ref, clb_ref,
                   glg_ref, glb_ref, ws_ref, bs_ref,
                   lw_ref, lb_ref, wax_ref, ba_ref, bx_ref, lam_ref, gn_ref,
                   q_ref, k_ref, v_ref, yb_ref, yc_ref, yd_ref,
                   rest_ref, hs_ref, sh_ref, xs_ref, hprev_ref):
    tm = x_ref.shape[0]
    W = GROUP_WIDTH

    @pl.when(pl.program_id(0) % blocks_per_seq == 0)
    def _():
        hs_ref[0:CONF_HALO, :] = jnp.zeros((CONF_HALO, W), F32)
        xs_ref[0:LRU_HALO, :] = jnp.zeros((LRU_HALO, W), F32)
        hprev_ref[...] = jnp.zeros(hprev_ref.shape, F32)

    xf = x_ref[...]
    ms = jnp.mean(xf * xf, axis=-1, keepdims=True)
    h = (xf * lax.rsqrt(ms + EPS) * g_ref[...]).astype(BF16)

    def proj(c):
        return jnp.dot(h, w_ref[:, c * W:(c + 1) * W], preferred_element_type=F32)

    def head_norm(p, gain):
        hi, lo = _split_hi_lo(p * p)
        ss = jnp.dot(jnp.concatenate([hi, lo], axis=1), bd_ref[...], preferred_element_type=F32)
        return p * lax.rsqrt(ss * (1.0 / HEAD_DIM) + EPS) * gain

    q_ref[...] = head_norm(proj(0), qg_ref[...]).astype(BF16)
    k_ref[...] = head_norm(proj(1), kg_ref[...]).astype(BF16)
    v_ref[...] = proj(2).astype(BF16)
    for c in range(rest_ref.shape[1] // W):
        rest_ref[:, c * W:(c + 1) * W] = proj(3 + c)

    def cols(r0, n, c):
        return rest_ref[r0:r0 + n, c * W:(c + 1) * W]

    T = min(MIX_T, tm)
    for r0 in range(0, tm, T):
        _conformer_rows(cols(r0, T, 0), cols(r0, T, 1), cw_ref, cb_ref, clg_ref, clb_ref, gn_ref[1:2, :],
                        yb_ref, r0, hs_ref, sh_ref)
        _lru_rows(cols(r0, T, 4), cols(r0, T, 5), lw_ref, lb_ref, wax_ref, ba_ref, bx_ref, lam_ref,
                  gn_ref[3:4, :], yd_ref, r0, xs_ref, hprev_ref)
    for r0 in range(0, tm, GMLP_CHUNK):
        _gmlp_rows(cols(r0, GMLP_CHUNK, 2), cols(r0, GMLP_CHUNK, 3), glg_ref, glb_ref, ws_ref, bs_ref,
                   gn_ref[2:3, :], yc_ref, r0)


def _in_mix(x2, seq_len, g, w_bf, qg, kg, bd2, conf, gmlp, lru, gn4):
    n, d = x2.shape
    cols = w_bf.shape[1]
    W = GROUP_WIDTH
    rest_cols = cols - 3 * W
    tm = min(IN_TM, seq_len)
    assert seq_len % tm == 0 and tm % min(MIX_T, tm) == 0 and tm % GMLP_CHUNK == 0
    t = min(MIX_T, tm)
    row = lambda i: (i, 0)
    whole = lambda a: pl.BlockSpec(a.shape, lambda i: (0,) * a.ndim)
    params = (g, w_bf, qg, kg, bd2) + tuple(conf) + tuple(gmlp) + tuple(lru) + (gn4,)
    out_block = pl.BlockSpec((tm, W), row)
    out_shape = jax.ShapeDtypeStruct((n, W), BF16)
    return pl.pallas_call(
        functools.partial(_in_mix_kernel, seq_len // tm),
        grid=(n // tm,),
        in_specs=[pl.BlockSpec((tm, d), row)] + [whole(a) for a in params],
        out_specs=[out_block] * 6,
        out_shape=[out_shape] * 6,
        scratch_shapes=[pltpu.VMEM((tm, rest_cols), F32),
                        pltpu.VMEM((t + CONF_HALO, W), F32),
                        pltpu.VMEM((7, t + CONF_HALO - 8, W), F32),
                        pltpu.VMEM((t + LRU_HALO, W), F32),
                        pltpu.VMEM((8, W), F32)],
        compiler_params=_params("arbitrary"),
        name="in_mix",
    )(x2, *params)


def _out_proj_kernel(ya_ref, yb_ref, yc_ref, yd_ref, w_ref, x_ref, g_ref, xo_ref, h_ref):
    W = GROUP_WIDTH
    acc = x_ref[...]
    for gi, y_ref in enumerate((ya_ref, yb_ref, yc_ref, yd_ref)):
        acc = acc + jnp.dot(y_ref[...], w_ref[gi * W:(gi + 1) * W, :], preferred_element_type=F32)
    xo_ref[...] = acc
    ms = jnp.mean(acc * acc, axis=-1, keepdims=True)
    h = acc * lax.rsqrt(ms + EPS) * g_ref[...]
    if h_ref.dtype == jnp.int32:
        h_ref[...] = _pack_bf16_pairs(h)
    else:
        h_ref[...] = h.astype(h_ref.dtype)


def _out_proj(ys, w_bf, x2, g, packed):
    n, d = x2.shape
    h_shape = jax.ShapeDtypeStruct((n, d // 2), jnp.int32) if packed else jax.ShapeDtypeStruct((n, d), BF16)
    W = GROUP_WIDTH
    tm = min(OUT_TM, n)
    row = lambda i: (i, 0)
    const = lambda i: (0, 0)
    ysp = pl.BlockSpec((tm, W), row)
    return pl.pallas_call(
        _out_proj_kernel,
        grid=(n // tm,),
        in_specs=[ysp, ysp, ysp, ysp,
                  pl.BlockSpec(w_bf.shape, const),
                  pl.BlockSpec((tm, d), row),
                  pl.BlockSpec((1, d), const)],
        out_specs=[pl.BlockSpec((tm, d), row), pl.BlockSpec((tm, h_shape.shape[1]), row)],
        out_shape=[jax.ShapeDtypeStruct((n, d), F32), h_shape],
        compiler_params=_params("arbitrary"),
        name="out_proj",
    )(*ys, w_bf, x2, g)


def _ffn_kernel(h_ref, x_ref, wg_ref, wu_ref, wd_ref, o_ref, act_ref):
    f = wg_ref.shape[1]
    h = h_ref[...]
    for c0 in range(0, f, FFN_TF):
        g = jnp.dot(h, wg_ref[:, c0:c0 + FFN_TF], preferred_element_type=F32)
        u = jnp.dot(h, wu_ref[:, c0:c0 + FFN_TF], preferred_element_type=F32)
        act_ref[:, c0:c0 + FFN_TF] = (g * jax.nn.sigmoid(g) * u).astype(BF16)
    o_ref[...] = x_ref[...] + jnp.dot(act_ref[...], wd_ref[...], preferred_element_type=F32)


def _dense_ffn(hn, x2, wg, wu, wd):
    n, d = x2.shape
    f = wg.shape[1]
    assert f % FFN_TF == 0
    tm = min(FFN_TM, n)
    resident = dict(pipeline_mode=pl.Buffered(1))
    return pl.pallas_call(
        _ffn_kernel,
        grid=(n // tm,),
        in_specs=[
            pl.BlockSpec((tm, d), lambda i: (i, 0)),
            pl.BlockSpec((tm, d), lambda i: (i, 0)),
            pl.BlockSpec((d, f), lambda i: (0, 0), **resident),
            pl.BlockSpec((d, f), lambda i: (0, 0), **resident),
            pl.BlockSpec((f, d), lambda i: (0, 0), **resident),
        ],
        out_specs=pl.BlockSpec((tm, d), lambda i: (i, 0)),
        out_shape=jax.ShapeDtypeStruct((n, d), F32),
        scratch_shapes=[pltpu.VMEM((tm, f), BF16)],
        compiler_params=_params("arbitrary"),
        name="dense_ffn",
    )(hn, x2, wg, wu, wd)


def _router_kernel(x_ref, g_ref, rh_ref, rl_ref, tri_ref, route_ref, idx_ref, cnt_ref, base_ref):
    tm = x_ref.shape[0]

    @pl.when(pl.program_id(0) == 0)
    def _():
        base_ref[...] = jnp.zeros(base_ref.shape, F32)

    xf = x_ref[...]
    ms = jnp.mean(xf * xf, axis=-1, keepdims=True)
    hh, hl = _split_hi_lo(xf * lax.rsqrt(ms + EPS) * g_ref[...])
    rh = rh_ref[...]
    logits = (jnp.dot(hh, rh, preferred_element_type=F32)
              + jnp.dot(hl, rh, preferred_element_type=F32)
              + jnp.dot(hh, rl_ref[...], preferred_element_type=F32))
    lane = lax.broadcasted_iota(jnp.int32, (tm, LANES), 1).astype(F32)
    neg = jnp.float32(-jnp.inf)
    logits = jnp.where(lane < N_EXPERTS, logits, neg)
    m1 = jnp.max(logits, axis=1, keepdims=True)
    i1 = jnp.min(jnp.where(logits == m1, lane, float(LANES)), axis=1, keepdims=True)
    l2 = jnp.where(lane == i1, neg, logits)
    m2 = jnp.max(l2, axis=1, keepdims=True)
    i2 = jnp.min(jnp.where(l2 == m2, lane, float(LANES)), axis=1, keepdims=True)
    e = jnp.exp(m2 - m1)
    g1 = 1.0 / (1.0 + e)
    g2 = e / (1.0 + e)
    oh1 = jnp.where(lane == i1, 1.0, 0.0)
    oh2 = jnp.where(lane == i2, 1.0, 0.0)
    oh = oh1 + oh2
    before = jnp.dot(tri_ref[...], oh.astype(BF16), preferred_element_type=F32) + base_ref[0:1, :]
    r1 = jnp.sum(oh1 * before, axis=1, keepdims=True)
    r2 = jnp.sum(oh2 * before, axis=1, keepdims=True)
    base = base_ref[0:1, :] + jnp.sum(oh, axis=0, keepdims=True)
    base_ref[...] = jnp.broadcast_to(base, base_ref.shape)
    cnt_ref[...] = jnp.broadcast_to(base, cnt_ref.shape)
    out = jnp.where(lane == 0, i1, 0.0)
    out = jnp.where(lane == 1, i2, out)
    out = jnp.where(lane == 2, g1, out)
    out = jnp.where(lane == 3, g2, out)
    out = jnp.where(lane == 4, r1, out)
    out = jnp.where(lane == 5, r2, out)
    route_ref[...] = out
    idx_ref[...] = out[:, :ROUTE_IDX_COLS].astype(jnp.int32)


def _router(x2, g, rh, rl, tri):
    n, d = x2.shape
    tm = min(ROUTE_TM, n)
    return pl.pallas_call(
        _router_kernel,
        grid=(n // tm,),
        in_specs=[
            pl.BlockSpec((tm, d), lambda i: (i, 0)),
            pl.BlockSpec((1, d), lambda i: (0, 0)),
            pl.BlockSpec((d, LANES), lambda i: (0, 0)),
            pl.BlockSpec((d, LANES), lambda i: (0, 0)),
            pl.BlockSpec((tm, tm), lambda i: (0, 0)),
        ],
        out_specs=[pl.BlockSpec((tm, LANES), lambda i: (i, 0)),
                   pl.BlockSpec((tm, ROUTE_IDX_COLS), lambda i: (i, 0)),
                   pl.BlockSpec((8, LANES), lambda i: (0, 0))],
        out_shape=[jax.ShapeDtypeStruct((n, LANES), F32), jax.ShapeDtypeStruct((n, ROUTE_IDX_COLS), jnp.int32),
                   jax.ShapeDtypeStruct((8, LANES), F32)],
        scratch_shapes=[pltpu.VMEM((8, LANES), F32)],
        compiler_params=_params("arbitrary"),
        name="router",
    )(x2, g, rh, rl, tri)


def _sc_window_rows(table):
    row_bytes = table.shape[1] * table.dtype.itemsize
    return min(SC_MAX_INDEX_VECTOR, SC_WINDOW_BYTES // row_bytes)


def _sc_gather_rows(table, idx):
    _, d = table.shape
    b = idx.shape[0]
    win = _sc_window_rows(table)
    sc = plsc.get_sparse_core_info()
    n_workers = sc.num_cores * sc.num_subcores
    per_w = b // n_workers
    n_win = per_w // win
    assert per_w * n_workers == b and n_win * win == per_w and n_win % 2 == 0, (b, n_workers, win)
    mesh = plsc.VectorSubcoreMesh(core_axis_name="c", subcore_axis_name="s")
    dma = pltpu.SemaphoreType.DMA

    @functools.partial(
        pl.kernel, mesh=mesh, out_type=jax.ShapeDtypeStruct((b, d), table.dtype), name="sc_gather_rows",
        scratch_types=[pltpu.VMEM((win,), jnp.int32), pltpu.VMEM((win,), jnp.int32),
                       pltpu.VMEM((win, d), table.dtype), pltpu.VMEM((win, d), table.dtype),
                       dma, dma, dma, dma])
    def gather_kernel(table_hbm, idx_hbm, out_hbm, i0, i1, r0, r1, g0, g1, w0, w1):
        wid = lax.axis_index("s") * sc.num_cores + lax.axis_index("c")
        base = wid * per_w
        idxb, rows, gsem, wsem = (i0, i1), (r0, r1), (g0, g1), (w0, w1)

        def off(c):
            return pl.multiple_of(base + c * win, win)

        def gather(s):
            return pltpu.make_async_copy(table_hbm.at[idxb[s]], rows[s], gsem[s])

        def write(c, s):
            return pltpu.make_async_copy(rows[s], out_hbm.at[pl.ds(off(c), win)], wsem[s])

        pltpu.sync_copy(idx_hbm.at[pl.ds(off(0), win)], idxb[0])
        gather(0).start()

        @pl.loop(0, n_win, step=2)
        def _(c):
            for s in (0, 1):
                cc = c + s

                @pl.when(cc + 1 < n_win)
                def _():
                    @pl.when(cc >= 1)
                    def _():
                        write(cc - 1, 1 - s).wait()

                    pltpu.sync_copy(idx_hbm.at[pl.ds(off(cc + 1), win)], idxb[1 - s])
                    gather(1 - s).start()

                gather(s).wait()
                write(cc, s).start()

        write(n_win - 2, 0).wait()
        write(n_win - 1, 1).wait()

    return gather_kernel(table, idx)


def _sc_dispatch_rows(src, dest1, dest2, n_rows):
    n, d = src.shape
    win = _sc_window_rows(src)
    sc = plsc.get_sparse_core_info()
    n_workers = sc.num_cores * sc.num_subcores
    per_w = n // n_workers
    n_win = per_w // win
    assert per_w * n_workers == n and n_win * win == per_w and n_win % 2 == 0, (n, n_workers, win)
    mesh = plsc.VectorSubcoreMesh(core_axis_name="c", subcore_axis_name="s")
    dma = pltpu.SemaphoreType.DMA
    ivec = pltpu.VMEM((win,), jnp.int32)
    rbuf = pltpu.VMEM((win, d), src.dtype)

    @functools.partial(
        pl.kernel, mesh=mesh, out_type=jax.ShapeDtypeStruct((n_rows, d), src.dtype), name="sc_dispatch_rows",
        scratch_types=[ivec, ivec, ivec, ivec, rbuf, rbuf, dma, dma, dma, dma, dma, dma])
    def dispatch_kernel(src_hbm, d1_hbm, d2_hbm, out_hbm, a0, a1, b0, b1, r0, r1, l0, l1, p0, p1, q0, q1):
        wid = lax.axis_index("s") * sc.num_cores + lax.axis_index("c")
        base = wid * per_w
        ia, ib, rows, lsem, psem, qsem = (a0, a1), (b0, b1), (r0, r1), (l0, l1), (p0, p1), (q0, q1)

        def off(c):
            return pl.multiple_of(base + c * win, win)

        def load(c, s):
            return pltpu.make_async_copy(src_hbm.at[pl.ds(off(c), win)], rows[s], lsem[s])

        def scatters(s):
            return (pltpu.make_async_copy(rows[s], out_hbm.at[ia[s]], psem[s]),
                    pltpu.make_async_copy(rows[s], out_hbm.at[ib[s]], qsem[s]))

        def fetch(c, s):
            pltpu.sync_copy(d1_hbm.at[pl.ds(off(c), win)], ia[s])
            pltpu.sync_copy(d2_hbm.at[pl.ds(off(c), win)], ib[s])
            load(c, s).start()

        fetch(0, 0)

        @pl.loop(0, n_win, step=2)
        def _(c):
            for s in (0, 1):
                cc = c + s

                @pl.when(cc + 1 < n_win)
                def _():
                    @pl.when(cc >= 1)
                    def _():
                        for cp in scatters(1 - s):
                            cp.wait()

                    fetch(cc + 1, 1 - s)

                load(cc, s).wait()
                for cp in scatters(s):
                    cp.start()

        for s in (0, 1):
            for cp in scatters(s):
                cp.wait()

    return dispatch_kernel(src, dest1, dest2)


def _moe_kernel(bexp_ref, nused_ref, nvalid_ref, x_ref, wg_ref, wu_ref, wd_ref, o_ref, xb, acc):
    i = pl.program_id(0)
    j = pl.program_id(1)
    nj = pl.num_programs(1)
    used = i < nused_ref[0]

    @pl.when(used & (j == 0))
    def _():
        x = _unpack_bf16_pairs(x_ref[...])
        row = lax.broadcasted_iota(jnp.int32, x.shape, 0)
        xb[...] = jnp.where(row < nvalid_ref[i], x, 0.0).astype(BF16)
        acc[...] = jnp.zeros(acc.shape, F32)

    @pl.when(used)
    def _():
        x = xb[...]
        g = jnp.dot(x, wg_ref[0].astype(BF16), preferred_element_type=F32)
        u = jnp.dot(x, wu_ref[0].astype(BF16), preferred_element_type=F32)
        act = (g * jax.nn.sigmoid(g) * u).astype(BF16)
        acc[...] += jnp.dot(act, wd_ref[0].astype(BF16), preferred_element_type=F32)

        @pl.when(j == nj - 1)
        def _():
            o_ref[...] = _pack_bf16_pairs(acc[...])

    @pl.when(jnp.logical_not(used) & (j == 0))
    def _():
        o_ref[...] = jnp.zeros(o_ref.shape, o_ref.dtype)


def _moe_experts(block_exp, n_used, n_valid, xs, wg, wu, wd, tm):
    n_rows, dp = xs.shape
    d = 2 * dp
    n_blocks = n_rows // tm
    f = wg.shape[2]
    tf = MOE_TF if f % MOE_TF == 0 else f
    nj = f // tf
    assert nj >= 2

    def live(i, j, be, nu):
        u = i < nu[0]
        return jnp.where(u, i, nu[0] - 1), jnp.where(u, j, nj - 1)

    def w_in_map(i, j, be, nu, nv):
        ii, jj = live(i, j, be, nu)
        return (be[ii], 0, jj)

    def w_down_map(i, j, be, nu, nv):
        ii, jj = live(i, j, be, nu)
        return (be[ii], jj, 0)

    def x_map(i, j, be, nu, nv):
        return (live(i, j, be, nu)[0], 0)

    def out_map(i, j, be, nu, nv):
        return (i, 0)

    grid_spec = pltpu.PrefetchScalarGridSpec(
        num_scalar_prefetch=3,
        grid=(n_blocks, nj),
        in_specs=[
            pl.BlockSpec((tm, dp), x_map),
            pl.BlockSpec((1, d, tf), w_in_map),
            pl.BlockSpec((1, d, tf), w_in_map),
            pl.BlockSpec((1, tf, d), w_down_map),
        ],
        out_specs=pl.BlockSpec((tm, dp), out_map),
        scratch_shapes=[pltpu.VMEM((tm, d), BF16), pltpu.VMEM((tm, d), F32)],
    )
    return pl.pallas_call(
        _moe_kernel,
        grid_spec=grid_spec,
        out_shape=jax.ShapeDtypeStruct((n_rows, dp), jnp.int32),
        compiler_params=_params("arbitrary", "arbitrary"),
        name="moe_experts",
    )(block_exp, n_used, n_valid, xs, wg, wu, wd)


def _combine_kernel(x_ref, route_ref, a_ref, b_ref, o_ref):
    route = route_ref[...]
    ya = _unpack_bf16_pairs(a_ref[...])
    yb = _unpack_bf16_pairs(b_ref[...])
    o_ref[...] = x_ref[...] + route[:, 2:3] * ya + route[:, 3:4] * yb


def _combine(yg, x2, route, part, n_parts):
    n, d = x2.shape
    dp = yg.shape[1]
    tm = min(COMB_TM, n // n_parts)
    nb = n // n_parts // tm
    first = part * nb
    return pl.pallas_call(
        _combine_kernel,
        grid=(nb,),
        in_specs=[pl.BlockSpec((tm, d), lambda i: (i + first, 0)),
                  pl.BlockSpec((tm, LANES), lambda i: (i + first, 0)),
                  pl.BlockSpec((tm, dp), lambda i: (i, 0)),
                  pl.BlockSpec((tm, dp), lambda i: (i + nb, 0))],
        out_specs=pl.BlockSpec((tm, d), lambda i: (i + first, 0)),
        out_shape=jax.ShapeDtypeStruct((n, d), F32),
        input_output_aliases={0: 0},
        compiler_params=_params("arbitrary"),
        name="moe_combine",
    )(x2, route, yg, yg)


def _top2_moe(hn, x2, g2, router, wg, wu, wd):
    n, d = x2.shape
    a = 2 * n
    tm = min(MOE_TM, n)
    r_pad = jnp.zeros((d, LANES), F32).at[:, :N_EXPERTS].set(router)
    rh, rl = _split_hi_lo(r_pad)
    rt = min(ROUTE_TM, n)
    tri = (lax.broadcasted_iota(jnp.int32, (rt, rt), 1) < lax.broadcasted_iota(jnp.int32, (rt, rt), 0)).astype(BF16)
    route, cols, cnt = _router(x2, g2, rh, rl, tri)

    counts = cnt[0, :N_EXPERTS].astype(jnp.int32)
    padded = (counts + tm - 1) // tm * tm
    pad_ends = jnp.cumsum(padded)
    pad_starts = pad_ends - padded
    def lookup(table, idx):
        hit = idx[:, None] == jnp.arange(N_EXPERTS, dtype=jnp.int32)[None, :]
        return jnp.sum(jnp.where(hit, table[None, :], 0), axis=1)

    dest1 = lookup(pad_starts, cols[:, 0]) + cols[:, 4]
    dest2 = lookup(pad_starts, cols[:, 1]) + cols[:, 5]
    n_blocks = a // tm + N_EXPERTS
    n_rows = n_blocks * tm
    block_start = jnp.arange(n_blocks, dtype=jnp.int32) * tm
    block_exp = jnp.minimum(jnp.sum(block_start[:, None] >= pad_ends[None, :], axis=1), N_EXPERTS - 1).astype(jnp.int32)
    n_used = (pad_ends[-1] // tm).astype(jnp.int32).reshape(1)
    n_valid = jnp.clip(lookup(counts, block_exp) - (block_start - lookup(pad_starts, block_exp)), 0, tm).astype(jnp.int32)

    xs = _sc_dispatch_rows(hn, dest1, dest2, n_rows)
    ys = _moe_experts(block_exp, n_used, n_valid, xs, wg, wu, wd, tm)
    n_parts = COMBINE_PARTS if n % (COMBINE_PARTS * COMB_TM) == 0 else 1
    step = n // n_parts
    gathered = [_sc_gather_rows(ys, jnp.concatenate([dest1[p * step:(p + 1) * step], dest2[p * step:(p + 1) * step]]))
                for p in range(n_parts)]
    for p in range(n_parts):
        x2 = _combine(gathered[p], x2, route, p, n_parts)
    return x2


def _block_diag(w):
    h, dh, _ = w.shape
    eye = jnp.eye(h, dtype=w.dtype)
    return jnp.einsum("hij,hg->higj", w, eye).reshape(h * dh, h * dh)


def kernel(x, norm1_g, w_in, q_norm_g, k_norm_g, conf_dw_w, conf_dw_b, conf_ln_g, conf_ln_b, gmlp_ln_g, gmlp_ln_b, gmlp_ws, gmlp_bs, lru_conv_w, lru_conv_b, lru_wa, lru_ba, lru_wx, lru_bx, lru_lambda, group_norm_g, w_out, norm2_g, ffn_w_gate, ffn_w_up, ffn_w_down, moe_router, moe_w_gate, moe_w_up, moe_w_down):
    bsz, s, d = x.shape
    n = bsz * s
    depth = w_in.shape[0]
    W = GROUP_WIDTH
    row = lambda v: v.reshape(1, -1).astype(F32)

    head_id = jnp.arange(W) // HEAD_DIM
    bd = (head_id[:, None] == head_id[None, :]).astype(BF16)
    bd2 = jnp.concatenate([bd, bd], axis=0)
    t_att = min(ATT_T, s)
    uu = (jnp.arange(t_att)[:, None] >= jnp.arange(t_att)[None, :]).astype(BF16)
    tril = jnp.tril(jnp.ones((GMLP_CHUNK, GMLP_CHUNK), dtype=bool))

    x2 = x.reshape(n, d)
    for l in range(depth):
        qg = row(jnp.tile(q_norm_g[l], GROUP_HEADS) * (HEAD_DIM ** -0.5))
        kg = row(jnp.tile(k_norm_g[l], GROUP_HEADS))
        gn4 = group_norm_g[l].reshape(N_GROUPS, W).astype(F32)
        cw = jnp.zeros((CONF_HALO, W), F32).at[:CONF_KERNEL].set(conf_dw_w[l])
        conf = (cw, row(conf_dw_b[l]), row(conf_ln_g[l]), row(conf_ln_b[l]))
        ws = jnp.where(tril, gmlp_ws[l], 0.0).astype(BF16)
        bs_mat = jnp.repeat(gmlp_bs[l].T, HEAD_DIM, axis=1)
        gmlp = (row(gmlp_ln_g[l]), row(gmlp_ln_b[l]), ws, bs_mat)
        lw = jnp.zeros((8, W), F32).at[:LRU_CONV].set(lru_conv_w[l])
        wax = jnp.concatenate([_block_diag(lru_wa[l]), _block_diag(lru_wx[l])], axis=1).astype(BF16)
        lru = (lw, row(lru_conv_b[l]), wax, row(lru_ba[l]), row(lru_bx[l]), row(lru_lambda[l]))
        q, k, v, y_b, y_c, y_d = _in_mix(x2, s, row(norm1_g[l]), w_in[l].astype(BF16), qg, kg, bd2,
                                         conf, gmlp, lru, gn4)
        to3 = lambda t: t.reshape(bsz, s, t.shape[-1])
        zmax = (ATT_ZMAX_SLACK * HEAD_DIM ** 0.5 * jnp.max(jnp.abs(q_norm_g[l] * k_norm_g[l]))).reshape(1)
        y_a = _sb_attention(zmax.astype(F32), to3(q), to3(k), to3(v), uu, gn4[0:1]).reshape(n, W)
        ys = [y_a, y_b, y_c, y_d]
        x2, hn = _out_proj(ys, w_out[l].astype(BF16), x2, row(norm2_g[l]), packed=(l % 2 == 1))
        j = l // 2
        if l % 2 == 0:
            x2 = _dense_ffn(hn, x2, ffn_w_gate[j].astype(BF16), ffn_w_up[j].astype(BF16),
                            ffn_w_down[j].astype(BF16))
        else:
            x2 = _top2_moe(hn, x2, row(norm2_g[l]), moe_router[j], moe_w_gate[j], moe_w_up[j], moe_w_down[j])
    return x2.reshape(bsz, s, d)
```

```python
import functools

import jax
import jax.numpy as jnp
from jax import lax
from jax.experimental import pallas as pl
from jax.experimental.pallas import tpu as pltpu
from jax.experimental.pallas import tpu_sc as plsc

F32 = jnp.float32
BF16 = jnp.bfloat16

HEAD_DIM = 64
GROUP_HEADS = 4
GROUP_WIDTH = GROUP_HEADS * HEAD_DIM
N_GROUPS = 4
CONF_KERNEL = 31
GMLP_CHUNK = 128
LRU_CONV = 4
LRU_C = 8.0
N_EXPERTS = 8
EPS = 1e-6

LANES = 128
VMEM_LIMIT = 56 * 1024 * 1024

IN_TM = 1024
MIX_T = 256
ATT_T = 256
ATT_UNDERFLOW = 110.0
ATT_ZMAX_SLACK = 1.05
CONF_R = 64
CONF_HALO = 32
LRU_HALO = 8
OUT_TM = 512
FFN_TM = 1024
FFN_TF = 256
ROUTE_TM = 512
ROUTE_IDX_COLS = 8
MOE_TM = 1024
MOE_TF = 512
COMB_TM = 512
COMBINE_PARTS = 2
SC_WINDOW_BYTES = 128 * 1024
SC_MAX_INDEX_VECTOR = 128


def _params(*sem):
    return pltpu.CompilerParams(dimension_semantics=sem, vmem_limit_bytes=VMEM_LIMIT)


def _split_hi_lo(x):
    hi = x.astype(BF16)
    lo = (x - hi.astype(F32)).astype(BF16)
    return hi, lo


SOFTPLUS_CLAMP = 60.0


def _softplus(z):
    return jnp.maximum(jnp.log(1.0 + jnp.exp(jnp.minimum(z, SOFTPLUS_CLAMP))), z)


def _gelu_tanh(x):
    c = 0.7978845608028654
    return 0.5 * x * (1.0 + jnp.tanh(c * (x + 0.044715 * (x * x * x))))


def _pack_bf16_pairs(y):
    c = y.shape[1] // 2
    bits = lax.bitcast_convert_type(y.astype(BF16).astype(F32), jnp.uint32)
    word = (bits[:, :c] >> 16) | bits[:, c:]
    return lax.bitcast_convert_type(word, jnp.int32)


def _unpack_bf16_pairs(w):
    bits = lax.bitcast_convert_type(w, jnp.uint32)
    lo = lax.bitcast_convert_type(bits << 16, F32)
    hi = lax.bitcast_convert_type(bits & jnp.uint32(0xFFFF0000), F32)
    return jnp.concatenate([lo, hi], axis=1)


def _group_rms(y, g):
    ms = jnp.mean(y * y, axis=-1, keepdims=True)
    return y * lax.rsqrt(ms + EPS) * g


def _sb_attn_kernel(zmax_ref, q_ref, k_ref, v_ref, uu_ref, gn_ref, o_ref):
    T = q_ref.shape[1]
    n_pairs = GROUP_WIDTH // LANES
    i = pl.program_id(1)
    lane = lax.broadcasted_iota(jnp.int32, (T, LANES), 1)
    rows = lax.broadcasted_iota(jnp.int32, (2 * T, T), 0)
    cols = lax.broadcasted_iota(jnp.int32, (2 * T, T), 1)
    causal = cols < jnp.where(rows >= T, rows - T, rows)
    uu = uu_ref[...]

    def tile(qs, k2, v2, carry, mask):
        z = lax.dot_general(qs, k2, (((1,), (1,)), ((), ())), preferred_element_type=F32)
        sp = _softplus(z)
        if mask is not None:
            sp = jnp.where(mask, sp, 0.0)
        cum = jnp.dot(sp.astype(BF16), uu, preferred_element_type=F32)
        w = jnp.exp(z - cum - carry)
        if mask is not None:
            w = jnp.where(mask, w, 0.0)
        pv = jnp.dot(w.astype(BF16), v2, preferred_element_type=F32)
        return pv, carry + jnp.sum(sp, axis=1, keepdims=True)

    def kv(r0, pair):
        cs = slice(pair * LANES, (pair + 1) * LANES)
        return k_ref[0, pl.ds(r0, T), cs], v_ref[0, pl.ds(r0, T), cs]

    qs = []
    for pair in range(n_pairs):
        q2 = q_ref[0, :, pair * LANES:(pair + 1) * LANES]
        zero = jnp.zeros_like(q2)
        qs.append(jnp.concatenate([jnp.where(lane < HEAD_DIM, q2, zero),
                                   jnp.where(lane >= HEAD_DIM, q2, zero)], axis=0))
    row0 = pl.multiple_of(i * T, T)
    state = tuple(tile(qs[p], *kv(row0, p), jnp.zeros((2 * T, 1), F32), causal) for p in range(n_pairs))

    dead_at = zmax_ref[0] + ATT_UNDERFLOW

    def alive(st):
        return functools.reduce(jnp.minimum, [jnp.min(carry) for _, carry in st]) <= dead_at

    def cond(c):
        step, live, _ = c
        return (step < i) & live

    def body(c):
        step, _, st = c
        r0 = pl.multiple_of((i - 1 - step) * T, T)
        new = []
        for p in range(n_pairs):
            acc, carry = st[p]
            pv, carry = tile(qs[p], *kv(r0, p), carry, None)
            new.append((acc + pv, carry))
        new = tuple(new)
        return step + 1, alive(new), new

    _, _, state = lax.while_loop(cond, body, (jnp.int32(0), alive(state), state))
    y = jnp.concatenate([jnp.where(lane < HEAD_DIM, acc[:T], acc[T:]) for acc, _ in state], axis=1)
    o_ref[0] = _group_rms(y, gn_ref[...]).astype(o_ref.dtype)


def _sb_attention(zmax, q3, k3, v3, uu, gn):
    b, s, w = q3.shape
    t = min(ATT_T, s)
    return pl.pallas_call(
        _sb_attn_kernel,
        grid=(b, s // t),
        in_specs=[
            pl.BlockSpec(memory_space=pltpu.SMEM),
            pl.BlockSpec((1, t, w), lambda bi, i: (bi, i, 0)),
            pl.BlockSpec((1, s, w), lambda bi, i: (bi, 0, 0)),
            pl.BlockSpec((1, s, w), lambda bi, i: (bi, 0, 0)),
            pl.BlockSpec((t, t), lambda bi, i: (0, 0)),
            pl.BlockSpec((1, w), lambda bi, i: (0, 0)),
        ],
        out_specs=pl.BlockSpec((1, t, w), lambda bi, i: (bi, i, 0)),
        out_shape=jax.ShapeDtypeStruct((b, s, w), BF16),
        compiler_params=_params("arbitrary", "arbitrary"),
        name="sb_attn",
    )(zmax, q3, k3, v3, uu, gn)


def _conformer_rows(val, gate, w_ref, b_ref, lng_ref, lnb_ref, gn, o_ref, row0, hs_ref, sh_ref):
    T = val.shape[0]
    H = CONF_HALO
    SUB = 8
    hs_ref[H:H + T, :] = val * jax.nn.sigmoid(gate)
    off = H - (CONF_KERNEL - 1)
    L = sh_ref.shape[1]
    for s in range(1, SUB):
        sh_ref[s - 1] = hs_ref[s:s + L, :]
    R = min(CONF_R, T)
    for r0 in range(0, T, R):
        acc = jnp.broadcast_to(b_ref[...], (R, GROUP_WIDTH))
        for k in range(CONF_KERNEL):
            s = (off + k) % SUB
            a = off + k - s + r0
            tap = hs_ref[a:a + R, :] if s == 0 else sh_ref[s - 1, a:a + R, :]
            acc = acc + w_ref[k:k + 1, :] * tap
        mu = jnp.mean(acc, axis=-1, keepdims=True)
        xc = acc - mu
        var = jnp.mean(xc * xc, axis=-1, keepdims=True)
        y = xc * lax.rsqrt(var + EPS) * lng_ref[...] + lnb_ref[...]
        y = y * jax.nn.sigmoid(y)
        o_ref[row0 + r0:row0 + r0 + R, :] = _group_rms(y, gn).astype(o_ref.dtype)
    hs_ref[0:H, :] = hs_ref[T:T + H, :]


def _gmlp_rows(u_in, v_in, lng_ref, lnb_ref, ws_ref, bs_ref, gn, o_ref, row0):
    C = GMLP_CHUNK
    lane = lax.broadcasted_iota(jnp.int32, (C, GROUP_WIDTH), 1)
    u = _gelu_tanh(u_in)
    v = _gelu_tanh(v_in)
    mu = jnp.mean(v, axis=-1, keepdims=True)
    xc = v - mu
    var = jnp.mean(xc * xc, axis=-1, keepdims=True)
    vb = (xc * lax.rsqrt(var + EPS) * lng_ref[...] + lnb_ref[...]).astype(BF16)
    mixed = bs_ref[...]
    for h in range(GROUP_HEADS):
        m = jnp.dot(ws_ref[h], vb, preferred_element_type=F32)
        in_head = (lane >= h * HEAD_DIM) & (lane < (h + 1) * HEAD_DIM)
        mixed = mixed + jnp.where(in_head, m, 0.0)
    o_ref[row0:row0 + C, :] = _group_rms(u * mixed, gn).astype(o_ref.dtype)


def _lru_rows(x_in, gate, cw_ref, cb_ref, wax_ref, ba_ref, bx_ref, lam_ref, gn, o_ref, row0, xs_ref, hprev_ref):
    T = x_in.shape[0]
    W = GROUP_WIDTH
    H = LRU_HALO
    xs_ref[H:H + T, :] = x_in
    off = H - (LRU_CONV - 1)
    xb = jnp.broadcast_to(cb_ref[...], (T, W))
    for k in range(LRU_CONV):
        xb = xb + cw_ref[k:k + 1, :] * xs_ref[off + k:off + k + T, :]
    xs_ref[0:H, :] = xs_ref[T:T + H, :]

    pre = jnp.dot(xb.astype(BF16), wax_ref[...], preferred_element_type=F32)
    r = jax.nn.sigmoid(pre[:, :W] + ba_ref[...])
    ig = jax.nn.sigmoid(pre[:, W:] + bx_ref[...])
    nlam = -lam_ref[...]
    log_a = (-LRU_C) * r * (jnp.maximum(nlam, 0.0) + jnp.log1p(jnp.exp(-jnp.abs(nlam))))
    a = jnp.exp(log_a)
    th = jnp.tanh(log_a)
    one_minus_a2 = 2.0 * th / (th - 1.0)
    b = jnp.sqrt(one_minus_a2) * (ig * xb)

    SUB = 8
    a = a.reshape(T // SUB, SUB, W)
    b = b.reshape(T // SUB, SUB, W)
    sub = lax.broadcasted_iota(jnp.int32, a.shape, 1)
    d = 1
    while d < SUB:
        keep = sub >= d
        a_sh = jnp.where(keep, pltpu.roll(a, d, 1), 1.0)
        b_sh = jnp.where(keep, pltpu.roll(b, d, 1), 0.0)
        b = a * b_sh + b
        a = a * a_sh
        d *= 2
    carry = hprev_ref[0:1, :]
    tiles = []
    for g in range(T // SUB):
        hg = b[g] + a[g] * carry
        carry = hg[SUB - 1:SUB]
        tiles.append(hg)
    h = jnp.concatenate(tiles, axis=0)
    hprev_ref[...] = jnp.broadcast_to(carry, hprev_ref.shape)
    y = h * _gelu_tanh(gate)
    o_ref[row0:row0 + T, :] = _group_rms(y, gn).astype(o_ref.dtype)


def _in_mix_kernel(blocks_per_seq, x_ref, g_ref, w_ref, qg_ref, kg_ref, bd_ref,
                   cw_ref, cb_ref, clg_ref, clb_ref,
                   glg_ref, glb_ref, ws_ref, bs_ref,
                   lw_ref, lb_ref, wax_ref, ba_ref, bx_ref, lam_ref, gn_ref,
                   q_ref, k_ref, v_ref, yb_ref, yc_ref, yd_ref,
                   rest_ref, hs_ref, sh_ref, xs_ref, hprev_ref):
    tm = x_ref.shape[0]
    W = GROUP_WIDTH

    @pl.when(pl.program_id(0) % blocks_per_seq == 0)
    def _():
        hs_ref[0:CONF_HALO, :] = jnp.zeros((CONF_HALO, W), F32)
        xs_ref[0:LRU_HALO, :] = jnp.zeros((LRU_HALO, W), F32)
        hprev_ref[...] = jnp.zeros(hprev_ref.shape, F32)

    def step(rest_w, rest_r):
        xf = x_ref[...]
        ms = jnp.mean(xf * xf, axis=-1, keepdims=True)
        h = (xf * lax.rsqrt(ms + EPS) * g_ref[...]).astype(BF16)

        def proj(c):
            return jnp.dot(h, w_ref[:, c * W:(c + 1) * W], preferred_element_type=F32)

        def head_norm(p, gain):
            hi, lo = _split_hi_lo(p * p)
            ss = jnp.dot(jnp.concatenate([hi, lo], axis=1), bd_ref[...], preferred_element_type=F32)
            return p * lax.rsqrt(ss * (1.0 / HEAD_DIM) + EPS) * gain

        def project(c):
            if c == 0:
                q_ref[...] = head_norm(proj(0), qg_ref[...]).astype(BF16)
            elif c == 1:
                k_ref[...] = head_norm(proj(1), kg_ref[...]).astype(BF16)
            elif c == 2:
                v_ref[...] = proj(2).astype(BF16)
            else:
                rest_w[:, (c - 3) * W:(c - 2) * W] = proj(c)

        def cols(r0, n, c):
            return rest_r[r0:r0 + n, c * W:(c + 1) * W]

        T = min(MIX_T, tm)

        def conformer(r0):
            _conformer_rows(cols(r0, T, 0), cols(r0, T, 1), cw_ref, cb_ref, clg_ref, clb_ref, gn_ref[1:2, :],
                            yb_ref, r0, hs_ref, sh_ref)

        def lru(r0):
            _lru_rows(cols(r0, T, 4), cols(r0, T, 5), lw_ref, lb_ref, wax_ref, ba_ref, bx_ref, lam_ref,
                      gn_ref[3:4, :], yd_ref, r0, xs_ref, hprev_ref)

        def gmlp(r0):
            _gmlp_rows(cols(r0, GMLP_CHUNK, 2), cols(r0, GMLP_CHUNK, 3), glg_ref, glb_ref, ws_ref, bs_ref,
                       gn_ref[2:3, :], yc_ref, r0)

        mix = []
        for r0 in range(0, tm, T):
            mix.append(functools.partial(conformer, r0))
            mix.append(functools.partial(lru, r0))
            mix.extend(functools.partial(gmlp, c0) for c0 in range(r0, r0 + T, GMLP_CHUNK))
        n_proj = w_ref.shape[1] // W
        for c in range(3, n_proj):
            project(c)
        per = -(-len(mix) // 4)
        for c in range(4):
            if c > 0:
                project(c - 1)
            for task in mix[c * per:(c + 1) * per]:
                task()

    step(rest_ref, rest_ref)


def _in_mix(x2, seq_len, g, w_bf, qg, kg, bd2, conf, gmlp, lru, gn4):
    n, d = x2.shape
    cols = w_bf.shape[1]
    W = GROUP_WIDTH
    rest_cols = cols - 3 * W
    tm = min(IN_TM, seq_len)
    assert seq_len % tm == 0 and tm % min(MIX_T, tm) == 0 and tm % GMLP_CHUNK == 0
    t = min(MIX_T, tm)
    row = lambda i: (i, 0)
    whole = lambda a: pl.BlockSpec(a.shape, lambda i: (0,) * a.ndim)
    params = (g, w_bf, qg, kg, bd2) + tuple(conf) + tuple(gmlp) + tuple(lru) + (gn4,)
    out_shape = jax.ShapeDtypeStruct((n, W), BF16)
    return pl.pallas_call(
        functools.partial(_in_mix_kernel, seq_len // tm),
        grid=(n // tm,),
        in_specs=[pl.BlockSpec((tm, d), row)] + [whole(a) for a in params],
        out_specs=[pl.BlockSpec((tm, W), row)] * 6,
        out_shape=[out_shape] * 6,
        scratch_shapes=[pltpu.VMEM((tm, rest_cols), F32),
                        pltpu.VMEM((t + CONF_HALO, W), F32),
                        pltpu.VMEM((7, t + CONF_HALO - 8, W), F32),
                        pltpu.VMEM((t + LRU_HALO, W), F32),
                        pltpu.VMEM((8, W), F32)],
        compiler_params=_params("arbitrary"),
        name="in_mix",
    )(x2, *params)


def _out_proj_kernel(ya_ref, yb_ref, yc_ref, yd_ref, w_ref, x_ref, g_ref, xo_ref, h_ref):
    W = GROUP_WIDTH
    acc = x_ref[...]
    for gi, y_ref in enumerate((ya_ref, yb_ref, yc_ref, yd_ref)):
        acc = acc + jnp.dot(y_ref[...], w_ref[gi * W:(gi + 1) * W, :], preferred_element_type=F32)
    xo_ref[...] = acc
    ms = jnp.mean(acc * acc, axis=-1, keepdims=True)
    h = acc * lax.rsqrt(ms + EPS) * g_ref[...]
    if h_ref.dtype == jnp.int32:
        h_ref[...] = _pack_bf16_pairs(h)
    else:
        h_ref[...] = h.astype(h_ref.dtype)


def _out_proj(ys, w_bf, x2, g, packed):
    n, d = x2.shape
    h_shape = jax.ShapeDtypeStruct((n, d // 2), jnp.int32) if packed else jax.ShapeDtypeStruct((n, d), BF16)
    W = GROUP_WIDTH
    tm = min(OUT_TM, n)
    row = lambda i: (i, 0)
    const = lambda i: (0, 0)
    ysp = pl.BlockSpec((tm, W), row)
    return pl.pallas_call(
        _out_proj_kernel,
        grid=(n // tm,),
        in_specs=[ysp, ysp, ysp, ysp,
                  pl.BlockSpec(w_bf.shape, const),
                  pl.BlockSpec((tm, d), row),
                  pl.BlockSpec((1, d), const)],
        out_specs=[pl.BlockSpec((tm, d), row), pl.BlockSpec((tm, h_shape.shape[1]), row)],
        out_shape=[jax.ShapeDtypeStruct((n, d), F32), h_shape],
        compiler_params=_params("arbitrary"),
        name="out_proj",
    )(*ys, w_bf, x2, g)


def _ffn_kernel(h_ref, x_ref, wg_ref, wu_ref, wd_ref, o_ref, act_ref):
    f = wg_ref.shape[1]
    h = h_ref[...]
    for c0 in range(0, f, FFN_TF):
        g = jnp.dot(h, wg_ref[:, c0:c0 + FFN_TF], preferred_element_type=F32)
        u = jnp.dot(h, wu_ref[:, c0:c0 + FFN_TF], preferred_element_type=F32)
        act_ref[:, c0:c0 + FFN_TF] = (g * jax.nn.sigmoid(g) * u).astype(BF16)
    o_ref[...] = x_ref[...] + jnp.dot(act_ref[...], wd_ref[...], preferred_element_type=F32)


def _dense_ffn(hn, x2, wg, wu, wd):
    n, d = x2.shape
    f = wg.shape[1]
    assert f % FFN_TF == 0
    tm = min(FFN_TM, n)
    resident = dict(pipeline_mode=pl.Buffered(1))
    return pl.pallas_call(
        _ffn_kernel,
        grid=(n // tm,),
        in_specs=[
            pl.BlockSpec((tm, d), lambda i: (i, 0)),
            pl.BlockSpec((tm, d), lambda i: (i, 0)),
            pl.BlockSpec((d, f), lambda i: (0, 0), **resident),
            pl.BlockSpec((d, f), lambda i: (0, 0), **resident),
            pl.BlockSpec((f, d), lambda i: (0, 0), **resident),
        ],
        out_specs=pl.BlockSpec((tm, d), lambda i: (i, 0)),
        out_shape=jax.ShapeDtypeStruct((n, d), F32),
        scratch_shapes=[pltpu.VMEM((tm, f), BF16)],
        compiler_params=_params("arbitrary"),
        name="dense_ffn",
    )(hn, x2, wg, wu, wd)


def _router_kernel(x_ref, g_ref, rh_ref, rl_ref, tri_ref, route_ref, idx_ref, cnt_ref, base_ref):
    tm = x_ref.shape[0]

    @pl.when(pl.program_id(0) == 0)
    def _():
        base_ref[...] = jnp.zeros(base_ref.shape, F32)

    xf = x_ref[...]
    ms = jnp.mean(xf * xf, axis=-1, keepdims=True)
    hh, hl = _split_hi_lo(xf * lax.rsqrt(ms + EPS) * g_ref[...])
    rh = rh_ref[...]
    logits = (jnp.dot(hh, rh, preferred_element_type=F32)
              + jnp.dot(hl, rh, preferred_element_type=F32)
              + jnp.dot(hh, rl_ref[...], preferred_element_type=F32))
    lane = lax.broadcasted_iota(jnp.int32, (tm, LANES), 1).astype(F32)
    neg = jnp.float32(-jnp.inf)
    logits = jnp.where(lane < N_EXPERTS, logits, neg)
    m1 = jnp.max(logits, axis=1, keepdims=True)
    i1 = jnp.min(jnp.where(logits == m1, lane, float(LANES)), axis=1, keepdims=True)
    l2 = jnp.where(lane == i1, neg, logits)
    m2 = jnp.max(l2, axis=1, keepdims=True)
    i2 = jnp.min(jnp.where(l2 == m2, lane, float(LANES)), axis=1, keepdims=True)
    e = jnp.exp(m2 - m1)
    g1 = 1.0 / (1.0 + e)
    g2 = e / (1.0 + e)
    oh1 = jnp.where(lane == i1, 1.0, 0.0)
    oh2 = jnp.where(lane == i2, 1.0, 0.0)
    oh = oh1 + oh2
    before = jnp.dot(tri_ref[...], oh.astype(BF16), preferred_element_type=F32) + base_ref[0:1, :]
    r1 = jnp.sum(oh1 * before, axis=1, keepdims=True)
    r2 = jnp.sum(oh2 * before, axis=1, keepdims=True)
    base = base_ref[0:1, :] + jnp.sum(oh, axis=0, keepdims=True)
    base_ref[...] = jnp.broadcast_to(base, base_ref.shape)
    cnt_ref[...] = jnp.broadcast_to(base, cnt_ref.shape)
    out = jnp.where(lane == 0, i1, 0.0)
    out = jnp.where(lane == 1, i2, out)
    out = jnp.where(lane == 2, g1, out)
    out = jnp.where(lane == 3, g2, out)
    out = jnp.where(lane == 4, r1, out)
    out = jnp.where(lane == 5, r2, out)
    route_ref[...] = out
    idx_ref[...] = out[:, :ROUTE_IDX_COLS].astype(jnp.int32)


def _router(x2, g, rh, rl, tri):
    n, d = x2.shape
    tm = min(ROUTE_TM, n)
    return pl.pallas_call(
        _router_kernel,
        grid=(n // tm,),
        in_specs=[
            pl.BlockSpec((tm, d), lambda i: (i, 0)),
            pl.BlockSpec((1, d), lambda i: (0, 0)),
            pl.BlockSpec((d, LANES), lambda i: (0, 0)),
            pl.BlockSpec((d, LANES), lambda i: (0, 0)),
            pl.BlockSpec((tm, tm), lambda i: (0, 0)),
        ],
        out_specs=[pl.BlockSpec((tm, LANES), lambda i: (i, 0)),
                   pl.BlockSpec((tm, ROUTE_IDX_COLS), lambda i: (i, 0)),
                   pl.BlockSpec((8, LANES), lambda i: (0, 0))],
        out_shape=[jax.ShapeDtypeStruct((n, LANES), F32), jax.ShapeDtypeStruct((n, ROUTE_IDX_COLS), jnp.int32),
                   jax.ShapeDtypeStruct((8, LANES), F32)],
        scratch_shapes=[pltpu.VMEM((8, LANES), F32)],
        compiler_params=_params("arbitrary"),
        name="router",
    )(x2, g, rh, rl, tri)


def _sc_window_rows(table):
    row_bytes = table.shape[1] * table.dtype.itemsize
    return min(SC_MAX_INDEX_VECTOR, SC_WINDOW_BYTES // row_bytes)


def _sc_gather_rows(table, idx):
    _, d = table.shape
    b = idx.shape[0]
    win = _sc_window_rows(table)
    sc = plsc.get_sparse_core_info()
    n_workers = sc.num_cores * sc.num_subcores
    per_w = b // n_workers
    n_win = per_w // win
    assert per_w * n_workers == b and n_win * win == per_w and n_win % 2 == 0, (b, n_workers, win)
    mesh = plsc.VectorSubcoreMesh(core_axis_name="c", subcore_axis_name="s")
    dma = pltpu.SemaphoreType.DMA

    @functools.partial(
        pl.kernel, mesh=mesh, out_type=jax.ShapeDtypeStruct((b, d), table.dtype), name="sc_gather_rows",
        scratch_types=[pltpu.VMEM((win,), jnp.int32), pltpu.VMEM((win,), jnp.int32),
                       pltpu.VMEM((win, d), table.dtype), pltpu.VMEM((win, d), table.dtype),
                       dma, dma, dma, dma])
    def gather_kernel(table_hbm, idx_hbm, out_hbm, i0, i1, r0, r1, g0, g1, w0, w1):
        wid = lax.axis_index("s") * sc.num_cores + lax.axis_index("c")
        base = wid * per_w
        idxb, rows, gsem, wsem = (i0, i1), (r0, r1), (g0, g1), (w0, w1)

        def off(c):
            return pl.multiple_of(base + c * win, win)

        def gather(s):
            return pltpu.make_async_copy(table_hbm.at[idxb[s]], rows[s], gsem[s])

        def write(c, s):
            return pltpu.make_async_copy(rows[s], out_hbm.at[pl.ds(off(c), win)], wsem[s])

        pltpu.sync_copy(idx_hbm.at[pl.ds(off(0), win)], idxb[0])
        gather(0).start()

        @pl.loop(0, n_win, step=2)
        def _(c):
            for s in (0, 1):
                cc = c + s

                @pl.when(cc + 1 < n_win)
                def _():
                    @pl.when(cc >= 1)
                    def _():
                        write(cc - 1, 1 - s).wait()

                    pltpu.sync_copy(idx_hbm.at[pl.ds(off(cc + 1), win)], idxb[1 - s])
                    gather(1 - s).start()

                gather(s).wait()
                write(cc, s).start()

        write(n_win - 2, 0).wait()
        write(n_win - 1, 1).wait()

    return gather_kernel(table, idx)


def _sc_dispatch_rows(src, dest1, dest2, n_rows):
    n, d = src.shape
    win = _sc_window_rows(src)
    sc = plsc.get_sparse_core_info()
    n_workers = sc.num_cores * sc.num_subcores
    per_w = n // n_workers
    n_win = per_w // win
    assert per_w * n_workers == n and n_win * win == per_w and n_win % 2 == 0, (n, n_workers, win)
    mesh = plsc.VectorSubcoreMesh(core_axis_name="c", subcore_axis_name="s")
    dma = pltpu.SemaphoreType.DMA
    ivec = pltpu.VMEM((win,), jnp.int32)
    rbuf = pltpu.VMEM((win, d), src.dtype)

    @functools.partial(
        pl.kernel, mesh=mesh, out_type=jax.ShapeDtypeStruct((n_rows, d), src.dtype), name="sc_dispatch_rows",
        scratch_types=[ivec, ivec, ivec, ivec, rbuf, rbuf, dma, dma, dma, dma, dma, dma])
    def dispatch_kernel(src_hbm, d1_hbm, d2_hbm, out_hbm, a0, a1, b0, b1, r0, r1, l0, l1, p0, p1, q0, q1):
        wid = lax.axis_index("s") * sc.num_cores + lax.axis_index("c")
        base = wid * per_w
        ia, ib, rows, lsem, psem, qsem = (a0, a1), (b0, b1), (r0, r1), (l0, l1), (p0, p1), (q0, q1)

        def off(c):
            return pl.multiple_of(base + c * win, win)

        def load(c, s):
            return pltpu.make_async_copy(src_hbm.at[pl.ds(off(c), win)], rows[s], lsem[s])

        def scatters(s):
            return (pltpu.make_async_copy(rows[s], out_hbm.at[ia[s]], psem[s]),
                    pltpu.make_async_copy(rows[s], out_hbm.at[ib[s]], qsem[s]))

        def fetch(c, s):
            pltpu.sync_copy(d1_hbm.at[pl.ds(off(c), win)], ia[s])
            pltpu.sync_copy(d2_hbm.at[pl.ds(off(c), win)], ib[s])
            load(c, s).start()

        fetch(0, 0)

        @pl.loop(0, n_win, step=2)
        def _(c):
            for s in (0, 1):
                cc = c + s

                @pl.when(cc + 1 < n_win)
                def _():
                    @pl.when(cc >= 1)
                    def _():
                        for cp in scatters(1 - s):
                            cp.wait()

                    fetch(cc + 1, 1 - s)

                load(cc, s).wait()
                for cp in scatters(s):
                    cp.start()

        for s in (0, 1):
            for cp in scatters(s):
                cp.wait()

    return dispatch_kernel(src, dest1, dest2)


def _moe_kernel(bexp_ref, nused_ref, nvalid_ref, x_ref, wg_ref, wu_ref, wd_ref, o_ref, xb, acc):
    i = pl.program_id(0)
    j = pl.program_id(1)
    nj = pl.num_programs(1)
    used = i < nused_ref[0]

    @pl.when(used & (j == 0))
    def _():
        x = _unpack_bf16_pairs(x_ref[...])
        row = lax.broadcasted_iota(jnp.int32, x.shape, 0)
        xb[...] = jnp.where(row < nvalid_ref[i], x, 0.0).astype(BF16)
        acc[...] = jnp.zeros(acc.shape, F32)

    @pl.when(used)
    def _():
        x = xb[...]
        g = jnp.dot(x, wg_ref[0].astype(BF16), preferred_element_type=F32)
        u = jnp.dot(x, wu_ref[0].astype(BF16), preferred_element_type=F32)
        act = (g * jax.nn.sigmoid(g) * u).astype(BF16)
        acc[...] += jnp.dot(act, wd_ref[0].astype(BF16), preferred_element_type=F32)

        @pl.when(j == nj - 1)
        def _():
            o_ref[...] = _pack_bf16_pairs(acc[...])

    @pl.when(jnp.logical_not(used) & (j == 0))
    def _():
        o_ref[...] = jnp.zeros(o_ref.shape, o_ref.dtype)


def _moe_experts(block_exp, n_used, n_valid, xs, wg, wu, wd, tm):
    n_rows, dp = xs.shape
    d = 2 * dp
    n_blocks = n_rows // tm
    f = wg.shape[2]
    tf = MOE_TF if f % MOE_TF == 0 else f
    nj = f // tf
    assert nj >= 2

    def live(i, j, be, nu):
        u = i < nu[0]
        return jnp.where(u, i, nu[0] - 1), jnp.where(u, j, nj - 1)

    def w_in_map(i, j, be, nu, nv):
        ii, jj = live(i, j, be, nu)
        return (be[ii], 0, jj)

    def w_down_map(i, j, be, nu, nv):
        ii, jj = live(i, j, be, nu)
        return (be[ii], jj, 0)

    def x_map(i, j, be, nu, nv):
        return (live(i, j, be, nu)[0], 0)

    def out_map(i, j, be, nu, nv):
        return (i, 0)

    grid_spec = pltpu.PrefetchScalarGridSpec(
        num_scalar_prefetch=3,
        grid=(n_blocks, nj),
        in_specs=[
            pl.BlockSpec((tm, dp), x_map),
            pl.BlockSpec((1, d, tf), w_in_map),
            pl.BlockSpec((1, d, tf), w_in_map),
            pl.BlockSpec((1, tf, d), w_down_map),
        ],
        out_specs=pl.BlockSpec((tm, dp), out_map),
        scratch_shapes=[pltpu.VMEM((tm, d), BF16), pltpu.VMEM((tm, d), F32)],
    )
    return pl.pallas_call(
        _moe_kernel,
        grid_spec=grid_spec,
        out_shape=jax.ShapeDtypeStruct((n_rows, dp), jnp.int32),
        compiler_params=_params("arbitrary", "arbitrary"),
        name="moe_experts",
    )(block_exp, n_used, n_valid, xs, wg, wu, wd)


def _combine_kernel(x_ref, route_ref, a_ref, b_ref, o_ref):
    route = route_ref[...]
    ya = _unpack_bf16_pairs(a_ref[...])
    yb = _unpack_bf16_pairs(b_ref[...])
    o_ref[...] = x_ref[...] + route[:, 2:3] * ya + route[:, 3:4] * yb


def _combine(yg, x2, route, part, n_parts):
    n, d = x2.shape
    dp = yg.shape[1]
    tm = min(COMB_TM, n // n_parts)
    nb = n // n_parts // tm
    first = part * nb
    return pl.pallas_call(
        _combine_kernel,
        grid=(nb,),
        in_specs=[pl.BlockSpec((tm, d), lambda i: (i + first, 0)),
                  pl.BlockSpec((tm, LANES), lambda i: (i + first, 0)),
                  pl.BlockSpec((tm, dp), lambda i: (i, 0)),
                  pl.BlockSpec((tm, dp), lambda i: (i + nb, 0))],
        out_specs=pl.BlockSpec((tm, d), lambda i: (i + first, 0)),
        out_shape=jax.ShapeDtypeStruct((n, d), F32),
        input_output_aliases={0: 0},
        compiler_params=_params("arbitrary"),
        name="moe_combine",
    )(x2, route, yg, yg)


def _top2_moe(hn, x2, g2, router, wg, wu, wd):
    n, d = x2.shape
    a = 2 * n
    tm = min(MOE_TM, n)
    r_pad = jnp.zeros((d, LANES), F32).at[:, :N_EXPERTS].set(router)
    rh, rl = _split_hi_lo(r_pad)
    rt = min(ROUTE_TM, n)
    tri = (lax.broadcasted_iota(jnp.int32, (rt, rt), 1) < lax.broadcasted_iota(jnp.int32, (rt, rt), 0)).astype(BF16)
    route, cols, cnt = _router(x2, g2, rh, rl, tri)

    counts = cnt[0, :N_EXPERTS].astype(jnp.int32)
    padded = (counts + tm - 1) // tm * tm
    pad_ends = jnp.cumsum(padded)
    pad_starts = pad_ends - padded
    def lookup(table, idx):
        hit = idx[:, None] == jnp.arange(N_EXPERTS, dtype=jnp.int32)[None, :]
        return jnp.sum(jnp.where(hit, table[None, :], 0), axis=1)

    dest1 = lookup(pad_starts, cols[:, 0]) + cols[:, 4]
    dest2 = lookup(pad_starts, cols[:, 1]) + cols[:, 5]
    n_blocks = a // tm + N_EXPERTS
    n_rows = n_blocks * tm
    block_start = jnp.arange(n_blocks, dtype=jnp.int32) * tm
    block_exp = jnp.minimum(jnp.sum(block_start[:, None] >= pad_ends[None, :], axis=1), N_EXPERTS - 1).astype(jnp.int32)
    n_used = (pad_ends[-1] // tm).astype(jnp.int32).reshape(1)
    n_valid = jnp.clip(lookup(counts, block_exp) - (block_start - lookup(pad_starts, block_exp)), 0, tm).astype(jnp.int32)

    xs = _sc_dispatch_rows(hn, dest1, dest2, n_rows)
    ys = _moe_experts(block_exp, n_used, n_valid, xs, wg, wu, wd, tm)
    n_parts = COMBINE_PARTS if n % (COMBINE_PARTS * COMB_TM) == 0 else 1
    step = n // n_parts
    gathered = [_sc_gather_rows(ys, jnp.concatenate([dest1[p * step:(p + 1) * step], dest2[p * step:(p + 1) * step]]))
                for p in range(n_parts)]
    for p in range(n_parts):
        x2 = _combine(gathered[p], x2, route, p, n_parts)
    return x2


def _block_diag(w):
    h, dh, _ = w.shape
    eye = jnp.eye(h, dtype=w.dtype)
    return jnp.einsum("hij,hg->higj", w, eye).reshape(h * dh, h * dh)


def kernel(x, norm1_g, w_in, q_norm_g, k_norm_g, conf_dw_w, conf_dw_b, conf_ln_g, conf_ln_b, gmlp_ln_g, gmlp_ln_b, gmlp_ws, gmlp_bs, lru_conv_w, lru_conv_b, lru_wa, lru_ba, lru_wx, lru_bx, lru_lambda, group_norm_g, w_out, norm2_g, ffn_w_gate, ffn_w_up, ffn_w_down, moe_router, moe_w_gate, moe_w_up, moe_w_down):
    bsz, s, d = x.shape
    n = bsz * s
    depth = w_in.shape[0]
    W = GROUP_WIDTH
    row = lambda v: v.reshape(1, -1).astype(F32)

    head_id = jnp.arange(W) // HEAD_DIM
    bd = (head_id[:, None] == head_id[None, :]).astype(BF16)
    bd2 = jnp.concatenate([bd, bd], axis=0)
    t_att = min(ATT_T, s)
    uu = (jnp.arange(t_att)[:, None] >= jnp.arange(t_att)[None, :]).astype(BF16)
    tril = jnp.tril(jnp.ones((GMLP_CHUNK, GMLP_CHUNK), dtype=bool))

    x2 = x.reshape(n, d)
    for l in range(depth):
        qg = row(jnp.tile(q_norm_g[l], GROUP_HEADS) * (HEAD_DIM ** -0.5))
        kg = row(jnp.tile(k_norm_g[l], GROUP_HEADS))
        gn4 = group_norm_g[l].reshape(N_GROUPS, W).astype(F32)
        cw = jnp.zeros((CONF_HALO, W), F32).at[:CONF_KERNEL].set(conf_dw_w[l])
        conf = (cw, row(conf_dw_b[l]), row(conf_ln_g[l]), row(conf_ln_b[l]))
        ws = jnp.where(tril, gmlp_ws[l], 0.0).astype(BF16)
        bs_mat = jnp.repeat(gmlp_bs[l].T, HEAD_DIM, axis=1)
        gmlp = (row(gmlp_ln_g[l]), row(gmlp_ln_b[l]), ws, bs_mat)
        lw = jnp.zeros((8, W), F32).at[:LRU_CONV].set(lru_conv_w[l])
        wax = jnp.concatenate([_block_diag(lru_wa[l]), _block_diag(lru_wx[l])], axis=1).astype(BF16)
        lru = (lw, row(lru_conv_b[l]), wax, row(lru_ba[l]), row(lru_bx[l]), row(lru_lambda[l]))
        q, k, v, y_b, y_c, y_d = _in_mix(x2, s, row(norm1_g[l]), w_in[l].astype(BF16), qg, kg, bd2,
                                         conf, gmlp, lru, gn4)
        to3 = lambda t: t.reshape(bsz, s, t.shape[-1])
        zmax = (ATT_ZMAX_SLACK * HEAD_DIM ** 0.5 * jnp.max(jnp.abs(q_norm_g[l] * k_norm_g[l]))).reshape(1)
        y_a = _sb_attention(zmax.astype(F32), to3(q), to3(k), to3(v), uu, gn4[0:1]).reshape(n, W)
        ys = [y_a, y_b, y_c, y_d]
        x2, hn = _out_proj(ys, w_out[l].astype(BF16), x2, row(norm2_g[l]), packed=(l % 2 == 1))
        j = l // 2
        if l % 2 == 0:
            x2 = _dense_ffn(hn, x2, ffn_w_gate[j].astype(BF16), ffn_w_up[j].astype(BF16),
                            ffn_w_down[j].astype(BF16))
        else:
            x2 = _top2_moe(hn, x2, row(norm2_g[l]), moe_router[j], moe_w_gate[j], moe_w_up[j], moe_w_down[j])
    return x2.reshape(bsz, s, d)
```

```python
import functools

import jax
import jax.numpy as jnp
from jax import lax
from jax.experimental import pallas as pl
from jax.experimental.pallas import tpu as pltpu
from jax.experimental.pallas import tpu_sc as plsc

F32 = jnp.float32
BF16 = jnp.bfloat16

HEAD_DIM = 64
GROUP_HEADS = 4
GROUP_WIDTH = GROUP_HEADS * HEAD_DIM
N_GROUPS = 4
CONF_KERNEL = 31
GMLP_CHUNK = 128
LRU_CONV = 4
LRU_C = 8.0
N_EXPERTS = 8
EPS = 1e-6

LANES = 128
VMEM_LIMIT = 56 * 1024 * 1024

IN_TM = 1024
MIX_T = 256
ATT_T = 256
ATT_UNDERFLOW = 110.0
ATT_ZMAX_SLACK = 1.05
CONF_R = 64
CONF_HALO = 32
LRU_HALO = 8
OUT_TM = 512
FFN_TM = 1024
FFN_TF = 256
ROUTE_TM = 512
ROUTE_IDX_COLS = 8
MOE_TM = 1024
MOE_TF = 512
COMB_TM = 512
COMBINE_PARTS = 2
SC_WINDOW_BYTES = 128 * 1024
SC_MAX_INDEX_VECTOR = 128


def _params(*sem):
    return pltpu.CompilerParams(dimension_semantics=sem, vmem_limit_bytes=VMEM_LIMIT)


def _split_hi_lo(x):
    hi = x.astype(BF16)
    lo = (x - hi.astype(F32)).astype(BF16)
    return hi, lo


SOFTPLUS_CLAMP = 60.0


def _softplus(z):
    return jnp.maximum(jnp.log(1.0 + jnp.exp(jnp.minimum(z, SOFTPLUS_CLAMP))), z)


def _gelu_tanh(x):
    c = 0.7978845608028654
    return 0.5 * x * (1.0 + jnp.tanh(c * (x + 0.044715 * (x * x * x))))


def _pack_bf16_pairs(y):
    c = y.shape[1] // 2
    bits = lax.bitcast_convert_type(y.astype(BF16).astype(F32), jnp.uint32)
    word = (bits[:, :c] >> 16) | bits[:, c:]
    return lax.bitcast_convert_type(word, jnp.int32)


def _unpack_bf16_pairs(w):
    bits = lax.bitcast_convert_type(w, jnp.uint32)
    lo = lax.bitcast_convert_type(bits << 16, F32)
    hi = lax.bitcast_convert_type(bits & jnp.uint32(0xFFFF0000), F32)
    return jnp.concatenate([lo, hi], axis=1)


def _group_rms(y, g):
    ms = jnp.mean(y * y, axis=-1, keepdims=True)
    return y * lax.rsqrt(ms + EPS) * g


def _sb_attn_kernel(zmax_ref, q_ref, k_ref, v_ref, uu_ref, gn_ref, o_ref):
    T = q_ref.shape[1]
    n_pairs = GROUP_WIDTH // LANES
    i = pl.program_id(1)
    lane = lax.broadcasted_iota(jnp.int32, (T, LANES), 1)
    rows = lax.broadcasted_iota(jnp.int32, (2 * T, T), 0)
    cols = lax.broadcasted_iota(jnp.int32, (2 * T, T), 1)
    causal = cols < jnp.where(rows >= T, rows - T, rows)
    uu = uu_ref[...]

    def tiles(r0, carries, mask):
        ps = range(n_pairs)
        cs = [slice(p * LANES, (p + 1) * LANES) for p in ps]
        z = [lax.dot_general(qs[p], k_ref[0, pl.ds(r0, T), cs[p]], (((1,), (1,)), ((), ())),
                             preferred_element_type=F32) for p in ps]
        sp = [_softplus(z[p]) for p in ps]
        if mask is not None:
            sp = [jnp.where(mask, s, 0.0) for s in sp]
        cum = [jnp.dot(sp[p].astype(BF16), uu, preferred_element_type=F32) for p in ps]
        w = [jnp.exp(z[p] - cum[p] - carries[p]) for p in ps]
        if mask is not None:
            w = [jnp.where(mask, x, 0.0) for x in w]
        pv = [jnp.dot(w[p].astype(BF16), v_ref[0, pl.ds(r0, T), cs[p]], preferred_element_type=F32)
              for p in ps]
        return pv, [carries[p] + jnp.sum(sp[p], axis=1, keepdims=True) for p in ps]

    qs = []
    for pair in range(n_pairs):
        q2 = q_ref[0, :, pair * LANES:(pair + 1) * LANES]
        zero = jnp.zeros_like(q2)
        qs.append(jnp.concatenate([jnp.where(lane < HEAD_DIM, q2, zero),
                                   jnp.where(lane >= HEAD_DIM, q2, zero)], axis=0))
    row0 = pl.multiple_of(i * T, T)
    state = tuple(zip(*tiles(row0, [jnp.zeros((2 * T, 1), F32)] * n_pairs, causal)))

    dead_at = zmax_ref[0] + ATT_UNDERFLOW

    def alive(st):
        return functools.reduce(jnp.minimum, [jnp.min(carry) for _, carry in st]) <= dead_at

    def cond(c):
        step, live, _ = c
        return (step < i) & live

    def body(c):
        step, _, st = c
        r0 = pl.multiple_of((i - 1 - step) * T, T)
        pv, carries = tiles(r0, [carry for _, carry in st], None)
        new = tuple((st[p][0] + pv[p], carries[p]) for p in range(n_pairs))
        return step + 1, alive(new), new

    _, _, state = lax.while_loop(cond, body, (jnp.int32(0), alive(state), state))
    y = jnp.concatenate([jnp.where(lane < HEAD_DIM, acc[:T], acc[T:]) for acc, _ in state], axis=1)
    o_ref[0] = _group_rms(y, gn_ref[...]).astype(o_ref.dtype)


def _sb_attention(zmax, q3, k3, v3, uu, gn):
    b, s, w = q3.shape
    t = min(ATT_T, s)
    return pl.pallas_call(
        _sb_attn_kernel,
        grid=(b, s // t),
        in_specs=[
            pl.BlockSpec(memory_space=pltpu.SMEM),
            pl.BlockSpec((1, t, w), lambda bi, i: (bi, i, 0)),
            pl.BlockSpec((1, s, w), lambda bi, i: (bi, 0, 0)),
            pl.BlockSpec((1, s, w), lambda bi, i: (bi, 0, 0)),
            pl.BlockSpec((t, t), lambda bi, i: (0, 0)),
            pl.BlockSpec((1, w), lambda bi, i: (0, 0)),
        ],
        out_specs=pl.BlockSpec((1, t, w), lambda bi, i: (bi, i, 0)),
        out_shape=jax.ShapeDtypeStruct((b, s, w), BF16),
        compiler_params=_params("arbitrary", "arbitrary"),
        name="sb_attn",
    )(zmax, q3, k3, v3, uu, gn)


def _conformer_rows(val, gate, w_ref, b_ref, lng_ref, lnb_ref, gn, o_ref, row0, hs_ref, sh_ref):
    T = val.shape[0]
    H = CONF_HALO
    SUB = 8
    hs_ref[H:H + T, :] = val * jax.nn.sigmoid(gate)
    off = H - (CONF_KERNEL - 1)
    L = sh_ref.shape[1]
    for s in range(1, SUB):
        sh_ref[s - 1] = hs_ref[s:s + L, :]
    R = min(CONF_R, T)
    for r0 in range(0, T, R):
        acc = jnp.broadcast_to(b_ref[...], (R, GROUP_WIDTH))
        for k in range(CONF_KERNEL):
            s = (off + k) % SUB
            a = off + k - s + r0
            tap = hs_ref[a:a + R, :] if s == 0 else sh_ref[s - 1, a:a + R, :]
            acc = acc + w_ref[k:k + 1, :] * tap
        mu = jnp.mean(acc, axis=-1, keepdims=True)
        xc = acc - mu
        var = jnp.mean(xc * xc, axis=-1, keepdims=True)
        y = xc * lax.rsqrt(var + EPS) * lng_ref[...] + lnb_ref[...]
        y = y * jax.nn.sigmoid(y)
        o_ref[row0 + r0:row0 + r0 + R, :] = _group_rms(y, gn).astype(o_ref.dtype)
    hs_ref[0:H, :] = hs_ref[T:T + H, :]


def _gmlp_rows(u_in, v_in, lng_ref, lnb_ref, ws_ref, bs_ref, gn, o_ref, row0):
    C = GMLP_CHUNK
    lane = lax.broadcasted_iota(jnp.int32, (C, GROUP_WIDTH), 1)
    u = _gelu_tanh(u_in)
    v = _gelu_tanh(v_in)
    mu = jnp.mean(v, axis=-1, keepdims=True)
    xc = v - mu
    var = jnp.mean(xc * xc, axis=-1, keepdims=True)
    vb = (xc * lax.rsqrt(var + EPS) * lng_ref[...] + lnb_ref[...]).astype(BF16)
    mixed = bs_ref[...]
    for h in range(GROUP_HEADS):
        m = jnp.dot(ws_ref[h], vb, preferred_element_type=F32)
        in_head = (lane >= h * HEAD_DIM) & (lane < (h + 1) * HEAD_DIM)
        mixed = mixed + jnp.where(in_head, m, 0.0)
    o_ref[row0:row0 + C, :] = _group_rms(u * mixed, gn).astype(o_ref.dtype)


def _lru_rows(x_in, gate, cw_ref, cb_ref, wax_ref, ba_ref, bx_ref, lam_ref, gn, o_ref, row0, xs_ref, hprev_ref):
    T = x_in.shape[0]
    W = GROUP_WIDTH
    H = LRU_HALO
    xs_ref[H:H + T, :] = x_in
    off = H - (LRU_CONV - 1)
    xb = jnp.broadcast_to(cb_ref[...], (T, W))
    for k in range(LRU_CONV):
        xb = xb + cw_ref[k:k + 1, :] * xs_ref[off + k:off + k + T, :]
    xs_ref[0:H, :] = xs_ref[T:T + H, :]

    pre = jnp.dot(xb.astype(BF16), wax_ref[...], preferred_element_type=F32)
    r = jax.nn.sigmoid(pre[:, :W] + ba_ref[...])
    ig = jax.nn.sigmoid(pre[:, W:] + bx_ref[...])
    nlam = -lam_ref[...]
    log_a = (-LRU_C) * r * (jnp.maximum(nlam, 0.0) + jnp.log1p(jnp.exp(-jnp.abs(nlam))))
    a = jnp.exp(log_a)
    th = jnp.tanh(log_a)
    one_minus_a2 = 2.0 * th / (th - 1.0)
    b = jnp.sqrt(one_minus_a2) * (ig * xb)

    SUB = 8
    a = a.reshape(T // SUB, SUB, W)
    b = b.reshape(T // SUB, SUB, W)
    sub = lax.broadcasted_iota(jnp.int32, a.shape, 1)
    d = 1
    while d < SUB:
        keep = sub >= d
        a_sh = jnp.where(keep, pltpu.roll(a, d, 1), 1.0)
        b_sh = jnp.where(keep, pltpu.roll(b, d, 1), 0.0)
        b = a * b_sh + b
        a = a * a_sh
        d *= 2
    carry = hprev_ref[0:1, :]
    tiles = []
    for g in range(T // SUB):
        hg = b[g] + a[g] * carry
        carry = hg[SUB - 1:SUB]
        tiles.append(hg)
    h = jnp.concatenate(tiles, axis=0)
    hprev_ref[...] = jnp.broadcast_to(carry, hprev_ref.shape)
    y = h * _gelu_tanh(gate)
    o_ref[row0:row0 + T, :] = _group_rms(y, gn).astype(o_ref.dtype)


def _in_mix_kernel(blocks_per_seq, x_ref, g_ref, w_ref, qg_ref, kg_ref, bd_ref,
                   cw_ref, cb_ref, clg_ref, clb_ref,
                   glg_ref, glb_ref, ws_ref, bs_ref,
                   lw_ref, lb_ref, wax_ref, ba_ref, bx_ref, lam_ref, gn_ref,
                   q_ref, k_ref, v_ref, yb_ref, yc_ref, yd_ref,
                   rest_ref, hs_ref, sh_ref, xs_ref, hprev_ref):
    tm = x_ref.shape[0]
    W = GROUP_WIDTH

    @pl.when(pl.program_id(0) % blocks_per_seq == 0)
    def _():
        hs_ref[0:CONF_HALO, :] = jnp.zeros((CONF_HALO, W), F32)
        xs_ref[0:LRU_HALO, :] = jnp.zeros((LRU_HALO, W), F32)
        hprev_ref[...] = jnp.zeros(hprev_ref.shape, F32)

    def step(rest_w, rest_r):
        xf = x_ref[...]
        ms = jnp.mean(xf * xf, axis=-1, keepdims=True)
        h = (xf * lax.rsqrt(ms + EPS) * g_ref[...]).astype(BF16)

        def proj(c):
            return jnp.dot(h, w_ref[:, c * W:(c + 1) * W], preferred_element_type=F32)

        def head_norm(p, gain):
            hi, lo = _split_hi_lo(p * p)
            ss = jnp.dot(jnp.concatenate([hi, lo], axis=1), bd_ref[...], preferred_element_type=F32)
            return p * lax.rsqrt(ss * (1.0 / HEAD_DIM) + EPS) * gain

        def project(c):
            if c == 0:
                q_ref[...] = head_norm(proj(0), qg_ref[...]).astype(BF16)
            elif c == 1:
                k_ref[...] = head_norm(proj(1), kg_ref[...]).astype(BF16)
            elif c == 2:
                v_ref[...] = proj(2).astype(BF16)
            else:
                rest_w[:, (c - 3) * W:(c - 2) * W] = proj(c)

        def cols(r0, n, c):
            return rest_r[r0:r0 + n, c * W:(c + 1) * W]

        T = min(MIX_T, tm)

        def conformer(r0):
            _conformer_rows(cols(r0, T, 0), cols(r0, T, 1), cw_ref, cb_ref, clg_ref, clb_ref, gn_ref[1:2, :],
                            yb_ref, r0, hs_ref, sh_ref)

        def lru(r0):
            _lru_rows(cols(r0, T, 4), cols(r0, T, 5), lw_ref, lb_ref, wax_ref, ba_ref, bx_ref, lam_ref,
                      gn_ref[3:4, :], yd_ref, r0, xs_ref, hprev_ref)

        def gmlp(r0):
            _gmlp_rows(cols(r0, GMLP_CHUNK, 2), cols(r0, GMLP_CHUNK, 3), glg_ref, glb_ref, ws_ref, bs_ref,
                       gn_ref[2:3, :], yc_ref, r0)

        mix = []
        for r0 in range(0, tm, T):
            mix.append(functools.partial(conformer, r0))
            mix.append(functools.partial(lru, r0))
            mix.extend(functools.partial(gmlp, c0) for c0 in range(r0, r0 + T, GMLP_CHUNK))
        n_proj = w_ref.shape[1] // W
        for c in range(3, n_proj):
            project(c)
        per = -(-len(mix) // 4)
        for c in range(4):
            if c > 0:
                project(c - 1)
            for task in mix[c * per:(c + 1) * per]:
                task()

    step(rest_ref, rest_ref)


def _in_mix(x2, seq_len, g, w_bf, qg, kg, bd2, conf, gmlp, lru, gn4):
    n, d = x2.shape
    cols = w_bf.shape[1]
    W = GROUP_WIDTH
    rest_cols = cols - 3 * W
    tm = min(IN_TM, seq_len)
    assert seq_len % tm == 0 and tm % min(MIX_T, tm) == 0 and tm % GMLP_CHUNK == 0
    t = min(MIX_T, tm)
    row = lambda i: (i, 0)
    whole = lambda a: pl.BlockSpec(a.shape, lambda i: (0,) * a.ndim)
    params = (g, w_bf, qg, kg, bd2) + tuple(conf) + tuple(gmlp) + tuple(lru) + (gn4,)
    out_shape = jax.ShapeDtypeStruct((n, W), BF16)
    return pl.pallas_call(
        functools.partial(_in_mix_kernel, seq_len // tm),
        grid=(n // tm,),
        in_specs=[pl.BlockSpec((tm, d), row)] + [whole(a) for a in params],
        out_specs=[pl.BlockSpec((tm, W), row)] * 6,
        out_shape=[out_shape] * 6,
        scratch_shapes=[pltpu.VMEM((tm, rest_cols), F32),
                        pltpu.VMEM((t + CONF_HALO, W), F32),
                        pltpu.VMEM((7, t + CONF_HALO - 8, W), F32),
                        pltpu.VMEM((t + LRU_HALO, W), F32),
                        pltpu.VMEM((8, W), F32)],
        compiler_params=_params("arbitrary"),
        name="in_mix",
    )(x2, *params)


def _out_proj_kernel(ya_ref, yb_ref, yc_ref, yd_ref, w_ref, x_ref, g_ref, xo_ref, h_ref):
    W = GROUP_WIDTH
    acc = x_ref[...]
    for gi, y_ref in enumerate((ya_ref, yb_ref, yc_ref, yd_ref)):
        acc = acc + jnp.dot(y_ref[...], w_ref[gi * W:(gi + 1) * W, :], preferred_element_type=F32)
    xo_ref[...] = acc
    ms = jnp.mean(acc * acc, axis=-1, keepdims=True)
    h_ref[...] = _pack_bf16_pairs(acc * lax.rsqrt(ms + EPS) * g_ref[...])


def _out_proj(ys, w_bf, x2, g):
    n, d = x2.shape
    h_shape = jax.ShapeDtypeStruct((n, d // 2), jnp.int32)
    W = GROUP_WIDTH
    tm = min(OUT_TM, n)
    row = lambda i: (i, 0)
    const = lambda i: (0, 0)
    ysp = pl.BlockSpec((tm, W), row)
    return pl.pallas_call(
        _out_proj_kernel,
        grid=(n // tm,),
        in_specs=[ysp, ysp, ysp, ysp,
                  pl.BlockSpec(w_bf.shape, const),
                  pl.BlockSpec((tm, d), row),
                  pl.BlockSpec((1, d), const)],
        out_specs=[pl.BlockSpec((tm, d), row), pl.BlockSpec((tm, h_shape.shape[1]), row)],
        out_shape=[jax.ShapeDtypeStruct((n, d), F32), h_shape],
        compiler_params=_params("arbitrary"),
        name="out_proj",
    )(*ys, w_bf, x2, g)


def _out_ffn_kernel(ya_ref, yb_ref, yc_ref, yd_ref, wo_ref, x_ref, g_ref, wg_ref, wu_ref, wd_ref, o_ref,
                    xn_ref, act_ref):
    W = GROUP_WIDTH
    f = wg_ref.shape[1]
    acc = x_ref[...]
    for gi, y_ref in enumerate((ya_ref, yb_ref, yc_ref, yd_ref)):
        acc = acc + jnp.dot(y_ref[...], wo_ref[gi * W:(gi + 1) * W, :], preferred_element_type=F32)
    xn_ref[...] = acc
    ms = jnp.mean(acc * acc, axis=-1, keepdims=True)
    h = (acc * lax.rsqrt(ms + EPS) * g_ref[...]).astype(BF16)
    for c0 in range(0, f, FFN_TF):
        g = jnp.dot(h, wg_ref[:, c0:c0 + FFN_TF], preferred_element_type=F32)
        u = jnp.dot(h, wu_ref[:, c0:c0 + FFN_TF], preferred_element_type=F32)
        act_ref[:, c0:c0 + FFN_TF] = (g * jax.nn.sigmoid(g) * u).astype(BF16)
    o_ref[...] = xn_ref[...] + jnp.dot(act_ref[...], wd_ref[...], preferred_element_type=F32)


def _out_proj_dense_ffn(ys, wo, x2, g, wg, wu, wd):
    n, d = x2.shape
    f = wg.shape[1]
    W = GROUP_WIDTH
    assert f % FFN_TF == 0
    tm = min(FFN_TM, n)
    row = lambda i: (i, 0)
    resident = dict(pipeline_mode=pl.Buffered(1))
    ysp = pl.BlockSpec((tm, W), row)
    return pl.pallas_call(
        _out_ffn_kernel,
        grid=(n // tm,),
        in_specs=[
            ysp, ysp, ysp, ysp,
            pl.BlockSpec(wo.shape, lambda i: (0, 0), **resident),
            pl.BlockSpec((tm, d), row),
            pl.BlockSpec((1, d), lambda i: (0, 0)),
            pl.BlockSpec((d, f), lambda i: (0, 0), **resident),
            pl.BlockSpec((d, f), lambda i: (0, 0), **resident),
            pl.BlockSpec((f, d), lambda i: (0, 0), **resident),
        ],
        out_specs=pl.BlockSpec((tm, d), row),
        out_shape=jax.ShapeDtypeStruct((n, d), F32),
        scratch_shapes=[pltpu.VMEM((tm, d), F32), pltpu.VMEM((tm, f), BF16)],
        compiler_params=_params("arbitrary"),
        name="out_proj_dense_ffn",
    )(*ys, wo, x2, g, wg, wu, wd)


def _router_kernel(x_ref, g_ref, rh_ref, rl_ref, tri_ref, route_ref, idx_ref, cnt_ref, base_ref):
    tm = x_ref.shape[0]

    @pl.when(pl.program_id(0) == 0)
    def _():
        base_ref[...] = jnp.zeros(base_ref.shape, F32)

    xf = x_ref[...]
    ms = jnp.mean(xf * xf, axis=-1, keepdims=True)
    hh, hl = _split_hi_lo(xf * lax.rsqrt(ms + EPS) * g_ref[...])
    rh = rh_ref[...]
    logits = (jnp.dot(hh, rh, preferred_element_type=F32)
              + jnp.dot(hl, rh, preferred_element_type=F32)
              + jnp.dot(hh, rl_ref[...], preferred_element_type=F32))
    lane = lax.broadcasted_iota(jnp.int32, (tm, LANES), 1).astype(F32)
    neg = jnp.float32(-jnp.inf)
    logits = jnp.where(lane < N_EXPERTS, logits, neg)
    m1 = jnp.max(logits, axis=1, keepdims=True)
    i1 = jnp.min(jnp.where(logits == m1, lane, float(LANES)), axis=1, keepdims=True)
    l2 = jnp.where(lane == i1, neg, logits)
    m2 = jnp.max(l2, axis=1, keepdims=True)
    i2 = jnp.min(jnp.where(l2 == m2, lane, float(LANES)), axis=1, keepdims=True)
    e = jnp.exp(m2 - m1)
    g1 = 1.0 / (1.0 + e)
    g2 = e / (1.0 + e)
    oh1 = jnp.where(lane == i1, 1.0, 0.0)
    oh2 = jnp.where(lane == i2, 1.0, 0.0)
    oh = oh1 + oh2
    before = jnp.dot(tri_ref[...], oh.astype(BF16), preferred_element_type=F32) + base_ref[0:1, :]
    r1 = jnp.sum(oh1 * before, axis=1, keepdims=True)
    r2 = jnp.sum(oh2 * before, axis=1, keepdims=True)
    base = base_ref[0:1, :] + jnp.sum(oh, axis=0, keepdims=True)
    base_ref[...] = jnp.broadcast_to(base, base_ref.shape)
    cnt_ref[...] = jnp.broadcast_to(base, cnt_ref.shape)
    out = jnp.where(lane == 0, i1, 0.0)
    out = jnp.where(lane == 1, i2, out)
    out = jnp.where(lane == 2, g1, out)
    out = jnp.where(lane == 3, g2, out)
    out = jnp.where(lane == 4, r1, out)
    out = jnp.where(lane == 5, r2, out)
    route_ref[...] = out
    idx_ref[...] = out[:, :ROUTE_IDX_COLS].astype(jnp.int32)


def _router(x2, g, rh, rl, tri):
    n, d = x2.shape
    tm = min(ROUTE_TM, n)
    return pl.pallas_call(
        _router_kernel,
        grid=(n // tm,),
        in_specs=[
            pl.BlockSpec((tm, d), lambda i: (i, 0)),
            pl.BlockSpec((1, d), lambda i: (0, 0)),
            pl.BlockSpec((d, LANES), lambda i: (0, 0)),
            pl.BlockSpec((d, LANES), lambda i: (0, 0)),
            pl.BlockSpec((tm, tm), lambda i: (0, 0)),
        ],
        out_specs=[pl.BlockSpec((tm, LANES), lambda i: (i, 0)),
                   pl.BlockSpec((tm, ROUTE_IDX_COLS), lambda i: (i, 0)),
                   pl.BlockSpec((8, LANES), lambda i: (0, 0))],
        out_shape=[jax.ShapeDtypeStruct((n, LANES), F32), jax.ShapeDtypeStruct((n, ROUTE_IDX_COLS), jnp.int32),
                   jax.ShapeDtypeStruct((8, LANES), F32)],
        scratch_shapes=[pltpu.VMEM((8, LANES), F32)],
        compiler_params=_params("arbitrary"),
        name="router",
    )(x2, g, rh, rl, tri)


def _sc_window_rows(table):
    row_bytes = table.shape[1] * table.dtype.itemsize
    return min(SC_MAX_INDEX_VECTOR, SC_WINDOW_BYTES // row_bytes)


def _sc_gather_rows(table, idx):
    _, d = table.shape
    b = idx.shape[0]
    win = _sc_window_rows(table)
    sc = plsc.get_sparse_core_info()
    n_workers = sc.num_cores * sc.num_subcores
    per_w = b // n_workers
    n_win = per_w // win
    assert per_w * n_workers == b and n_win * win == per_w and n_win % 2 == 0, (b, n_workers, win)
    mesh = plsc.VectorSubcoreMesh(core_axis_name="c", subcore_axis_name="s")
    dma = pltpu.SemaphoreType.DMA

    @functools.partial(
        pl.kernel, mesh=mesh, out_type=jax.ShapeDtypeStruct((b, d), table.dtype), name="sc_gather_rows",
        scratch_types=[pltpu.VMEM((win,), jnp.int32), pltpu.VMEM((win,), jnp.int32),
                       pltpu.VMEM((win, d), table.dtype), pltpu.VMEM((win, d), table.dtype),
                       dma, dma, dma, dma])
    def gather_kernel(table_hbm, idx_hbm, out_hbm, i0, i1, r0, r1, g0, g1, w0, w1):
        wid = lax.axis_index("s") * sc.num_cores + lax.axis_index("c")
        base = wid * per_w
        idxb, rows, gsem, wsem = (i0, i1), (r0, r1), (g0, g1), (w0, w1)

        def off(c):
            return pl.multiple_of(base + c * win, win)

        def gather(s):
            return pltpu.make_async_copy(table_hbm.at[idxb[s]], rows[s], gsem[s])

        def write(c, s):
            return pltpu.make_async_copy(rows[s], out_hbm.at[pl.ds(off(c), win)], wsem[s])

        pltpu.sync_copy(idx_hbm.at[pl.ds(off(0), win)], idxb[0])
        gather(0).start()

        @pl.loop(0, n_win, step=2)
        def _(c):
            for s in (0, 1):
                cc = c + s

                @pl.when(cc + 1 < n_win)
                def _():
                    @pl.when(cc >= 1)
                    def _():
                        write(cc - 1, 1 - s).wait()

                    pltpu.sync_copy(idx_hbm.at[pl.ds(off(cc + 1), win)], idxb[1 - s])
                    gather(1 - s).start()

                gather(s).wait()
                write(cc, s).start()

        write(n_win - 2, 0).wait()
        write(n_win - 1, 1).wait()

    return gather_kernel(table, idx)


def _sc_dispatch_rows(src, dest1, dest2, n_rows):
    n, d = src.shape
    win = _sc_window_rows(src)
    sc = plsc.get_sparse_core_info()
    n_workers = sc.num_cores * sc.num_subcores
    per_w = n // n_workers
    n_win = per_w // win
    assert per_w * n_workers == n and n_win * win == per_w and n_win % 2 == 0, (n, n_workers, win)
    mesh = plsc.VectorSubcoreMesh(core_axis_name="c", subcore_axis_name="s")
    dma = pltpu.SemaphoreType.DMA
    ivec = pltpu.VMEM((win,), jnp.int32)
    rbuf = pltpu.VMEM((win, d), src.dtype)

    @functools.partial(
        pl.kernel, mesh=mesh, out_type=jax.ShapeDtypeStruct((n_rows, d), src.dtype), name="sc_dispatch_rows",
        scratch_types=[ivec, ivec, ivec, ivec, rbuf, rbuf, dma, dma, dma, dma, dma, dma])
    def dispatch_kernel(src_hbm, d1_hbm, d2_hbm, out_hbm, a0, a1, b0, b1, r0, r1, l0, l1, p0, p1, q0, q1):
        wid = lax.axis_index("s") * sc.num_cores + lax.axis_index("c")
        base = wid * per_w
        ia, ib, rows, lsem, psem, qsem = (a0, a1), (b0, b1), (r0, r1), (l0, l1), (p0, p1), (q0, q1)

        def off(c):
            return pl.multiple_of(base + c * win, win)

        def load(c, s):
            return pltpu.make_async_copy(src_hbm.at[pl.ds(off(c), win)], rows[s], lsem[s])

        def scatters(s):
            return (pltpu.make_async_copy(rows[s], out_hbm.at[ia[s]], psem[s]),
                    pltpu.make_async_copy(rows[s], out_hbm.at[ib[s]], qsem[s]))

        def fetch(c, s):
            pltpu.sync_copy(d1_hbm.at[pl.ds(off(c), win)], ia[s])
            pltpu.sync_copy(d2_hbm.at[pl.ds(off(c), win)], ib[s])
            load(c, s).start()

        fetch(0, 0)

        @pl.loop(0, n_win, step=2)
        def _(c):
            for s in (0, 1):
                cc = c + s

                @pl.when(cc + 1 < n_win)
                def _():
                    @pl.when(cc >= 1)
                    def _():
                        for cp in scatters(1 - s):
                            cp.wait()

                    fetch(cc + 1, 1 - s)

                load(cc, s).wait()
                for cp in scatters(s):
                    cp.start()

        for s in (0, 1):
            for cp in scatters(s):
                cp.wait()

    return dispatch_kernel(src, dest1, dest2)


def _moe_kernel(bexp_ref, nused_ref, nvalid_ref, x_ref, wg_ref, wu_ref, wd_ref, o_ref, xb, acc):
    i = pl.program_id(0)
    j = pl.program_id(1)
    nj = pl.num_programs(1)
    used = i < nused_ref[0]

    @pl.when(used & (j == 0))
    def _():
        x = _unpack_bf16_pairs(x_ref[...])
        row = lax.broadcasted_iota(jnp.int32, x.shape, 0)
        xb[...] = jnp.where(row < nvalid_ref[i], x, 0.0).astype(BF16)
        acc[...] = jnp.zeros(acc.shape, F32)

    @pl.when(used)
    def _():
        x = xb[...]
        g = jnp.dot(x, wg_ref[0].astype(BF16), preferred_element_type=F32)
        u = jnp.dot(x, wu_ref[0].astype(BF16), preferred_element_type=F32)
        act = (g * jax.nn.sigmoid(g) * u).astype(BF16)
        acc[...] += jnp.dot(act, wd_ref[0].astype(BF16), preferred_element_type=F32)

        @pl.when(j == nj - 1)
        def _():
            o_ref[...] = _pack_bf16_pairs(acc[...])

    @pl.when(jnp.logical_not(used) & (j == 0))
    def _():
        o_ref[...] = jnp.zeros(o_ref.shape, o_ref.dtype)


def _moe_experts(block_exp, n_used, n_valid, xs, wg, wu, wd, tm):
    n_rows, dp = xs.shape
    d = 2 * dp
    n_blocks = n_rows // tm
    f = wg.shape[2]
    tf = MOE_TF if f % MOE_TF == 0 else f
    nj = f // tf
    assert nj >= 2

    def live(i, j, be, nu):
        u = i < nu[0]
        return jnp.where(u, i, nu[0] - 1), jnp.where(u, j, nj - 1)

    def w_in_map(i, j, be, nu, nv):
        ii, jj = live(i, j, be, nu)
        return (be[ii], 0, jj)

    def w_down_map(i, j, be, nu, nv):
        ii, jj = live(i, j, be, nu)
        return (be[ii], jj, 0)

    def x_map(i, j, be, nu, nv):
        return (live(i, j, be, nu)[0], 0)

    def out_map(i, j, be, nu, nv):
        return (i, 0)

    grid_spec = pltpu.PrefetchScalarGridSpec(
        num_scalar_prefetch=3,
        grid=(n_blocks, nj),
        in_specs=[
            pl.BlockSpec((tm, dp), x_map),
            pl.BlockSpec((1, d, tf), w_in_map),
            pl.BlockSpec((1, d, tf), w_in_map),
            pl.BlockSpec((1, tf, d), w_down_map),
        ],
        out_specs=pl.BlockSpec((tm, dp), out_map),
        scratch_shapes=[pltpu.VMEM((tm, d), BF16), pltpu.VMEM((tm, d), F32)],
    )
    return pl.pallas_call(
        _moe_kernel,
        grid_spec=grid_spec,
        out_shape=jax.ShapeDtypeStruct((n_rows, dp), jnp.int32),
        compiler_params=_params("arbitrary", "arbitrary"),
        name="moe_experts",
    )(block_exp, n_used, n_valid, xs, wg, wu, wd)


def _combine_kernel(x_ref, route_ref, a_ref, b_ref, o_ref):
    route = route_ref[...]
    ya = _unpack_bf16_pairs(a_ref[...])
    yb = _unpack_bf16_pairs(b_ref[...])
    o_ref[...] = x_ref[...] + route[:, 2:3] * ya + route[:, 3:4] * yb


def _combine(yg, x2, route, part, n_parts):
    n, d = x2.shape
    dp = yg.shape[1]
    tm = min(COMB_TM, n // n_parts)
    nb = n // n_parts // tm
    first = part * nb
    return pl.pallas_call(
        _combine_kernel,
        grid=(nb,),
        in_specs=[pl.BlockSpec((tm, d), lambda i: (i + first, 0)),
                  pl.BlockSpec((tm, LANES), lambda i: (i + first, 0)),
                  pl.BlockSpec((tm, dp), lambda i: (i, 0)),
                  pl.BlockSpec((tm, dp), lambda i: (i + nb, 0))],
        out_specs=pl.BlockSpec((tm, d), lambda i: (i + first, 0)),
        out_shape=jax.ShapeDtypeStruct((n, d), F32),
        input_output_aliases={0: 0},
        compiler_params=_params("arbitrary"),
        name="moe_combine",
    )(x2, route, yg, yg)


def _top2_moe(hn, x2, g2, router, wg, wu, wd):
    n, d = x2.shape
    a = 2 * n
    tm = min(MOE_TM, n)
    r_pad = jnp.zeros((d, LANES), F32).at[:, :N_EXPERTS].set(router)
    rh, rl = _split_hi_lo(r_pad)
    rt = min(ROUTE_TM, n)
    tri = (lax.broadcasted_iota(jnp.int32, (rt, rt), 1) < lax.broadcasted_iota(jnp.int32, (rt, rt), 0)).astype(BF16)
    route, cols, cnt = _router(x2, g2, rh, rl, tri)

    counts = cnt[0, :N_EXPERTS].astype(jnp.int32)
    padded = (counts + tm - 1) // tm * tm
    pad_ends = jnp.cumsum(padded)
    pad_starts = pad_ends - padded
    def lookup(table, idx):
        hit = idx[:, None] == jnp.arange(N_EXPERTS, dtype=jnp.int32)[None, :]
        return jnp.sum(jnp.where(hit, table[None, :], 0), axis=1)

    dest1 = lookup(pad_starts, cols[:, 0]) + cols[:, 4]
    dest2 = lookup(pad_starts, cols[:, 1]) + cols[:, 5]
    n_blocks = a // tm + N_EXPERTS
    n_rows = n_blocks * tm
    block_start = jnp.arange(n_blocks, dtype=jnp.int32) * tm
    block_exp = jnp.minimum(jnp.sum(block_start[:, None] >= pad_ends[None, :], axis=1), N_EXPERTS - 1).astype(jnp.int32)
    n_used = (pad_ends[-1] // tm).astype(jnp.int32).reshape(1)
    n_valid = jnp.clip(lookup(counts, block_exp) - (block_start - lookup(pad_starts, block_exp)), 0, tm).astype(jnp.int32)

    xs = _sc_dispatch_rows(hn, dest1, dest2, n_rows)
    ys = _moe_experts(block_exp, n_used, n_valid, xs, wg, wu, wd, tm)
    n_parts = COMBINE_PARTS if n % (COMBINE_PARTS * COMB_TM) == 0 else 1
    step = n // n_parts
    gathered = [_sc_gather_rows(ys, jnp.concatenate([dest1[p * step:(p + 1) * step], dest2[p * step:(p + 1) * step]]))
                for p in range(n_parts)]
    for p in range(n_parts):
        x2 = _combine(gathered[p], x2, route, p, n_parts)
    return x2


def _block_diag(w):
    h, dh, _ = w.shape
    eye = jnp.eye(h, dtype=w.dtype)
    return jnp.einsum("hij,hg->higj", w, eye).reshape(h * dh, h * dh)


def kernel(x, norm1_g, w_in, q_norm_g, k_norm_g, conf_dw_w, conf_dw_b, conf_ln_g, conf_ln_b, gmlp_ln_g, gmlp_ln_b, gmlp_ws, gmlp_bs, lru_conv_w, lru_conv_b, lru_wa, lru_ba, lru_wx, lru_bx, lru_lambda, group_norm_g, w_out, norm2_g, ffn_w_gate, ffn_w_up, ffn_w_down, moe_router, moe_w_gate, moe_w_up, moe_w_down):
    bsz, s, d = x.shape
    n = bsz * s
    depth = w_in.shape[0]
    W = GROUP_WIDTH
    row = lambda v: v.reshape(1, -1).astype(F32)

    head_id = jnp.arange(W) // HEAD_DIM
    bd = (head_id[:, None] == head_id[None, :]).astype(BF16)
    bd2 = jnp.concatenate([bd, bd], axis=0)
    t_att = min(ATT_T, s)
    uu = (jnp.arange(t_att)[:, None] >= jnp.arange(t_att)[None, :]).astype(BF16)
    tril = jnp.tril(jnp.ones((GMLP_CHUNK, GMLP_CHUNK), dtype=bool))

    x2 = x.reshape(n, d)
    for l in range(depth):
        qg = row(jnp.tile(q_norm_g[l], GROUP_HEADS) * (HEAD_DIM ** -0.5))
        kg = row(jnp.tile(k_norm_g[l], GROUP_HEADS))
        gn4 = group_norm_g[l].reshape(N_GROUPS, W).astype(F32)
        cw = jnp.zeros((CONF_HALO, W), F32).at[:CONF_KERNEL].set(conf_dw_w[l])
        conf = (cw, row(conf_dw_b[l]), row(conf_ln_g[l]), row(conf_ln_b[l]))
        ws = jnp.where(tril, gmlp_ws[l], 0.0).astype(BF16)
        bs_mat = jnp.repeat(gmlp_bs[l].T, HEAD_DIM, axis=1)
        gmlp = (row(gmlp_ln_g[l]), row(gmlp_ln_b[l]), ws, bs_mat)
        lw = jnp.zeros((8, W), F32).at[:LRU_CONV].set(lru_conv_w[l])
        wax = jnp.concatenate([_block_diag(lru_wa[l]), _block_diag(lru_wx[l])], axis=1).astype(BF16)
        lru = (lw, row(lru_conv_b[l]), wax, row(lru_ba[l]), row(lru_bx[l]), row(lru_lambda[l]))
        q, k, v, y_b, y_c, y_d = _in_mix(x2, s, row(norm1_g[l]), w_in[l].astype(BF16), qg, kg, bd2,
                                         conf, gmlp, lru, gn4)
        to3 = lambda t: t.reshape(bsz, s, t.shape[-1])
        zmax = (ATT_ZMAX_SLACK * HEAD_DIM ** 0.5 * jnp.max(jnp.abs(q_norm_g[l] * k_norm_g[l]))).reshape(1)
        y_a = _sb_attention(zmax.astype(F32), to3(q), to3(k), to3(v), uu, gn4[0:1]).reshape(n, W)
        ys = [y_a, y_b, y_c, y_d]
        j = l // 2
        if l % 2 == 0:
            x2 = _out_proj_dense_ffn(ys, w_out[l].astype(BF16), x2, row(norm2_g[l]), ffn_w_gate[j].astype(BF16),
                                     ffn_w_up[j].astype(BF16), ffn_w_down[j].astype(BF16))
        else:
            x2, hn = _out_proj(ys, w_out[l].astype(BF16), x2, row(norm2_g[l]))
            x2 = _top2_moe(hn, x2, row(norm2_g[l]), moe_router[j], moe_w_gate[j], moe_w_up[j], moe_w_down[j])
    return x2.reshape(bsz, s, d)
```

```python
import functools

import jax
import jax.numpy as jnp
from jax import lax
from jax.experimental import pallas as pl
from jax.experimental.pallas import tpu as pltpu
from jax.experimental.pallas import tpu_sc as plsc

F32 = jnp.float32
BF16 = jnp.bfloat16

HEAD_DIM = 64
GROUP_HEADS = 4
GROUP_WIDTH = GROUP_HEADS * HEAD_DIM
N_GROUPS = 4
CONF_KERNEL = 31
GMLP_CHUNK = 128
LRU_CONV = 4
LRU_C = 8.0
N_EXPERTS = 8
EPS = 1e-6

LANES = 128
VMEM_LIMIT = 56 * 1024 * 1024

IN_TM = 1024
MIX_T = 256
ATT_T = 256
ATT_UNDERFLOW = 110.0
ATT_ZMAX_SLACK = 1.05
CONF_R = 64
CONF_HALO = 32
LRU_HALO = 8
FFN_TM = 1024
FFN_TF = 256
ROUTE_TM = 512
ROUTE_IDX_COLS = 8
MOE_TM = 1024
MOE_TF = 512
COMB_TM = 512
COMBINE_PARTS = 2
SC_WINDOW_BYTES = 128 * 1024
SC_MAX_INDEX_VECTOR = 128


def _params(*sem):
    return pltpu.CompilerParams(dimension_semantics=sem, vmem_limit_bytes=VMEM_LIMIT)


def _split_hi_lo(x):
    hi = x.astype(BF16)
    lo = (x - hi.astype(F32)).astype(BF16)
    return hi, lo


SOFTPLUS_CLAMP = 60.0


def _softplus(z):
    return jnp.maximum(jnp.log(1.0 + jnp.exp(jnp.minimum(z, SOFTPLUS_CLAMP))), z)


def _gelu_tanh(x):
    c = 0.7978845608028654
    return 0.5 * x * (1.0 + jnp.tanh(c * (x + 0.044715 * (x * x * x))))


def _pack_bf16_pairs(y):
    c = y.shape[1] // 2
    bits = lax.bitcast_convert_type(y.astype(BF16).astype(F32), jnp.uint32)
    word = (bits[:, :c] >> 16) | bits[:, c:]
    return lax.bitcast_convert_type(word, jnp.int32)


def _unpack_bf16_pairs(w):
    bits = lax.bitcast_convert_type(w, jnp.uint32)
    lo = lax.bitcast_convert_type(bits << 16, F32)
    hi = lax.bitcast_convert_type(bits & jnp.uint32(0xFFFF0000), F32)
    return jnp.concatenate([lo, hi], axis=1)


def _group_rms(y, g):
    ms = jnp.mean(y * y, axis=-1, keepdims=True)
    return y * lax.rsqrt(ms + EPS) * g


def _sb_attn_kernel(zmax_ref, q_ref, k_ref, v_ref, uu_ref, gn_ref, o_ref):
    T = q_ref.shape[1]
    n_pairs = GROUP_WIDTH // LANES
    i = pl.program_id(1)
    lane = lax.broadcasted_iota(jnp.int32, (T, LANES), 1)
    rows = lax.broadcasted_iota(jnp.int32, (2 * T, T), 0)
    cols = lax.broadcasted_iota(jnp.int32, (2 * T, T), 1)
    causal = cols < jnp.where(rows >= T, rows - T, rows)
    uu = uu_ref[...]

    def tiles(r0, carries, mask):
        ps = range(n_pairs)
        cs = [slice(p * LANES, (p + 1) * LANES) for p in ps]
        z = [lax.dot_general(qs[p], k_ref[0, pl.ds(r0, T), cs[p]], (((1,), (1,)), ((), ())),
                             preferred_element_type=F32) for p in ps]
        sp = [_softplus(z[p]) for p in ps]
        if mask is not None:
            sp = [jnp.where(mask, s, 0.0) for s in sp]
        cum = [jnp.dot(sp[p].astype(BF16), uu, preferred_element_type=F32) for p in ps]
        w = [jnp.exp(z[p] - cum[p] - carries[p]) for p in ps]
        if mask is not None:
            w = [jnp.where(mask, x, 0.0) for x in w]
        pv = [jnp.dot(w[p].astype(BF16), v_ref[0, pl.ds(r0, T), cs[p]], preferred_element_type=F32)
              for p in ps]
        return pv, [carries[p] + jnp.sum(sp[p], axis=1, keepdims=True) for p in ps]

    qs = []
    for pair in range(n_pairs):
        q2 = q_ref[0, :, pair * LANES:(pair + 1) * LANES]
        zero = jnp.zeros_like(q2)
        qs.append(jnp.concatenate([jnp.where(lane < HEAD_DIM, q2, zero),
                                   jnp.where(lane >= HEAD_DIM, q2, zero)], axis=0))
    row0 = pl.multiple_of(i * T, T)
    state = tuple(zip(*tiles(row0, [jnp.zeros((2 * T, 1), F32)] * n_pairs, causal)))

    dead_at = zmax_ref[0] + ATT_UNDERFLOW

    def alive(st):
        return functools.reduce(jnp.minimum, [jnp.min(carry) for _, carry in st]) <= dead_at

    def cond(c):
        step, live, _ = c
        return (step < i) & live

    def body(c):
        step, _, st = c
        r0 = pl.multiple_of((i - 1 - step) * T, T)
        pv, carries = tiles(r0, [carry for _, carry in st], None)
        new = tuple((st[p][0] + pv[p], carries[p]) for p in range(n_pairs))
        return step + 1, alive(new), new

    _, _, state = lax.while_loop(cond, body, (jnp.int32(0), alive(state), state))
    y = jnp.concatenate([jnp.where(lane < HEAD_DIM, acc[:T], acc[T:]) for acc, _ in state], axis=1)
    o_ref[0] = _group_rms(y, gn_ref[...]).astype(o_ref.dtype)


def _sb_attention(zmax, q3, k3, v3, uu, gn):
    b, s, w = q3.shape
    t = min(ATT_T, s)
    return pl.pallas_call(
        _sb_attn_kernel,
        grid=(b, s // t),
        in_specs=[
            pl.BlockSpec(memory_space=pltpu.SMEM),
            pl.BlockSpec((1, t, w), lambda bi, i: (bi, i, 0)),
            pl.BlockSpec((1, s, w), lambda bi, i: (bi, 0, 0)),
            pl.BlockSpec((1, s, w), lambda bi, i: (bi, 0, 0)),
            pl.BlockSpec((t, t), lambda bi, i: (0, 0)),
            pl.BlockSpec((1, w), lambda bi, i: (0, 0)),
        ],
        out_specs=pl.BlockSpec((1, t, w), lambda bi, i: (bi, i, 0)),
        out_shape=jax.ShapeDtypeStruct((b, s, w), BF16),
        compiler_params=_params("arbitrary", "arbitrary"),
        name="sb_attn",
    )(zmax, q3, k3, v3, uu, gn)


def _conformer_rows(val, gate, w_ref, b_ref, lng_ref, lnb_ref, gn, o_ref, row0, hs_ref, sh_ref):
    T = val.shape[0]
    H = CONF_HALO
    SUB = 8
    hs_ref[H:H + T, :] = val * jax.nn.sigmoid(gate)
    off = H - (CONF_KERNEL - 1)
    L = sh_ref.shape[1]
    for s in range(1, SUB):
        sh_ref[s - 1] = hs_ref[s:s + L, :]
    R = min(CONF_R, T)
    for r0 in range(0, T, R):
        acc = jnp.broadcast_to(b_ref[...], (R, GROUP_WIDTH))
        for k in range(CONF_KERNEL):
            s = (off + k) % SUB
            a = off + k - s + r0
            tap = hs_ref[a:a + R, :] if s == 0 else sh_ref[s - 1, a:a + R, :]
            acc = acc + w_ref[k:k + 1, :] * tap
        mu = jnp.mean(acc, axis=-1, keepdims=True)
        xc = acc - mu
        var = jnp.mean(xc * xc, axis=-1, keepdims=True)
        y = xc * lax.rsqrt(var + EPS) * lng_ref[...] + lnb_ref[...]
        y = y * jax.nn.sigmoid(y)
        o_ref[row0 + r0:row0 + r0 + R, :] = _group_rms(y, gn).astype(o_ref.dtype)
    hs_ref[0:H, :] = hs_ref[T:T + H, :]


def _gmlp_rows(u_in, v_in, lng_ref, lnb_ref, ws_ref, bs_ref, gn, o_ref, row0):
    C = GMLP_CHUNK
    lane = lax.broadcasted_iota(jnp.int32, (C, GROUP_WIDTH), 1)
    u = _gelu_tanh(u_in)
    v = _gelu_tanh(v_in)
    mu = jnp.mean(v, axis=-1, keepdims=True)
    xc = v - mu
    var = jnp.mean(xc * xc, axis=-1, keepdims=True)
    vb = (xc * lax.rsqrt(var + EPS) * lng_ref[...] + lnb_ref[...]).astype(BF16)
    mixed = bs_ref[...]
    for h in range(GROUP_HEADS):
        m = jnp.dot(ws_ref[h], vb, preferred_element_type=F32)
        in_head = (lane >= h * HEAD_DIM) & (lane < (h + 1) * HEAD_DIM)
        mixed = mixed + jnp.where(in_head, m, 0.0)
    o_ref[row0:row0 + C, :] = _group_rms(u * mixed, gn).astype(o_ref.dtype)


def _lru_rows(x_in, gate, cw_ref, cb_ref, wax_ref, ba_ref, bx_ref, lam_ref, gn, o_ref, row0, xs_ref, hprev_ref):
    T = x_in.shape[0]
    W = GROUP_WIDTH
    H = LRU_HALO
    xs_ref[H:H + T, :] = x_in
    off = H - (LRU_CONV - 1)
    xb = jnp.broadcast_to(cb_ref[...], (T, W))
    for k in range(LRU_CONV):
        xb = xb + cw_ref[k:k + 1, :] * xs_ref[off + k:off + k + T, :]
    xs_ref[0:H, :] = xs_ref[T:T + H, :]

    pre = jnp.dot(xb.astype(BF16), wax_ref[...], preferred_element_type=F32)
    r = jax.nn.sigmoid(pre[:, :W] + ba_ref[...])
    ig = jax.nn.sigmoid(pre[:, W:] + bx_ref[...])
    nlam = -lam_ref[...]
    log_a = (-LRU_C) * r * (jnp.maximum(nlam, 0.0) + jnp.log1p(jnp.exp(-jnp.abs(nlam))))
    a = jnp.exp(log_a)
    th = jnp.tanh(log_a)
    one_minus_a2 = 2.0 * th / (th - 1.0)
    b = jnp.sqrt(one_minus_a2) * (ig * xb)

    SUB = 8
    a = a.reshape(T // SUB, SUB, W)
    b = b.reshape(T // SUB, SUB, W)
    sub = lax.broadcasted_iota(jnp.int32, a.shape, 1)
    d = 1
    while d < SUB:
        keep = sub >= d
        a_sh = jnp.where(keep, pltpu.roll(a, d, 1), 1.0)
        b_sh = jnp.where(keep, pltpu.roll(b, d, 1), 0.0)
        b = a * b_sh + b
        a = a * a_sh
        d *= 2
    carry = hprev_ref[0:1, :]
    tiles = []
    for g in range(T // SUB):
        hg = b[g] + a[g] * carry
        carry = hg[SUB - 1:SUB]
        tiles.append(hg)
    h = jnp.concatenate(tiles, axis=0)
    hprev_ref[...] = jnp.broadcast_to(carry, hprev_ref.shape)
    y = h * _gelu_tanh(gate)
    o_ref[row0:row0 + T, :] = _group_rms(y, gn).astype(o_ref.dtype)


def _in_mix_kernel(blocks_per_seq, x_ref, g_ref, w_ref, qg_ref, kg_ref, bd_ref,
                   cw_ref, cb_ref, clg_ref, clb_ref,
                   glg_ref, glb_ref, ws_ref, bs_ref,
                   lw_ref, lb_ref, wax_ref, ba_ref, bx_ref, lam_ref, gn_ref,
                   q_ref, k_ref, v_ref, yb_ref, yc_ref, yd_ref,
                   rest_ref, hs_ref, sh_ref, xs_ref, hprev_ref):
    tm = x_ref.shape[0]
    W = GROUP_WIDTH

    @pl.when(pl.program_id(0) % blocks_per_seq == 0)
    def _():
        hs_ref[0:CONF_HALO, :] = jnp.zeros((CONF_HALO, W), F32)
        xs_ref[0:LRU_HALO, :] = jnp.zeros((LRU_HALO, W), F32)
        hprev_ref[...] = jnp.zeros(hprev_ref.shape, F32)

    def step(rest_w, rest_r):
        xf = x_ref[...]
        ms = jnp.mean(xf * xf, axis=-1, keepdims=True)
        h = (xf * lax.rsqrt(ms + EPS) * g_ref[...]).astype(BF16)

        def proj(c):
            return jnp.dot(h, w_ref[:, c * W:(c + 1) * W], preferred_element_type=F32)

        def head_norm(p, gain):
            hi, lo = _split_hi_lo(p * p)
            ss = jnp.dot(jnp.concatenate([hi, lo], axis=1), bd_ref[...], preferred_element_type=F32)
            return p * lax.rsqrt(ss * (1.0 / HEAD_DIM) + EPS) * gain

        def project(c):
            if c == 0:
                q_ref[...] = head_norm(proj(0), qg_ref[...]).astype(BF16)
            elif c == 1:
                k_ref[...] = head_norm(proj(1), kg_ref[...]).astype(BF16)
            elif c == 2:
                v_ref[...] = proj(2).astype(BF16)
            else:
                rest_w[:, (c - 3) * W:(c - 2) * W] = proj(c)

        def cols(r0, n, c):
            return rest_r[r0:r0 + n, c * W:(c + 1) * W]

        T = min(MIX_T, tm)

        def conformer(r0):
            _conformer_rows(cols(r0, T, 0), cols(r0, T, 1), cw_ref, cb_ref, clg_ref, clb_ref, gn_ref[1:2, :],
                            yb_ref, r0, hs_ref, sh_ref)

        def lru(r0):
            _lru_rows(cols(r0, T, 4), cols(r0, T, 5), lw_ref, lb_ref, wax_ref, ba_ref, bx_ref, lam_ref,
                      gn_ref[3:4, :], yd_ref, r0, xs_ref, hprev_ref)

        def gmlp(r0):
            _gmlp_rows(cols(r0, GMLP_CHUNK, 2), cols(r0, GMLP_CHUNK, 3), glg_ref, glb_ref, ws_ref, bs_ref,
                       gn_ref[2:3, :], yc_ref, r0)

        mix = []
        for r0 in range(0, tm, T):
            mix.append(functools.partial(conformer, r0))
            mix.append(functools.partial(lru, r0))
            mix.extend(functools.partial(gmlp, c0) for c0 in range(r0, r0 + T, GMLP_CHUNK))
        n_proj = w_ref.shape[1] // W
        for c in range(3, n_proj):
            project(c)
        per = -(-len(mix) // 4)
        for c in range(4):
            if c > 0:
                project(c - 1)
            for task in mix[c * per:(c + 1) * per]:
                task()

    step(rest_ref, rest_ref)


def _in_mix(x2, seq_len, g, w_bf, qg, kg, bd2, conf, gmlp, lru, gn4):
    n, d = x2.shape
    cols = w_bf.shape[1]
    W = GROUP_WIDTH
    rest_cols = cols - 3 * W
    tm = min(IN_TM, seq_len)
    assert seq_len % tm == 0 and tm % min(MIX_T, tm) == 0 and tm % GMLP_CHUNK == 0
    t = min(MIX_T, tm)
    row = lambda i: (i, 0)
    whole = lambda a: pl.BlockSpec(a.shape, lambda i: (0,) * a.ndim)
    params = (g, w_bf, qg, kg, bd2) + tuple(conf) + tuple(gmlp) + tuple(lru) + (gn4,)
    out_shape = jax.ShapeDtypeStruct((n, W), BF16)
    return pl.pallas_call(
        functools.partial(_in_mix_kernel, seq_len // tm),
        grid=(n // tm,),
        in_specs=[pl.BlockSpec((tm, d), row)] + [whole(a) for a in params],
        out_specs=[pl.BlockSpec((tm, W), row)] * 6,
        out_shape=[out_shape] * 6,
        scratch_shapes=[pltpu.VMEM((tm, rest_cols), F32),
                        pltpu.VMEM((t + CONF_HALO, W), F32),
                        pltpu.VMEM((7, t + CONF_HALO - 8, W), F32),
                        pltpu.VMEM((t + LRU_HALO, W), F32),
                        pltpu.VMEM((8, W), F32)],
        compiler_params=_params("arbitrary"),
        name="in_mix",
    )(x2, *params)


def _out_proj_kernel(ya_ref, yb_ref, yc_ref, yd_ref, w_ref, x_ref, g_ref, rh_ref, rl_ref, tri_ref,
                     xo_ref, h_ref, route_ref, idx_ref, cnt_ref, base_ref):
    W = GROUP_WIDTH

    @pl.when(pl.program_id(0) == 0)
    def _():
        base_ref[...] = jnp.zeros(base_ref.shape, F32)

    acc = x_ref[...]
    for gi, y_ref in enumerate((ya_ref, yb_ref, yc_ref, yd_ref)):
        acc = acc + jnp.dot(y_ref[...], w_ref[gi * W:(gi + 1) * W, :], preferred_element_type=F32)
    xo_ref[...] = acc
    ms = jnp.mean(acc * acc, axis=-1, keepdims=True)
    h = acc * lax.rsqrt(ms + EPS) * g_ref[...]
    h_ref[...] = _pack_bf16_pairs(h)
    _route_rows(h, rh_ref, rl_ref, tri_ref, route_ref, idx_ref, cnt_ref, base_ref)


def _out_proj_route(ys, w_bf, x2, g, rh, rl, tri):
    n, d = x2.shape
    W = GROUP_WIDTH
    tm = tri.shape[0]
    row = lambda i: (i, 0)
    const = lambda i: (0, 0)
    ysp = pl.BlockSpec((tm, W), row)
    return pl.pallas_call(
        _out_proj_kernel,
        grid=(n // tm,),
        in_specs=[ysp, ysp, ysp, ysp,
                  pl.BlockSpec(w_bf.shape, const),
                  pl.BlockSpec((tm, d), row),
                  pl.BlockSpec((1, d), const),
                  pl.BlockSpec((d, LANES), const),
                  pl.BlockSpec((d, LANES), const),
                  pl.BlockSpec((tm, tm), const)],
        out_specs=[pl.BlockSpec((tm, d), row),
                   pl.BlockSpec((tm, d // 2), row),
                   pl.BlockSpec((tm, LANES), row),
                   pl.BlockSpec((tm, ROUTE_IDX_COLS), row),
                   pl.BlockSpec((8, LANES), const)],
        out_shape=[jax.ShapeDtypeStruct((n, d), F32),
                   jax.ShapeDtypeStruct((n, d // 2), jnp.int32),
                   jax.ShapeDtypeStruct((n, LANES), F32),
                   jax.ShapeDtypeStruct((n, ROUTE_IDX_COLS), jnp.int32),
                   jax.ShapeDtypeStruct((8, LANES), F32)],
        scratch_shapes=[pltpu.VMEM((8, LANES), F32)],
        compiler_params=_params("arbitrary"),
        name="out_proj_route",
    )(*ys, w_bf, x2, g, rh, rl, tri)


def _out_ffn_kernel(ya_ref, yb_ref, yc_ref, yd_ref, wo_ref, x_ref, g_ref, wg_ref, wu_ref, wd_ref, o_ref,
                    xn_ref, act_ref):
    W = GROUP_WIDTH
    f = wg_ref.shape[1]
    acc = x_ref[...]
    for gi, y_ref in enumerate((ya_ref, yb_ref, yc_ref, yd_ref)):
        acc = acc + jnp.dot(y_ref[...], wo_ref[gi * W:(gi + 1) * W, :], preferred_element_type=F32)
    xn_ref[...] = acc
    ms = jnp.mean(acc * acc, axis=-1, keepdims=True)
    h = (acc * lax.rsqrt(ms + EPS) * g_ref[...]).astype(BF16)
    for c0 in range(0, f, FFN_TF):
        g = jnp.dot(h, wg_ref[:, c0:c0 + FFN_TF], preferred_element_type=F32)
        u = jnp.dot(h, wu_ref[:, c0:c0 + FFN_TF], preferred_element_type=F32)
        act_ref[:, c0:c0 + FFN_TF] = (g * jax.nn.sigmoid(g) * u).astype(BF16)
    o_ref[...] = xn_ref[...] + jnp.dot(act_ref[...], wd_ref[...], preferred_element_type=F32)


def _out_proj_dense_ffn(ys, wo, x2, g, wg, wu, wd):
    n, d = x2.shape
    f = wg.shape[1]
    W = GROUP_WIDTH
    assert f % FFN_TF == 0
    tm = min(FFN_TM, n)
    row = lambda i: (i, 0)
    resident = dict(pipeline_mode=pl.Buffered(1))
    ysp = pl.BlockSpec((tm, W), row)
    return pl.pallas_call(
        _out_ffn_kernel,
        grid=(n // tm,),
        in_specs=[
            ysp, ysp, ysp, ysp,
            pl.BlockSpec(wo.shape, lambda i: (0, 0), **resident),
            pl.BlockSpec((tm, d), row),
            pl.BlockSpec((1, d), lambda i: (0, 0)),
            pl.BlockSpec((d, f), lambda i: (0, 0), **resident),
            pl.BlockSpec((d, f), lambda i: (0, 0), **resident),
            pl.BlockSpec((f, d), lambda i: (0, 0), **resident),
        ],
        out_specs=pl.BlockSpec((tm, d), row),
        out_shape=jax.ShapeDtypeStruct((n, d), F32),
        scratch_shapes=[pltpu.VMEM((tm, d), F32), pltpu.VMEM((tm, f), BF16)],
        compiler_params=_params("arbitrary"),
        name="out_proj_dense_ffn",
    )(*ys, wo, x2, g, wg, wu, wd)


def _route_rows(h, rh_ref, rl_ref, tri_ref, route_ref, idx_ref, cnt_ref, base_ref):
    tm = h.shape[0]
    hh, hl = _split_hi_lo(h)
    rh = rh_ref[...]
    logits = (jnp.dot(hh, rh, preferred_element_type=F32)
              + jnp.dot(hl, rh, preferred_element_type=F32)
              + jnp.dot(hh, rl_ref[...], preferred_element_type=F32))
    lane = lax.broadcasted_iota(jnp.int32, (tm, LANES), 1).astype(F32)
    neg = jnp.float32(-jnp.inf)
    logits = jnp.where(lane < N_EXPERTS, logits, neg)
    m1 = jnp.max(logits, axis=1, keepdims=True)
    i1 = jnp.min(jnp.where(logits == m1, lane, float(LANES)), axis=1, keepdims=True)
    l2 = jnp.where(lane == i1, neg, logits)
    m2 = jnp.max(l2, axis=1, keepdims=True)
    i2 = jnp.min(jnp.where(l2 == m2, lane, float(LANES)), axis=1, keepdims=True)
    e = jnp.exp(m2 - m1)
    g1 = 1.0 / (1.0 + e)
    g2 = e / (1.0 + e)
    oh1 = jnp.where(lane == i1, 1.0, 0.0)
    oh2 = jnp.where(lane == i2, 1.0, 0.0)
    oh = oh1 + oh2
    before = jnp.dot(tri_ref[...], oh.astype(BF16), preferred_element_type=F32) + base_ref[0:1, :]
    r1 = jnp.sum(oh1 * before, axis=1, keepdims=True)
    r2 = jnp.sum(oh2 * before, axis=1, keepdims=True)
    base = base_ref[0:1, :] + jnp.sum(oh, axis=0, keepdims=True)
    base_ref[...] = jnp.broadcast_to(base, base_ref.shape)
    cnt_ref[...] = jnp.broadcast_to(base, cnt_ref.shape)
    out = jnp.where(lane == 0, i1, 0.0)
    out = jnp.where(lane == 1, i2, out)
    out = jnp.where(lane == 2, g1, out)
    out = jnp.where(lane == 3, g2, out)
    out = jnp.where(lane == 4, r1, out)
    out = jnp.where(lane == 5, r2, out)
    route_ref[...] = out
    idx_ref[...] = out[:, :ROUTE_IDX_COLS].astype(jnp.int32)


def _sc_window_rows(table):
    row_bytes = table.shape[1] * table.dtype.itemsize
    return min(SC_MAX_INDEX_VECTOR, SC_WINDOW_BYTES // row_bytes)


def _sc_gather_rows(table, idx):
    _, d = table.shape
    b = idx.shape[0]
    win = _sc_window_rows(table)
    sc = plsc.get_sparse_core_info()
    n_workers = sc.num_cores * sc.num_subcores
    per_w = b // n_workers
    n_win = per_w // win
    assert per_w * n_workers == b and n_win * win == per_w and n_win % 2 == 0, (b, n_workers, win)
    mesh = plsc.VectorSubcoreMesh(core_axis_name="c", subcore_axis_name="s")
    dma = pltpu.SemaphoreType.DMA

    @functools.partial(
        pl.kernel, mesh=mesh, out_type=jax.ShapeDtypeStruct((b, d), table.dtype), name="sc_gather_rows",
        scratch_types=[pltpu.VMEM((win,), jnp.int32), pltpu.VMEM((win,), jnp.int32),
                       pltpu.VMEM((win, d), table.dtype), pltpu.VMEM((win, d), table.dtype),
                       dma, dma, dma, dma])
    def gather_kernel(table_hbm, idx_hbm, out_hbm, i0, i1, r0, r1, g0, g1, w0, w1):
        wid = lax.axis_index("s") * sc.num_cores + lax.axis_index("c")
        base = wid * per_w
        idxb, rows, gsem, wsem = (i0, i1), (r0, r1), (g0, g1), (w0, w1)

        def off(c):
            return pl.multiple_of(base + c * win, win)

        def gather(s):
            return pltpu.make_async_copy(table_hbm.at[idxb[s]], rows[s], gsem[s])

        def write(c, s):
            return pltpu.make_async_copy(rows[s], out_hbm.at[pl.ds(off(c), win)], wsem[s])

        pltpu.sync_copy(idx_hbm.at[pl.ds(off(0), win)], idxb[0])
        gather(0).start()

        @pl.loop(0, n_win, step=2)
        def _(c):
            for s in (0, 1):
                cc = c + s

                @pl.when(cc + 1 < n_win)
                def _():
                    @pl.when(cc >= 1)
                    def _():
                        write(cc - 1, 1 - s).wait()

                    pltpu.sync_copy(idx_hbm.at[pl.ds(off(cc + 1), win)], idxb[1 - s])
                    gather(1 - s).start()

                gather(s).wait()
                write(cc, s).start()

        write(n_win - 2, 0).wait()
        write(n_win - 1, 1).wait()

    return gather_kernel(table, idx)


def _sc_dispatch_rows(src, dest1, dest2, pad_idx, n_rows):
    n, d = src.shape
    win = _sc_window_rows(src)
    sc = plsc.get_sparse_core_info()
    n_workers = sc.num_cores * sc.num_subcores
    per_w = n // n_workers
    n_win = per_w // win
    assert per_w * n_workers == n and n_win * win == per_w and n_win % 2 == 0, (n, n_workers, win)
    n_pad = pad_idx.shape[0]
    assert n_pad + 2 * n == n_rows and n_pad % (n_workers * win) == 0, (n_pad, n, n_rows)
    pad_per_w = n_pad // n_workers
    zero_rows = jnp.zeros((win, d), src.dtype)
    mesh = plsc.VectorSubcoreMesh(core_axis_name="c", subcore_axis_name="s")
    dma = pltpu.SemaphoreType.DMA
    ivec = pltpu.VMEM((win,), jnp.int32)
    rbuf = pltpu.VMEM((win, d), src.dtype)

    @functools.partial(
        pl.kernel, mesh=mesh, out_type=jax.ShapeDtypeStruct((n_rows, d), src.dtype), name="sc_dispatch_rows",
        scratch_types=[ivec, ivec, ivec, ivec, rbuf, rbuf, dma, dma, dma, dma, dma, dma])
    def dispatch_kernel(src_hbm, d1_hbm, d2_hbm, pad_hbm, zero_hbm, out_hbm,
                        a0, a1, b0, b1, r0, r1, l0, l1, p0, p1, q0, q1):
        wid = lax.axis_index("s") * sc.num_cores + lax.axis_index("c")
        base = wid * per_w
        ia, ib, rows, lsem, psem, qsem = (a0, a1), (b0, b1), (r0, r1), (l0, l1), (p0, p1), (q0, q1)

        def off(c):
            return pl.multiple_of(base + c * win, win)

        def load(c, s):
            return pltpu.make_async_copy(src_hbm.at[pl.ds(off(c), win)], rows[s], lsem[s])

        def scatters(s):
            return (pltpu.make_async_copy(rows[s], out_hbm.at[ia[s]], psem[s]),
                    pltpu.make_async_copy(rows[s], out_hbm.at[ib[s]], qsem[s]))

        def fetch(c, s):
            pltpu.sync_copy(d1_hbm.at[pl.ds(off(c), win)], ia[s])
            pltpu.sync_copy(d2_hbm.at[pl.ds(off(c), win)], ib[s])
            load(c, s).start()

        fetch(0, 0)

        @pl.loop(0, n_win, step=2)
        def _(c):
            for s in (0, 1):
                cc = c + s

                @pl.when(cc + 1 < n_win)
                def _():
                    @pl.when(cc >= 1)
                    def _():
                        for cp in scatters(1 - s):
                            cp.wait()

                    fetch(cc + 1, 1 - s)

                load(cc, s).wait()
                for cp in scatters(s):
                    cp.start()

        for s in (0, 1):
            for cp in scatters(s):
                cp.wait()

        pltpu.sync_copy(zero_hbm, rows[0])

        @pl.loop(0, pad_per_w // win)
        def _(c):
            o = pl.multiple_of(wid * pad_per_w + c * win, win)
            pltpu.sync_copy(pad_hbm.at[pl.ds(o, win)], ia[0])
            pltpu.sync_copy(rows[0], out_hbm.at[ia[0]])

    return dispatch_kernel(src, dest1, dest2, pad_idx, zero_rows)


def _moe_kernel(bexp_ref, nused_ref, x_ref, wg_ref, wu_ref, wd_ref, o_ref, xb, acc):
    i = pl.program_id(0)
    j = pl.program_id(1)
    nj = pl.num_programs(1)
    used = i < nused_ref[0]

    @pl.when(used & (j == 0))
    def _():
        xb[...] = _unpack_bf16_pairs(x_ref[...]).astype(BF16)
        acc[...] = jnp.zeros(acc.shape, F32)

    @pl.when(used)
    def _():
        x = xb[...]
        g = jnp.dot(x, wg_ref[0].astype(BF16), preferred_element_type=F32)
        u = jnp.dot(x, wu_ref[0].astype(BF16), preferred_element_type=F32)
        act = (g * jax.nn.sigmoid(g) * u).astype(BF16)
        acc[...] += jnp.dot(act, wd_ref[0].astype(BF16), preferred_element_type=F32)

        @pl.when(j == nj - 1)
        def _():
            o_ref[...] = _pack_bf16_pairs(acc[...])

    @pl.when(jnp.logical_not(used) & (j == 0))
    def _():
        o_ref[...] = jnp.zeros(o_ref.shape, o_ref.dtype)


def _moe_experts(block_exp, n_used, xs, wg, wu, wd, tm):
    n_rows, dp = xs.shape
    d = 2 * dp
    n_blocks = n_rows // tm
    f = wg.shape[2]
    tf = MOE_TF if f % MOE_TF == 0 else f
    nj = f // tf
    assert nj >= 2

    def live(i, j, be, nu):
        u = i < nu[0]
        return jnp.where(u, i, nu[0] - 1), jnp.where(u, j, nj - 1)

    def w_in_map(i, j, be, nu):
        ii, jj = live(i, j, be, nu)
        return (be[ii], 0, jj)

    def w_down_map(i, j, be, nu):
        ii, jj = live(i, j, be, nu)
        return (be[ii], jj, 0)

    def x_map(i, j, be, nu):
        return (live(i, j, be, nu)[0], 0)

    def out_map(i, j, be, nu):
        return (i, 0)

    grid_spec = pltpu.PrefetchScalarGridSpec(
        num_scalar_prefetch=2,
        grid=(n_blocks, nj),
        in_specs=[
            pl.BlockSpec((tm, dp), x_map),
            pl.BlockSpec((1, d, tf), w_in_map),
            pl.BlockSpec((1, d, tf), w_in_map),
            pl.BlockSpec((1, tf, d), w_down_map),
        ],
        out_specs=pl.BlockSpec((tm, dp), out_map),
        scratch_shapes=[pltpu.VMEM((tm, d), BF16), pltpu.VMEM((tm, d), F32)],
    )
    return pl.pallas_call(
        _moe_kernel,
        grid_spec=grid_spec,
        out_shape=jax.ShapeDtypeStruct((n_rows, dp), jnp.int32),
        compiler_params=_params("arbitrary", "arbitrary"),
        name="moe_experts",
    )(block_exp, n_used, xs, wg, wu, wd)


def _combine_kernel(x_ref, route_ref, a_ref, b_ref, o_ref):
    route = route_ref[...]
    ya = _unpack_bf16_pairs(a_ref[...])
    yb = _unpack_bf16_pairs(b_ref[...])
    o_ref[...] = x_ref[...] + route[:, 2:3] * ya + route[:, 3:4] * yb


def _combine(yg, x2, route, part, n_parts):
    n, d = x2.shape
    dp = yg.shape[1]
    tm = min(COMB_TM, n // n_parts)
    nb = n // n_parts // tm
    first = part * nb
    return pl.pallas_call(
        _combine_kernel,
        grid=(nb,),
        in_specs=[pl.BlockSpec((tm, d), lambda i: (i + first, 0)),
                  pl.BlockSpec((tm, LANES), lambda i: (i + first, 0)),
                  pl.BlockSpec((tm, dp), lambda i: (i, 0)),
                  pl.BlockSpec((tm, dp), lambda i: (i + nb, 0))],
        out_specs=pl.BlockSpec((tm, d), lambda i: (i + first, 0)),
        out_shape=jax.ShapeDtypeStruct((n, d), F32),
        input_output_aliases={0: 0},
        compiler_params=_params("arbitrary"),
        name="moe_combine",
    )(x2, route, yg, yg)


def _top2_moe(ys_mix, w_out_bf, x2, g2, router, wg, wu, wd):
    n, d = x2.shape
    a = 2 * n
    tm = min(MOE_TM, n)
    r_pad = jnp.zeros((d, LANES), F32).at[:, :N_EXPERTS].set(router)
    rh, rl = _split_hi_lo(r_pad)
    rt = min(ROUTE_TM, n)
    tri = (lax.broadcasted_iota(jnp.int32, (rt, rt), 1) < lax.broadcasted_iota(jnp.int32, (rt, rt), 0)).astype(BF16)
    x2, hn, route, cols, cnt = _out_proj_route(ys_mix, w_out_bf, x2, g2, rh, rl, tri)

    counts = cnt[0, :N_EXPERTS].astype(jnp.int32)
    padded = (counts + tm - 1) // tm * tm
    pad_ends = jnp.cumsum(padded)
    pad_starts = pad_ends - padded
    def lookup(table, idx):
        hit = idx[:, None] == jnp.arange(N_EXPERTS, dtype=jnp.int32)[None, :]
        return jnp.sum(jnp.where(hit, table[None, :], 0), axis=1)

    dest1 = lookup(pad_starts, cols[:, 0]) + cols[:, 4]
    dest2 = lookup(pad_starts, cols[:, 1]) + cols[:, 5]
    n_blocks = a // tm + N_EXPERTS
    n_rows = n_blocks * tm
    block_start = jnp.arange(n_blocks, dtype=jnp.int32) * tm
    block_exp = jnp.minimum(jnp.sum(block_start[:, None] >= pad_ends[None, :], axis=1), N_EXPERTS - 1).astype(jnp.int32)
    n_used = (pad_ends[-1] // tm).astype(jnp.int32).reshape(1)
    pad_cnt = padded - counts
    pad_cum = jnp.cumsum(pad_cnt)
    k = jnp.arange(n_rows - a, dtype=jnp.int32)
    seg = jnp.sum(k[:, None] >= pad_cum[None, :], axis=1).astype(jnp.int32)
    in_expert = lookup(pad_starts + counts - (pad_cum - pad_cnt), jnp.minimum(seg, N_EXPERTS - 1)) + k
    pad_idx = jnp.where(seg < N_EXPERTS, in_expert, pad_ends[-1] + k - pad_cum[-1])

    xs = _sc_dispatch_rows(hn, dest1, dest2, pad_idx, n_rows)
    ys = _moe_experts(block_exp, n_used, xs, wg, wu, wd, tm)
    n_parts = COMBINE_PARTS if n % (COMBINE_PARTS * COMB_TM) == 0 else 1
    step = n // n_parts
    gathered = [_sc_gather_rows(ys, jnp.concatenate([dest1[p * step:(p + 1) * step], dest2[p * step:(p + 1) * step]]))
                for p in range(n_parts)]
    for p in range(n_parts):
        x2 = _combine(gathered[p], x2, route, p, n_parts)
    return x2


def _block_diag(w):
    h, dh, _ = w.shape
    eye = jnp.eye(h, dtype=w.dtype)
    return jnp.einsum("hij,hg->higj", w, eye).reshape(h * dh, h * dh)


def kernel(x, norm1_g, w_in, q_norm_g, k_norm_g, conf_dw_w, conf_dw_b, conf_ln_g, conf_ln_b, gmlp_ln_g, gmlp_ln_b, gmlp_ws, gmlp_bs, lru_conv_w, lru_conv_b, lru_wa, lru_ba, lru_wx, lru_bx, lru_lambda, group_norm_g, w_out, norm2_g, ffn_w_gate, ffn_w_up, ffn_w_down, moe_router, moe_w_gate, moe_w_up, moe_w_down):
    bsz, s, d = x.shape
    n = bsz * s
    depth = w_in.shape[0]
    W = GROUP_WIDTH
    row = lambda v: v.reshape(1, -1).astype(F32)

    head_id = jnp.arange(W) // HEAD_DIM
    bd = (head_id[:, None] == head_id[None, :]).astype(BF16)
    bd2 = jnp.concatenate([bd, bd], axis=0)
    t_att = min(ATT_T, s)
    uu = (jnp.arange(t_att)[:, None] >= jnp.arange(t_att)[None, :]).astype(BF16)
    tril = jnp.tril(jnp.ones((GMLP_CHUNK, GMLP_CHUNK), dtype=bool))

    x2 = x.reshape(n, d)
    for l in range(depth):
        qg = row(jnp.tile(q_norm_g[l], GROUP_HEADS) * (HEAD_DIM ** -0.5))
        kg = row(jnp.tile(k_norm_g[l], GROUP_HEADS))
        gn4 = group_norm_g[l].reshape(N_GROUPS, W).astype(F32)
        cw = jnp.zeros((CONF_HALO, W), F32).at[:CONF_KERNEL].set(conf_dw_w[l])
        conf = (cw, row(conf_dw_b[l]), row(conf_ln_g[l]), row(conf_ln_b[l]))
        ws = jnp.where(tril, gmlp_ws[l], 0.0).astype(BF16)
        bs_mat = jnp.repeat(gmlp_bs[l].T, HEAD_DIM, axis=1)
        gmlp = (row(gmlp_ln_g[l]), row(gmlp_ln_b[l]), ws, bs_mat)
        lw = jnp.zeros((8, W), F32).at[:LRU_CONV].set(lru_conv_w[l])
        wax = jnp.concatenate([_block_diag(lru_wa[l]), _block_diag(lru_wx[l])], axis=1).astype(BF16)
        lru = (lw, row(lru_conv_b[l]), wax, row(lru_ba[l]), row(lru_bx[l]), row(lru_lambda[l]))
        q, k, v, y_b, y_c, y_d = _in_mix(x2, s, row(norm1_g[l]), w_in[l].astype(BF16), qg, kg, bd2,
                                         conf, gmlp, lru, gn4)
        to3 = lambda t: t.reshape(bsz, s, t.shape[-1])
        zmax = (ATT_ZMAX_SLACK * HEAD_DIM ** 0.5 * jnp.max(jnp.abs(q_norm_g[l] * k_norm_g[l]))).reshape(1)
        y_a = _sb_attention(zmax.astype(F32), to3(q), to3(k), to3(v), uu, gn4[0:1]).reshape(n, W)
        ys = [y_a, y_b, y_c, y_d]
        j = l // 2
        if l % 2 == 0:
            x2 = _out_proj_dense_ffn(ys, w_out[l].astype(BF16), x2, row(norm2_g[l]), ffn_w_gate[j].astype(BF16),
                                     ffn_w_up[j].astype(BF16), ffn_w_down[j].astype(BF16))
        else:
            x2 = _top2_moe(ys, w_out[l].astype(BF16), x2, row(norm2_g[l]), moe_router[j], moe_w_gate[j],
                           moe_w_up[j], moe_w_down[j])
    return x2.reshape(bsz, s, d)
```

```python
import functools

import jax
import jax.numpy as jnp
from jax import lax
from jax.experimental import pallas as pl
from jax.experimental.pallas import tpu as pltpu
from jax.experimental.pallas import tpu_sc as plsc

F32 = jnp.float32
BF16 = jnp.bfloat16

HEAD_DIM = 64
GROUP_HEADS = 4
GROUP_WIDTH = GROUP_HEADS * HEAD_DIM
N_GROUPS = 4
CONF_KERNEL = 31
GMLP_CHUNK = 128
LRU_CONV = 4
LRU_C = 8.0
N_EXPERTS = 8
EPS = 1e-6

LANES = 128
VMEM_LIMIT = 56 * 1024 * 1024

IN_TM = 1024
MIX_T = 256
ATT_T = 256
ATT_UNDERFLOW = 110.0
ATT_ZMAX_SLACK = 1.05
CONF_R = 64
CONF_HALO = 32
LRU_HALO = 8
FFN_TM = 1024
FFN_TF = 256
ROUTE_TM = 512
ROUTE_IDX_COLS = 8
MOE_TM = 1024
MOE_TF = 512
COMB_TM = 512
COMBINE_PARTS = 2
SC_WINDOW_BYTES = 128 * 1024
SC_MAX_INDEX_VECTOR = 128


def _params(*sem):
    return pltpu.CompilerParams(dimension_semantics=sem, vmem_limit_bytes=VMEM_LIMIT)


def _split_hi_lo(x):
    hi = x.astype(BF16)
    lo = (x - hi.astype(F32)).astype(BF16)
    return hi, lo


SOFTPLUS_CLAMP = 60.0


def _softplus(z):
    return jnp.maximum(jnp.log(1.0 + jnp.exp(jnp.minimum(z, SOFTPLUS_CLAMP))), z)


def _gelu_tanh(x):
    c = 0.7978845608028654
    return 0.5 * x * (1.0 + jnp.tanh(c * (x + 0.044715 * (x * x * x))))


def _pack_bf16_pairs(y):
    c = y.shape[1] // 2
    bits = lax.bitcast_convert_type(y.astype(BF16).astype(F32), jnp.uint32)
    word = (bits[:, :c] >> 16) | bits[:, c:]
    return lax.bitcast_convert_type(word, jnp.int32)


def _unpack_bf16_pairs(w):
    bits = lax.bitcast_convert_type(w, jnp.uint32)
    lo = lax.bitcast_convert_type(bits << 16, F32)
    hi = lax.bitcast_convert_type(bits & jnp.uint32(0xFFFF0000), F32)
    return jnp.concatenate([lo, hi], axis=1)


def _group_rms(y, g):
    ms = jnp.mean(y * y, axis=-1, keepdims=True)
    return y * lax.rsqrt(ms + EPS) * g


def _sb_attn_kernel(zmax_ref, q_ref, k_ref, v_ref, uu_ref, gn_ref, o_ref):
    T = q_ref.shape[1]
    n_pairs = GROUP_WIDTH // LANES
    i = pl.program_id(1)
    lane = lax.broadcasted_iota(jnp.int32, (T, LANES), 1)
    rows = lax.broadcasted_iota(jnp.int32, (2 * T, T), 0)
    cols = lax.broadcasted_iota(jnp.int32, (2 * T, T), 1)
    causal = cols < jnp.where(rows >= T, rows - T, rows)
    uu = uu_ref[...]

    def tiles(r0, carries, mask):
        ps = range(n_pairs)
        cs = [slice(p * LANES, (p + 1) * LANES) for p in ps]
        z = [lax.dot_general(qs[p], k_ref[0, pl.ds(r0, T), cs[p]], (((1,), (1,)), ((), ())),
                             preferred_element_type=F32) for p in ps]
        sp = [_softplus(z[p]) for p in ps]
        if mask is not None:
            sp = [jnp.where(mask, s, 0.0) for s in sp]
        cum = [jnp.dot(sp[p].astype(BF16), uu, preferred_element_type=F32) for p in ps]
        w = [jnp.exp(z[p] - cum[p] - carries[p]) for p in ps]
        if mask is not None:
            w = [jnp.where(mask, x, 0.0) for x in w]
        pv = [jnp.dot(w[p].astype(BF16), v_ref[0, pl.ds(r0, T), cs[p]], preferred_element_type=F32)
              for p in ps]
        return pv, [carries[p] + jnp.sum(sp[p], axis=1, keepdims=True) for p in ps]

    qs = []
    for pair in range(n_pairs):
        q2 = q_ref[0, :, pair * LANES:(pair + 1) * LANES]
        zero = jnp.zeros_like(q2)
        qs.append(jnp.concatenate([jnp.where(lane < HEAD_DIM, q2, zero),
                                   jnp.where(lane >= HEAD_DIM, q2, zero)], axis=0))
    row0 = pl.multiple_of(i * T, T)
    state = tuple(zip(*tiles(row0, [jnp.zeros((2 * T, 1), F32)] * n_pairs, causal)))

    dead_at = zmax_ref[0] + ATT_UNDERFLOW

    def alive(st):
        return functools.reduce(jnp.minimum, [jnp.min(carry) for _, carry in st]) <= dead_at

    def cond(c):
        step, live, _ = c
        return (step < i) & live

    def body(c):
        step, _, st = c
        r0 = pl.multiple_of((i - 1 - step) * T, T)
        pv, carries = tiles(r0, [carry for _, carry in st], None)
        new = tuple((st[p][0] + pv[p], carries[p]) for p in range(n_pairs))
        return step + 1, alive(new), new

    _, _, state = lax.while_loop(cond, body, (jnp.int32(0), alive(state), state))
    y = jnp.concatenate([jnp.where(lane < HEAD_DIM, acc[:T], acc[T:]) for acc, _ in state], axis=1)
    o_ref[0] = _group_rms(y, gn_ref[...]).astype(o_ref.dtype)


def _sb_attention(zmax, q3, k3, v3, uu, gn):
    b, s, w = q3.shape
    t = min(ATT_T, s)
    return pl.pallas_call(
        _sb_attn_kernel,
        grid=(b, s // t),
        in_specs=[
            pl.BlockSpec(memory_space=pltpu.SMEM),
            pl.BlockSpec((1, t, w), lambda bi, i: (bi, i, 0)),
            pl.BlockSpec((1, s, w), lambda bi, i: (bi, 0, 0)),
            pl.BlockSpec((1, s, w), lambda bi, i: (bi, 0, 0)),
            pl.BlockSpec((t, t), lambda bi, i: (0, 0)),
            pl.BlockSpec((1, w), lambda bi, i: (0, 0)),
        ],
        out_specs=pl.BlockSpec((1, t, w), lambda bi, i: (bi, i, 0)),
        out_shape=jax.ShapeDtypeStruct((b, s, w), BF16),
        compiler_params=_params("arbitrary", "arbitrary"),
        name="sb_attn",
    )(zmax, q3, k3, v3, uu, gn)


def _conformer_rows(val, gate, w_ref, b_ref, lng_ref, lnb_ref, gn, o_ref, row0, hs_ref, sh_ref):
    T = val.shape[0]
    H = CONF_HALO
    SUB = 8
    hs_ref[H:H + T, :] = val * jax.nn.sigmoid(gate)
    off = H - (CONF_KERNEL - 1)
    L = sh_ref.shape[1]
    for s in range(1, SUB):
        sh_ref[s - 1] = hs_ref[s:s + L, :]
    R = min(CONF_R, T)
    taps_w = [jnp.broadcast_to(w_ref[k:k + 1, :], (R, GROUP_WIDTH)) for k in range(CONF_KERNEL)]
    for r0 in range(0, T, R):
        acc = jnp.broadcast_to(b_ref[...], (R, GROUP_WIDTH))
        for k in range(CONF_KERNEL):
            s = (off + k) % SUB
            a = off + k - s + r0
            tap = hs_ref[a:a + R, :] if s == 0 else sh_ref[s - 1, a:a + R, :]
            acc = acc + taps_w[k] * tap
        mu = jnp.mean(acc, axis=-1, keepdims=True)
        xc = acc - mu
        var = jnp.mean(xc * xc, axis=-1, keepdims=True)
        y = xc * lax.rsqrt(var + EPS) * lng_ref[...] + lnb_ref[...]
        y = y * jax.nn.sigmoid(y)
        o_ref[row0 + r0:row0 + r0 + R, :] = _group_rms(y, gn).astype(o_ref.dtype)
    hs_ref[0:H, :] = hs_ref[T:T + H, :]


def _gmlp_rows(u_in, v_in, lng_ref, lnb_ref, ws_ref, bs_ref, gn, o_ref, row0):
    C = GMLP_CHUNK
    lane = lax.broadcasted_iota(jnp.int32, (C, GROUP_WIDTH), 1)
    u = _gelu_tanh(u_in)
    v = _gelu_tanh(v_in)
    mu = jnp.mean(v, axis=-1, keepdims=True)
    xc = v - mu
    var = jnp.mean(xc * xc, axis=-1, keepdims=True)
    vb = (xc * lax.rsqrt(var + EPS) * lng_ref[...] + lnb_ref[...]).astype(BF16)
    mixed = bs_ref[...]
    for h in range(GROUP_HEADS):
        m = jnp.dot(ws_ref[h], vb, preferred_element_type=F32)
        in_head = (lane >= h * HEAD_DIM) & (lane < (h + 1) * HEAD_DIM)
        mixed = mixed + jnp.where(in_head, m, 0.0)
    o_ref[row0:row0 + C, :] = _group_rms(u * mixed, gn).astype(o_ref.dtype)


def _lru_rows(x_in, gate, cw_ref, cb_ref, wax_ref, ba_ref, bx_ref, lam_ref, gn, o_ref, row0, xs_ref, hprev_ref):
    T = x_in.shape[0]
    W = GROUP_WIDTH
    H = LRU_HALO
    xs_ref[H:H + T, :] = x_in
    off = H - (LRU_CONV - 1)
    xb = jnp.broadcast_to(cb_ref[...], (T, W))
    for k in range(LRU_CONV):
        xb = xb + cw_ref[k:k + 1, :] * xs_ref[off + k:off + k + T, :]
    xs_ref[0:H, :] = xs_ref[T:T + H, :]

    pre = jnp.dot(xb.astype(BF16), wax_ref[...], preferred_element_type=F32)
    r = jax.nn.sigmoid(pre[:, :W] + ba_ref[...])
    ig = jax.nn.sigmoid(pre[:, W:] + bx_ref[...])
    nlam = -lam_ref[...]
    log_a = (-LRU_C) * r * (jnp.maximum(nlam, 0.0) + jnp.log1p(jnp.exp(-jnp.abs(nlam))))
    a = jnp.exp(log_a)
    th = jnp.tanh(log_a)
    one_minus_a2 = 2.0 * th / (th - 1.0)
    b = jnp.sqrt(one_minus_a2) * (ig * xb)

    SUB = 8
    a = a.reshape(T // SUB, SUB, W)
    b = b.reshape(T // SUB, SUB, W)
    sub = lax.broadcasted_iota(jnp.int32, a.shape, 1)
    d = 1
    while d < SUB:
        keep = sub >= d
        a_sh = jnp.where(keep, pltpu.roll(a, d, 1), 1.0)
        b_sh = jnp.where(keep, pltpu.roll(b, d, 1), 0.0)
        b = a * b_sh + b
        a = a * a_sh
        d *= 2
    carry = hprev_ref[0:1, :]
    tiles = []
    for g in range(T // SUB):
        hg = b[g] + a[g] * carry
        carry = hg[SUB - 1:SUB]
        tiles.append(hg)
    h = jnp.concatenate(tiles, axis=0)
    hprev_ref[...] = jnp.broadcast_to(carry, hprev_ref.shape)
    y = h * _gelu_tanh(gate)
    o_ref[row0:row0 + T, :] = _group_rms(y, gn).astype(o_ref.dtype)


def _in_mix_kernel(blocks_per_seq, x_ref, g_ref, w_ref, qg_ref, kg_ref, bd_ref,
                   cw_ref, cb_ref, clg_ref, clb_ref,
                   glg_ref, glb_ref, ws_ref, bs_ref,
                   lw_ref, lb_ref, wax_ref, ba_ref, bx_ref, lam_ref, gn_ref,
                   q_ref, k_ref, v_ref, yb_ref, yc_ref, yd_ref,
                   rest_ref, hs_ref, sh_ref, xs_ref, hprev_ref):
    tm = x_ref.shape[0]
    W = GROUP_WIDTH

    @pl.when(pl.program_id(0) % blocks_per_seq == 0)
    def _():
        hs_ref[0:CONF_HALO, :] = jnp.zeros((CONF_HALO, W), F32)
        xs_ref[0:LRU_HALO, :] = jnp.zeros((LRU_HALO, W), F32)
        hprev_ref[...] = jnp.zeros(hprev_ref.shape, F32)

    def step(rest_w, rest_r):
        xf = x_ref[...]
        ms = jnp.mean(xf * xf, axis=-1, keepdims=True)
        h = (xf * lax.rsqrt(ms + EPS) * g_ref[...]).astype(BF16)

        def proj(c):
            return jnp.dot(h, w_ref[:, c * W:(c + 1) * W], preferred_element_type=F32)

        def head_norm(p, gain):
            hi, lo = _split_hi_lo(p * p)
            ss = jnp.dot(jnp.concatenate([hi, lo], axis=1), bd_ref[...], preferred_element_type=F32)
            return p * lax.rsqrt(ss * (1.0 / HEAD_DIM) + EPS) * gain

        def project(c):
            if c == 0:
                q_ref[...] = head_norm(proj(0), qg_ref[...]).astype(BF16)
            elif c == 1:
                k_ref[...] = head_norm(proj(1), kg_ref[...]).astype(BF16)
            elif c == 2:
                v_ref[...] = proj(2).astype(BF16)
            else:
                rest_w[:, (c - 3) * W:(c - 2) * W] = proj(c)

        def cols(r0, n, c):
            return rest_r[r0:r0 + n, c * W:(c + 1) * W]

        T = min(MIX_T, tm)

        def conformer(r0):
            _conformer_rows(cols(r0, T, 0), cols(r0, T, 1), cw_ref, cb_ref, clg_ref, clb_ref, gn_ref[1:2, :],
                            yb_ref, r0, hs_ref, sh_ref)

        def lru(r0):
            _lru_rows(cols(r0, T, 4), cols(r0, T, 5), lw_ref, lb_ref, wax_ref, ba_ref, bx_ref, lam_ref,
                      gn_ref[3:4, :], yd_ref, r0, xs_ref, hprev_ref)

        def gmlp(r0):
            _gmlp_rows(cols(r0, GMLP_CHUNK, 2), cols(r0, GMLP_CHUNK, 3), glg_ref, glb_ref, ws_ref, bs_ref,
                       gn_ref[2:3, :], yc_ref, r0)

        mix = []
        for r0 in range(0, tm, T):
            mix.append(functools.partial(conformer, r0))
            mix.append(functools.partial(lru, r0))
            mix.extend(functools.partial(gmlp, c0) for c0 in range(r0, r0 + T, GMLP_CHUNK))
        n_proj = w_ref.shape[1] // W
        for c in range(3, n_proj):
            project(c)
        per = -(-len(mix) // 4)
        for c in range(4):
            if c > 0:
                project(c - 1)
            for task in mix[c * per:(c + 1) * per]:
                task()

    step(rest_ref, rest_ref)


def _in_mix(x2, seq_len, g, w_bf, qg, kg, bd2, conf, gmlp, lru, gn4):
    n, d = x2.shape
    cols = w_bf.shape[1]
    W = GROUP_WIDTH
    rest_cols = cols - 3 * W
    tm = min(IN_TM, seq_len)
    assert seq_len % tm == 0 and tm % min(MIX_T, tm) == 0 and tm % GMLP_CHUNK == 0
    t = min(MIX_T, tm)
    row = lambda i: (i, 0)
    whole = lambda a: pl.BlockSpec(a.shape, lambda i: (0,) * a.ndim)
    params = (g, w_bf, qg, kg, bd2) + tuple(conf) + tuple(gmlp) + tuple(lru) + (gn4,)
    out_shape = jax.ShapeDtypeStruct((n, W), BF16)
    return pl.pallas_call(
        functools.partial(_in_mix_kernel, seq_len // tm),
        grid=(n // tm,),
        in_specs=[pl.BlockSpec((tm, d), row)] + [whole(a) for a in params],
        out_specs=[pl.BlockSpec((tm, W), row)] * 6,
        out_shape=[out_shape] * 6,
        scratch_shapes=[pltpu.VMEM((tm, rest_cols), F32),
                        pltpu.VMEM((t + CONF_HALO, W), F32),
                        pltpu.VMEM((7, t + CONF_HALO - 8, W), F32),
                        pltpu.VMEM((t + LRU_HALO, W), F32),
                        pltpu.VMEM((8, W), F32)],
        compiler_params=_params("arbitrary"),
        name="in_mix",
    )(x2, *params)


def _out_proj_kernel(ya_ref, yb_ref, yc_ref, yd_ref, w_ref, x_ref, g_ref, rh_ref, rl_ref, tri_ref,
                     xo_ref, h_ref, route_ref, idx_ref, cnt_ref, base_ref):
    W = GROUP_WIDTH

    @pl.when(pl.program_id(0) == 0)
    def _():
        base_ref[...] = jnp.zeros(base_ref.shape, F32)

    acc = x_ref[...]
    for gi, y_ref in enumerate((ya_ref, yb_ref, yc_ref, yd_ref)):
        acc = acc + jnp.dot(y_ref[...], w_ref[gi * W:(gi + 1) * W, :], preferred_element_type=F32)
    xo_ref[...] = acc
    ms = jnp.mean(acc * acc, axis=-1, keepdims=True)
    h = acc * lax.rsqrt(ms + EPS) * g_ref[...]
    h_ref[...] = _pack_bf16_pairs(h)
    _route_rows(h, rh_ref, rl_ref, tri_ref, route_ref, idx_ref, cnt_ref, base_ref)


def _out_proj_route(ys, w_bf, x2, g, rh, rl, tri):
    n, d = x2.shape
    W = GROUP_WIDTH
    tm = tri.shape[0]
    row = lambda i: (i, 0)
    const = lambda i: (0, 0)
    ysp = pl.BlockSpec((tm, W), row)
    return pl.pallas_call(
        _out_proj_kernel,
        grid=(n // tm,),
        in_specs=[ysp, ysp, ysp, ysp,
                  pl.BlockSpec(w_bf.shape, const),
                  pl.BlockSpec((tm, d), row),
                  pl.BlockSpec((1, d), const),
                  pl.BlockSpec((d, LANES), const),
                  pl.BlockSpec((d, LANES), const),
                  pl.BlockSpec((tm, tm), const)],
        out_specs=[pl.BlockSpec((tm, d), row),
                   pl.BlockSpec((tm, d // 2), row),
                   pl.BlockSpec((tm, LANES), row),
                   pl.BlockSpec((tm, ROUTE_IDX_COLS), row),
                   pl.BlockSpec((8, LANES), const)],
        out_shape=[jax.ShapeDtypeStruct((n, d), F32),
                   jax.ShapeDtypeStruct((n, d // 2), jnp.int32),
                   jax.ShapeDtypeStruct((n, LANES), F32),
                   jax.ShapeDtypeStruct((n, ROUTE_IDX_COLS), jnp.int32),
                   jax.ShapeDtypeStruct((8, LANES), F32)],
        scratch_shapes=[pltpu.VMEM((8, LANES), F32)],
        compiler_params=_params("arbitrary"),
        name="out_proj_route",
    )(*ys, w_bf, x2, g, rh, rl, tri)


def _out_ffn_kernel(ya_ref, yb_ref, yc_ref, yd_ref, wo_ref, x_ref, g_ref, wg_ref, wu_ref, wd_ref, o_ref,
                    xn_ref, act_ref):
    W = GROUP_WIDTH
    f = wg_ref.shape[1]
    acc = x_ref[...]
    for gi, y_ref in enumerate((ya_ref, yb_ref, yc_ref, yd_ref)):
        acc = acc + jnp.dot(y_ref[...], wo_ref[gi * W:(gi + 1) * W, :], preferred_element_type=F32)
    xn_ref[...] = acc
    ms = jnp.mean(acc * acc, axis=-1, keepdims=True)
    h = (acc * lax.rsqrt(ms + EPS) * g_ref[...]).astype(BF16)
    for c0 in range(0, f, FFN_TF):
        g = jnp.dot(h, wg_ref[:, c0:c0 + FFN_TF], preferred_element_type=F32)
        u = jnp.dot(h, wu_ref[:, c0:c0 + FFN_TF], preferred_element_type=F32)
        act_ref[:, c0:c0 + FFN_TF] = (g * jax.nn.sigmoid(g) * u).astype(BF16)
    o_ref[...] = xn_ref[...] + jnp.dot(act_ref[...], wd_ref[...], preferred_element_type=F32)


def _out_proj_dense_ffn(ys, wo, x2, g, wg, wu, wd):
    n, d = x2.shape
    f = wg.shape[1]
    W = GROUP_WIDTH
    assert f % FFN_TF == 0
    tm = min(FFN_TM, n)
    row = lambda i: (i, 0)
    resident = dict(pipeline_mode=pl.Buffered(1))
    ysp = pl.BlockSpec((tm, W), row)
    return pl.pallas_call(
        _out_ffn_kernel,
        grid=(n // tm,),
        in_specs=[
            ysp, ysp, ysp, ysp,
            pl.BlockSpec(wo.shape, lambda i: (0, 0), **resident),
            pl.BlockSpec((tm, d), row),
            pl.BlockSpec((1, d), lambda i: (0, 0)),
            pl.BlockSpec((d, f), lambda i: (0, 0), **resident),
            pl.BlockSpec((d, f), lambda i: (0, 0), **resident),
            pl.BlockSpec((f, d), lambda i: (0, 0), **resident),
        ],
        out_specs=pl.BlockSpec((tm, d), row),
        out_shape=jax.ShapeDtypeStruct((n, d), F32),
        scratch_shapes=[pltpu.VMEM((tm, d), F32), pltpu.VMEM((tm, f), BF16)],
        compiler_params=_params("arbitrary"),
        name="out_proj_dense_ffn",
    )(*ys, wo, x2, g, wg, wu, wd)


def _route_rows(h, rh_ref, rl_ref, tri_ref, route_ref, idx_ref, cnt_ref, base_ref):
    tm = h.shape[0]
    hh, hl = _split_hi_lo(h)
    rh = rh_ref[...]
    logits = (jnp.dot(hh, rh, preferred_element_type=F32)
              + jnp.dot(hl, rh, preferred_element_type=F32)
              + jnp.dot(hh, rl_ref[...], preferred_element_type=F32))
    lane = lax.broadcasted_iota(jnp.int32, (tm, LANES), 1).astype(F32)
    neg = jnp.float32(-jnp.inf)
    logits = jnp.where(lane < N_EXPERTS, logits, neg)
    m1 = jnp.max(logits, axis=1, keepdims=True)
    i1 = jnp.min(jnp.where(logits == m1, lane, float(LANES)), axis=1, keepdims=True)
    l2 = jnp.where(lane == i1, neg, logits)
    m2 = jnp.max(l2, axis=1, keepdims=True)
    i2 = jnp.min(jnp.where(l2 == m2, lane, float(LANES)), axis=1, keepdims=True)
    e = jnp.exp(m2 - m1)
    g1 = 1.0 / (1.0 + e)
    g2 = e / (1.0 + e)
    oh1 = jnp.where(lane == i1, 1.0, 0.0)
    oh2 = jnp.where(lane == i2, 1.0, 0.0)
    oh = oh1 + oh2
    before = jnp.dot(tri_ref[...], oh.astype(BF16), preferred_element_type=F32) + base_ref[0:1, :]
    r1 = jnp.sum(oh1 * before, axis=1, keepdims=True)
    r2 = jnp.sum(oh2 * before, axis=1, keepdims=True)
    base = base_ref[0:1, :] + jnp.sum(oh, axis=0, keepdims=True)
    base_ref[...] = jnp.broadcast_to(base, base_ref.shape)
    cnt_ref[...] = jnp.broadcast_to(base, cnt_ref.shape)
    out = jnp.where(lane == 0, i1, 0.0)
    out = jnp.where(lane == 1, i2, out)
    out = jnp.where(lane == 2, g1, out)
    out = jnp.where(lane == 3, g2, out)
    out = jnp.where(lane == 4, r1, out)
    out = jnp.where(lane == 5, r2, out)
    route_ref[...] = out
    idx_ref[...] = out[:, :ROUTE_IDX_COLS].astype(jnp.int32)


def _sc_window_rows(table):
    row_bytes = table.shape[1] * table.dtype.itemsize
    return min(SC_MAX_INDEX_VECTOR, SC_WINDOW_BYTES // row_bytes)


def _sc_gather_rows(table, idx):
    _, d = table.shape
    b = idx.shape[0]
    win = _sc_window_rows(table)
    sc = plsc.get_sparse_core_info()
    n_workers = sc.num_cores * sc.num_subcores
    per_w = b // n_workers
    n_win = per_w // win
    assert per_w * n_workers == b and n_win * win == per_w and n_win % 2 == 0, (b, n_workers, win)
    mesh = plsc.VectorSubcoreMesh(core_axis_name="c", subcore_axis_name="s")
    dma = pltpu.SemaphoreType.DMA

    @functools.partial(
        pl.kernel, mesh=mesh, out_type=jax.ShapeDtypeStruct((b, d), table.dtype), name="sc_gather_rows",
        scratch_types=[pltpu.VMEM((win,), jnp.int32), pltpu.VMEM((win,), jnp.int32),
                       pltpu.VMEM((win, d), table.dtype), pltpu.VMEM((win, d), table.dtype),
                       dma, dma, dma, dma])
    def gather_kernel(table_hbm, idx_hbm, out_hbm, i0, i1, r0, r1, g0, g1, w0, w1):
        wid = lax.axis_index("s") * sc.num_cores + lax.axis_index("c")
        base = wid * per_w
        idxb, rows, gsem, wsem = (i0, i1), (r0, r1), (g0, g1), (w0, w1)

        def off(c):
            return pl.multiple_of(base + c * win, win)

        def gather(s):
            return pltpu.make_async_copy(table_hbm.at[idxb[s]], rows[s], gsem[s])

        def write(c, s):
            return pltpu.make_async_copy(rows[s], out_hbm.at[pl.ds(off(c), win)], wsem[s])

        pltpu.sync_copy(idx_hbm.at[pl.ds(off(0), win)], idxb[0])
        gather(0).start()

        @pl.loop(0, n_win, step=2)
        def _(c):
            for s in (0, 1):
                cc = c + s

                @pl.when(cc + 1 < n_win)
                def _():
                    @pl.when(cc >= 1)
                    def _():
                        write(cc - 1, 1 - s).wait()

                    pltpu.sync_copy(idx_hbm.at[pl.ds(off(cc + 1), win)], idxb[1 - s])
                    gather(1 - s).start()

                gather(s).wait()
                write(cc, s).start()

        write(n_win - 2, 0).wait()
        write(n_win - 1, 1).wait()

    return gather_kernel(table, idx)


def _sc_dispatch_rows(src, dest1, dest2, pad_idx, n_rows):
    n, d = src.shape
    win = _sc_window_rows(src)
    sc = plsc.get_sparse_core_info()
    n_workers = sc.num_cores * sc.num_subcores
    per_w = n // n_workers
    n_win = per_w // win
    assert per_w * n_workers == n and n_win * win == per_w and n_win % 2 == 0, (n, n_workers, win)
    n_pad = pad_idx.shape[0]
    assert n_pad + 2 * n == n_rows and n_pad % (n_workers * win) == 0, (n_pad, n, n_rows)
    pad_per_w = n_pad // n_workers
    zero_rows = jnp.zeros((win, d), src.dtype)
    mesh = plsc.VectorSubcoreMesh(core_axis_name="c", subcore_axis_name="s")
    dma = pltpu.SemaphoreType.DMA
    ivec = pltpu.VMEM((win,), jnp.int32)
    rbuf = pltpu.VMEM((win, d), src.dtype)

    @functools.partial(
        pl.kernel, mesh=mesh, out_type=jax.ShapeDtypeStruct((n_rows, d), src.dtype), name="sc_dispatch_rows",
        scratch_types=[ivec, ivec, ivec, ivec, rbuf, rbuf, dma, dma, dma, dma, dma, dma])
    def dispatch_kernel(src_hbm, d1_hbm, d2_hbm, pad_hbm, zero_hbm, out_hbm,
                        a0, a1, b0, b1, r0, r1, l0, l1, p0, p1, q0, q1):
        wid = lax.axis_index("s") * sc.num_cores + lax.axis_index("c")
        base = wid * per_w
        ia, ib, rows, lsem, psem, qsem = (a0, a1), (b0, b1), (r0, r1), (l0, l1), (p0, p1), (q0, q1)

        def off(c):
            return pl.multiple_of(base + c * win, win)

        def load(c, s):
            return pltpu.make_async_copy(src_hbm.at[pl.ds(off(c), win)], rows[s], lsem[s])

        def scatters(s):
            return (pltpu.make_async_copy(rows[s], out_hbm.at[ia[s]], psem[s]),
                    pltpu.make_async_copy(rows[s], out_hbm.at[ib[s]], qsem[s]))

        def fetch(c, s):
            pltpu.sync_copy(d1_hbm.at[pl.ds(off(c), win)], ia[s])
            pltpu.sync_copy(d2_hbm.at[pl.ds(off(c), win)], ib[s])
            load(c, s).start()

        fetch(0, 0)

        @pl.loop(0, n_win, step=2)
        def _(c):
            for s in (0, 1):
                cc = c + s

                @pl.when(cc + 1 < n_win)
                def _():
                    @pl.when(cc >= 1)
                    def _():
                        for cp in scatters(1 - s):
                            cp.wait()

                    fetch(cc + 1, 1 - s)

                load(cc, s).wait()
                for cp in scatters(s):
                    cp.start()

        for s in (0, 1):
            for cp in scatters(s):
                cp.wait()

        pltpu.sync_copy(zero_hbm, rows[0])

        @pl.loop(0, pad_per_w // win)
        def _(c):
            o = pl.multiple_of(wid * pad_per_w + c * win, win)
            pltpu.sync_copy(pad_hbm.at[pl.ds(o, win)], ia[0])
            pltpu.sync_copy(rows[0], out_hbm.at[ia[0]])

    return dispatch_kernel(src, dest1, dest2, pad_idx, zero_rows)


def _moe_kernel(bexp_ref, nused_ref, x_ref, wg_ref, wu_ref, wd_ref, o_ref, xb, acc):
    i = pl.program_id(0)
    j = pl.program_id(1)
    nj = pl.num_programs(1)
    used = i < nused_ref[0]

    @pl.when(used & (j == 0))
    def _():
        xb[...] = _unpack_bf16_pairs(x_ref[...]).astype(BF16)
        acc[...] = jnp.zeros(acc.shape, F32)

    @pl.when(used)
    def _():
        x = xb[...]
        g = jnp.dot(x, wg_ref[0].astype(BF16), preferred_element_type=F32)
        u = jnp.dot(x, wu_ref[0].astype(BF16), preferred_element_type=F32)
        act = (g * jax.nn.sigmoid(g) * u).astype(BF16)
        acc[...] += jnp.dot(act, wd_ref[0].astype(BF16), preferred_element_type=F32)

        @pl.when(j == nj - 1)
        def _():
            o_ref[...] = _pack_bf16_pairs(acc[...])

    @pl.when(jnp.logical_not(used) & (j == 0))
    def _():
        o_ref[...] = jnp.zeros(o_ref.shape, o_ref.dtype)


def _moe_experts(block_exp, n_used, xs, wg, wu, wd, tm):
    n_rows, dp = xs.shape
    d = 2 * dp
    n_blocks = n_rows // tm
    f = wg.shape[2]
    tf = MOE_TF if f % MOE_TF == 0 else f
    nj = f // tf
    assert nj >= 2

    def live(i, j, be, nu):
        u = i < nu[0]
        return jnp.where(u, i, nu[0] - 1), jnp.where(u, j, nj - 1)

    def w_in_map(i, j, be, nu):
        ii, jj = live(i, j, be, nu)
        return (be[ii], 0, jj)

    def w_down_map(i, j, be, nu):
        ii, jj = live(i, j, be, nu)
        return (be[ii], jj, 0)

    def x_map(i, j, be, nu):
        return (live(i, j, be, nu)[0], 0)

    def out_map(i, j, be, nu):
        return (i, 0)

    grid_spec = pltpu.PrefetchScalarGridSpec(
        num_scalar_prefetch=2,
        grid=(n_blocks, nj),
        in_specs=[
            pl.BlockSpec((tm, dp), x_map),
            pl.BlockSpec((1, d, tf), w_in_map),
            pl.BlockSpec((1, d, tf), w_in_map),
            pl.BlockSpec((1, tf, d), w_down_map),
        ],
        out_specs=pl.BlockSpec((tm, dp), out_map),
        scratch_shapes=[pltpu.VMEM((tm, d), BF16), pltpu.VMEM((tm, d), F32)],
    )
    return pl.pallas_call(
        _moe_kernel,
        grid_spec=grid_spec,
        out_shape=jax.ShapeDtypeStruct((n_rows, dp), jnp.int32),
        compiler_params=_params("arbitrary", "arbitrary"),
        name="moe_experts",
    )(block_exp, n_used, xs, wg, wu, wd)


def _combine_kernel(x_ref, route_ref, a_ref, b_ref, o_ref):
    route = route_ref[...]
    ya = _unpack_bf16_pairs(a_ref[...])
    yb = _unpack_bf16_pairs(b_ref[...])
    o_ref[...] = x_ref[...] + route[:, 2:3] * ya + route[:, 3:4] * yb


def _combine(yg, x2, route, part, n_parts):
    n, d = x2.shape
    dp = yg.shape[1]
    tm = min(COMB_TM, n // n_parts)
    nb = n // n_parts // tm
    first = part * nb
    return pl.pallas_call(
        _combine_kernel,
        grid=(nb,),
        in_specs=[pl.BlockSpec((tm, d), lambda i: (i + first, 0)),
                  pl.BlockSpec((tm, LANES), lambda i: (i + first, 0)),
                  pl.BlockSpec((tm, dp), lambda i: (i, 0)),
                  pl.BlockSpec((tm, dp), lambda i: (i + nb, 0))],
        out_specs=pl.BlockSpec((tm, d), lambda i: (i + first, 0)),
        out_shape=jax.ShapeDtypeStruct((n, d), F32),
        input_output_aliases={0: 0},
        compiler_params=_params("arbitrary"),
        name="moe_combine",
    )(x2, route, yg, yg)


def _top2_moe(ys_mix, w_out_bf, x2, g2, router, wg, wu, wd):
    n, d = x2.shape
    a = 2 * n
    tm = min(MOE_TM, n)
    r_pad = jnp.zeros((d, LANES), F32).at[:, :N_EXPERTS].set(router)
    rh, rl = _split_hi_lo(r_pad)
    rt = min(ROUTE_TM, n)
    tri = (lax.broadcasted_iota(jnp.int32, (rt, rt), 1) < lax.broadcasted_iota(jnp.int32, (rt, rt), 0)).astype(BF16)
    x2, hn, route, cols, cnt = _out_proj_route(ys_mix, w_out_bf, x2, g2, rh, rl, tri)

    counts = cnt[0, :N_EXPERTS].astype(jnp.int32)
    padded = (counts + tm - 1) // tm * tm
    pad_ends = jnp.cumsum(padded)
    pad_starts = pad_ends - padded
    def lookup(table, idx):
        hit = idx[:, None] == jnp.arange(N_EXPERTS, dtype=jnp.int32)[None, :]
        return jnp.sum(jnp.where(hit, table[None, :], 0), axis=1)

    cols = cols.T
    dest1 = lookup(pad_starts, cols[0]) + cols[4]
    dest2 = lookup(pad_starts, cols[1]) + cols[5]
    n_blocks = a // tm + N_EXPERTS
    n_rows = n_blocks * tm
    block_start = jnp.arange(n_blocks, dtype=jnp.int32) * tm
    block_exp = jnp.minimum(jnp.sum(block_start[:, None] >= pad_ends[None, :], axis=1), N_EXPERTS - 1).astype(jnp.int32)
    n_used = (pad_ends[-1] // tm).astype(jnp.int32).reshape(1)
    pad_cnt = padded - counts
    pad_cum = jnp.cumsum(pad_cnt)
    k = jnp.arange(n_rows - a, dtype=jnp.int32)
    seg = jnp.sum(k[:, None] >= pad_cum[None, :], axis=1).astype(jnp.int32)
    in_expert = lookup(pad_starts + counts - (pad_cum - pad_cnt), jnp.minimum(seg, N_EXPERTS - 1)) + k
    pad_idx = jnp.where(seg < N_EXPERTS, in_expert, pad_ends[-1] + k - pad_cum[-1])

    xs = _sc_dispatch_rows(hn, dest1, dest2, pad_idx, n_rows)
    ys = _moe_experts(block_exp, n_used, xs, wg, wu, wd, tm)
    n_parts = COMBINE_PARTS if n % (COMBINE_PARTS * COMB_TM) == 0 else 1
    step = n // n_parts
    gathered = [_sc_gather_rows(ys, jnp.concatenate([dest1[p * step:(p + 1) * step], dest2[p * step:(p + 1) * step]]))
                for p in range(n_parts)]
    for p in range(n_parts):
        x2 = _combine(gathered[p], x2, route, p, n_parts)
    return x2


def _block_diag(w):
    h, dh, _ = w.shape
    eye = jnp.eye(h, dtype=w.dtype)
    return jnp.einsum("hij,hg->higj", w, eye).reshape(h * dh, h * dh)


def kernel(x, norm1_g, w_in, q_norm_g, k_norm_g, conf_dw_w, conf_dw_b, conf_ln_g, conf_ln_b, gmlp_ln_g, gmlp_ln_b, gmlp_ws, gmlp_bs, lru_conv_w, lru_conv_b, lru_wa, lru_ba, lru_wx, lru_bx, lru_lambda, group_norm_g, w_out, norm2_g, ffn_w_gate, ffn_w_up, ffn_w_down, moe_router, moe_w_gate, moe_w_up, moe_w_down):
    bsz, s, d = x.shape
    n = bsz * s
    depth = w_in.shape[0]
    W = GROUP_WIDTH
    row = lambda v: v.reshape(1, -1).astype(F32)

    head_id = jnp.arange(W) // HEAD_DIM
    bd = (head_id[:, None] == head_id[None, :]).astype(BF16)
    bd2 = jnp.concatenate([bd, bd], axis=0)
    t_att = min(ATT_T, s)
    uu = (jnp.arange(t_att)[:, None] >= jnp.arange(t_att)[None, :]).astype(BF16)
    tril = jnp.tril(jnp.ones((GMLP_CHUNK, GMLP_CHUNK), dtype=bool))

    x2 = x.reshape(n, d)
    for l in range(depth):
        qg = row(jnp.tile(q_norm_g[l], GROUP_HEADS) * (HEAD_DIM ** -0.5))
        kg = row(jnp.tile(k_norm_g[l], GROUP_HEADS))
        gn4 = group_norm_g[l].reshape(N_GROUPS, W).astype(F32)
        cw = jnp.zeros((CONF_HALO, W), F32).at[:CONF_KERNEL].set(conf_dw_w[l])
        conf = (cw, row(conf_dw_b[l]), row(conf_ln_g[l]), row(conf_ln_b[l]))
        ws = jnp.where(tril, gmlp_ws[l], 0.0).astype(BF16)
        bs_mat = jnp.repeat(gmlp_bs[l].T, HEAD_DIM, axis=1)
        gmlp = (row(gmlp_ln_g[l]), row(gmlp_ln_b[l]), ws, bs_mat)
        lw = jnp.zeros((8, W), F32).at[:LRU_CONV].set(lru_conv_w[l])
        wax = jnp.concatenate([_block_diag(lru_wa[l]), _block_diag(lru_wx[l])], axis=1).astype(BF16)
        lru = (lw, row(lru_conv_b[l]), wax, row(lru_ba[l]), row(lru_bx[l]), row(lru_lambda[l]))
        q, k, v, y_b, y_c, y_d = _in_mix(x2, s, row(norm1_g[l]), w_in[l].astype(BF16), qg, kg, bd2,
                                         conf, gmlp, lru, gn4)
        to3 = lambda t: t.reshape(bsz, s, t.shape[-1])
        zmax = (ATT_ZMAX_SLACK * HEAD_DIM ** 0.5 * jnp.max(jnp.abs(q_norm_g[l] * k_norm_g[l]))).reshape(1)
        y_a = _sb_attention(zmax.astype(F32), to3(q), to3(k), to3(v), uu, gn4[0:1]).reshape(n, W)
        ys = [y_a, y_b, y_c, y_d]
        j = l // 2
        if l % 2 == 0:
            x2 = _out_proj_dense_ffn(ys, w_out[l].astype(BF16), x2, row(norm2_g[l]), ffn_w_gate[j].astype(BF16),
                                     ffn_w_up[j].astype(BF16), ffn_w_down[j].astype(BF16))
        else:
            x2 = _top2_moe(ys, w_out[l].astype(BF16), x2, row(norm2_g[l]), moe_router[j], moe_w_gate[j],
                           moe_w_up[j], moe_w_down[j])
    return x2.reshape(bsz, s, d)
```

```python
import functools

import jax
import jax.numpy as jnp
from jax import lax
from jax.experimental import pallas as pl
from jax.experimental.pallas import tpu as pltpu
from jax.experimental.pallas import tpu_sc as plsc

F32 = jnp.float32
BF16 = jnp.bfloat16

HEAD_DIM = 64
GROUP_HEADS = 4
GROUP_WIDTH = GROUP_HEADS * HEAD_DIM
N_GROUPS = 4
CONF_KERNEL = 31
GMLP_CHUNK = 128
LRU_CONV = 4
LRU_C = 8.0
N_EXPERTS = 8
EPS = 1e-6

LANES = 128
VMEM_LIMIT = 56 * 1024 * 1024

IN_TM = 1024
MIX_T = 256
ATT_T = 256
ATT_UNDERFLOW = 110.0
ATT_ZMAX_SLACK = 1.05
CONF_R = 64
CONF_HALO = 32
LRU_HALO = 8
FFN_TM = 1024
FFN_TF = 256
ROUTE_TM = 512
ROUTE_IDX_COLS = 8
MOE_TM = 1024
MOE_TF = 512
COMB_TM = 512
COMBINE_PARTS = 2
SC_WINDOW_BYTES = 128 * 1024
SC_MAX_INDEX_VECTOR = 128


def _params(*sem):
    return pltpu.CompilerParams(dimension_semantics=sem, vmem_limit_bytes=VMEM_LIMIT)


def _split_hi_lo(x):
    hi = x.astype(BF16)
    lo = (x - hi.astype(F32)).astype(BF16)
    return hi, lo


SOFTPLUS_CLAMP = 60.0


def _softplus(z):
    return jnp.maximum(jnp.log(1.0 + jnp.exp(jnp.minimum(z, SOFTPLUS_CLAMP))), z)


def _gelu_tanh(x):
    c = 0.7978845608028654
    return 0.5 * x * (1.0 + jnp.tanh(c * (x + 0.044715 * (x * x * x))))


def _pack_bf16_pairs(y):
    c = y.shape[1] // 2
    bits = lax.bitcast_convert_type(y.astype(BF16).astype(F32), jnp.uint32)
    word = (bits[:, :c] >> 16) | bits[:, c:]
    return lax.bitcast_convert_type(word, jnp.int32)


def _unpack_bf16_pairs(w):
    bits = lax.bitcast_convert_type(w, jnp.uint32)
    lo = lax.bitcast_convert_type(bits << 16, F32)
    hi = lax.bitcast_convert_type(bits & jnp.uint32(0xFFFF0000), F32)
    return jnp.concatenate([lo, hi], axis=1)


def _group_rms(y, g):
    ms = jnp.mean(y * y, axis=-1, keepdims=True)
    return y * lax.rsqrt(ms + EPS) * g


def _sb_attn_kernel(zmax_ref, q_ref, k_ref, v_ref, uu_ref, gn_ref, o_ref):
    T = q_ref.shape[1]
    n_pairs = GROUP_WIDTH // LANES
    i = pl.program_id(1)
    lane = lax.broadcasted_iota(jnp.int32, (T, LANES), 1)
    rows = lax.broadcasted_iota(jnp.int32, (2 * T, T), 0)
    cols = lax.broadcasted_iota(jnp.int32, (2 * T, T), 1)
    causal = cols < jnp.where(rows >= T, rows - T, rows)
    uu = uu_ref[...]

    def tiles(r0, carries, mask):
        ps = range(n_pairs)
        cs = [slice(p * LANES, (p + 1) * LANES) for p in ps]
        z = [lax.dot_general(qs[p], k_ref[0, pl.ds(r0, T), cs[p]], (((1,), (1,)), ((), ())),
                             preferred_element_type=F32) for p in ps]
        sp = [_softplus(z[p]) for p in ps]
        if mask is not None:
            sp = [jnp.where(mask, s, 0.0) for s in sp]
        cum = [jnp.dot(sp[p].astype(BF16), uu, preferred_element_type=F32) for p in ps]
        w = [jnp.exp(z[p] - cum[p] - carries[p]) for p in ps]
        if mask is not None:
            w = [jnp.where(mask, x, 0.0) for x in w]
        pv = [jnp.dot(w[p].astype(BF16), v_ref[0, pl.ds(r0, T), cs[p]], preferred_element_type=F32)
              for p in ps]
        return pv, [carries[p] + jnp.sum(sp[p], axis=1, keepdims=True) for p in ps]

    qs = []
    for pair in range(n_pairs):
        q2 = q_ref[0, :, pair * LANES:(pair + 1) * LANES]
        zero = jnp.zeros_like(q2)
        qs.append(jnp.concatenate([jnp.where(lane < HEAD_DIM, q2, zero),
                                   jnp.where(lane >= HEAD_DIM, q2, zero)], axis=0))
    row0 = pl.multiple_of(i * T, T)
    state = tuple(zip(*tiles(row0, [jnp.zeros((2 * T, 1), F32)] * n_pairs, causal)))

    dead_at = zmax_ref[0] + ATT_UNDERFLOW

    def alive(st):
        return functools.reduce(jnp.minimum, [jnp.min(carry) for _, carry in st]) <= dead_at

    def cond(c):
        step, live, _ = c
        return (step < i) & live

    def body(c):
        step, _, st = c
        r0 = pl.multiple_of((i - 1 - step) * T, T)
        pv, carries = tiles(r0, [carry for _, carry in st], None)
        new = tuple((st[p][0] + pv[p], carries[p]) for p in range(n_pairs))
        return step + 1, alive(new), new

    _, _, state = lax.while_loop(cond, body, (jnp.int32(0), alive(state), state))
    y = jnp.concatenate([jnp.where(lane < HEAD_DIM, acc[:T], acc[T:]) for acc, _ in state], axis=1)
    o_ref[0] = _group_rms(y, gn_ref[...]).astype(o_ref.dtype)


def _sb_attention(zmax, q3, k3, v3, uu, gn):
    b, s, w = q3.shape
    t = min(ATT_T, s)
    return pl.pallas_call(
        _sb_attn_kernel,
        grid=(b, s // t),
        in_specs=[
            pl.BlockSpec(memory_space=pltpu.SMEM),
            pl.BlockSpec((1, t, w), lambda bi, i: (bi, i, 0)),
            pl.BlockSpec((1, s, w), lambda bi, i: (bi, 0, 0)),
            pl.BlockSpec((1, s, w), lambda bi, i: (bi, 0, 0)),
            pl.BlockSpec((t, t), lambda bi, i: (0, 0)),
            pl.BlockSpec((1, w), lambda bi, i: (0, 0)),
        ],
        out_specs=pl.BlockSpec((1, t, w), lambda bi, i: (bi, i, 0)),
        out_shape=jax.ShapeDtypeStruct((b, s, w), BF16),
        compiler_params=_params("arbitrary", "arbitrary"),
        name="sb_attn",
    )(zmax, q3, k3, v3, uu, gn)


def _conformer_rows(val, gate, w_ref, b_ref, lng_ref, lnb_ref, gn, o_ref, row0, hs_ref, sh_ref):
    T = val.shape[0]
    H = CONF_HALO
    SUB = 8
    hs_ref[H:H + T, :] = val * jax.nn.sigmoid(gate)
    off = H - (CONF_KERNEL - 1)
    L = sh_ref.shape[1]
    for s in range(1, SUB):
        sh_ref[s - 1] = hs_ref[s:s + L, :]
    R = min(CONF_R, T)
    taps_w = [jnp.broadcast_to(w_ref[k:k + 1, :], (R, GROUP_WIDTH)) for k in range(CONF_KERNEL)]
    for r0 in range(0, T, R):
        acc = jnp.broadcast_to(b_ref[...], (R, GROUP_WIDTH))
        for k in range(CONF_KERNEL):
            s = (off + k) % SUB
            a = off + k - s + r0
            tap = hs_ref[a:a + R, :] if s == 0 else sh_ref[s - 1, a:a + R, :]
            acc = acc + taps_w[k] * tap
        mu = jnp.mean(acc, axis=-1, keepdims=True)
        xc = acc - mu
        var = jnp.mean(xc * xc, axis=-1, keepdims=True)
        y = xc * lax.rsqrt(var + EPS) * lng_ref[...] + lnb_ref[...]
        y = y * jax.nn.sigmoid(y)
        o_ref[row0 + r0:row0 + r0 + R, :] = _group_rms(y, gn).astype(o_ref.dtype)
    hs_ref[0:H, :] = hs_ref[T:T + H, :]


def _gmlp_rows(u_in, v_in, lng_ref, lnb_ref, ws_ref, bs_ref, gn, o_ref, row0):
    C = GMLP_CHUNK
    lane = lax.broadcasted_iota(jnp.int32, (C, GROUP_WIDTH), 1)
    u = _gelu_tanh(u_in)
    v = _gelu_tanh(v_in)
    mu = jnp.mean(v, axis=-1, keepdims=True)
    xc = v - mu
    var = jnp.mean(xc * xc, axis=-1, keepdims=True)
    vb = (xc * lax.rsqrt(var + EPS) * lng_ref[...] + lnb_ref[...]).astype(BF16)
    zero = jnp.zeros_like(vb)
    vstack = jnp.concatenate(
        [jnp.where((lane >= h * HEAD_DIM) & (lane < (h + 1) * HEAD_DIM), vb, zero) for h in range(GROUP_HEADS)],
        axis=0)
    mixed = bs_ref[...] + jnp.dot(ws_ref[...], vstack, preferred_element_type=F32)
    o_ref[row0:row0 + C, :] = _group_rms(u * mixed, gn).astype(o_ref.dtype)


def _lru_rows(x_in, gate, cw_ref, cb_ref, wax_ref, ba_ref, bx_ref, lam_ref, gn, o_ref, row0, xs_ref, hprev_ref):
    T = x_in.shape[0]
    W = GROUP_WIDTH
    H = LRU_HALO
    xs_ref[H:H + T, :] = x_in
    off = H - (LRU_CONV - 1)
    xb = jnp.broadcast_to(cb_ref[...], (T, W))
    for k in range(LRU_CONV):
        xb = xb + cw_ref[k:k + 1, :] * xs_ref[off + k:off + k + T, :]
    xs_ref[0:H, :] = xs_ref[T:T + H, :]

    pre = jnp.dot(xb.astype(BF16), wax_ref[...], preferred_element_type=F32)
    r = jax.nn.sigmoid(pre[:, :W] + ba_ref[...])
    ig = jax.nn.sigmoid(pre[:, W:] + bx_ref[...])
    nlam = -lam_ref[...]
    log_a = (-LRU_C) * r * (jnp.maximum(nlam, 0.0) + jnp.log1p(jnp.exp(-jnp.abs(nlam))))
    a = jnp.exp(log_a)
    th = jnp.tanh(log_a)
    one_minus_a2 = 2.0 * th / (th - 1.0)
    b = jnp.sqrt(one_minus_a2) * (ig * xb)

    SUB = 8
    a = a.reshape(T // SUB, SUB, W)
    b = b.reshape(T // SUB, SUB, W)
    sub = lax.broadcasted_iota(jnp.int32, a.shape, 1)
    d = 1
    while d < SUB:
        keep = sub >= d
        a_sh = jnp.where(keep, pltpu.roll(a, d, 1), 1.0)
        b_sh = jnp.where(keep, pltpu.roll(b, d, 1), 0.0)
        b = a * b_sh + b
        a = a * a_sh
        d *= 2
    carry = hprev_ref[0:1, :]
    tiles = []
    for g in range(T // SUB):
        hg = b[g] + a[g] * carry
        carry = hg[SUB - 1:SUB]
        tiles.append(hg)
    h = jnp.concatenate(tiles, axis=0)
    hprev_ref[...] = jnp.broadcast_to(carry, hprev_ref.shape)
    y = h * _gelu_tanh(gate)
    o_ref[row0:row0 + T, :] = _group_rms(y, gn).astype(o_ref.dtype)


def _in_mix_kernel(blocks_per_seq, x_ref, g_ref, w_ref, qg_ref, kg_ref, bd_ref,
                   cw_ref, cb_ref, clg_ref, clb_ref,
                   glg_ref, glb_ref, ws_ref, bs_ref,
                   lw_ref, lb_ref, wax_ref, ba_ref, bx_ref, lam_ref, gn_ref,
                   q_ref, k_ref, v_ref, yb_ref, yc_ref, yd_ref,
                   rest_ref, hs_ref, sh_ref, xs_ref, hprev_ref):
    tm = x_ref.shape[0]
    W = GROUP_WIDTH

    @pl.when(pl.program_id(0) % blocks_per_seq == 0)
    def _():
        hs_ref[0:CONF_HALO, :] = jnp.zeros((CONF_HALO, W), F32)
        xs_ref[0:LRU_HALO, :] = jnp.zeros((LRU_HALO, W), F32)
        hprev_ref[...] = jnp.zeros(hprev_ref.shape, F32)

    def step(rest_w, rest_r):
        xf = x_ref[...]
        ms = jnp.mean(xf * xf, axis=-1, keepdims=True)
        h = (xf * lax.rsqrt(ms + EPS) * g_ref[...]).astype(BF16)

        def proj(c):
            return jnp.dot(h, w_ref[:, c * W:(c + 1) * W], preferred_element_type=F32)

        def head_norm(p, gain):
            hi, lo = _split_hi_lo(p * p)
            ss = jnp.dot(jnp.concatenate([hi, lo], axis=1), bd_ref[...], preferred_element_type=F32)
            return p * lax.rsqrt(ss * (1.0 / HEAD_DIM) + EPS) * gain

        def project(c):
            if c == 0:
                q_ref[...] = head_norm(proj(0), qg_ref[...]).astype(BF16)
            elif c == 1:
                k_ref[...] = head_norm(proj(1), kg_ref[...]).astype(BF16)
            elif c == 2:
                v_ref[...] = proj(2).astype(BF16)
            else:
                rest_w[:, (c - 3) * W:(c - 2) * W] = proj(c)

        def cols(r0, n, c):
            return rest_r[r0:r0 + n, c * W:(c + 1) * W]

        T = min(MIX_T, tm)

        def conformer(r0):
            _conformer_rows(cols(r0, T, 0), cols(r0, T, 1), cw_ref, cb_ref, clg_ref, clb_ref, gn_ref[1:2, :],
                            yb_ref, r0, hs_ref, sh_ref)

        def lru(r0):
            _lru_rows(cols(r0, T, 4), cols(r0, T, 5), lw_ref, lb_ref, wax_ref, ba_ref, bx_ref, lam_ref,
                      gn_ref[3:4, :], yd_ref, r0, xs_ref, hprev_ref)

        def gmlp(r0):
            _gmlp_rows(cols(r0, GMLP_CHUNK, 2), cols(r0, GMLP_CHUNK, 3), glg_ref, glb_ref, ws_ref, bs_ref,
                       gn_ref[2:3, :], yc_ref, r0)

        mix = []
        for r0 in range(0, tm, T):
            mix.append(functools.partial(conformer, r0))
            mix.append(functools.partial(lru, r0))
            mix.extend(functools.partial(gmlp, c0) for c0 in range(r0, r0 + T, GMLP_CHUNK))
        n_proj = w_ref.shape[1] // W
        for c in range(3, n_proj):
            project(c)
        per = -(-len(mix) // 4)
        for c in range(4):
            if c > 0:
                project(c - 1)
            for task in mix[c * per:(c + 1) * per]:
                task()

    step(rest_ref, rest_ref)


def _in_mix(x2, seq_len, g, w_bf, qg, kg, bd2, conf, gmlp, lru, gn4):
    n, d = x2.shape
    cols = w_bf.shape[1]
    W = GROUP_WIDTH
    rest_cols = cols - 3 * W
    tm = min(IN_TM, seq_len)
    assert seq_len % tm == 0 and tm % min(MIX_T, tm) == 0 and tm % GMLP_CHUNK == 0
    t = min(MIX_T, tm)
    row = lambda i: (i, 0)
    whole = lambda a: pl.BlockSpec(a.shape, lambda i: (0,) * a.ndim)
    params = (g, w_bf, qg, kg, bd2) + tuple(conf) + tuple(gmlp) + tuple(lru) + (gn4,)
    out_shape = jax.ShapeDtypeStruct((n, W), BF16)
    return pl.pallas_call(
        functools.partial(_in_mix_kernel, seq_len // tm),
        grid=(n // tm,),
        in_specs=[pl.BlockSpec((tm, d), row)] + [whole(a) for a in params],
        out_specs=[pl.BlockSpec((tm, W), row)] * 6,
        out_shape=[out_shape] * 6,
        scratch_shapes=[pltpu.VMEM((tm, rest_cols), F32),
                        pltpu.VMEM((t + CONF_HALO, W), F32),
                        pltpu.VMEM((7, t + CONF_HALO - 8, W), F32),
                        pltpu.VMEM((t + LRU_HALO, W), F32),
                        pltpu.VMEM((8, W), F32)],
        compiler_params=_params("arbitrary"),
        name="in_mix",
    )(x2, *params)


def _out_proj_kernel(ya_ref, yb_ref, yc_ref, yd_ref, w_ref, x_ref, g_ref, rh_ref, rl_ref, tri_ref,
                     xo_ref, h_ref, route_ref, idx_ref, cnt_ref, base_ref):
    W = GROUP_WIDTH

    @pl.when(pl.program_id(0) == 0)
    def _():
        base_ref[...] = jnp.zeros(base_ref.shape, F32)

    acc = x_ref[...]
    for gi, y_ref in enumerate((ya_ref, yb_ref, yc_ref, yd_ref)):
        acc = acc + jnp.dot(y_ref[...], w_ref[gi * W:(gi + 1) * W, :], preferred_element_type=F32)
    xo_ref[...] = acc
    ms = jnp.mean(acc * acc, axis=-1, keepdims=True)
    h = acc * lax.rsqrt(ms + EPS) * g_ref[...]
    h_ref[...] = _pack_bf16_pairs(h)
    _route_rows(h, rh_ref, rl_ref, tri_ref, route_ref, idx_ref, cnt_ref, base_ref)


def _out_proj_route(ys, w_bf, x2, g, rh, rl, tri):
    n, d = x2.shape
    W = GROUP_WIDTH
    tm = tri.shape[0]
    row = lambda i: (i, 0)
    const = lambda i: (0, 0)
    ysp = pl.BlockSpec((tm, W), row)
    return pl.pallas_call(
        _out_proj_kernel,
        grid=(n // tm,),
        in_specs=[ysp, ysp, ysp, ysp,
                  pl.BlockSpec(w_bf.shape, const),
                  pl.BlockSpec((tm, d), row),
                  pl.BlockSpec((1, d), const),
                  pl.BlockSpec((d, LANES), const),
                  pl.BlockSpec((d, LANES), const),
                  pl.BlockSpec((tm, tm), const)],
        out_specs=[pl.BlockSpec((tm, d), row),
                   pl.BlockSpec((tm, d // 2), row),
                   pl.BlockSpec((tm, LANES), row),
                   pl.BlockSpec((tm, ROUTE_IDX_COLS), row),
                   pl.BlockSpec((8, LANES), const)],
        out_shape=[jax.ShapeDtypeStruct((n, d), F32),
                   jax.ShapeDtypeStruct((n, d // 2), jnp.int32),
                   jax.ShapeDtypeStruct((n, LANES), F32),
                   jax.ShapeDtypeStruct((n, ROUTE_IDX_COLS), jnp.int32),
                   jax.ShapeDtypeStruct((8, LANES), F32)],
        scratch_shapes=[pltpu.VMEM((8, LANES), F32)],
        compiler_params=_params("arbitrary"),
        name="out_proj_route",
    )(*ys, w_bf, x2, g, rh, rl, tri)


def _out_ffn_kernel(ya_ref, yb_ref, yc_ref, yd_ref, wo_ref, x_ref, g_ref, wg_ref, wu_ref, wd_ref, o_ref,
                    xn_ref, act_ref):
    W = GROUP_WIDTH
    f = wg_ref.shape[1]
    acc = x_ref[...]
    for gi, y_ref in enumerate((ya_ref, yb_ref, yc_ref, yd_ref)):
        acc = acc + jnp.dot(y_ref[...], wo_ref[gi * W:(gi + 1) * W, :], preferred_element_type=F32)
    xn_ref[...] = acc
    ms = jnp.mean(acc * acc, axis=-1, keepdims=True)
    h = (acc * lax.rsqrt(ms + EPS) * g_ref[...]).astype(BF16)
    for c0 in range(0, f, FFN_TF):
        g = jnp.dot(h, wg_ref[:, c0:c0 + FFN_TF], preferred_element_type=F32)
        u = jnp.dot(h, wu_ref[:, c0:c0 + FFN_TF], preferred_element_type=F32)
        act_ref[:, c0:c0 + FFN_TF] = (g * jax.nn.sigmoid(g) * u).astype(BF16)
    o_ref[...] = xn_ref[...] + jnp.dot(act_ref[...], wd_ref[...], preferred_element_type=F32)


def _out_proj_dense_ffn(ys, wo, x2, g, wg, wu, wd):
    n, d = x2.shape
    f = wg.shape[1]
    W = GROUP_WIDTH
    assert f % FFN_TF == 0
    tm = min(FFN_TM, n)
    row = lambda i: (i, 0)
    resident = dict(pipeline_mode=pl.Buffered(1))
    ysp = pl.BlockSpec((tm, W), row)
    return pl.pallas_call(
        _out_ffn_kernel,
        grid=(n // tm,),
        in_specs=[
            ysp, ysp, ysp, ysp,
            pl.BlockSpec(wo.shape, lambda i: (0, 0), **resident),
            pl.BlockSpec((tm, d), row),
            pl.BlockSpec((1, d), lambda i: (0, 0)),
            pl.BlockSpec((d, f), lambda i: (0, 0), **resident),
            pl.BlockSpec((d, f), lambda i: (0, 0), **resident),
            pl.BlockSpec((f, d), lambda i: (0, 0), **resident),
        ],
        out_specs=pl.BlockSpec((tm, d), row),
        out_shape=jax.ShapeDtypeStruct((n, d), F32),
        scratch_shapes=[pltpu.VMEM((tm, d), F32), pltpu.VMEM((tm, f), BF16)],
        compiler_params=_params("arbitrary"),
        name="out_proj_dense_ffn",
    )(*ys, wo, x2, g, wg, wu, wd)


def _route_rows(h, rh_ref, rl_ref, tri_ref, route_ref, idx_ref, cnt_ref, base_ref):
    tm = h.shape[0]
    hh, hl = _split_hi_lo(h)
    rh = rh_ref[...]
    logits = (jnp.dot(hh, rh, preferred_element_type=F32)
              + jnp.dot(hl, rh, preferred_element_type=F32)
              + jnp.dot(hh, rl_ref[...], preferred_element_type=F32))
    lane = lax.broadcasted_iota(jnp.int32, (tm, LANES), 1).astype(F32)
    neg = jnp.float32(-jnp.inf)
    logits = jnp.where(lane < N_EXPERTS, logits, neg)
    m1 = jnp.max(logits, axis=1, keepdims=True)
    i1 = jnp.min(jnp.where(logits == m1, lane, float(LANES)), axis=1, keepdims=True)
    l2 = jnp.where(lane == i1, neg, logits)
    m2 = jnp.max(l2, axis=1, keepdims=True)
    i2 = jnp.min(jnp.where(l2 == m2, lane, float(LANES)), axis=1, keepdims=True)
    e = jnp.exp(m2 - m1)
    g1 = 1.0 / (1.0 + e)
    g2 = e / (1.0 + e)
    oh1 = jnp.where(lane == i1, 1.0, 0.0)
    oh2 = jnp.where(lane == i2, 1.0, 0.0)
    oh = oh1 + oh2
    before = jnp.dot(tri_ref[...], oh.astype(BF16), preferred_element_type=F32) + base_ref[0:1, :]
    r1 = jnp.sum(oh1 * before, axis=1, keepdims=True)
    r2 = jnp.sum(oh2 * before, axis=1, keepdims=True)
    base = base_ref[0:1, :] + jnp.sum(oh, axis=0, keepdims=True)
    base_ref[...] = jnp.broadcast_to(base, base_ref.shape)
    cnt_ref[...] = jnp.broadcast_to(base, cnt_ref.shape)
    out = jnp.where(lane == 0, i1, 0.0)
    out = jnp.where(lane == 1, i2, out)
    out = jnp.where(lane == 2, g1, out)
    out = jnp.where(lane == 3, g2, out)
    out = jnp.where(lane == 4, r1, out)
    out = jnp.where(lane == 5, r2, out)
    route_ref[...] = out
    idx_ref[...] = out[:, :ROUTE_IDX_COLS].astype(jnp.int32)


def _sc_window_rows(table):
    row_bytes = table.shape[1] * table.dtype.itemsize
    return min(SC_MAX_INDEX_VECTOR, SC_WINDOW_BYTES // row_bytes)


def _sc_gather_rows(table, idx):
    _, d = table.shape
    b = idx.shape[0]
    win = _sc_window_rows(table)
    sc = plsc.get_sparse_core_info()
    n_workers = sc.num_cores * sc.num_subcores
    per_w = b // n_workers
    n_win = per_w // win
    assert per_w * n_workers == b and n_win * win == per_w and n_win % 2 == 0, (b, n_workers, win)
    mesh = plsc.VectorSubcoreMesh(core_axis_name="c", subcore_axis_name="s")
    dma = pltpu.SemaphoreType.DMA

    @functools.partial(
        pl.kernel, mesh=mesh, out_type=jax.ShapeDtypeStruct((b, d), table.dtype), name="sc_gather_rows",
        scratch_types=[pltpu.VMEM((win,), jnp.int32), pltpu.VMEM((win,), jnp.int32),
                       pltpu.VMEM((win, d), table.dtype), pltpu.VMEM((win, d), table.dtype),
                       dma, dma, dma, dma])
    def gather_kernel(table_hbm, idx_hbm, out_hbm, i0, i1, r0, r1, g0, g1, w0, w1):
        wid = lax.axis_index("s") * sc.num_cores + lax.axis_index("c")
        base = wid * per_w
        idxb, rows, gsem, wsem = (i0, i1), (r0, r1), (g0, g1), (w0, w1)

        def off(c):
            return pl.multiple_of(base + c * win, win)

        def gather(s):
            return pltpu.make_async_copy(table_hbm.at[idxb[s]], rows[s], gsem[s])

        def write(c, s):
            return pltpu.make_async_copy(rows[s], out_hbm.at[pl.ds(off(c), win)], wsem[s])

        pltpu.sync_copy(idx_hbm.at[pl.ds(off(0), win)], idxb[0])
        gather(0).start()

        @pl.loop(0, n_win, step=2)
        def _(c):
            for s in (0, 1):
                cc = c + s

                @pl.when(cc + 1 < n_win)
                def _():
                    @pl.when(cc >= 1)
                    def _():
                        write(cc - 1, 1 - s).wait()

                    pltpu.sync_copy(idx_hbm.at[pl.ds(off(cc + 1), win)], idxb[1 - s])
                    gather(1 - s).start()

                gather(s).wait()
                write(cc, s).start()

        write(n_win - 2, 0).wait()
        write(n_win - 1, 1).wait()

    return gather_kernel(table, idx)


def _sc_dispatch_rows(src, dest1, dest2, pad_idx, n_rows):
    n, d = src.shape
    win = _sc_window_rows(src)
    sc = plsc.get_sparse_core_info()
    n_workers = sc.num_cores * sc.num_subcores
    per_w = n // n_workers
    n_win = per_w // win
    assert per_w * n_workers == n and n_win * win == per_w and n_win % 2 == 0, (n, n_workers, win)
    n_pad = pad_idx.shape[0]
    assert n_pad + 2 * n == n_rows and n_pad % (n_workers * win) == 0, (n_pad, n, n_rows)
    pad_per_w = n_pad // n_workers
    zero_rows = jnp.zeros((win, d), src.dtype)
    mesh = plsc.VectorSubcoreMesh(core_axis_name="c", subcore_axis_name="s")
    dma = pltpu.SemaphoreType.DMA
    ivec = pltpu.VMEM((win,), jnp.int32)
    rbuf = pltpu.VMEM((win, d), src.dtype)

    @functools.partial(
        pl.kernel, mesh=mesh, out_type=jax.ShapeDtypeStruct((n_rows, d), src.dtype), name="sc_dispatch_rows",
        scratch_types=[ivec, ivec, ivec, ivec, rbuf, rbuf, dma, dma, dma, dma, dma, dma])
    def dispatch_kernel(src_hbm, d1_hbm, d2_hbm, pad_hbm, zero_hbm, out_hbm,
                        a0, a1, b0, b1, r0, r1, l0, l1, p0, p1, q0, q1):
        wid = lax.axis_index("s") * sc.num_cores + lax.axis_index("c")
        base = wid * per_w
        ia, ib, rows, lsem, psem, qsem = (a0, a1), (b0, b1), (r0, r1), (l0, l1), (p0, p1), (q0, q1)

        def off(c):
            return pl.multiple_of(base + c * win, win)

        def load(c, s):
            return pltpu.make_async_copy(src_hbm.at[pl.ds(off(c), win)], rows[s], lsem[s])

        def scatters(s):
            return (pltpu.make_async_copy(rows[s], out_hbm.at[ia[s]], psem[s]),
                    pltpu.make_async_copy(rows[s], out_hbm.at[ib[s]], qsem[s]))

        def fetch(c, s):
            pltpu.sync_copy(d1_hbm.at[pl.ds(off(c), win)], ia[s])
            pltpu.sync_copy(d2_hbm.at[pl.ds(off(c), win)], ib[s])
            load(c, s).start()

        fetch(0, 0)

        @pl.loop(0, n_win, step=2)
        def _(c):
            for s in (0, 1):
                cc = c + s

                @pl.when(cc + 1 < n_win)
                def _():
                    @pl.when(cc >= 1)
                    def _():
                        for cp in scatters(1 - s):
                            cp.wait()

                    fetch(cc + 1, 1 - s)

                load(cc, s).wait()
                for cp in scatters(s):
                    cp.start()

        for s in (0, 1):
            for cp in scatters(s):
                cp.wait()

        pltpu.sync_copy(zero_hbm, rows[0])

        @pl.loop(0, pad_per_w // win)
        def _(c):
            o = pl.multiple_of(wid * pad_per_w + c * win, win)
            pltpu.sync_copy(pad_hbm.at[pl.ds(o, win)], ia[0])
            pltpu.sync_copy(rows[0], out_hbm.at[ia[0]])

    return dispatch_kernel(src, dest1, dest2, pad_idx, zero_rows)


def _moe_kernel(bexp_ref, nused_ref, x_ref, wg_ref, wu_ref, wd_ref, o_ref, xb, acc):
    i = pl.program_id(0)
    j = pl.program_id(1)
    nj = pl.num_programs(1)
    used = i < nused_ref[0]

    @pl.when(used & (j == 0))
    def _():
        xb[...] = _unpack_bf16_pairs(x_ref[...]).astype(BF16)
        acc[...] = jnp.zeros(acc.shape, F32)

    @pl.when(used)
    def _():
        x = xb[...]
        g = jnp.dot(x, wg_ref[0].astype(BF16), preferred_element_type=F32)
        u = jnp.dot(x, wu_ref[0].astype(BF16), preferred_element_type=F32)
        act = (g * jax.nn.sigmoid(g) * u).astype(BF16)
        acc[...] += jnp.dot(act, wd_ref[0].astype(BF16), preferred_element_type=F32)

        @pl.when(j == nj - 1)
        def _():
            o_ref[...] = _pack_bf16_pairs(acc[...])

    @pl.when(jnp.logical_not(used) & (j == 0))
    def _():
        o_ref[...] = jnp.zeros(o_ref.shape, o_ref.dtype)


def _moe_experts(block_exp, n_used, xs, wg, wu, wd, tm):
    n_rows, dp = xs.shape
    d = 2 * dp
    n_blocks = n_rows // tm
    f = wg.shape[2]
    tf = MOE_TF if f % MOE_TF == 0 else f
    nj = f // tf
    assert nj >= 2

    def live(i, j, be, nu):
        u = i < nu[0]
        return jnp.where(u, i, nu[0] - 1), jnp.where(u, j, nj - 1)

    def w_in_map(i, j, be, nu):
        ii, jj = live(i, j, be, nu)
        return (be[ii], 0, jj)

    def w_down_map(i, j, be, nu):
        ii, jj = live(i, j, be, nu)
        return (be[ii], jj, 0)

    def x_map(i, j, be, nu):
        return (live(i, j, be, nu)[0], 0)

    def out_map(i, j, be, nu):
        return (i, 0)

    grid_spec = pltpu.PrefetchScalarGridSpec(
        num_scalar_prefetch=2,
        grid=(n_blocks, nj),
        in_specs=[
            pl.BlockSpec((tm, dp), x_map),
            pl.BlockSpec((1, d, tf), w_in_map),
            pl.BlockSpec((1, d, tf), w_in_map),
            pl.BlockSpec((1, tf, d), w_down_map),
        ],
        out_specs=pl.BlockSpec((tm, dp), out_map),
        scratch_shapes=[pltpu.VMEM((tm, d), BF16), pltpu.VMEM((tm, d), F32)],
    )
    return pl.pallas_call(
        _moe_kernel,
        grid_spec=grid_spec,
        out_shape=jax.ShapeDtypeStruct((n_rows, dp), jnp.int32),
        compiler_params=_params("arbitrary", "arbitrary"),
        name="moe_experts",
    )(block_exp, n_used, xs, wg, wu, wd)


def _combine_kernel(x_ref, route_ref, a_ref, b_ref, o_ref):
    route = route_ref[...]
    ya = _unpack_bf16_pairs(a_ref[...])
    yb = _unpack_bf16_pairs(b_ref[...])
    o_ref[...] = x_ref[...] + route[:, 2:3] * ya + route[:, 3:4] * yb


def _combine(yg, x2, route, part, n_parts):
    n, d = x2.shape
    dp = yg.shape[1]
    tm = min(COMB_TM, n // n_parts)
    nb = n // n_parts // tm
    first = part * nb
    return pl.pallas_call(
        _combine_kernel,
        grid=(nb,),
        in_specs=[pl.BlockSpec((tm, d), lambda i: (i + first, 0)),
                  pl.BlockSpec((tm, LANES), lambda i: (i + first, 0)),
                  pl.BlockSpec((tm, dp), lambda i: (i, 0)),
                  pl.BlockSpec((tm, dp), lambda i: (i + nb, 0))],
        out_specs=pl.BlockSpec((tm, d), lambda i: (i + first, 0)),
        out_shape=jax.ShapeDtypeStruct((n, d), F32),
        input_output_aliases={0: 0},
        compiler_params=_params("arbitrary"),
        name="moe_combine",
    )(x2, route, yg, yg)


def _top2_moe(ys_mix, w_out_bf, x2, g2, router, wg, wu, wd):
    n, d = x2.shape
    a = 2 * n
    tm = min(MOE_TM, n)
    r_pad = jnp.zeros((d, LANES), F32).at[:, :N_EXPERTS].set(router)
    rh, rl = _split_hi_lo(r_pad)
    rt = min(ROUTE_TM, n)
    tri = (lax.broadcasted_iota(jnp.int32, (rt, rt), 1) < lax.broadcasted_iota(jnp.int32, (rt, rt), 0)).astype(BF16)
    x2, hn, route, cols, cnt = _out_proj_route(ys_mix, w_out_bf, x2, g2, rh, rl, tri)

    counts = cnt[0, :N_EXPERTS].astype(jnp.int32)
    padded = (counts + tm - 1) // tm * tm
    pad_ends = jnp.cumsum(padded)
    pad_starts = pad_ends - padded
    def lookup(table, idx):
        hit = idx[None, :] == jnp.arange(N_EXPERTS, dtype=jnp.int32)[:, None]
        return jnp.sum(jnp.where(hit, table[:, None], 0), axis=0)

    cols = cols.T
    dest1 = lookup(pad_starts, cols[0]) + cols[4]
    dest2 = lookup(pad_starts, cols[1]) + cols[5]
    n_blocks = a // tm + N_EXPERTS
    n_rows = n_blocks * tm
    block_start = jnp.arange(n_blocks, dtype=jnp.int32) * tm
    block_exp = jnp.minimum(jnp.sum(block_start[None, :] >= pad_ends[:, None], axis=0), N_EXPERTS - 1).astype(jnp.int32)
    n_used = (pad_ends[-1] // tm).astype(jnp.int32).reshape(1)
    pad_cnt = padded - counts
    pad_cum = jnp.cumsum(pad_cnt)
    k = jnp.arange(n_rows - a, dtype=jnp.int32)
    seg = jnp.sum(k[None, :] >= pad_cum[:, None], axis=0).astype(jnp.int32)
    in_expert = lookup(pad_starts + counts - (pad_cum - pad_cnt), jnp.minimum(seg, N_EXPERTS - 1)) + k
    pad_idx = jnp.where(seg < N_EXPERTS, in_expert, pad_ends[-1] + k - pad_cum[-1])

    xs = _sc_dispatch_rows(hn, dest1, dest2, pad_idx, n_rows)
    ys = _moe_experts(block_exp, n_used, xs, wg, wu, wd, tm)
    n_parts = COMBINE_PARTS if n % (COMBINE_PARTS * COMB_TM) == 0 else 1
    step = n // n_parts
    gathered = [_sc_gather_rows(ys, jnp.concatenate([dest1[p * step:(p + 1) * step], dest2[p * step:(p + 1) * step]]))
                for p in range(n_parts)]
    for p in range(n_parts):
        x2 = _combine(gathered[p], x2, route, p, n_parts)
    return x2


def _block_diag(w):
    h, dh, _ = w.shape
    eye = jnp.eye(h, dtype=w.dtype)
    return jnp.einsum("hij,hg->higj", w, eye).reshape(h * dh, h * dh)


def kernel(x, norm1_g, w_in, q_norm_g, k_norm_g, conf_dw_w, conf_dw_b, conf_ln_g, conf_ln_b, gmlp_ln_g, gmlp_ln_b, gmlp_ws, gmlp_bs, lru_conv_w, lru_conv_b, lru_wa, lru_ba, lru_wx, lru_bx, lru_lambda, group_norm_g, w_out, norm2_g, ffn_w_gate, ffn_w_up, ffn_w_down, moe_router, moe_w_gate, moe_w_up, moe_w_down):
    bsz, s, d = x.shape
    n = bsz * s
    depth = w_in.shape[0]
    W = GROUP_WIDTH
    row = lambda v: v.reshape(1, -1).astype(F32)

    head_id = jnp.arange(W) // HEAD_DIM
    bd = (head_id[:, None] == head_id[None, :]).astype(BF16)
    bd2 = jnp.concatenate([bd, bd], axis=0)
    t_att = min(ATT_T, s)
    uu = (jnp.arange(t_att)[:, None] >= jnp.arange(t_att)[None, :]).astype(BF16)
    tril = jnp.tril(jnp.ones((GMLP_CHUNK, GMLP_CHUNK), dtype=bool))

    x2 = x.reshape(n, d)
    for l in range(depth):
        qg = row(jnp.tile(q_norm_g[l], GROUP_HEADS) * (HEAD_DIM ** -0.5))
        kg = row(jnp.tile(k_norm_g[l], GROUP_HEADS))
        gn4 = group_norm_g[l].reshape(N_GROUPS, W).astype(F32)
        cw = jnp.zeros((CONF_HALO, W), F32).at[:CONF_KERNEL].set(conf_dw_w[l])
        conf = (cw, row(conf_dw_b[l]), row(conf_ln_g[l]), row(conf_ln_b[l]))
        ws = jnp.concatenate(list(jnp.where(tril, gmlp_ws[l], 0.0).astype(BF16)), axis=1)
        bs_mat = jnp.repeat(gmlp_bs[l].T, HEAD_DIM, axis=1)
        gmlp = (row(gmlp_ln_g[l]), row(gmlp_ln_b[l]), ws, bs_mat)
        lw = jnp.zeros((8, W), F32).at[:LRU_CONV].set(lru_conv_w[l])
        wax = jnp.concatenate([_block_diag(lru_wa[l]), _block_diag(lru_wx[l])], axis=1).astype(BF16)
        lru = (lw, row(lru_conv_b[l]), wax, row(lru_ba[l]), row(lru_bx[l]), row(lru_lambda[l]))
        q, k, v, y_b, y_c, y_d = _in_mix(x2, s, row(norm1_g[l]), w_in[l].astype(BF16), qg, kg, bd2,
                                         conf, gmlp, lru, gn4)
        to3 = lambda t: t.reshape(bsz, s, t.shape[-1])
        zmax = (ATT_ZMAX_SLACK * HEAD_DIM ** 0.5 * jnp.max(jnp.abs(q_norm_g[l] * k_norm_g[l]))).reshape(1)
        y_a = _sb_attention(zmax.astype(F32), to3(q), to3(k), to3(v), uu, gn4[0:1]).reshape(n, W)
        ys = [y_a, y_b, y_c, y_d]
        j = l // 2
        if l % 2 == 0:
            x2 = _out_proj_dense_ffn(ys, w_out[l].astype(BF16), x2, row(norm2_g[l]), ffn_w_gate[j].astype(BF16),
                                     ffn_w_up[j].astype(BF16), ffn_w_down[j].astype(BF16))
        else:
            x2 = _top2_moe(ys, w_out[l].astype(BF16), x2, row(norm2_g[l]), moe_router[j], moe_w_gate[j],
                           moe_w_up[j], moe_w_down[j])
    return x2.reshape(bsz, s, d)
```

```python
import functools

import jax
import jax.numpy as jnp
from jax import lax
from jax.experimental import pallas as pl
from jax.experimental.pallas import tpu as pltpu
from jax.experimental.pallas import tpu_sc as plsc

F32 = jnp.float32
BF16 = jnp.bfloat16

HEAD_DIM = 64
GROUP_HEADS = 4
GROUP_WIDTH = GROUP_HEADS * HEAD_DIM
N_GROUPS = 4
CONF_KERNEL = 31
GMLP_CHUNK = 128
LRU_CONV = 4
LRU_C = 8.0
N_EXPERTS = 8
EPS = 1e-6

LANES = 128
VMEM_LIMIT = 56 * 1024 * 1024

IN_TM = 1024
MIX_T = 256
ATT_T = 256
ATT_UNDERFLOW = 110.0
ATT_ZMAX_SLACK = 1.05
CONF_R = 64
CONF_HALO = 32
LRU_HALO = 8
FFN_TM = 1024
FFN_TF = 256
ROUTE_TM = 512
ROUTE_IDX_COLS = 8
MOE_TM = 1024
MOE_TF = 512
COMB_TM = 512
COMBINE_PARTS = 2
SC_WINDOW_BYTES = 128 * 1024
SC_MAX_INDEX_VECTOR = 128


def _params(*sem):
    return pltpu.CompilerParams(dimension_semantics=sem, vmem_limit_bytes=VMEM_LIMIT)


def _split_hi_lo(x):
    hi = x.astype(BF16)
    lo = (x - hi.astype(F32)).astype(BF16)
    return hi, lo


SOFTPLUS_CLAMP = 60.0


def _softplus(z):
    return jnp.maximum(jnp.log(1.0 + jnp.exp(jnp.minimum(z, SOFTPLUS_CLAMP))), z)


def _gelu_tanh(x):
    c = 0.7978845608028654
    return 0.5 * x * (1.0 + jnp.tanh(c * (x + 0.044715 * (x * x * x))))


def _pack_bf16_pairs(y):
    c = y.shape[1] // 2
    bits = lax.bitcast_convert_type(y.astype(BF16).astype(F32), jnp.uint32)
    word = (bits[:, :c] >> 16) | bits[:, c:]
    return lax.bitcast_convert_type(word, jnp.int32)


def _unpack_bf16_pairs(w):
    bits = lax.bitcast_convert_type(w, jnp.uint32)
    lo = lax.bitcast_convert_type(bits << 16, F32)
    hi = lax.bitcast_convert_type(bits & jnp.uint32(0xFFFF0000), F32)
    return jnp.concatenate([lo, hi], axis=1)


def _group_rms(y, g):
    ms = jnp.mean(y * y, axis=-1, keepdims=True)
    return y * lax.rsqrt(ms + EPS) * g


def _sb_attn_kernel(zmax_ref, q_ref, k_ref, v_ref, uu_ref, gn_ref, o_ref):
    T = q_ref.shape[1]
    n_pairs = GROUP_WIDTH // LANES
    i = pl.program_id(1)
    lane = lax.broadcasted_iota(jnp.int32, (T, LANES), 1)
    rows = lax.broadcasted_iota(jnp.int32, (2 * T, T), 0)
    cols = lax.broadcasted_iota(jnp.int32, (2 * T, T), 1)
    causal = cols < jnp.where(rows >= T, rows - T, rows)
    uu = uu_ref[...]

    def tiles(r0, carries, mask):
        ps = range(n_pairs)
        cs = [slice(p * LANES, (p + 1) * LANES) for p in ps]
        z = [lax.dot_general(qs[p], k_ref[0, pl.ds(r0, T), cs[p]], (((1,), (1,)), ((), ())),
                             preferred_element_type=F32) for p in ps]
        sp = [_softplus(z[p]) for p in ps]
        if mask is not None:
            sp = [jnp.where(mask, s, 0.0) for s in sp]
        cum = [jnp.dot(sp[p].astype(BF16), uu, preferred_element_type=F32) for p in ps]
        w = [jnp.exp(z[p] - cum[p] - carries[p]) for p in ps]
        if mask is not None:
            w = [jnp.where(mask, x, 0.0) for x in w]
        pv = [jnp.dot(w[p].astype(BF16), v_ref[0, pl.ds(r0, T), cs[p]], preferred_element_type=F32)
              for p in ps]
        return pv, [carries[p] + jnp.sum(sp[p], axis=1, keepdims=True) for p in ps]

    qs = []
    for pair in range(n_pairs):
        q2 = q_ref[0, :, pair * LANES:(pair + 1) * LANES]
        zero = jnp.zeros_like(q2)
        qs.append(jnp.concatenate([jnp.where(lane < HEAD_DIM, q2, zero),
                                   jnp.where(lane >= HEAD_DIM, q2, zero)], axis=0))
    row0 = pl.multiple_of(i * T, T)
    state = tuple(zip(*tiles(row0, [jnp.zeros((2 * T, 1), F32)] * n_pairs, causal)))

    dead_at = zmax_ref[0] + ATT_UNDERFLOW

    def alive(st):
        return functools.reduce(jnp.minimum, [jnp.min(carry) for _, carry in st]) <= dead_at

    def cond(c):
        step, live, _ = c
        return (step < i) & live

    def body(c):
        step, _, st = c
        r0 = pl.multiple_of((i - 1 - step) * T, T)
        pv, carries = tiles(r0, [carry for _, carry in st], None)
        new = tuple((st[p][0] + pv[p], carries[p]) for p in range(n_pairs))
        return step + 1, alive(new), new

    _, _, state = lax.while_loop(cond, body, (jnp.int32(0), alive(state), state))
    y = jnp.concatenate([jnp.where(lane < HEAD_DIM, acc[:T], acc[T:]) for acc, _ in state], axis=1)
    o_ref[0] = _group_rms(y, gn_ref[...]).astype(o_ref.dtype)


def _sb_attention(zmax, q3, k3, v3, uu, gn):
    b, s, w = q3.shape
    t = min(ATT_T, s)
    return pl.pallas_call(
        _sb_attn_kernel,
        grid=(b, s // t),
        in_specs=[
            pl.BlockSpec(memory_space=pltpu.SMEM),
            pl.BlockSpec((1, t, w), lambda bi, i: (bi, i, 0)),
            pl.BlockSpec((1, s, w), lambda bi, i: (bi, 0, 0)),
            pl.BlockSpec((1, s, w), lambda bi, i: (bi, 0, 0)),
            pl.BlockSpec((t, t), lambda bi, i: (0, 0)),
            pl.BlockSpec((1, w), lambda bi, i: (0, 0)),
        ],
        out_specs=pl.BlockSpec((1, t, w), lambda bi, i: (bi, i, 0)),
        out_shape=jax.ShapeDtypeStruct((b, s, w), BF16),
        compiler_params=_params("arbitrary", "arbitrary"),
        name="sb_attn",
    )(zmax, q3, k3, v3, uu, gn)


def _conformer_rows(val, gate, w_ref, b_ref, lng_ref, lnb_ref, gn, o_ref, row0, hs_ref, sh_ref):
    T = val.shape[0]
    H = CONF_HALO
    SUB = 8
    hs_ref[H:H + T, :] = val * jax.nn.sigmoid(gate)
    off = H - (CONF_KERNEL - 1)
    L = sh_ref.shape[1]
    for s in range(1, SUB):
        sh_ref[s - 1] = hs_ref[s:s + L, :]
    R = min(CONF_R, T)
    taps_w = [jnp.broadcast_to(w_ref[k:k + 1, :], (R, GROUP_WIDTH)) for k in range(CONF_KERNEL)]
    for r0 in range(0, T, R):
        acc = jnp.broadcast_to(b_ref[...], (R, GROUP_WIDTH))
        for k in range(CONF_KERNEL):
            s = (off + k) % SUB
            a = off + k - s + r0
            tap = hs_ref[a:a + R, :] if s == 0 else sh_ref[s - 1, a:a + R, :]
            acc = acc + taps_w[k] * tap
        mu = jnp.mean(acc, axis=-1, keepdims=True)
        xc = acc - mu
        var = jnp.mean(xc * xc, axis=-1, keepdims=True)
        y = xc * lax.rsqrt(var + EPS) * lng_ref[...] + lnb_ref[...]
        y = y * jax.nn.sigmoid(y)
        o_ref[row0 + r0:row0 + r0 + R, :] = _group_rms(y, gn).astype(o_ref.dtype)
    hs_ref[0:H, :] = hs_ref[T:T + H, :]


def _gmlp_rows(u_in, v_in, lng_ref, lnb_ref, ws_ref, bs_ref, gn, o_ref, row0):
    C = GMLP_CHUNK
    lane = lax.broadcasted_iota(jnp.int32, (C, GROUP_WIDTH), 1)
    u = _gelu_tanh(u_in)
    v = _gelu_tanh(v_in)
    mu = jnp.mean(v, axis=-1, keepdims=True)
    xc = v - mu
    var = jnp.mean(xc * xc, axis=-1, keepdims=True)
    vb = (xc * lax.rsqrt(var + EPS) * lng_ref[...] + lnb_ref[...]).astype(BF16)
    zero = jnp.zeros_like(vb)
    vstack = jnp.concatenate(
        [jnp.where((lane >= h * HEAD_DIM) & (lane < (h + 1) * HEAD_DIM), vb, zero) for h in range(GROUP_HEADS)],
        axis=0)
    mixed = bs_ref[...] + jnp.dot(ws_ref[...], vstack, preferred_element_type=F32)
    o_ref[row0:row0 + C, :] = _group_rms(u * mixed, gn).astype(o_ref.dtype)


def _lru_rows(x_in, gate, cw_ref, cb_ref, wax_ref, ba_ref, bx_ref, lam_ref, gn, o_ref, row0, xs_ref, hprev_ref):
    T = x_in.shape[0]
    W = GROUP_WIDTH
    H = LRU_HALO
    xs_ref[H:H + T, :] = x_in
    off = H - (LRU_CONV - 1)
    xb = jnp.broadcast_to(cb_ref[...], (T, W))
    for k in range(LRU_CONV):
        xb = xb + cw_ref[k:k + 1, :] * xs_ref[off + k:off + k + T, :]
    xs_ref[0:H, :] = xs_ref[T:T + H, :]

    pre = jnp.dot(xb.astype(BF16), wax_ref[...], preferred_element_type=F32)
    r = jax.nn.sigmoid(pre[:, :W] + ba_ref[...])
    ig = jax.nn.sigmoid(pre[:, W:] + bx_ref[...])
    nlam = -lam_ref[...]
    log_a = (-LRU_C) * r * (jnp.maximum(nlam, 0.0) + jnp.log1p(jnp.exp(-jnp.abs(nlam))))
    a = jnp.exp(log_a)
    th = jnp.tanh(log_a)
    one_minus_a2 = 2.0 * th / (th - 1.0)
    b = jnp.sqrt(one_minus_a2) * (ig * xb)

    SUB = 8
    a = a.reshape(T // SUB, SUB, W)
    b = b.reshape(T // SUB, SUB, W)
    sub = lax.broadcasted_iota(jnp.int32, a.shape, 1)
    d = 1
    while d < SUB:
        keep = sub >= d
        a_sh = jnp.where(keep, pltpu.roll(a, d, 1), 1.0)
        b_sh = jnp.where(keep, pltpu.roll(b, d, 1), 0.0)
        b = a * b_sh + b
        a = a * a_sh
        d *= 2
    carry = hprev_ref[0:1, :]
    tiles = []
    for g in range(T // SUB):
        hg = b[g] + a[g] * carry
        carry = hg[SUB - 1:SUB]
        tiles.append(hg)
    h = jnp.concatenate(tiles, axis=0)
    hprev_ref[...] = jnp.broadcast_to(carry, hprev_ref.shape)
    y = h * _gelu_tanh(gate)
    o_ref[row0:row0 + T, :] = _group_rms(y, gn).astype(o_ref.dtype)


def _in_mix_kernel(blocks_per_seq, x_ref, g_ref, w_ref, qg_ref, kg_ref, bd_ref,
                   cw_ref, cb_ref, clg_ref, clb_ref,
                   glg_ref, glb_ref, ws_ref, bs_ref,
                   lw_ref, lb_ref, wax_ref, ba_ref, bx_ref, lam_ref, gn_ref,
                   q_ref, k_ref, v_ref, yb_ref, yc_ref, yd_ref,
                   rest_ref, hs_ref, sh_ref, xs_ref, hprev_ref):
    tm = x_ref.shape[0]
    W = GROUP_WIDTH

    @pl.when(pl.program_id(0) % blocks_per_seq == 0)
    def _():
        hs_ref[0:CONF_HALO, :] = jnp.zeros((CONF_HALO, W), F32)
        xs_ref[0:LRU_HALO, :] = jnp.zeros((LRU_HALO, W), F32)
        hprev_ref[...] = jnp.zeros(hprev_ref.shape, F32)

    def step(rest_w, rest_r):
        xf = x_ref[...]
        ms = jnp.mean(xf * xf, axis=-1, keepdims=True)
        h = (xf * lax.rsqrt(ms + EPS) * g_ref[...]).astype(BF16)

        def proj(c):
            return jnp.dot(h, w_ref[:, c * W:(c + 1) * W], preferred_element_type=F32)

        def head_norm(p, gain):
            hi, lo = _split_hi_lo(p * p)
            ss = jnp.dot(jnp.concatenate([hi, lo], axis=1), bd_ref[...], preferred_element_type=F32)
            return p * lax.rsqrt(ss * (1.0 / HEAD_DIM) + EPS) * gain

        def project(c):
            if c == 0:
                q_ref[...] = head_norm(proj(0), qg_ref[...]).astype(BF16)
            elif c == 1:
                k_ref[...] = head_norm(proj(1), kg_ref[...]).astype(BF16)
            elif c == 2:
                v_ref[...] = proj(2).astype(BF16)
            else:
                rest_w[:, (c - 3) * W:(c - 2) * W] = proj(c)

        def cols(r0, n, c):
            return rest_r[r0:r0 + n, c * W:(c + 1) * W]

        T = min(MIX_T, tm)

        def conformer(r0):
            _conformer_rows(cols(r0, T, 0), cols(r0, T, 1), cw_ref, cb_ref, clg_ref, clb_ref, gn_ref[1:2, :],
                            yb_ref, r0, hs_ref, sh_ref)

        def lru(r0):
            _lru_rows(cols(r0, T, 4), cols(r0, T, 5), lw_ref, lb_ref, wax_ref, ba_ref, bx_ref, lam_ref,
                      gn_ref[3:4, :], yd_ref, r0, xs_ref, hprev_ref)

        def gmlp(r0):
            _gmlp_rows(cols(r0, GMLP_CHUNK, 2), cols(r0, GMLP_CHUNK, 3), glg_ref, glb_ref, ws_ref, bs_ref,
                       gn_ref[2:3, :], yc_ref, r0)

        mix = []
        for r0 in range(0, tm, T):
            mix.append(functools.partial(conformer, r0))
            mix.append(functools.partial(lru, r0))
            mix.extend(functools.partial(gmlp, c0) for c0 in range(r0, r0 + T, GMLP_CHUNK))
        n_proj = w_ref.shape[1] // W
        for c in range(3, n_proj):
            project(c)
        per = -(-len(mix) // 4)
        for c in range(4):
            if c > 0:
                project(c - 1)
            for task in mix[c * per:(c + 1) * per]:
                task()

    step(rest_ref, rest_ref)


def _in_mix(x2, seq_len, g, w_bf, qg, kg, bd2, conf, gmlp, lru, gn4):
    n, d = x2.shape
    cols = w_bf.shape[1]
    W = GROUP_WIDTH
    rest_cols = cols - 3 * W
    tm = min(IN_TM, seq_len)
    assert seq_len % tm == 0 and tm % min(MIX_T, tm) == 0 and tm % GMLP_CHUNK == 0
    t = min(MIX_T, tm)
    row = lambda i: (i, 0)
    whole = lambda a: pl.BlockSpec(a.shape, lambda i: (0,) * a.ndim)
    params = (g, w_bf, qg, kg, bd2) + tuple(conf) + tuple(gmlp) + tuple(lru) + (gn4,)
    out_shape = jax.ShapeDtypeStruct((n, W), BF16)
    return pl.pallas_call(
        functools.partial(_in_mix_kernel, seq_len // tm),
        grid=(n // tm,),
        in_specs=[pl.BlockSpec((tm, d), row)] + [whole(a) for a in params],
        out_specs=[pl.BlockSpec((tm, W), row)] * 6,
        out_shape=[out_shape] * 6,
        scratch_shapes=[pltpu.VMEM((tm, rest_cols), F32),
                        pltpu.VMEM((t + CONF_HALO, W), F32),
                        pltpu.VMEM((7, t + CONF_HALO - 8, W), F32),
                        pltpu.VMEM((t + LRU_HALO, W), F32),
                        pltpu.VMEM((8, W), F32)],
        compiler_params=_params("arbitrary"),
        name="in_mix",
    )(x2, *params)


def _out_proj_kernel(ya_ref, yb_ref, yc_ref, yd_ref, w_ref, x_ref, g_ref, rh_ref, rl_ref, tri_ref,
                     xo_ref, h_ref, route_ref, idx_ref, cnt_ref, base_ref):
    W = GROUP_WIDTH

    @pl.when(pl.program_id(0) == 0)
    def _():
        base_ref[...] = jnp.zeros(base_ref.shape, F32)

    acc = x_ref[...]
    for gi, y_ref in enumerate((ya_ref, yb_ref, yc_ref, yd_ref)):
        acc = acc + jnp.dot(y_ref[...], w_ref[gi * W:(gi + 1) * W, :], preferred_element_type=F32)
    xo_ref[...] = acc
    ms = jnp.mean(acc * acc, axis=-1, keepdims=True)
    h = acc * lax.rsqrt(ms + EPS) * g_ref[...]
    h_ref[...] = _pack_bf16_pairs(h)
    _route_rows(h, rh_ref, rl_ref, tri_ref, route_ref, idx_ref, cnt_ref, base_ref)


def _out_proj_route(ys, w_bf, x2, g, rh, rl, tri):
    n, d = x2.shape
    W = GROUP_WIDTH
    tm = tri.shape[0]
    row = lambda i: (i, 0)
    const = lambda i: (0, 0)
    ysp = pl.BlockSpec((tm, W), row)
    return pl.pallas_call(
        _out_proj_kernel,
        grid=(n // tm,),
        in_specs=[ysp, ysp, ysp, ysp,
                  pl.BlockSpec(w_bf.shape, const),
                  pl.BlockSpec((tm, d), row),
                  pl.BlockSpec((1, d), const),
                  pl.BlockSpec((d, LANES), const),
                  pl.BlockSpec((d, LANES), const),
                  pl.BlockSpec((tm, tm), const)],
        out_specs=[pl.BlockSpec((tm, d), row),
                   pl.BlockSpec((tm, d // 2), row),
                   pl.BlockSpec((tm, LANES), row),
                   pl.BlockSpec((ROUTE_IDX_COLS, tm), lambda i: (0, i)),
                   pl.BlockSpec((8, LANES), const)],
        out_shape=[jax.ShapeDtypeStruct((n, d), F32),
                   jax.ShapeDtypeStruct((n, d // 2), jnp.int32),
                   jax.ShapeDtypeStruct((n, LANES), F32),
                   jax.ShapeDtypeStruct((ROUTE_IDX_COLS, n), jnp.int32),
                   jax.ShapeDtypeStruct((8, LANES), F32)],
        scratch_shapes=[pltpu.VMEM((8, LANES), F32)],
        compiler_params=_params("arbitrary"),
        name="out_proj_route",
    )(*ys, w_bf, x2, g, rh, rl, tri)


def _out_ffn_kernel(ya_ref, yb_ref, yc_ref, yd_ref, wo_ref, x_ref, g_ref, wg_ref, wu_ref, wd_ref, o_ref,
                    xn_ref, act_ref):
    W = GROUP_WIDTH
    f = wg_ref.shape[1]
    acc = x_ref[...]
    for gi, y_ref in enumerate((ya_ref, yb_ref, yc_ref, yd_ref)):
        acc = acc + jnp.dot(y_ref[...], wo_ref[gi * W:(gi + 1) * W, :], preferred_element_type=F32)
    xn_ref[...] = acc
    ms = jnp.mean(acc * acc, axis=-1, keepdims=True)
    h = (acc * lax.rsqrt(ms + EPS) * g_ref[...]).astype(BF16)
    for c0 in range(0, f, FFN_TF):
        g = jnp.dot(h, wg_ref[:, c0:c0 + FFN_TF], preferred_element_type=F32)
        u = jnp.dot(h, wu_ref[:, c0:c0 + FFN_TF], preferred_element_type=F32)
        act_ref[:, c0:c0 + FFN_TF] = (g * jax.nn.sigmoid(g) * u).astype(BF16)
    o_ref[...] = xn_ref[...] + jnp.dot(act_ref[...], wd_ref[...], preferred_element_type=F32)


def _out_proj_dense_ffn(ys, wo, x2, g, wg, wu, wd):
    n, d = x2.shape
    f = wg.shape[1]
    W = GROUP_WIDTH
    assert f % FFN_TF == 0
    tm = min(FFN_TM, n)
    row = lambda i: (i, 0)
    resident = dict(pipeline_mode=pl.Buffered(1))
    ysp = pl.BlockSpec((tm, W), row)
    return pl.pallas_call(
        _out_ffn_kernel,
        grid=(n // tm,),
        in_specs=[
            ysp, ysp, ysp, ysp,
            pl.BlockSpec(wo.shape, lambda i: (0, 0), **resident),
            pl.BlockSpec((tm, d), row),
            pl.BlockSpec((1, d), lambda i: (0, 0)),
            pl.BlockSpec((d, f), lambda i: (0, 0), **resident),
            pl.BlockSpec((d, f), lambda i: (0, 0), **resident),
            pl.BlockSpec((f, d), lambda i: (0, 0), **resident),
        ],
        out_specs=pl.BlockSpec((tm, d), row),
        out_shape=jax.ShapeDtypeStruct((n, d), F32),
        scratch_shapes=[pltpu.VMEM((tm, d), F32), pltpu.VMEM((tm, f), BF16)],
        compiler_params=_params("arbitrary"),
        name="out_proj_dense_ffn",
    )(*ys, wo, x2, g, wg, wu, wd)


def _route_rows(h, rh_ref, rl_ref, tri_ref, route_ref, idx_ref, cnt_ref, base_ref):
    tm = h.shape[0]
    hh, hl = _split_hi_lo(h)
    rh = rh_ref[...]
    logits = (jnp.dot(hh, rh, preferred_element_type=F32)
              + jnp.dot(hl, rh, preferred_element_type=F32)
              + jnp.dot(hh, rl_ref[...], preferred_element_type=F32))
    lane = lax.broadcasted_iota(jnp.int32, (tm, LANES), 1).astype(F32)
    neg = jnp.float32(-jnp.inf)
    logits = jnp.where(lane < N_EXPERTS, logits, neg)
    m1 = jnp.max(logits, axis=1, keepdims=True)
    i1 = jnp.min(jnp.where(logits == m1, lane, float(LANES)), axis=1, keepdims=True)
    l2 = jnp.where(lane == i1, neg, logits)
    m2 = jnp.max(l2, axis=1, keepdims=True)
    i2 = jnp.min(jnp.where(l2 == m2, lane, float(LANES)), axis=1, keepdims=True)
    e = jnp.exp(m2 - m1)
    g1 = 1.0 / (1.0 + e)
    g2 = e / (1.0 + e)
    oh1 = jnp.where(lane == i1, 1.0, 0.0)
    oh2 = jnp.where(lane == i2, 1.0, 0.0)
    oh = oh1 + oh2
    before = jnp.dot(tri_ref[...], oh.astype(BF16), preferred_element_type=F32) + base_ref[0:1, :]
    r1 = jnp.sum(oh1 * before, axis=1, keepdims=True)
    r2 = jnp.sum(oh2 * before, axis=1, keepdims=True)
    base = base_ref[0:1, :] + jnp.sum(oh, axis=0, keepdims=True)
    base_ref[...] = jnp.broadcast_to(base, base_ref.shape)
    cnt_ref[...] = jnp.broadcast_to(base, cnt_ref.shape)
    out = jnp.where(lane == 0, i1, 0.0)
    out = jnp.where(lane == 1, i2, out)
    out = jnp.where(lane == 2, g1, out)
    out = jnp.where(lane == 3, g2, out)
    out = jnp.where(lane == 4, r1, out)
    out = jnp.where(lane == 5, r2, out)
    route_ref[...] = out
    hi = jnp.floor(out * (1.0 / 256.0))
    lo = out - 256.0 * hi
    sel = (lax.broadcasted_iota(jnp.int32, (2 * ROUTE_IDX_COLS, LANES), 0)
           == lax.broadcasted_iota(jnp.int32, (2 * ROUTE_IDX_COLS, LANES), 1)).astype(BF16)
    nt = (((1,), (1,)), ((), ()))
    hi_t = lax.dot_general(sel, hi.astype(BF16), nt, preferred_element_type=F32)
    lo_t = lax.dot_general(sel, lo.astype(BF16), nt, preferred_element_type=F32)
    idx_ref[...] = (256.0 * hi_t + lo_t)[:ROUTE_IDX_COLS].astype(jnp.int32)


def _sc_window_rows(table):
    row_bytes = table.shape[1] * table.dtype.itemsize
    return min(SC_MAX_INDEX_VECTOR, SC_WINDOW_BYTES // row_bytes)


def _sc_gather_rows(table, idx):
    _, d = table.shape
    b = idx.shape[0]
    win = _sc_window_rows(table)
    sc = plsc.get_sparse_core_info()
    n_workers = sc.num_cores * sc.num_subcores
    per_w = b // n_workers
    n_win = per_w // win
    assert per_w * n_workers == b and n_win * win == per_w and n_win % 2 == 0, (b, n_workers, win)
    mesh = plsc.VectorSubcoreMesh(core_axis_name="c", subcore_axis_name="s")
    dma = pltpu.SemaphoreType.DMA

    @functools.partial(
        pl.kernel, mesh=mesh, out_type=jax.ShapeDtypeStruct((b, d), table.dtype), name="sc_gather_rows",
        scratch_types=[pltpu.VMEM((win,), jnp.int32), pltpu.VMEM((win,), jnp.int32),
                       pltpu.VMEM((win, d), table.dtype), pltpu.VMEM((win, d), table.dtype),
                       dma, dma, dma, dma])
    def gather_kernel(table_hbm, idx_hbm, out_hbm, i0, i1, r0, r1, g0, g1, w0, w1):
        wid = lax.axis_index("s") * sc.num_cores + lax.axis_index("c")
        base = wid * per_w
        idxb, rows, gsem, wsem = (i0, i1), (r0, r1), (g0, g1), (w0, w1)

        def off(c):
            return pl.multiple_of(base + c * win, win)

        def gather(s):
            return pltpu.make_async_copy(table_hbm.at[idxb[s]], rows[s], gsem[s])

        def write(c, s):
            return pltpu.make_async_copy(rows[s], out_hbm.at[pl.ds(off(c), win)], wsem[s])

        pltpu.sync_copy(idx_hbm.at[pl.ds(off(0), win)], idxb[0])
        gather(0).start()

        @pl.loop(0, n_win, step=2)
        def _(c):
            for s in (0, 1):
                cc = c + s

                @pl.when(cc + 1 < n_win)
                def _():
                    @pl.when(cc >= 1)
                    def _():
                        write(cc - 1, 1 - s).wait()

                    pltpu.sync_copy(idx_hbm.at[pl.ds(off(cc + 1), win)], idxb[1 - s])
                    gather(1 - s).start()

                gather(s).wait()
                write(cc, s).start()

        write(n_win - 2, 0).wait()
        write(n_win - 1, 1).wait()

    return gather_kernel(table, idx)


def _sc_dispatch_rows(src, dest1, dest2, pad_idx, n_rows):
    n, d = src.shape
    win = _sc_window_rows(src)
    sc = plsc.get_sparse_core_info()
    n_workers = sc.num_cores * sc.num_subcores
    per_w = n // n_workers
    n_win = per_w // win
    assert per_w * n_workers == n and n_win * win == per_w and n_win % 2 == 0, (n, n_workers, win)
    n_pad = pad_idx.shape[0]
    assert n_pad + 2 * n == n_rows and n_pad % (n_workers * win) == 0, (n_pad, n, n_rows)
    pad_per_w = n_pad // n_workers
    zero_rows = jnp.zeros((win, d), src.dtype)
    mesh = plsc.VectorSubcoreMesh(core_axis_name="c", subcore_axis_name="s")
    dma = pltpu.SemaphoreType.DMA
    ivec = pltpu.VMEM((win,), jnp.int32)
    rbuf = pltpu.VMEM((win, d), src.dtype)

    @functools.partial(
        pl.kernel, mesh=mesh, out_type=jax.ShapeDtypeStruct((n_rows, d), src.dtype), name="sc_dispatch_rows",
        scratch_types=[ivec, ivec, ivec, ivec, rbuf, rbuf, dma, dma, dma, dma, dma, dma])
    def dispatch_kernel(src_hbm, d1_hbm, d2_hbm, pad_hbm, zero_hbm, out_hbm,
                        a0, a1, b0, b1, r0, r1, l0, l1, p0, p1, q0, q1):
        wid = lax.axis_index("s") * sc.num_cores + lax.axis_index("c")
        base = wid * per_w
        ia, ib, rows, lsem, psem, qsem = (a0, a1), (b0, b1), (r0, r1), (l0, l1), (p0, p1), (q0, q1)

        def off(c):
            return pl.multiple_of(base + c * win, win)

        def load(c, s):
            return pltpu.make_async_copy(src_hbm.at[pl.ds(off(c), win)], rows[s], lsem[s])

        def scatters(s):
            return (pltpu.make_async_copy(rows[s], out_hbm.at[ia[s]], psem[s]),
                    pltpu.make_async_copy(rows[s], out_hbm.at[ib[s]], qsem[s]))

        def fetch(c, s):
            pltpu.sync_copy(d1_hbm.at[pl.ds(off(c), win)], ia[s])
            pltpu.sync_copy(d2_hbm.at[pl.ds(off(c), win)], ib[s])
            load(c, s).start()

        fetch(0, 0)

        @pl.loop(0, n_win, step=2)
        def _(c):
            for s in (0, 1):
                cc = c + s

                @pl.when(cc + 1 < n_win)
                def _():
                    @pl.when(cc >= 1)
                    def _():
                        for cp in scatters(1 - s):
                            cp.wait()

                    fetch(cc + 1, 1 - s)

                load(cc, s).wait()
                for cp in scatters(s):
                    cp.start()

        for s in (0, 1):
            for cp in scatters(s):
                cp.wait()

        pltpu.sync_copy(zero_hbm, rows[0])

        @pl.loop(0, pad_per_w // win)
        def _(c):
            o = pl.multiple_of(wid * pad_per_w + c * win, win)
            pltpu.sync_copy(pad_hbm.at[pl.ds(o, win)], ia[0])
            pltpu.sync_copy(rows[0], out_hbm.at[ia[0]])

    return dispatch_kernel(src, dest1, dest2, pad_idx, zero_rows)


def _moe_kernel(bexp_ref, nused_ref, x_ref, wg_ref, wu_ref, wd_ref, o_ref, xb, acc):
    i = pl.program_id(0)
    j = pl.program_id(1)
    nj = pl.num_programs(1)
    used = i < nused_ref[0]

    @pl.when(used & (j == 0))
    def _():
        xb[...] = _unpack_bf16_pairs(x_ref[...]).astype(BF16)
        acc[...] = jnp.zeros(acc.shape, F32)

    @pl.when(used)
    def _():
        x = xb[...]
        g = jnp.dot(x, wg_ref[0].astype(BF16), preferred_element_type=F32)
        u = jnp.dot(x, wu_ref[0].astype(BF16), preferred_element_type=F32)
        act = (g * jax.nn.sigmoid(g) * u).astype(BF16)
        acc[...] += jnp.dot(act, wd_ref[0].astype(BF16), preferred_element_type=F32)

        @pl.when(j == nj - 1)
        def _():
            o_ref[...] = _pack_bf16_pairs(acc[...])

    @pl.when(jnp.logical_not(used) & (j == 0))
    def _():
        o_ref[...] = jnp.zeros(o_ref.shape, o_ref.dtype)


def _moe_experts(block_exp, n_used, xs, wg, wu, wd, tm):
    n_rows, dp = xs.shape
    d = 2 * dp
    n_blocks = n_rows // tm
    f = wg.shape[2]
    tf = MOE_TF if f % MOE_TF == 0 else f
    nj = f // tf
    assert nj >= 2

    def live(i, j, be, nu):
        u = i < nu[0]
        return jnp.where(u, i, nu[0] - 1), jnp.where(u, j, nj - 1)

    def w_in_map(i, j, be, nu):
        ii, jj = live(i, j, be, nu)
        return (be[ii], 0, jj)

    def w_down_map(i, j, be, nu):
        ii, jj = live(i, j, be, nu)
        return (be[ii], jj, 0)

    def x_map(i, j, be, nu):
        return (live(i, j, be, nu)[0], 0)

    def out_map(i, j, be, nu):
        return (i, 0)

    grid_spec = pltpu.PrefetchScalarGridSpec(
        num_scalar_prefetch=2,
        grid=(n_blocks, nj),
        in_specs=[
            pl.BlockSpec((tm, dp), x_map),
            pl.BlockSpec((1, d, tf), w_in_map),
            pl.BlockSpec((1, d, tf), w_in_map),
            pl.BlockSpec((1, tf, d), w_down_map),
        ],
        out_specs=pl.BlockSpec((tm, dp), out_map),
        scratch_shapes=[pltpu.VMEM((tm, d), BF16), pltpu.VMEM((tm, d), F32)],
    )
    return pl.pallas_call(
        _moe_kernel,
        grid_spec=grid_spec,
        out_shape=jax.ShapeDtypeStruct((n_rows, dp), jnp.int32),
        compiler_params=_params("arbitrary", "arbitrary"),
        name="moe_experts",
    )(block_exp, n_used, xs, wg, wu, wd)


def _combine_kernel(x_ref, route_ref, a_ref, b_ref, o_ref):
    route = route_ref[...]
    ya = _unpack_bf16_pairs(a_ref[...])
    yb = _unpack_bf16_pairs(b_ref[...])
    o_ref[...] = x_ref[...] + route[:, 2:3] * ya + route[:, 3:4] * yb


def _combine(yg, x2, route, part, n_parts):
    n, d = x2.shape
    dp = yg.shape[1]
    tm = min(COMB_TM, n // n_parts)
    nb = n // n_parts // tm
    first = part * nb
    return pl.pallas_call(
        _combine_kernel,
        grid=(nb,),
        in_specs=[pl.BlockSpec((tm, d), lambda i: (i + first, 0)),
                  pl.BlockSpec((tm, LANES), lambda i: (i + first, 0)),
                  pl.BlockSpec((tm, dp), lambda i: (i, 0)),
                  pl.BlockSpec((tm, dp), lambda i: (i + nb, 0))],
        out_specs=pl.BlockSpec((tm, d), lambda i: (i + first, 0)),
        out_shape=jax.ShapeDtypeStruct((n, d), F32),
        input_output_aliases={0: 0},
        compiler_params=_params("arbitrary"),
        name="moe_combine",
    )(x2, route, yg, yg)


def _top2_moe(ys_mix, w_out_bf, x2, g2, router, wg, wu, wd):
    n, d = x2.shape
    a = 2 * n
    tm = min(MOE_TM, n)
    r_pad = jnp.zeros((d, LANES), F32).at[:, :N_EXPERTS].set(router)
    rh, rl = _split_hi_lo(r_pad)
    rt = min(ROUTE_TM, n)
    tri = (lax.broadcasted_iota(jnp.int32, (rt, rt), 1) < lax.broadcasted_iota(jnp.int32, (rt, rt), 0)).astype(BF16)
    x2, hn, route, cols, cnt = _out_proj_route(ys_mix, w_out_bf, x2, g2, rh, rl, tri)

    counts = cnt[0, :N_EXPERTS].astype(jnp.int32)
    padded = (counts + tm - 1) // tm * tm
    pad_ends = jnp.cumsum(padded)
    pad_starts = pad_ends - padded
    def lookup(table, idx):
        hit = idx[None, :] == jnp.arange(N_EXPERTS, dtype=jnp.int32)[:, None]
        return jnp.sum(jnp.where(hit, table[:, None], 0), axis=0)

    dest1 = lookup(pad_starts, cols[0]) + cols[4]
    dest2 = lookup(pad_starts, cols[1]) + cols[5]
    n_blocks = a // tm + N_EXPERTS
    n_rows = n_blocks * tm
    block_start = jnp.arange(n_blocks, dtype=jnp.int32) * tm
    block_exp = jnp.minimum(jnp.sum(block_start[None, :] >= pad_ends[:, None], axis=0), N_EXPERTS - 1).astype(jnp.int32)
    n_used = (pad_ends[-1] // tm).astype(jnp.int32).reshape(1)
    pad_cnt = padded - counts
    pad_cum = jnp.cumsum(pad_cnt)
    k = jnp.arange(n_rows - a, dtype=jnp.int32)
    seg = jnp.sum(k[None, :] >= pad_cum[:, None], axis=0).astype(jnp.int32)
    in_expert = lookup(pad_starts + counts - (pad_cum - pad_cnt), jnp.minimum(seg, N_EXPERTS - 1)) + k
    pad_idx = jnp.where(seg < N_EXPERTS, in_expert, pad_ends[-1] + k - pad_cum[-1])

    xs = _sc_dispatch_rows(hn, dest1, dest2, pad_idx, n_rows)
    ys = _moe_experts(block_exp, n_used, xs, wg, wu, wd, tm)
    n_parts = COMBINE_PARTS if n % (COMBINE_PARTS * COMB_TM) == 0 else 1
    step = n // n_parts
    gathered = [_sc_gather_rows(ys, jnp.concatenate([dest1[p * step:(p + 1) * step], dest2[p * step:(p + 1) * step]]))
                for p in range(n_parts)]
    for p in range(n_parts):
        x2 = _combine(gathered[p], x2, route, p, n_parts)
    return x2


def _block_diag(w):
    h, dh, _ = w.shape
    eye = jnp.eye(h, dtype=w.dtype)
    return jnp.einsum("hij,hg->higj", w, eye).reshape(h * dh, h * dh)


def kernel(x, norm1_g, w_in, q_norm_g, k_norm_g, conf_dw_w, conf_dw_b, conf_ln_g, conf_ln_b, gmlp_ln_g, gmlp_ln_b, gmlp_ws, gmlp_bs, lru_conv_w, lru_conv_b, lru_wa, lru_ba, lru_wx, lru_bx, lru_lambda, group_norm_g, w_out, norm2_g, ffn_w_gate, ffn_w_up, ffn_w_down, moe_router, moe_w_gate, moe_w_up, moe_w_down):
    bsz, s, d = x.shape
    n = bsz * s
    depth = w_in.shape[0]
    W = GROUP_WIDTH
    row = lambda v: v.reshape(1, -1).astype(F32)

    head_id = jnp.arange(W) // HEAD_DIM
    bd = (head_id[:, None] == head_id[None, :]).astype(BF16)
    bd2 = jnp.concatenate([bd, bd], axis=0)
    t_att = min(ATT_T, s)
    uu = (jnp.arange(t_att)[:, None] >= jnp.arange(t_att)[None, :]).astype(BF16)
    tril = jnp.tril(jnp.ones((GMLP_CHUNK, GMLP_CHUNK), dtype=bool))

    x2 = x.reshape(n, d)
    for l in range(depth):
        qg = row(jnp.tile(q_norm_g[l], GROUP_HEADS) * (HEAD_DIM ** -0.5))
        kg = row(jnp.tile(k_norm_g[l], GROUP_HEADS))
        gn4 = group_norm_g[l].reshape(N_GROUPS, W).astype(F32)
        cw = jnp.zeros((CONF_HALO, W), F32).at[:CONF_KERNEL].set(conf_dw_w[l])
        conf = (cw, row(conf_dw_b[l]), row(conf_ln_g[l]), row(conf_ln_b[l]))
        ws = jnp.concatenate(list(jnp.where(tril, gmlp_ws[l], 0.0).astype(BF16)), axis=1)
        bs_mat = jnp.repeat(gmlp_bs[l].T, HEAD_DIM, axis=1)
        gmlp = (row(gmlp_ln_g[l]), row(gmlp_ln_b[l]), ws, bs_mat)
        lw = jnp.zeros((8, W), F32).at[:LRU_CONV].set(lru_conv_w[l])
        wax = jnp.concatenate([_block_diag(lru_wa[l]), _block_diag(lru_wx[l])], axis=1).astype(BF16)
        lru = (lw, row(lru_conv_b[l]), wax, row(lru_ba[l]), row(lru_bx[l]), row(lru_lambda[l]))
        q, k, v, y_b, y_c, y_d = _in_mix(x2, s, row(norm1_g[l]), w_in[l].astype(BF16), qg, kg, bd2,
                                         conf, gmlp, lru, gn4)
        to3 = lambda t: t.reshape(bsz, s, t.shape[-1])
        zmax = (ATT_ZMAX_SLACK * HEAD_DIM ** 0.5 * jnp.max(jnp.abs(q_norm_g[l] * k_norm_g[l]))).reshape(1)
        y_a = _sb_attention(zmax.astype(F32), to3(q), to3(k), to3(v), uu, gn4[0:1]).reshape(n, W)
        ys = [y_a, y_b, y_c, y_d]
        j = l // 2
        if l % 2 == 0:
            x2 = _out_proj_dense_ffn(ys, w_out[l].astype(BF16), x2, row(norm2_g[l]), ffn_w_gate[j].astype(BF16),
                                     ffn_w_up[j].astype(BF16), ffn_w_down[j].astype(BF16))
        else:
            x2 = _top2_moe(ys, w_out[l].astype(BF16), x2, row(norm2_g[l]), moe_router[j], moe_w_gate[j],
                           moe_w_up[j], moe_w_down[j])
    return x2.reshape(bsz, s, d)
```

```python
import functools

import jax
import jax.numpy as jnp
from jax import lax
from jax.experimental import pallas as pl
from jax.experimental.pallas import tpu as pltpu
from jax.experimental.pallas import tpu_sc as plsc

F32 = jnp.float32
BF16 = jnp.bfloat16

HEAD_DIM = 64
GROUP_HEADS = 4
GROUP_WIDTH = GROUP_HEADS * HEAD_DIM
N_GROUPS = 4
CONF_KERNEL = 31
GMLP_CHUNK = 128
LRU_CONV = 4
LRU_C = 8.0
N_EXPERTS = 8
EPS = 1e-6

LANES = 128
VMEM_LIMIT = 56 * 1024 * 1024

IN_TM = 1024
MIX_T = 256
ATT_T = 256
ATT_UNDERFLOW = 110.0
ATT_ZMAX_SLACK = 1.05
CONF_R = 64
CONF_HALO = 32
LRU_HALO = 8
FFN_TM = 1024
FFN_TF = 256
ROUTE_TM = 1024
ROUTE_IDX_COLS = 8
MOE_TM = 1024
MOE_TF = 512
COMB_TM = 1024
COMBINE_PARTS = 2
SC_WINDOW_BYTES = 128 * 1024
SC_MAX_INDEX_VECTOR = 128


def _params(*sem):
    return pltpu.CompilerParams(dimension_semantics=sem, vmem_limit_bytes=VMEM_LIMIT)


def _split_hi_lo(x):
    hi = x.astype(BF16)
    lo = (x - hi.astype(F32)).astype(BF16)
    return hi, lo


SOFTPLUS_CLAMP = 60.0


def _softplus(z):
    return jnp.maximum(jnp.log(1.0 + jnp.exp(jnp.minimum(z, SOFTPLUS_CLAMP))), z)


def _gelu_tanh(x):
    c = 0.7978845608028654
    return 0.5 * x * (1.0 + jnp.tanh(c * (x + 0.044715 * (x * x * x))))


def _pack_bf16_pairs(y):
    c = y.shape[1] // 2
    bits = lax.bitcast_convert_type(y.astype(BF16).astype(F32), jnp.uint32)
    word = (bits[:, :c] >> 16) | bits[:, c:]
    return lax.bitcast_convert_type(word, jnp.int32)


def _unpack_bf16_pairs(w):
    bits = lax.bitcast_convert_type(w, jnp.uint32)
    lo = lax.bitcast_convert_type(bits << 16, F32)
    hi = lax.bitcast_convert_type(bits & jnp.uint32(0xFFFF0000), F32)
    return jnp.concatenate([lo, hi], axis=1)


def _group_rms(y, g):
    ms = jnp.mean(y * y, axis=-1, keepdims=True)
    return y * lax.rsqrt(ms + EPS) * g


def _sb_attn_kernel(zmax_ref, q_ref, k_ref, v_ref, uu_ref, gn_ref, o_ref):
    T = q_ref.shape[1]
    n_pairs = GROUP_WIDTH // LANES
    i = pl.program_id(1)
    lane = lax.broadcasted_iota(jnp.int32, (T, LANES), 1)
    rows = lax.broadcasted_iota(jnp.int32, (2 * T, T), 0)
    cols = lax.broadcasted_iota(jnp.int32, (2 * T, T), 1)
    causal = cols < jnp.where(rows >= T, rows - T, rows)
    uu = uu_ref[...]

    def tiles(r0, carries, mask):
        ps = range(n_pairs)
        cs = [slice(p * LANES, (p + 1) * LANES) for p in ps]
        z = [lax.dot_general(qs[p], k_ref[0, pl.ds(r0, T), cs[p]], (((1,), (1,)), ((), ())),
                             preferred_element_type=F32) for p in ps]
        sp = [_softplus(z[p]) for p in ps]
        if mask is not None:
            sp = [jnp.where(mask, s, 0.0) for s in sp]
        cum = [jnp.dot(sp[p].astype(BF16), uu, preferred_element_type=F32) for p in ps]
        w = [jnp.exp(z[p] - cum[p] - carries[p]) for p in ps]
        if mask is not None:
            w = [jnp.where(mask, x, 0.0) for x in w]
        pv = [jnp.dot(w[p].astype(BF16), v_ref[0, pl.ds(r0, T), cs[p]], preferred_element_type=F32)
              for p in ps]
        return pv, [carries[p] + jnp.sum(sp[p], axis=1, keepdims=True) for p in ps]

    qs = []
    for pair in range(n_pairs):
        q2 = q_ref[0, :, pair * LANES:(pair + 1) * LANES]
        zero = jnp.zeros_like(q2)
        qs.append(jnp.concatenate([jnp.where(lane < HEAD_DIM, q2, zero),
                                   jnp.where(lane >= HEAD_DIM, q2, zero)], axis=0))
    row0 = pl.multiple_of(i * T, T)
    state = tuple(zip(*tiles(row0, [jnp.zeros((2 * T, 1), F32)] * n_pairs, causal)))

    dead_at = zmax_ref[0] + ATT_UNDERFLOW

    def alive(st):
        return functools.reduce(jnp.minimum, [jnp.min(carry) for _, carry in st]) <= dead_at

    def cond(c):
        step, live, _ = c
        return (step < i) & live

    def body(c):
        step, _, st = c
        r0 = pl.multiple_of((i - 1 - step) * T, T)
        pv, carries = tiles(r0, [carry for _, carry in st], None)
        new = tuple((st[p][0] + pv[p], carries[p]) for p in range(n_pairs))
        return step + 1, alive(new), new

    _, _, state = lax.while_loop(cond, body, (jnp.int32(0), alive(state), state))
    y = jnp.concatenate([jnp.where(lane < HEAD_DIM, acc[:T], acc[T:]) for acc, _ in state], axis=1)
    o_ref[0] = _group_rms(y, gn_ref[...]).astype(o_ref.dtype)


def _sb_attention(zmax, q3, k3, v3, uu, gn):
    b, s, w = q3.shape
    t = min(ATT_T, s)
    return pl.pallas_call(
        _sb_attn_kernel,
        grid=(b, s // t),
        in_specs=[
            pl.BlockSpec(memory_space=pltpu.SMEM),
            pl.BlockSpec((1, t, w), lambda bi, i: (bi, i, 0)),
            pl.BlockSpec((1, s, w), lambda bi, i: (bi, 0, 0)),
            pl.BlockSpec((1, s, w), lambda bi, i: (bi, 0, 0)),
            pl.BlockSpec((t, t), lambda bi, i: (0, 0)),
            pl.BlockSpec((1, w), lambda bi, i: (0, 0)),
        ],
        out_specs=pl.BlockSpec((1, t, w), lambda bi, i: (bi, i, 0)),
        out_shape=jax.ShapeDtypeStruct((b, s, w), BF16),
        compiler_params=_params("arbitrary", "arbitrary"),
        name="sb_attn",
    )(zmax, q3, k3, v3, uu, gn)


def _conformer_rows(val, gate, w_ref, b_ref, lng_ref, lnb_ref, gn, o_ref, row0, hs_ref, sh_ref):
    T = val.shape[0]
    H = CONF_HALO
    SUB = 8
    hs_ref[H:H + T, :] = val * jax.nn.sigmoid(gate)
    off = H - (CONF_KERNEL - 1)
    L = sh_ref.shape[1]
    for s in range(1, SUB):
        sh_ref[s - 1] = hs_ref[s:s + L, :]
    R = min(CONF_R, T)
    taps_w = [jnp.broadcast_to(w_ref[k:k + 1, :], (R, GROUP_WIDTH)) for k in range(CONF_KERNEL)]
    for r0 in range(0, T, R):
        acc = jnp.broadcast_to(b_ref[...], (R, GROUP_WIDTH))
        for k in range(CONF_KERNEL):
            s = (off + k) % SUB
            a = off + k - s + r0
            tap = hs_ref[a:a + R, :] if s == 0 else sh_ref[s - 1, a:a + R, :]
            acc = acc + taps_w[k] * tap
        mu = jnp.mean(acc, axis=-1, keepdims=True)
        xc = acc - mu
        var = jnp.mean(xc * xc, axis=-1, keepdims=True)
        y = xc * lax.rsqrt(var + EPS) * lng_ref[...] + lnb_ref[...]
        y = y * jax.nn.sigmoid(y)
        o_ref[row0 + r0:row0 + r0 + R, :] = _group_rms(y, gn).astype(o_ref.dtype)
    hs_ref[0:H, :] = hs_ref[T:T + H, :]


def _gmlp_rows(u_in, v_in, lng_ref, lnb_ref, ws_ref, bs_ref, gn, o_ref, row0):
    C = GMLP_CHUNK
    lane = lax.broadcasted_iota(jnp.int32, (C, GROUP_WIDTH), 1)
    u = _gelu_tanh(u_in)
    v = _gelu_tanh(v_in)
    mu = jnp.mean(v, axis=-1, keepdims=True)
    xc = v - mu
    var = jnp.mean(xc * xc, axis=-1, keepdims=True)
    vb = (xc * lax.rsqrt(var + EPS) * lng_ref[...] + lnb_ref[...]).astype(BF16)
    zero = jnp.zeros_like(vb)
    vstack = jnp.concatenate(
        [jnp.where((lane >= h * HEAD_DIM) & (lane < (h + 1) * HEAD_DIM), vb, zero) for h in range(GROUP_HEADS)],
        axis=0)
    mixed = bs_ref[...] + jnp.dot(ws_ref[...], vstack, preferred_element_type=F32)
    o_ref[row0:row0 + C, :] = _group_rms(u * mixed, gn).astype(o_ref.dtype)


def _lru_rows(x_in, gate, cw_ref, cb_ref, wax_ref, ba_ref, bx_ref, lam_ref, gn, o_ref, row0, xs_ref, hprev_ref):
    T = x_in.shape[0]
    W = GROUP_WIDTH
    H = LRU_HALO
    xs_ref[H:H + T, :] = x_in
    off = H - (LRU_CONV - 1)
    xb = jnp.broadcast_to(cb_ref[...], (T, W))
    for k in range(LRU_CONV):
        xb = xb + cw_ref[k:k + 1, :] * xs_ref[off + k:off + k + T, :]
    xs_ref[0:H, :] = xs_ref[T:T + H, :]

    pre = jnp.dot(xb.astype(BF16), wax_ref[...], preferred_element_type=F32)
    r = jax.nn.sigmoid(pre[:, :W] + ba_ref[...])
    ig = jax.nn.sigmoid(pre[:, W:] + bx_ref[...])
    nlam = -lam_ref[...]
    log_a = (-LRU_C) * r * (jnp.maximum(nlam, 0.0) + jnp.log1p(jnp.exp(-jnp.abs(nlam))))
    a = jnp.exp(log_a)
    th = jnp.tanh(log_a)
    one_minus_a2 = 2.0 * th / (th - 1.0)
    b = jnp.sqrt(one_minus_a2) * (ig * xb)

    SUB = 8
    a = a.reshape(T // SUB, SUB, W)
    b = b.reshape(T // SUB, SUB, W)
    sub = lax.broadcasted_iota(jnp.int32, a.shape, 1)
    d = 1
    while d < SUB:
        keep = sub >= d
        a_sh = jnp.where(keep, pltpu.roll(a, d, 1), 1.0)
        b_sh = jnp.where(keep, pltpu.roll(b, d, 1), 0.0)
        b = a * b_sh + b
        a = a * a_sh
        d *= 2
    carry = hprev_ref[0:1, :]
    tiles = []
    for g in range(T // SUB):
        hg = b[g] + a[g] * carry
        carry = hg[SUB - 1:SUB]
        tiles.append(hg)
    h = jnp.concatenate(tiles, axis=0)
    hprev_ref[...] = jnp.broadcast_to(carry, hprev_ref.shape)
    y = h * _gelu_tanh(gate)
    o_ref[row0:row0 + T, :] = _group_rms(y, gn).astype(o_ref.dtype)


def _in_mix_kernel(blocks_per_seq, x_ref, g_ref, w_ref, qg_ref, kg_ref, bd_ref,
                   cw_ref, cb_ref, clg_ref, clb_ref,
                   glg_ref, glb_ref, ws_ref, bs_ref,
                   lw_ref, lb_ref, wax_ref, ba_ref, bx_ref, lam_ref, gn_ref,
                   q_ref, k_ref, v_ref, yb_ref, yc_ref, yd_ref,
                   rest_ref, hs_ref, sh_ref, xs_ref, hprev_ref):
    tm = x_ref.shape[0]
    W = GROUP_WIDTH

    @pl.when(pl.program_id(0) % blocks_per_seq == 0)
    def _():
        hs_ref[0:CONF_HALO, :] = jnp.zeros((CONF_HALO, W), F32)
        xs_ref[0:LRU_HALO, :] = jnp.zeros((LRU_HALO, W), F32)
        hprev_ref[...] = jnp.zeros(hprev_ref.shape, F32)

    def step(rest_w, rest_r):
        xf = x_ref[...]
        ms = jnp.mean(xf * xf, axis=-1, keepdims=True)
        h = (xf * lax.rsqrt(ms + EPS) * g_ref[...]).astype(BF16)

        def proj(c):
            return jnp.dot(h, w_ref[:, c * W:(c + 1) * W], preferred_element_type=F32)

        def head_norm(p, gain):
            hi, lo = _split_hi_lo(p * p)
            ss = jnp.dot(jnp.concatenate([hi, lo], axis=1), bd_ref[...], preferred_element_type=F32)
            return p * lax.rsqrt(ss * (1.0 / HEAD_DIM) + EPS) * gain

        def project(c):
            if c == 0:
                q_ref[...] = head_norm(proj(0), qg_ref[...]).astype(BF16)
            elif c == 1:
                k_ref[...] = head_norm(proj(1), kg_ref[...]).astype(BF16)
            elif c == 2:
                v_ref[...] = proj(2).astype(BF16)
            else:
                rest_w[:, (c - 3) * W:(c - 2) * W] = proj(c)

        def cols(r0, n, c):
            return rest_r[r0:r0 + n, c * W:(c + 1) * W]

        T = min(MIX_T, tm)

        def conformer(r0):
            _conformer_rows(cols(r0, T, 0), cols(r0, T, 1), cw_ref, cb_ref, clg_ref, clb_ref, gn_ref[1:2, :],
                            yb_ref, r0, hs_ref, sh_ref)

        def lru(r0):
            _lru_rows(cols(r0, T, 4), cols(r0, T, 5), lw_ref, lb_ref, wax_ref, ba_ref, bx_ref, lam_ref,
                      gn_ref[3:4, :], yd_ref, r0, xs_ref, hprev_ref)

        def gmlp(r0):
            _gmlp_rows(cols(r0, GMLP_CHUNK, 2), cols(r0, GMLP_CHUNK, 3), glg_ref, glb_ref, ws_ref, bs_ref,
                       gn_ref[2:3, :], yc_ref, r0)

        mix = []
        for r0 in range(0, tm, T):
            mix.append(functools.partial(conformer, r0))
            mix.append(functools.partial(lru, r0))
            mix.extend(functools.partial(gmlp, c0) for c0 in range(r0, r0 + T, GMLP_CHUNK))
        n_proj = w_ref.shape[1] // W
        for c in range(3, n_proj):
            project(c)
        per = -(-len(mix) // 4)
        for c in range(4):
            if c > 0:
                project(c - 1)
            for task in mix[c * per:(c + 1) * per]:
                task()

    step(rest_ref, rest_ref)


def _in_mix(x2, seq_len, g, w_bf, qg, kg, bd2, conf, gmlp, lru, gn4):
    n, d = x2.shape
    cols = w_bf.shape[1]
    W = GROUP_WIDTH
    rest_cols = cols - 3 * W
    tm = min(IN_TM, seq_len)
    assert seq_len % tm == 0 and tm % min(MIX_T, tm) == 0 and tm % GMLP_CHUNK == 0
    t = min(MIX_T, tm)
    row = lambda i: (i, 0)
    whole = lambda a: pl.BlockSpec(a.shape, lambda i: (0,) * a.ndim)
    params = (g, w_bf, qg, kg, bd2) + tuple(conf) + tuple(gmlp) + tuple(lru) + (gn4,)
    out_shape = jax.ShapeDtypeStruct((n, W), BF16)
    return pl.pallas_call(
        functools.partial(_in_mix_kernel, seq_len // tm),
        grid=(n // tm,),
        in_specs=[pl.BlockSpec((tm, d), row)] + [whole(a) for a in params],
        out_specs=[pl.BlockSpec((tm, W), row)] * 6,
        out_shape=[out_shape] * 6,
        scratch_shapes=[pltpu.VMEM((tm, rest_cols), F32),
                        pltpu.VMEM((t + CONF_HALO, W), F32),
                        pltpu.VMEM((7, t + CONF_HALO - 8, W), F32),
                        pltpu.VMEM((t + LRU_HALO, W), F32),
                        pltpu.VMEM((8, W), F32)],
        compiler_params=_params("arbitrary"),
        name="in_mix",
    )(x2, *params)


def _out_proj_kernel(ya_ref, yb_ref, yc_ref, yd_ref, w_ref, x_ref, g_ref, rh_ref, rl_ref, tri_ref,
                     xo_ref, h_ref, route_ref, idx_ref, cnt_ref, base_ref):
    W = GROUP_WIDTH

    @pl.when(pl.program_id(0) == 0)
    def _():
        base_ref[...] = jnp.zeros(base_ref.shape, F32)

    acc = x_ref[...]
    for gi, y_ref in enumerate((ya_ref, yb_ref, yc_ref, yd_ref)):
        acc = acc + jnp.dot(y_ref[...], w_ref[gi * W:(gi + 1) * W, :], preferred_element_type=F32)
    xo_ref[...] = acc
    ms = jnp.mean(acc * acc, axis=-1, keepdims=True)
    h = acc * lax.rsqrt(ms + EPS) * g_ref[...]
    h_ref[...] = _pack_bf16_pairs(h)
    _route_rows(h, rh_ref, rl_ref, tri_ref, route_ref, idx_ref, cnt_ref, base_ref)


def _out_proj_route(ys, w_bf, x2, g, rh, rl, tri):
    n, d = x2.shape
    W = GROUP_WIDTH
    tm = tri.shape[0]
    row = lambda i: (i, 0)
    const = lambda i: (0, 0)
    ysp = pl.BlockSpec((tm, W), row)
    return pl.pallas_call(
        _out_proj_kernel,
        grid=(n // tm,),
        in_specs=[ysp, ysp, ysp, ysp,
                  pl.BlockSpec(w_bf.shape, const),
                  pl.BlockSpec((tm, d), row),
                  pl.BlockSpec((1, d), const),
                  pl.BlockSpec((d, LANES), const),
                  pl.BlockSpec((d, LANES), const),
                  pl.BlockSpec((tm, tm), const)],
        out_specs=[pl.BlockSpec((tm, d), row),
                   pl.BlockSpec((tm, d // 2), row),
                   pl.BlockSpec((tm, LANES), row),
                   pl.BlockSpec((ROUTE_IDX_COLS, tm), lambda i: (0, i)),
                   pl.BlockSpec((8, LANES), const)],
        out_shape=[jax.ShapeDtypeStruct((n, d), F32),
                   jax.ShapeDtypeStruct((n, d // 2), jnp.int32),
                   jax.ShapeDtypeStruct((n, LANES), F32),
                   jax.ShapeDtypeStruct((ROUTE_IDX_COLS, n), jnp.int32),
                   jax.ShapeDtypeStruct((8, LANES), F32)],
        scratch_shapes=[pltpu.VMEM((8, LANES), F32)],
        compiler_params=_params("arbitrary"),
        name="out_proj_route",
    )(*ys, w_bf, x2, g, rh, rl, tri)


def _out_ffn_kernel(ya_ref, yb_ref, yc_ref, yd_ref, wo_ref, x_ref, g_ref, wg_ref, wu_ref, wd_ref, o_ref,
                    xn_ref, act_ref):
    W = GROUP_WIDTH
    f = wg_ref.shape[1]
    acc = x_ref[...]
    for gi, y_ref in enumerate((ya_ref, yb_ref, yc_ref, yd_ref)):
        acc = acc + jnp.dot(y_ref[...], wo_ref[gi * W:(gi + 1) * W, :], preferred_element_type=F32)
    xn_ref[...] = acc
    ms = jnp.mean(acc * acc, axis=-1, keepdims=True)
    h = (acc * lax.rsqrt(ms + EPS) * g_ref[...]).astype(BF16)
    for c0 in range(0, f, FFN_TF):
        g = jnp.dot(h, wg_ref[:, c0:c0 + FFN_TF], preferred_element_type=F32)
        u = jnp.dot(h, wu_ref[:, c0:c0 + FFN_TF], preferred_element_type=F32)
        act_ref[:, c0:c0 + FFN_TF] = (g * jax.nn.sigmoid(g) * u).astype(BF16)
    o_ref[...] = xn_ref[...] + jnp.dot(act_ref[...], wd_ref[...], preferred_element_type=F32)


def _out_proj_dense_ffn(ys, wo, x2, g, wg, wu, wd):
    n, d = x2.shape
    f = wg.shape[1]
    W = GROUP_WIDTH
    assert f % FFN_TF == 0
    tm = min(FFN_TM, n)
    row = lambda i: (i, 0)
    resident = dict(pipeline_mode=pl.Buffered(1))
    ysp = pl.BlockSpec((tm, W), row)
    return pl.pallas_call(
        _out_ffn_kernel,
        grid=(n // tm,),
        in_specs=[
            ysp, ysp, ysp, ysp,
            pl.BlockSpec(wo.shape, lambda i: (0, 0), **resident),
            pl.BlockSpec((tm, d), row),
            pl.BlockSpec((1, d), lambda i: (0, 0)),
            pl.BlockSpec((d, f), lambda i: (0, 0), **resident),
            pl.BlockSpec((d, f), lambda i: (0, 0), **resident),
            pl.BlockSpec((f, d), lambda i: (0, 0), **resident),
        ],
        out_specs=pl.BlockSpec((tm, d), row),
        out_shape=jax.ShapeDtypeStruct((n, d), F32),
        scratch_shapes=[pltpu.VMEM((tm, d), F32), pltpu.VMEM((tm, f), BF16)],
        compiler_params=_params("arbitrary"),
        name="out_proj_dense_ffn",
    )(*ys, wo, x2, g, wg, wu, wd)


def _route_rows(h, rh_ref, rl_ref, tri_ref, route_ref, idx_ref, cnt_ref, base_ref):
    tm = h.shape[0]
    hh, hl = _split_hi_lo(h)
    rh = rh_ref[...]
    logits = (jnp.dot(hh, rh, preferred_element_type=F32)
              + jnp.dot(hl, rh, preferred_element_type=F32)
              + jnp.dot(hh, rl_ref[...], preferred_element_type=F32))
    lane = lax.broadcasted_iota(jnp.int32, (tm, LANES), 1).astype(F32)
    neg = jnp.float32(-jnp.inf)
    logits = jnp.where(lane < N_EXPERTS, logits, neg)
    m1 = jnp.max(logits, axis=1, keepdims=True)
    i1 = jnp.min(jnp.where(logits == m1, lane, float(LANES)), axis=1, keepdims=True)
    l2 = jnp.where(lane == i1, neg, logits)
    m2 = jnp.max(l2, axis=1, keepdims=True)
    i2 = jnp.min(jnp.where(l2 == m2, lane, float(LANES)), axis=1, keepdims=True)
    e = jnp.exp(m2 - m1)
    g1 = 1.0 / (1.0 + e)
    g2 = e / (1.0 + e)
    oh1 = jnp.where(lane == i1, 1.0, 0.0)
    oh2 = jnp.where(lane == i2, 1.0, 0.0)
    oh = oh1 + oh2
    before = jnp.dot(tri_ref[...], oh.astype(BF16), preferred_element_type=F32) + base_ref[0:1, :]
    r1 = jnp.sum(oh1 * before, axis=1, keepdims=True)
    r2 = jnp.sum(oh2 * before, axis=1, keepdims=True)
    base = base_ref[0:1, :] + jnp.sum(oh, axis=0, keepdims=True)
    base_ref[...] = jnp.broadcast_to(base, base_ref.shape)
    cnt_ref[...] = jnp.broadcast_to(base, cnt_ref.shape)
    out = jnp.where(lane == 0, i1, 0.0)
    out = jnp.where(lane == 1, i2, out)
    out = jnp.where(lane == 2, g1, out)
    out = jnp.where(lane == 3, g2, out)
    out = jnp.where(lane == 4, r1, out)
    out = jnp.where(lane == 5, r2, out)
    route_ref[...] = out
    hi = jnp.floor(out * (1.0 / 256.0))
    lo = out - 256.0 * hi
    sel = (lax.broadcasted_iota(jnp.int32, (2 * ROUTE_IDX_COLS, LANES), 0)
           == lax.broadcasted_iota(jnp.int32, (2 * ROUTE_IDX_COLS, LANES), 1)).astype(BF16)
    nt = (((1,), (1,)), ((), ()))
    hi_t = lax.dot_general(sel, hi.astype(BF16), nt, preferred_element_type=F32)
    lo_t = lax.dot_general(sel, lo.astype(BF16), nt, preferred_element_type=F32)
    idx_ref[...] = (256.0 * hi_t + lo_t)[:ROUTE_IDX_COLS].astype(jnp.int32)


def _sc_window_rows(table):
    row_bytes = table.shape[1] * table.dtype.itemsize
    return min(SC_MAX_INDEX_VECTOR, SC_WINDOW_BYTES // row_bytes)


def _sc_gather_rows(table, idx):
    _, d = table.shape
    b = idx.shape[0]
    win = _sc_window_rows(table)
    sc = plsc.get_sparse_core_info()
    n_workers = sc.num_cores * sc.num_subcores
    per_w = b // n_workers
    n_win = per_w // win
    assert per_w * n_workers == b and n_win * win == per_w and n_win % 2 == 0, (b, n_workers, win)
    mesh = plsc.VectorSubcoreMesh(core_axis_name="c", subcore_axis_name="s")
    dma = pltpu.SemaphoreType.DMA

    @functools.partial(
        pl.kernel, mesh=mesh, out_type=jax.ShapeDtypeStruct((b, d), table.dtype), name="sc_gather_rows",
        scratch_types=[pltpu.VMEM((win,), jnp.int32), pltpu.VMEM((win,), jnp.int32),
                       pltpu.VMEM((win, d), table.dtype), pltpu.VMEM((win, d), table.dtype),
                       dma, dma, dma, dma])
    def gather_kernel(table_hbm, idx_hbm, out_hbm, i0, i1, r0, r1, g0, g1, w0, w1):
        wid = lax.axis_index("s") * sc.num_cores + lax.axis_index("c")
        base = wid * per_w
        idxb, rows, gsem, wsem = (i0, i1), (r0, r1), (g0, g1), (w0, w1)

        def off(c):
            return pl.multiple_of(base + c * win, win)

        def gather(s):
            return pltpu.make_async_copy(table_hbm.at[idxb[s]], rows[s], gsem[s])

        def write(c, s):
            return pltpu.make_async_copy(rows[s], out_hbm.at[pl.ds(off(c), win)], wsem[s])

        pltpu.sync_copy(idx_hbm.at[pl.ds(off(0), win)], idxb[0])
        gather(0).start()

        @pl.loop(0, n_win, step=2)
        def _(c):
            for s in (0, 1):
                cc = c + s

                @pl.when(cc + 1 < n_win)
                def _():
                    @pl.when(cc >= 1)
                    def _():
                        write(cc - 1, 1 - s).wait()

                    pltpu.sync_copy(idx_hbm.at[pl.ds(off(cc + 1), win)], idxb[1 - s])
                    gather(1 - s).start()

                gather(s).wait()
                write(cc, s).start()

        write(n_win - 2, 0).wait()
        write(n_win - 1, 1).wait()

    return gather_kernel(table, idx)


def _sc_dispatch_rows(src, dest1, dest2, pad_idx, n_rows):
    n, d = src.shape
    win = _sc_window_rows(src)
    sc = plsc.get_sparse_core_info()
    n_workers = sc.num_cores * sc.num_subcores
    per_w = n // n_workers
    n_win = per_w // win
    assert per_w * n_workers == n and n_win * win == per_w and n_win % 2 == 0, (n, n_workers, win)
    n_pad = pad_idx.shape[0]
    assert n_pad + 2 * n == n_rows and n_pad % (n_workers * win) == 0, (n_pad, n, n_rows)
    pad_per_w = n_pad // n_workers
    zero_rows = jnp.zeros((win, d), src.dtype)
    mesh = plsc.VectorSubcoreMesh(core_axis_name="c", subcore_axis_name="s")
    dma = pltpu.SemaphoreType.DMA
    ivec = pltpu.VMEM((win,), jnp.int32)
    rbuf = pltpu.VMEM((win, d), src.dtype)

    @functools.partial(
        pl.kernel, mesh=mesh, out_type=jax.ShapeDtypeStruct((n_rows, d), src.dtype), name="sc_dispatch_rows",
        scratch_types=[ivec, ivec, ivec, ivec, rbuf, rbuf, dma, dma, dma, dma, dma, dma])
    def dispatch_kernel(src_hbm, d1_hbm, d2_hbm, pad_hbm, zero_hbm, out_hbm,
                        a0, a1, b0, b1, r0, r1, l0, l1, p0, p1, q0, q1):
        wid = lax.axis_index("s") * sc.num_cores + lax.axis_index("c")
        base = wid * per_w
        ia, ib, rows, lsem, psem, qsem = (a0, a1), (b0, b1), (r0, r1), (l0, l1), (p0, p1), (q0, q1)

        def off(c):
            return pl.multiple_of(base + c * win, win)

        def load(c, s):
            return pltpu.make_async_copy(src_hbm.at[pl.ds(off(c), win)], rows[s], lsem[s])

        def scatters(s):
            return (pltpu.make_async_copy(rows[s], out_hbm.at[ia[s]], psem[s]),
                    pltpu.make_async_copy(rows[s], out_hbm.at[ib[s]], qsem[s]))

        def fetch(c, s):
            pltpu.sync_copy(d1_hbm.at[pl.ds(off(c), win)], ia[s])
            pltpu.sync_copy(d2_hbm.at[pl.ds(off(c), win)], ib[s])
            load(c, s).start()

        fetch(0, 0)

        @pl.loop(0, n_win, step=2)
        def _(c):
            for s in (0, 1):
                cc = c + s

                @pl.when(cc + 1 < n_win)
                def _():
                    @pl.when(cc >= 1)
                    def _():
                        for cp in scatters(1 - s):
                            cp.wait()

                    fetch(cc + 1, 1 - s)

                load(cc, s).wait()
                for cp in scatters(s):
                    cp.start()

        for s in (0, 1):
            for cp in scatters(s):
                cp.wait()

        pltpu.sync_copy(zero_hbm, rows[0])

        @pl.loop(0, pad_per_w // win)
        def _(c):
            o = pl.multiple_of(wid * pad_per_w + c * win, win)
            pltpu.sync_copy(pad_hbm.at[pl.ds(o, win)], ia[0])
            pltpu.sync_copy(rows[0], out_hbm.at[ia[0]])

    return dispatch_kernel(src, dest1, dest2, pad_idx, zero_rows)


def _moe_kernel(bexp_ref, nused_ref, x_ref, wg_ref, wu_ref, wd_ref, o_ref, xb, acc):
    i = pl.program_id(0)
    j = pl.program_id(1)
    nj = pl.num_programs(1)
    used = i < nused_ref[0]

    @pl.when(used & (j == 0))
    def _():
        xb[...] = _unpack_bf16_pairs(x_ref[...]).astype(BF16)
        acc[...] = jnp.zeros(acc.shape, F32)

    @pl.when(used)
    def _():
        x = xb[...]
        g = jnp.dot(x, wg_ref[0].astype(BF16), preferred_element_type=F32)
        u = jnp.dot(x, wu_ref[0].astype(BF16), preferred_element_type=F32)
        act = (g * jax.nn.sigmoid(g) * u).astype(BF16)
        acc[...] += jnp.dot(act, wd_ref[0].astype(BF16), preferred_element_type=F32)

        @pl.when(j == nj - 1)
        def _():
            o_ref[...] = _pack_bf16_pairs(acc[...])

    @pl.when(jnp.logical_not(used) & (j == 0))
    def _():
        o_ref[...] = jnp.zeros(o_ref.shape, o_ref.dtype)


def _moe_experts(block_exp, n_used, xs, wg, wu, wd, tm):
    n_rows, dp = xs.shape
    d = 2 * dp
    n_blocks = n_rows // tm
    f = wg.shape[2]
    tf = MOE_TF if f % MOE_TF == 0 else f
    nj = f // tf
    assert nj >= 2

    def live(i, j, be, nu):
        u = i < nu[0]
        return jnp.where(u, i, nu[0] - 1), jnp.where(u, j, nj - 1)

    def w_in_map(i, j, be, nu):
        ii, jj = live(i, j, be, nu)
        return (be[ii], 0, jj)

    def w_down_map(i, j, be, nu):
        ii, jj = live(i, j, be, nu)
        return (be[ii], jj, 0)

    def x_map(i, j, be, nu):
        return (live(i, j, be, nu)[0], 0)

    def out_map(i, j, be, nu):
        return (i, 0)

    grid_spec = pltpu.PrefetchScalarGridSpec(
        num_scalar_prefetch=2,
        grid=(n_blocks, nj),
        in_specs=[
            pl.BlockSpec((tm, dp), x_map),
            pl.BlockSpec((1, d, tf), w_in_map),
            pl.BlockSpec((1, d, tf), w_in_map),
            pl.BlockSpec((1, tf, d), w_down_map),
        ],
        out_specs=pl.BlockSpec((tm, dp), out_map),
        scratch_shapes=[pltpu.VMEM((tm, d), BF16), pltpu.VMEM((tm, d), F32)],
    )
    return pl.pallas_call(
        _moe_kernel,
        grid_spec=grid_spec,
        out_shape=jax.ShapeDtypeStruct((n_rows, dp), jnp.int32),
        compiler_params=_params("arbitrary", "arbitrary"),
        name="moe_experts",
    )(block_exp, n_used, xs, wg, wu, wd)


def _combine_kernel(x_ref, route_ref, a_ref, b_ref, o_ref):
    route = route_ref[...]
    ya = _unpack_bf16_pairs(a_ref[...])
    yb = _unpack_bf16_pairs(b_ref[...])
    o_ref[...] = x_ref[...] + route[:, 2:3] * ya + route[:, 3:4] * yb


def _combine(yg, x2, route, part, n_parts):
    n, d = x2.shape
    dp = yg.shape[1]
    tm = min(COMB_TM, n // n_parts)
    nb = n // n_parts // tm
    first = part * nb
    return pl.pallas_call(
        _combine_kernel,
        grid=(nb,),
        in_specs=[pl.BlockSpec((tm, d), lambda i: (i + first, 0)),
                  pl.BlockSpec((tm, LANES), lambda i: (i + first, 0)),
                  pl.BlockSpec((tm, dp), lambda i: (i, 0)),
                  pl.BlockSpec((tm, dp), lambda i: (i + nb, 0))],
        out_specs=pl.BlockSpec((tm, d), lambda i: (i + first, 0)),
        out_shape=jax.ShapeDtypeStruct((n, d), F32),
        input_output_aliases={0: 0},
        compiler_params=_params("arbitrary"),
        name="moe_combine",
    )(x2, route, yg, yg)


def _top2_moe(ys_mix, w_out_bf, x2, g2, router, wg, wu, wd):
    n, d = x2.shape
    a = 2 * n
    tm = min(MOE_TM, n)
    r_pad = jnp.zeros((d, LANES), F32).at[:, :N_EXPERTS].set(router)
    rh, rl = _split_hi_lo(r_pad)
    rt = min(ROUTE_TM, n)
    tri = (lax.broadcasted_iota(jnp.int32, (rt, rt), 1) < lax.broadcasted_iota(jnp.int32, (rt, rt), 0)).astype(BF16)
    x2, hn, route, cols, cnt = _out_proj_route(ys_mix, w_out_bf, x2, g2, rh, rl, tri)

    counts = cnt[0, :N_EXPERTS].astype(jnp.int32)
    padded = (counts + tm - 1) // tm * tm
    pad_ends = jnp.cumsum(padded)
    pad_starts = pad_ends - padded
    def lookup(table, idx):
        hit = idx[None, :] == jnp.arange(N_EXPERTS, dtype=jnp.int32)[:, None]
        return jnp.sum(jnp.where(hit, table[:, None], 0), axis=0)

    dest1 = lookup(pad_starts, cols[0]) + cols[4]
    dest2 = lookup(pad_starts, cols[1]) + cols[5]
    n_blocks = a // tm + N_EXPERTS
    n_rows = n_blocks * tm
    block_start = jnp.arange(n_blocks, dtype=jnp.int32) * tm
    block_exp = jnp.minimum(jnp.sum(block_start[None, :] >= pad_ends[:, None], axis=0), N_EXPERTS - 1).astype(jnp.int32)
    n_used = (pad_ends[-1] // tm).astype(jnp.int32).reshape(1)
    pad_cnt = padded - counts
    pad_cum = jnp.cumsum(pad_cnt)
    k = jnp.arange(n_rows - a, dtype=jnp.int32)
    seg = jnp.sum(k[None, :] >= pad_cum[:, None], axis=0).astype(jnp.int32)
    in_expert = lookup(pad_starts + counts - (pad_cum - pad_cnt), jnp.minimum(seg, N_EXPERTS - 1)) + k
    pad_idx = jnp.where(seg < N_EXPERTS, in_expert, pad_ends[-1] + k - pad_cum[-1])

    xs = _sc_dispatch_rows(hn, dest1, dest2, pad_idx, n_rows)
    ys = _moe_experts(block_exp, n_used, xs, wg, wu, wd, tm)
    n_parts = COMBINE_PARTS if n % (COMBINE_PARTS * COMB_TM) == 0 else 1
    step = n // n_parts
    gathered = [_sc_gather_rows(ys, jnp.concatenate([dest1[p * step:(p + 1) * step], dest2[p * step:(p + 1) * step]]))
                for p in range(n_parts)]
    for p in range(n_parts):
        x2 = _combine(gathered[p], x2, route, p, n_parts)
    return x2


def _block_diag(w):
    h, dh, _ = w.shape
    eye = jnp.eye(h, dtype=w.dtype)
    return jnp.einsum("hij,hg->higj", w, eye).reshape(h * dh, h * dh)


def kernel(x, norm1_g, w_in, q_norm_g, k_norm_g, conf_dw_w, conf_dw_b, conf_ln_g, conf_ln_b, gmlp_ln_g, gmlp_ln_b, gmlp_ws, gmlp_bs, lru_conv_w, lru_conv_b, lru_wa, lru_ba, lru_wx, lru_bx, lru_lambda, group_norm_g, w_out, norm2_g, ffn_w_gate, ffn_w_up, ffn_w_down, moe_router, moe_w_gate, moe_w_up, moe_w_down):
    bsz, s, d = x.shape
    n = bsz * s
    depth = w_in.shape[0]
    W = GROUP_WIDTH
    row = lambda v: v.reshape(1, -1).astype(F32)

    head_id = jnp.arange(W) // HEAD_DIM
    bd = (head_id[:, None] == head_id[None, :]).astype(BF16)
    bd2 = jnp.concatenate([bd, bd], axis=0)
    t_att = min(ATT_T, s)
    uu = (jnp.arange(t_att)[:, None] >= jnp.arange(t_att)[None, :]).astype(BF16)
    tril = jnp.tril(jnp.ones((GMLP_CHUNK, GMLP_CHUNK), dtype=bool))

    x2 = x.reshape(n, d)
    for l in range(depth):
        qg = row(jnp.tile(q_norm_g[l], GROUP_HEADS) * (HEAD_DIM ** -0.5))
        kg = row(jnp.tile(k_norm_g[l], GROUP_HEADS))
        gn4 = group_norm_g[l].reshape(N_GROUPS, W).astype(F32)
        cw = jnp.zeros((CONF_HALO, W), F32).at[:CONF_KERNEL].set(conf_dw_w[l])
        conf = (cw, row(conf_dw_b[l]), row(conf_ln_g[l]), row(conf_ln_b[l]))
        ws = jnp.concatenate(list(jnp.where(tril, gmlp_ws[l], 0.0).astype(BF16)), axis=1)
        bs_mat = jnp.repeat(gmlp_bs[l].T, HEAD_DIM, axis=1)
        gmlp = (row(gmlp_ln_g[l]), row(gmlp_ln_b[l]), ws, bs_mat)
        lw = jnp.zeros((8, W), F32).at[:LRU_CONV].set(lru_conv_w[l])
        wax = jnp.concatenate([_block_diag(lru_wa[l]), _block_diag(lru_wx[l])], axis=1).astype(BF16)
        lru = (lw, row(lru_conv_b[l]), wax, row(lru_ba[l]), row(lru_bx[l]), row(lru_lambda[l]))
        q, k, v, y_b, y_c, y_d = _in_mix(x2, s, row(norm1_g[l]), w_in[l].astype(BF16), qg, kg, bd2,
                                         conf, gmlp, lru, gn4)
        to3 = lambda t: t.reshape(bsz, s, t.shape[-1])
        zmax = (ATT_ZMAX_SLACK * HEAD_DIM ** 0.5 * jnp.max(jnp.abs(q_norm_g[l] * k_norm_g[l]))).reshape(1)
        y_a = _sb_attention(zmax.astype(F32), to3(q), to3(k), to3(v), uu, gn4[0:1]).reshape(n, W)
        ys = [y_a, y_b, y_c, y_d]
        j = l // 2
        if l % 2 == 0:
            x2 = _out_proj_dense_ffn(ys, w_out[l].astype(BF16), x2, row(norm2_g[l]), ffn_w_gate[j].astype(BF16),
                                     ffn_w_up[j].astype(BF16), ffn_w_down[j].astype(BF16))
        else:
            x2 = _top2_moe(ys, w_out[l].astype(BF16), x2, row(norm2_g[l]), moe_router[j], moe_w_gate[j],
                           moe_w_up[j], moe_w_down[j])
    return x2.reshape(bsz, s, d)
```

```python
import functools

import jax
import jax.numpy as jnp
from jax import lax
from jax.experimental import pallas as pl
from jax.experimental.pallas import tpu as pltpu
from jax.experimental.pallas import tpu_sc as plsc

F32 = jnp.float32
BF16 = jnp.bfloat16

HEAD_DIM = 64
GROUP_HEADS = 4
GROUP_WIDTH = GROUP_HEADS * HEAD_DIM
N_GROUPS = 4
CONF_KERNEL = 31
GMLP_CHUNK = 128
LRU_CONV = 4
LRU_C = 8.0
N_EXPERTS = 8
EPS = 1e-6

LANES = 128
VMEM_LIMIT = 56 * 1024 * 1024

IN_TM = 1024
MIX_T = 256
ATT_T = 256
ATT_UNDERFLOW = 110.0
ATT_ZMAX_SLACK = 1.05
CONF_R = 32
CONF_HALO = 32
LRU_HALO = 8
FFN_TM = 1024
FFN_TF = 256
ROUTE_TM = 1024
ROUTE_IDX_COLS = 8
MOE_TM = 1024
MOE_TF = 512
COMB_TM = 1024
COMBINE_PARTS = 2
SC_WINDOW_BYTES = 128 * 1024
SC_MAX_INDEX_VECTOR = 128


def _params(*sem):
    return pltpu.CompilerParams(dimension_semantics=sem, vmem_limit_bytes=VMEM_LIMIT)


def _split_hi_lo(x):
    hi = x.astype(BF16)
    lo = (x - hi.astype(F32)).astype(BF16)
    return hi, lo


SOFTPLUS_CLAMP = 60.0


def _softplus(z):
    return jnp.maximum(jnp.log(1.0 + jnp.exp(jnp.minimum(z, SOFTPLUS_CLAMP))), z)


def _gelu_tanh(x):
    c = 0.7978845608028654
    return 0.5 * x * (1.0 + jnp.tanh(c * (x + 0.044715 * (x * x * x))))


def _pack_bf16_pairs(y):
    c = y.shape[1] // 2
    bits = lax.bitcast_convert_type(y.astype(BF16).astype(F32), jnp.uint32)
    word = (bits[:, :c] >> 16) | bits[:, c:]
    return lax.bitcast_convert_type(word, jnp.int32)


def _unpack_bf16_pairs(w):
    bits = lax.bitcast_convert_type(w, jnp.uint32)
    lo = lax.bitcast_convert_type(bits << 16, F32)
    hi = lax.bitcast_convert_type(bits & jnp.uint32(0xFFFF0000), F32)
    return jnp.concatenate([lo, hi], axis=1)


def _group_rms(y, g):
    ms = jnp.mean(y * y, axis=-1, keepdims=True)
    return y * lax.rsqrt(ms + EPS) * g


def _sb_attn_kernel(zmax_ref, q_ref, k_ref, v_ref, uu_ref, gn_ref, o_ref):
    T = q_ref.shape[1]
    n_pairs = GROUP_WIDTH // LANES
    i = pl.program_id(1)
    lane = lax.broadcasted_iota(jnp.int32, (T, LANES), 1)
    rows = lax.broadcasted_iota(jnp.int32, (2 * T, T), 0)
    cols = lax.broadcasted_iota(jnp.int32, (2 * T, T), 1)
    causal = cols < jnp.where(rows >= T, rows - T, rows)
    uu = uu_ref[...]

    def tiles(r0, carries, mask):
        ps = range(n_pairs)
        cs = [slice(p * LANES, (p + 1) * LANES) for p in ps]
        z = [lax.dot_general(qs[p], k_ref[0, pl.ds(r0, T), cs[p]], (((1,), (1,)), ((), ())),
                             preferred_element_type=F32) for p in ps]
        sp = [_softplus(z[p]) for p in ps]
        if mask is not None:
            sp = [jnp.where(mask, s, 0.0) for s in sp]
        cum = [jnp.dot(sp[p].astype(BF16), uu, preferred_element_type=F32) for p in ps]
        w = [jnp.exp(z[p] - cum[p] - carries[p]) for p in ps]
        if mask is not None:
            w = [jnp.where(mask, x, 0.0) for x in w]
        pv = [jnp.dot(w[p].astype(BF16), v_ref[0, pl.ds(r0, T), cs[p]], preferred_element_type=F32)
              for p in ps]
        return pv, [carries[p] + jnp.sum(sp[p], axis=1, keepdims=True) for p in ps]

    qs = []
    for pair in range(n_pairs):
        q2 = q_ref[0, :, pair * LANES:(pair + 1) * LANES]
        zero = jnp.zeros_like(q2)
        qs.append(jnp.concatenate([jnp.where(lane < HEAD_DIM, q2, zero),
                                   jnp.where(lane >= HEAD_DIM, q2, zero)], axis=0))
    row0 = pl.multiple_of(i * T, T)
    state = tuple(zip(*tiles(row0, [jnp.zeros((2 * T, 1), F32)] * n_pairs, causal)))

    dead_at = zmax_ref[0] + ATT_UNDERFLOW

    def alive(st):
        return functools.reduce(jnp.minimum, [jnp.min(carry) for _, carry in st]) <= dead_at

    def cond(c):
        step, live, _ = c
        return (step < i) & live

    def body(c):
        step, _, st = c
        r0 = pl.multiple_of((i - 1 - step) * T, T)
        pv, carries = tiles(r0, [carry for _, carry in st], None)
        new = tuple((st[p][0] + pv[p], carries[p]) for p in range(n_pairs))
        return step + 1, alive(new), new

    _, _, state = lax.while_loop(cond, body, (jnp.int32(0), alive(state), state))
    y = jnp.concatenate([jnp.where(lane < HEAD_DIM, acc[:T], acc[T:]) for acc, _ in state], axis=1)
    o_ref[0] = _group_rms(y, gn_ref[...]).astype(o_ref.dtype)


def _sb_attention(zmax, q3, k3, v3, uu, gn):
    b, s, w = q3.shape
    t = min(ATT_T, s)
    return pl.pallas_call(
        _sb_attn_kernel,
        grid=(b, s // t),
        in_specs=[
            pl.BlockSpec(memory_space=pltpu.SMEM),
            pl.BlockSpec((1, t, w), lambda bi, i: (bi, i, 0)),
            pl.BlockSpec((1, s, w), lambda bi, i: (bi, 0, 0)),
            pl.BlockSpec((1, s, w), lambda bi, i: (bi, 0, 0)),
            pl.BlockSpec((t, t), lambda bi, i: (0, 0)),
            pl.BlockSpec((1, w), lambda bi, i: (0, 0)),
        ],
        out_specs=pl.BlockSpec((1, t, w), lambda bi, i: (bi, i, 0)),
        out_shape=jax.ShapeDtypeStruct((b, s, w), BF16),
        compiler_params=_params("arbitrary", "arbitrary"),
        name="sb_attn",
    )(zmax, q3, k3, v3, uu, gn)


def _conformer_rows(val, gate, w_ref, b_ref, lng_ref, lnb_ref, gn, o_ref, row0, hs_ref, sh_ref):
    T = val.shape[0]
    H = CONF_HALO
    SUB = 8
    hs_ref[H:H + T, :] = val * jax.nn.sigmoid(gate)
    off = H - (CONF_KERNEL - 1)
    L = sh_ref.shape[1]
    for s in range(1, SUB):
        sh_ref[s - 1] = hs_ref[s:s + L, :]
    R = min(CONF_R, T)
    taps_w = [jnp.broadcast_to(w_ref[k:k + 1, :], (R, GROUP_WIDTH)) for k in range(CONF_KERNEL)]
    for r0 in range(0, T, R):
        acc = jnp.broadcast_to(b_ref[...], (R, GROUP_WIDTH))
        for k in range(CONF_KERNEL):
            s = (off + k) % SUB
            a = off + k - s + r0
            tap = hs_ref[a:a + R, :] if s == 0 else sh_ref[s - 1, a:a + R, :]
            acc = acc + taps_w[k] * tap
        mu = jnp.mean(acc, axis=-1, keepdims=True)
        xc = acc - mu
        var = jnp.mean(xc * xc, axis=-1, keepdims=True)
        y = xc * lax.rsqrt(var + EPS) * lng_ref[...] + lnb_ref[...]
        y = y * jax.nn.sigmoid(y)
        o_ref[row0 + r0:row0 + r0 + R, :] = _group_rms(y, gn).astype(o_ref.dtype)
    hs_ref[0:H, :] = hs_ref[T:T + H, :]


def _gmlp_rows(u_in, v_in, lng_ref, lnb_ref, ws_ref, bs_ref, gn, o_ref, row0):
    C = GMLP_CHUNK
    lane = lax.broadcasted_iota(jnp.int32, (C, GROUP_WIDTH), 1)
    u = _gelu_tanh(u_in)
    v = _gelu_tanh(v_in)
    mu = jnp.mean(v, axis=-1, keepdims=True)
    xc = v - mu
    var = jnp.mean(xc * xc, axis=-1, keepdims=True)
    vb = (xc * lax.rsqrt(var + EPS) * lng_ref[...] + lnb_ref[...]).astype(BF16)
    zero = jnp.zeros_like(vb)
    vstack = jnp.concatenate(
        [jnp.where((lane >= h * HEAD_DIM) & (lane < (h + 1) * HEAD_DIM), vb, zero) for h in range(GROUP_HEADS)],
        axis=0)
    mixed = bs_ref[...] + jnp.dot(ws_ref[...], vstack, preferred_element_type=F32)
    o_ref[row0:row0 + C, :] = _group_rms(u * mixed, gn).astype(o_ref.dtype)


def _lru_rows(x_in, gate, cw_ref, cb_ref, wax_ref, ba_ref, bx_ref, lam_ref, gn, o_ref, row0, xs_ref, hprev_ref):
    T = x_in.shape[0]
    W = GROUP_WIDTH
    H = LRU_HALO
    xs_ref[H:H + T, :] = x_in
    off = H - (LRU_CONV - 1)
    xb = jnp.broadcast_to(cb_ref[...], (T, W))
    for k in range(LRU_CONV):
        xb = xb + cw_ref[k:k + 1, :] * xs_ref[off + k:off + k + T, :]
    xs_ref[0:H, :] = xs_ref[T:T + H, :]

    pre = jnp.dot(xb.astype(BF16), wax_ref[...], preferred_element_type=F32)
    r = jax.nn.sigmoid(pre[:, :W] + ba_ref[...])
    ig = jax.nn.sigmoid(pre[:, W:] + bx_ref[...])
    nlam = -lam_ref[...]
    log_a = (-LRU_C) * r * (jnp.maximum(nlam, 0.0) + jnp.log1p(jnp.exp(-jnp.abs(nlam))))
    a = jnp.exp(log_a)
    th = jnp.tanh(log_a)
    one_minus_a2 = 2.0 * th / (th - 1.0)
    b = jnp.sqrt(one_minus_a2) * (ig * xb)

    SUB = 8
    a = a.reshape(T // SUB, SUB, W)
    b = b.reshape(T // SUB, SUB, W)
    sub = lax.broadcasted_iota(jnp.int32, a.shape, 1)
    d = 1
    while d < SUB:
        keep = sub >= d
        a_sh = jnp.where(keep, pltpu.roll(a, d, 1), 1.0)
        b_sh = jnp.where(keep, pltpu.roll(b, d, 1), 0.0)
        b = a * b_sh + b
        a = a * a_sh
        d *= 2
    carry = hprev_ref[0:1, :]
    tiles = []
    for g in range(T // SUB):
        hg = b[g] + a[g] * carry
        carry = hg[SUB - 1:SUB]
        tiles.append(hg)
    h = jnp.concatenate(tiles, axis=0)
    hprev_ref[...] = jnp.broadcast_to(carry, hprev_ref.shape)
    y = h * _gelu_tanh(gate)
    o_ref[row0:row0 + T, :] = _group_rms(y, gn).astype(o_ref.dtype)


def _in_mix_kernel(blocks_per_seq, x_ref, g_ref, w_ref, qg_ref, kg_ref, bd_ref,
                   cw_ref, cb_ref, clg_ref, clb_ref,
                   glg_ref, glb_ref, ws_ref, bs_ref,
                   lw_ref, lb_ref, wax_ref, ba_ref, bx_ref, lam_ref, gn_ref,
                   q_ref, k_ref, v_ref, yb_ref, yc_ref, yd_ref,
                   rest_ref, hs_ref, sh_ref, xs_ref, hprev_ref):
    tm = x_ref.shape[0]
    W = GROUP_WIDTH

    @pl.when(pl.program_id(0) % blocks_per_seq == 0)
    def _():
        hs_ref[0:CONF_HALO, :] = jnp.zeros((CONF_HALO, W), F32)
        xs_ref[0:LRU_HALO, :] = jnp.zeros((LRU_HALO, W), F32)
        hprev_ref[...] = jnp.zeros(hprev_ref.shape, F32)

    def step(rest_w, rest_r):
        xf = x_ref[...]
        ms = jnp.mean(xf * xf, axis=-1, keepdims=True)
        h = (xf * lax.rsqrt(ms + EPS) * g_ref[...]).astype(BF16)

        def proj(c):
            return jnp.dot(h, w_ref[:, c * W:(c + 1) * W], preferred_element_type=F32)

        def head_norm(p, gain):
            hi, lo = _split_hi_lo(p * p)
            ss = jnp.dot(jnp.concatenate([hi, lo], axis=1), bd_ref[...], preferred_element_type=F32)
            return p * lax.rsqrt(ss * (1.0 / HEAD_DIM) + EPS) * gain

        def project(c):
            if c == 0:
                q_ref[...] = head_norm(proj(0), qg_ref[...]).astype(BF16)
            elif c == 1:
                k_ref[...] = head_norm(proj(1), kg_ref[...]).astype(BF16)
            elif c == 2:
                v_ref[...] = proj(2).astype(BF16)
            else:
                rest_w[:, (c - 3) * W:(c - 2) * W] = proj(c)

        def cols(r0, n, c):
            return rest_r[r0:r0 + n, c * W:(c + 1) * W]

        T = min(MIX_T, tm)

        def conformer(r0):
            _conformer_rows(cols(r0, T, 0), cols(r0, T, 1), cw_ref, cb_ref, clg_ref, clb_ref, gn_ref[1:2, :],
                            yb_ref, r0, hs_ref, sh_ref)

        def lru(r0):
            _lru_rows(cols(r0, T, 4), cols(r0, T, 5), lw_ref, lb_ref, wax_ref, ba_ref, bx_ref, lam_ref,
                      gn_ref[3:4, :], yd_ref, r0, xs_ref, hprev_ref)

        def gmlp(r0):
            _gmlp_rows(cols(r0, GMLP_CHUNK, 2), cols(r0, GMLP_CHUNK, 3), glg_ref, glb_ref, ws_ref, bs_ref,
                       gn_ref[2:3, :], yc_ref, r0)

        mix = []
        for r0 in range(0, tm, T):
            mix.append(functools.partial(conformer, r0))
            mix.append(functools.partial(lru, r0))
            mix.extend(functools.partial(gmlp, c0) for c0 in range(r0, r0 + T, GMLP_CHUNK))
        n_proj = w_ref.shape[1] // W
        for c in range(3, n_proj):
            project(c)
        per = -(-len(mix) // 4)
        for c in range(4):
            if c > 0:
                project(c - 1)
            for task in mix[c * per:(c + 1) * per]:
                task()

    step(rest_ref, rest_ref)


def _in_mix(x2, seq_len, g, w_bf, qg, kg, bd2, conf, gmlp, lru, gn4):
    n, d = x2.shape
    cols = w_bf.shape[1]
    W = GROUP_WIDTH
    rest_cols = cols - 3 * W
    tm = min(IN_TM, seq_len)
    assert seq_len % tm == 0 and tm % min(MIX_T, tm) == 0 and tm % GMLP_CHUNK == 0
    t = min(MIX_T, tm)
    row = lambda i: (i, 0)
    whole = lambda a: pl.BlockSpec(a.shape, lambda i: (0,) * a.ndim)
    params = (g, w_bf, qg, kg, bd2) + tuple(conf) + tuple(gmlp) + tuple(lru) + (gn4,)
    out_shape = jax.ShapeDtypeStruct((n, W), BF16)
    return pl.pallas_call(
        functools.partial(_in_mix_kernel, seq_len // tm),
        grid=(n // tm,),
        in_specs=[pl.BlockSpec((tm, d), row)] + [whole(a) for a in params],
        out_specs=[pl.BlockSpec((tm, W), row)] * 6,
        out_shape=[out_shape] * 6,
        scratch_shapes=[pltpu.VMEM((tm, rest_cols), F32),
                        pltpu.VMEM((t + CONF_HALO, W), F32),
                        pltpu.VMEM((7, t + CONF_HALO - 8, W), F32),
                        pltpu.VMEM((t + LRU_HALO, W), F32),
                        pltpu.VMEM((8, W), F32)],
        compiler_params=_params("arbitrary"),
        name="in_mix",
    )(x2, *params)


def _out_proj_kernel(ya_ref, yb_ref, yc_ref, yd_ref, w_ref, x_ref, g_ref, rh_ref, rl_ref, tri_ref,
                     xo_ref, h_ref, route_ref, idx_ref, cnt_ref, base_ref):
    W = GROUP_WIDTH

    @pl.when(pl.program_id(0) == 0)
    def _():
        base_ref[...] = jnp.zeros(base_ref.shape, F32)

    acc = x_ref[...]
    for gi, y_ref in enumerate((ya_ref, yb_ref, yc_ref, yd_ref)):
        acc = acc + jnp.dot(y_ref[...], w_ref[gi * W:(gi + 1) * W, :], preferred_element_type=F32)
    xo_ref[...] = acc
    ms = jnp.mean(acc * acc, axis=-1, keepdims=True)
    h = acc * lax.rsqrt(ms + EPS) * g_ref[...]
    h_ref[...] = _pack_bf16_pairs(h)
    _route_rows(h, rh_ref, rl_ref, tri_ref, route_ref, idx_ref, cnt_ref, base_ref)


def _out_proj_route(ys, w_bf, x2, g, rh, rl, tri):
    n, d = x2.shape
    W = GROUP_WIDTH
    tm = tri.shape[0]
    row = lambda i: (i, 0)
    const = lambda i: (0, 0)
    ysp = pl.BlockSpec((tm, W), row)
    return pl.pallas_call(
        _out_proj_kernel,
        grid=(n // tm,),
        in_specs=[ysp, ysp, ysp, ysp,
                  pl.BlockSpec(w_bf.shape, const),
                  pl.BlockSpec((tm, d), row),
                  pl.BlockSpec((1, d), const),
                  pl.BlockSpec((d, LANES), const),
                  pl.BlockSpec((d, LANES), const),
                  pl.BlockSpec((tm, tm), const)],
        out_specs=[pl.BlockSpec((tm, d), row),
                   pl.BlockSpec((tm, d // 2), row),
                   pl.BlockSpec((tm, LANES), row),
                   pl.BlockSpec((ROUTE_IDX_COLS, tm), lambda i: (0, i)),
                   pl.BlockSpec((8, LANES), const)],
        out_shape=[jax.ShapeDtypeStruct((n, d), F32),
                   jax.ShapeDtypeStruct((n, d // 2), jnp.int32),
                   jax.ShapeDtypeStruct((n, LANES), F32),
                   jax.ShapeDtypeStruct((ROUTE_IDX_COLS, n), jnp.int32),
                   jax.ShapeDtypeStruct((8, LANES), F32)],
        scratch_shapes=[pltpu.VMEM((8, LANES), F32)],
        compiler_params=_params("arbitrary"),
        name="out_proj_route",
    )(*ys, w_bf, x2, g, rh, rl, tri)


def _out_ffn_kernel(ya_ref, yb_ref, yc_ref, yd_ref, wo_ref, x_ref, g_ref, wg_ref, wu_ref, wd_ref, o_ref,
                    xn_ref, act_ref):
    W = GROUP_WIDTH
    f = wg_ref.shape[1]
    acc = x_ref[...]
    for gi, y_ref in enumerate((ya_ref, yb_ref, yc_ref, yd_ref)):
        acc = acc + jnp.dot(y_ref[...], wo_ref[gi * W:(gi + 1) * W, :], preferred_element_type=F32)
    xn_ref[...] = acc
    ms = jnp.mean(acc * acc, axis=-1, keepdims=True)
    h = (acc * lax.rsqrt(ms + EPS) * g_ref[...]).astype(BF16)
    for c0 in range(0, f, FFN_TF):
        g = jnp.dot(h, wg_ref[:, c0:c0 + FFN_TF], preferred_element_type=F32)
        u = jnp.dot(h, wu_ref[:, c0:c0 + FFN_TF], preferred_element_type=F32)
        act_ref[:, c0:c0 + FFN_TF] = (g * jax.nn.sigmoid(g) * u).astype(BF16)
    o_ref[...] = xn_ref[...] + jnp.dot(act_ref[...], wd_ref[...], preferred_element_type=F32)


def _out_proj_dense_ffn(ys, wo, x2, g, wg, wu, wd):
    n, d = x2.shape
    f = wg.shape[1]
    W = GROUP_WIDTH
    assert f % FFN_TF == 0
    tm = min(FFN_TM, n)
    row = lambda i: (i, 0)
    resident = dict(pipeline_mode=pl.Buffered(1))
    ysp = pl.BlockSpec((tm, W), row)
    return pl.pallas_call(
        _out_ffn_kernel,
        grid=(n // tm,),
        in_specs=[
            ysp, ysp, ysp, ysp,
            pl.BlockSpec(wo.shape, lambda i: (0, 0), **resident),
            pl.BlockSpec((tm, d), row),
            pl.BlockSpec((1, d), lambda i: (0, 0)),
            pl.BlockSpec((d, f), lambda i: (0, 0), **resident),
            pl.BlockSpec((d, f), lambda i: (0, 0), **resident),
            pl.BlockSpec((f, d), lambda i: (0, 0), **resident),
        ],
        out_specs=pl.BlockSpec((tm, d), row),
        out_shape=jax.ShapeDtypeStruct((n, d), F32),
        scratch_shapes=[pltpu.VMEM((tm, d), F32), pltpu.VMEM((tm, f), BF16)],
        compiler_params=_params("arbitrary"),
        name="out_proj_dense_ffn",
    )(*ys, wo, x2, g, wg, wu, wd)


def _route_rows(h, rh_ref, rl_ref, tri_ref, route_ref, idx_ref, cnt_ref, base_ref):
    tm = h.shape[0]
    hh, hl = _split_hi_lo(h)
    rh = rh_ref[...]
    logits = (jnp.dot(hh, rh, preferred_element_type=F32)
              + jnp.dot(hl, rh, preferred_element_type=F32)
              + jnp.dot(hh, rl_ref[...], preferred_element_type=F32))
    lane = lax.broadcasted_iota(jnp.int32, (tm, LANES), 1).astype(F32)
    neg = jnp.float32(-jnp.inf)
    logits = jnp.where(lane < N_EXPERTS, logits, neg)
    m1 = jnp.max(logits, axis=1, keepdims=True)
    i1 = jnp.min(jnp.where(logits == m1, lane, float(LANES)), axis=1, keepdims=True)
    l2 = jnp.where(lane == i1, neg, logits)
    m2 = jnp.max(l2, axis=1, keepdims=True)
    i2 = jnp.min(jnp.where(l2 == m2, lane, float(LANES)), axis=1, keepdims=True)
    e = jnp.exp(m2 - m1)
    g1 = 1.0 / (1.0 + e)
    g2 = e / (1.0 + e)
    oh1 = jnp.where(lane == i1, 1.0, 0.0)
    oh2 = jnp.where(lane == i2, 1.0, 0.0)
    oh = oh1 + oh2
    before = jnp.dot(tri_ref[...], oh.astype(BF16), preferred_element_type=F32) + base_ref[0:1, :]
    r1 = jnp.sum(oh1 * before, axis=1, keepdims=True)
    r2 = jnp.sum(oh2 * before, axis=1, keepdims=True)
    base = base_ref[0:1, :] + jnp.sum(oh, axis=0, keepdims=True)
    base_ref[...] = jnp.broadcast_to(base, base_ref.shape)
    cnt_ref[...] = jnp.broadcast_to(base, cnt_ref.shape)
    out = jnp.where(lane == 0, i1, 0.0)
    out = jnp.where(lane == 1, i2, out)
    out = jnp.where(lane == 2, g1, out)
    out = jnp.where(lane == 3, g2, out)
    out = jnp.where(lane == 4, r1, out)
    out = jnp.where(lane == 5, r2, out)
    route_ref[...] = out
    hi = jnp.floor(out * (1.0 / 256.0))
    lo = out - 256.0 * hi
    sel = (lax.broadcasted_iota(jnp.int32, (2 * ROUTE_IDX_COLS, LANES), 0)
           == lax.broadcasted_iota(jnp.int32, (2 * ROUTE_IDX_COLS, LANES), 1)).astype(BF16)
    nt = (((1,), (1,)), ((), ()))
    hi_t = lax.dot_general(sel, hi.astype(BF16), nt, preferred_element_type=F32)
    lo_t = lax.dot_general(sel, lo.astype(BF16), nt, preferred_element_type=F32)
    idx_ref[...] = (256.0 * hi_t + lo_t)[:ROUTE_IDX_COLS].astype(jnp.int32)


def _sc_window_rows(table):
    row_bytes = table.shape[1] * table.dtype.itemsize
    return min(SC_MAX_INDEX_VECTOR, SC_WINDOW_BYTES // row_bytes)


def _sc_gather_rows(table, idx):
    _, d = table.shape
    b = idx.shape[0]
    win = _sc_window_rows(table)
    sc = plsc.get_sparse_core_info()
    n_workers = sc.num_cores * sc.num_subcores
    per_w = b // n_workers
    n_win = per_w // win
    assert per_w * n_workers == b and n_win * win == per_w and n_win % 2 == 0, (b, n_workers, win)
    mesh = plsc.VectorSubcoreMesh(core_axis_name="c", subcore_axis_name="s")
    dma = pltpu.SemaphoreType.DMA

    @functools.partial(
        pl.kernel, mesh=mesh, out_type=jax.ShapeDtypeStruct((b, d), table.dtype), name="sc_gather_rows",
        scratch_types=[pltpu.VMEM((win,), jnp.int32), pltpu.VMEM((win,), jnp.int32),
                       pltpu.VMEM((win, d), table.dtype), pltpu.VMEM((win, d), table.dtype),
                       dma, dma, dma, dma])
    def gather_kernel(table_hbm, idx_hbm, out_hbm, i0, i1, r0, r1, g0, g1, w0, w1):
        wid = lax.axis_index("s") * sc.num_cores + lax.axis_index("c")
        base = wid * per_w
        idxb, rows, gsem, wsem = (i0, i1), (r0, r1), (g0, g1), (w0, w1)

        def off(c):
            return pl.multiple_of(base + c * win, win)

        def gather(s):
            return pltpu.make_async_copy(table_hbm.at[idxb[s]], rows[s], gsem[s])

        def write(c, s):
            return pltpu.make_async_copy(rows[s], out_hbm.at[pl.ds(off(c), win)], wsem[s])

        pltpu.sync_copy(idx_hbm.at[pl.ds(off(0), win)], idxb[0])
        gather(0).start()

        @pl.loop(0, n_win, step=2)
        def _(c):
            for s in (0, 1):
                cc = c + s

                @pl.when(cc + 1 < n_win)
                def _():
                    @pl.when(cc >= 1)
                    def _():
                        write(cc - 1, 1 - s).wait()

                    pltpu.sync_copy(idx_hbm.at[pl.ds(off(cc + 1), win)], idxb[1 - s])
                    gather(1 - s).start()

                gather(s).wait()
                write(cc, s).start()

        write(n_win - 2, 0).wait()
        write(n_win - 1, 1).wait()

    return gather_kernel(table, idx)


def _sc_dispatch_rows(src, dest1, dest2, pad_idx, n_rows):
    n, d = src.shape
    win = _sc_window_rows(src)
    sc = plsc.get_sparse_core_info()
    n_workers = sc.num_cores * sc.num_subcores
    per_w = n // n_workers
    n_win = per_w // win
    assert per_w * n_workers == n and n_win * win == per_w and n_win % 2 == 0, (n, n_workers, win)
    n_pad = pad_idx.shape[0]
    assert n_pad + 2 * n == n_rows and n_pad % (n_workers * win) == 0, (n_pad, n, n_rows)
    pad_per_w = n_pad // n_workers
    zero_rows = jnp.zeros((win, d), src.dtype)
    mesh = plsc.VectorSubcoreMesh(core_axis_name="c", subcore_axis_name="s")
    dma = pltpu.SemaphoreType.DMA
    ivec = pltpu.VMEM((win,), jnp.int32)
    rbuf = pltpu.VMEM((win, d), src.dtype)

    @functools.partial(
        pl.kernel, mesh=mesh, out_type=jax.ShapeDtypeStruct((n_rows, d), src.dtype), name="sc_dispatch_rows",
        scratch_types=[ivec, ivec, ivec, ivec, rbuf, rbuf, dma, dma, dma, dma, dma, dma])
    def dispatch_kernel(src_hbm, d1_hbm, d2_hbm, pad_hbm, zero_hbm, out_hbm,
                        a0, a1, b0, b1, r0, r1, l0, l1, p0, p1, q0, q1):
        wid = lax.axis_index("s") * sc.num_cores + lax.axis_index("c")
        base = wid * per_w
        ia, ib, rows, lsem, psem, qsem = (a0, a1), (b0, b1), (r0, r1), (l0, l1), (p0, p1), (q0, q1)

        def off(c):
            return pl.multiple_of(base + c * win, win)

        def load(c, s):
            return pltpu.make_async_copy(src_hbm.at[pl.ds(off(c), win)], rows[s], lsem[s])

        def scatters(s):
            return (pltpu.make_async_copy(rows[s], out_hbm.at[ia[s]], psem[s]),
                    pltpu.make_async_copy(rows[s], out_hbm.at[ib[s]], qsem[s]))

        def fetch(c, s):
            pltpu.sync_copy(d1_hbm.at[pl.ds(off(c), win)], ia[s])
            pltpu.sync_copy(d2_hbm.at[pl.ds(off(c), win)], ib[s])
            load(c, s).start()

        fetch(0, 0)

        @pl.loop(0, n_win, step=2)
        def _(c):
            for s in (0, 1):
                cc = c + s

                @pl.when(cc + 1 < n_win)
                def _():
                    @pl.when(cc >= 1)
                    def _():
                        for cp in scatters(1 - s):
                            cp.wait()

                    fetch(cc + 1, 1 - s)

                load(cc, s).wait()
                for cp in scatters(s):
                    cp.start()

        for s in (0, 1):
            for cp in scatters(s):
                cp.wait()

        pltpu.sync_copy(zero_hbm, rows[0])

        @pl.loop(0, pad_per_w // win)
        def _(c):
            o = pl.multiple_of(wid * pad_per_w + c * win, win)
            pltpu.sync_copy(pad_hbm.at[pl.ds(o, win)], ia[0])
            pltpu.sync_copy(rows[0], out_hbm.at[ia[0]])

    return dispatch_kernel(src, dest1, dest2, pad_idx, zero_rows)


def _moe_kernel(bexp_ref, nused_ref, x_ref, wg_ref, wu_ref, wd_ref, o_ref, xb, acc):
    i = pl.program_id(0)
    j = pl.program_id(1)
    nj = pl.num_programs(1)
    used = i < nused_ref[0]

    @pl.when(used & (j == 0))
    def _():
        xb[...] = _unpack_bf16_pairs(x_ref[...]).astype(BF16)
        acc[...] = jnp.zeros(acc.shape, F32)

    @pl.when(used)
    def _():
        x = xb[...]
        g = jnp.dot(x, wg_ref[0].astype(BF16), preferred_element_type=F32)
        u = jnp.dot(x, wu_ref[0].astype(BF16), preferred_element_type=F32)
        act = (g * jax.nn.sigmoid(g) * u).astype(BF16)
        acc[...] += jnp.dot(act, wd_ref[0].astype(BF16), preferred_element_type=F32)

        @pl.when(j == nj - 1)
        def _():
            o_ref[...] = _pack_bf16_pairs(acc[...])

    @pl.when(jnp.logical_not(used) & (j == 0))
    def _():
        o_ref[...] = jnp.zeros(o_ref.shape, o_ref.dtype)


def _moe_experts(block_exp, n_used, xs, wg, wu, wd, tm):
    n_rows, dp = xs.shape
    d = 2 * dp
    n_blocks = n_rows // tm
    f = wg.shape[2]
    tf = MOE_TF if f % MOE_TF == 0 else f
    nj = f // tf
    assert nj >= 2

    def live(i, j, be, nu):
        u = i < nu[0]
        return jnp.where(u, i, nu[0] - 1), jnp.where(u, j, nj - 1)

    def w_in_map(i, j, be, nu):
        ii, jj = live(i, j, be, nu)
        return (be[ii], 0, jj)

    def w_down_map(i, j, be, nu):
        ii, jj = live(i, j, be, nu)
        return (be[ii], jj, 0)

    def x_map(i, j, be, nu):
        return (live(i, j, be, nu)[0], 0)

    def out_map(i, j, be, nu):
        return (i, 0)

    grid_spec = pltpu.PrefetchScalarGridSpec(
        num_scalar_prefetch=2,
        grid=(n_blocks, nj),
        in_specs=[
            pl.BlockSpec((tm, dp), x_map),
            pl.BlockSpec((1, d, tf), w_in_map),
            pl.BlockSpec((1, d, tf), w_in_map),
            pl.BlockSpec((1, tf, d), w_down_map),
        ],
        out_specs=pl.BlockSpec((tm, dp), out_map),
        scratch_shapes=[pltpu.VMEM((tm, d), BF16), pltpu.VMEM((tm, d), F32)],
    )
    return pl.pallas_call(
        _moe_kernel,
        grid_spec=grid_spec,
        out_shape=jax.ShapeDtypeStruct((n_rows, dp), jnp.int32),
        compiler_params=_params("arbitrary", "arbitrary"),
        name="moe_experts",
    )(block_exp, n_used, xs, wg, wu, wd)


def _combine_kernel(x_ref, route_ref, a_ref, b_ref, o_ref):
    route = route_ref[...]
    ya = _unpack_bf16_pairs(a_ref[...])
    yb = _unpack_bf16_pairs(b_ref[...])
    o_ref[...] = x_ref[...] + route[:, 2:3] * ya + route[:, 3:4] * yb


def _combine(yg, x2, route, part, n_parts):
    n, d = x2.shape
    dp = yg.shape[1]
    tm = min(COMB_TM, n // n_parts)
    nb = n // n_parts // tm
    first = part * nb
    return pl.pallas_call(
        _combine_kernel,
        grid=(nb,),
        in_specs=[pl.BlockSpec((tm, d), lambda i: (i + first, 0)),
                  pl.BlockSpec((tm, LANES), lambda i: (i + first, 0)),
                  pl.BlockSpec((tm, dp), lambda i: (i, 0)),
                  pl.BlockSpec((tm, dp), lambda i: (i + nb, 0))],
        out_specs=pl.BlockSpec((tm, d), lambda i: (i + first, 0)),
        out_shape=jax.ShapeDtypeStruct((n, d), F32),
        input_output_aliases={0: 0},
        compiler_params=_params("arbitrary"),
        name="moe_combine",
    )(x2, route, yg, yg)


def _top2_moe(ys_mix, w_out_bf, x2, g2, router, wg, wu, wd):
    n, d = x2.shape
    a = 2 * n
    tm = min(MOE_TM, n)
    r_pad = jnp.zeros((d, LANES), F32).at[:, :N_EXPERTS].set(router)
    rh, rl = _split_hi_lo(r_pad)
    rt = min(ROUTE_TM, n)
    tri = (lax.broadcasted_iota(jnp.int32, (rt, rt), 1) < lax.broadcasted_iota(jnp.int32, (rt, rt), 0)).astype(BF16)
    x2, hn, route, cols, cnt = _out_proj_route(ys_mix, w_out_bf, x2, g2, rh, rl, tri)

    counts = cnt[0, :N_EXPERTS].astype(jnp.int32)
    padded = (counts + tm - 1) // tm * tm
    pad_ends = jnp.cumsum(padded)
    pad_starts = pad_ends - padded
    def lookup(table, idx):
        hit = idx[None, :] == jnp.arange(N_EXPERTS, dtype=jnp.int32)[:, None]
        return jnp.sum(jnp.where(hit, table[:, None], 0), axis=0)

    dest1 = lookup(pad_starts, cols[0]) + cols[4]
    dest2 = lookup(pad_starts, cols[1]) + cols[5]
    n_blocks = a // tm + N_EXPERTS
    n_rows = n_blocks * tm
    block_start = jnp.arange(n_blocks, dtype=jnp.int32) * tm
    block_exp = jnp.minimum(jnp.sum(block_start[None, :] >= pad_ends[:, None], axis=0), N_EXPERTS - 1).astype(jnp.int32)
    n_used = (pad_ends[-1] // tm).astype(jnp.int32).reshape(1)
    pad_cnt = padded - counts
    pad_cum = jnp.cumsum(pad_cnt)
    k = jnp.arange(n_rows - a, dtype=jnp.int32)
    seg = jnp.sum(k[None, :] >= pad_cum[:, None], axis=0).astype(jnp.int32)
    in_expert = lookup(pad_starts + counts - (pad_cum - pad_cnt), jnp.minimum(seg, N_EXPERTS - 1)) + k
    pad_idx = jnp.where(seg < N_EXPERTS, in_expert, pad_ends[-1] + k - pad_cum[-1])

    xs = _sc_dispatch_rows(hn, dest1, dest2, pad_idx, n_rows)
    ys = _moe_experts(block_exp, n_used, xs, wg, wu, wd, tm)
    n_parts = COMBINE_PARTS if n % (COMBINE_PARTS * COMB_TM) == 0 else 1
    step = n // n_parts
    gathered = [_sc_gather_rows(ys, jnp.concatenate([dest1[p * step:(p + 1) * step], dest2[p * step:(p + 1) * step]]))
                for p in range(n_parts)]
    for p in range(n_parts):
        x2 = _combine(gathered[p], x2, route, p, n_parts)
    return x2


def _block_diag(w):
    h, dh, _ = w.shape
    eye = jnp.eye(h, dtype=w.dtype)
    return jnp.einsum("hij,hg->higj", w, eye).reshape(h * dh, h * dh)


def kernel(x, norm1_g, w_in, q_norm_g, k_norm_g, conf_dw_w, conf_dw_b, conf_ln_g, conf_ln_b, gmlp_ln_g, gmlp_ln_b, gmlp_ws, gmlp_bs, lru_conv_w, lru_conv_b, lru_wa, lru_ba, lru_wx, lru_bx, lru_lambda, group_norm_g, w_out, norm2_g, ffn_w_gate, ffn_w_up, ffn_w_down, moe_router, moe_w_gate, moe_w_up, moe_w_down):
    bsz, s, d = x.shape
    n = bsz * s
    depth = w_in.shape[0]
    W = GROUP_WIDTH
    row = lambda v: v.reshape(1, -1).astype(F32)

    head_id = jnp.arange(W) // HEAD_DIM
    bd = (head_id[:, None] == head_id[None, :]).astype(BF16)
    bd2 = jnp.concatenate([bd, bd], axis=0)
    t_att = min(ATT_T, s)
    uu = (jnp.arange(t_att)[:, None] >= jnp.arange(t_att)[None, :]).astype(BF16)
    tril = jnp.tril(jnp.ones((GMLP_CHUNK, GMLP_CHUNK), dtype=bool))

    x2 = x.reshape(n, d)
    for l in range(depth):
        qg = row(jnp.tile(q_norm_g[l], GROUP_HEADS) * (HEAD_DIM ** -0.5))
        kg = row(jnp.tile(k_norm_g[l], GROUP_HEADS))
        gn4 = group_norm_g[l].reshape(N_GROUPS, W).astype(F32)
        cw = jnp.zeros((CONF_HALO, W), F32).at[:CONF_KERNEL].set(conf_dw_w[l])
        conf = (cw, row(conf_dw_b[l]), row(conf_ln_g[l]), row(conf_ln_b[l]))
        ws = jnp.concatenate(list(jnp.where(tril, gmlp_ws[l], 0.0).astype(BF16)), axis=1)
        bs_mat = jnp.repeat(gmlp_bs[l].T, HEAD_DIM, axis=1)
        gmlp = (row(gmlp_ln_g[l]), row(gmlp_ln_b[l]), ws, bs_mat)
        lw = jnp.zeros((8, W), F32).at[:LRU_CONV].set(lru_conv_w[l])
        wax = jnp.concatenate([_block_diag(lru_wa[l]), _block_diag(lru_wx[l])], axis=1).astype(BF16)
        lru = (lw, row(lru_conv_b[l]), wax, row(lru_ba[l]), row(lru_bx[l]), row(lru_lambda[l]))
        q, k, v, y_b, y_c, y_d = _in_mix(x2, s, row(norm1_g[l]), w_in[l].astype(BF16), qg, kg, bd2,
                                         conf, gmlp, lru, gn4)
        to3 = lambda t: t.reshape(bsz, s, t.shape[-1])
        zmax = (ATT_ZMAX_SLACK * HEAD_DIM ** 0.5 * jnp.max(jnp.abs(q_norm_g[l] * k_norm_g[l]))).reshape(1)
        y_a = _sb_attention(zmax.astype(F32), to3(q), to3(k), to3(v), uu, gn4[0:1]).reshape(n, W)
        ys = [y_a, y_b, y_c, y_d]
        j = l // 2
        if l % 2 == 0:
            x2 = _out_proj_dense_ffn(ys, w_out[l].astype(BF16), x2, row(norm2_g[l]), ffn_w_gate[j].astype(BF16),
                                     ffn_w_up[j].astype(BF16), ffn_w_down[j].astype(BF16))
        else:
            x2 = _top2_moe(ys, w_out[l].astype(BF16), x2, row(norm2_g[l]), moe_router[j], moe_w_gate[j],
                           moe_w_up[j], moe_w_down[j])
    return x2.reshape(bsz, s, d)
```

```python
import functools

import jax
import jax.numpy as jnp
from jax import lax
from jax.experimental import pallas as pl
from jax.experimental.pallas import tpu as pltpu
from jax.experimental.pallas import tpu_sc as plsc

F32 = jnp.float32
BF16 = jnp.bfloat16

HEAD_DIM = 64
GROUP_HEADS = 4
GROUP_WIDTH = GROUP_HEADS * HEAD_DIM
N_GROUPS = 4
CONF_KERNEL = 31
GMLP_CHUNK = 128
LRU_CONV = 4
LRU_C = 8.0
N_EXPERTS = 8
EPS = 1e-6

LANES = 128
VMEM_LIMIT = 56 * 1024 * 1024

IN_TM = 1024
MIX_T = 256
ATT_T = 256
ATT_UNDERFLOW = 110.0
ATT_ZMAX_SLACK = 1.05
CONF_R = 32
CONF_HALO = 32
LRU_HALO = 8
FFN_TM = 1024
FFN_TF = 256
ROUTE_TM = 1024
ROUTE_IDX_COLS = 8
MOE_TM = 1024
MOE_TF = 512
COMB_TM = 1024
COMBINE_PARTS = 2
SC_WINDOW_BYTES = 128 * 1024
SC_MAX_INDEX_VECTOR = 128


def _params(*sem):
    return pltpu.CompilerParams(dimension_semantics=sem, vmem_limit_bytes=VMEM_LIMIT)


def _split_hi_lo(x):
    hi = x.astype(BF16)
    lo = (x - hi.astype(F32)).astype(BF16)
    return hi, lo


SOFTPLUS_CLAMP = 60.0


def _softplus(z):
    return jnp.maximum(jnp.log(1.0 + jnp.exp(jnp.minimum(z, SOFTPLUS_CLAMP))), z)


def _gelu_tanh(x):
    c = 0.7978845608028654
    return 0.5 * x * (1.0 + jnp.tanh(c * (x + 0.044715 * (x * x * x))))


def _pack_bf16_pairs(y):
    c = y.shape[1] // 2
    bits = lax.bitcast_convert_type(y.astype(BF16).astype(F32), jnp.uint32)
    word = (bits[:, :c] >> 16) | bits[:, c:]
    return lax.bitcast_convert_type(word, jnp.int32)


def _unpack_bf16_pairs(w):
    bits = lax.bitcast_convert_type(w, jnp.uint32)
    lo = lax.bitcast_convert_type(bits << 16, F32)
    hi = lax.bitcast_convert_type(bits & jnp.uint32(0xFFFF0000), F32)
    return jnp.concatenate([lo, hi], axis=1)


def _group_rms(y, g):
    ms = jnp.mean(y * y, axis=-1, keepdims=True)
    return y * lax.rsqrt(ms + EPS) * g


def _sb_attn_kernel(zmax_ref, q_ref, k_ref, v_ref, uu_ref, gn_ref, o_ref):
    T = q_ref.shape[1]
    n_pairs = GROUP_WIDTH // LANES
    i = pl.program_id(1)
    lane = lax.broadcasted_iota(jnp.int32, (T, LANES), 1)
    rows = lax.broadcasted_iota(jnp.int32, (2 * T, T), 0)
    cols = lax.broadcasted_iota(jnp.int32, (2 * T, T), 1)
    causal = cols < jnp.where(rows >= T, rows - T, rows)
    uu = uu_ref[...]

    def tiles(r0, carries, mask):
        ps = range(n_pairs)
        cs = [slice(p * LANES, (p + 1) * LANES) for p in ps]
        z = [lax.dot_general(qs[p], k_ref[0, pl.ds(r0, T), cs[p]], (((1,), (1,)), ((), ())),
                             preferred_element_type=F32) for p in ps]
        sp = [_softplus(z[p]) for p in ps]
        if mask is not None:
            sp = [jnp.where(mask, s, 0.0) for s in sp]
        cum = [jnp.dot(sp[p].astype(BF16), uu, preferred_element_type=F32) for p in ps]
        w = [jnp.exp(z[p] - cum[p] - carries[p]) for p in ps]
        if mask is not None:
            w = [jnp.where(mask, x, 0.0) for x in w]
        pv = [jnp.dot(w[p].astype(BF16), v_ref[0, pl.ds(r0, T), cs[p]], preferred_element_type=F32)
              for p in ps]
        return pv, [carries[p] + jnp.sum(sp[p], axis=1, keepdims=True) for p in ps]

    qs = []
    for pair in range(n_pairs):
        q2 = q_ref[0, :, pair * LANES:(pair + 1) * LANES]
        zero = jnp.zeros_like(q2)
        qs.append(jnp.concatenate([jnp.where(lane < HEAD_DIM, q2, zero),
                                   jnp.where(lane >= HEAD_DIM, q2, zero)], axis=0))
    row0 = pl.multiple_of(i * T, T)
    state = tuple(zip(*tiles(row0, [jnp.zeros((2 * T, 1), F32)] * n_pairs, causal)))

    dead_at = zmax_ref[0] + ATT_UNDERFLOW

    def alive(st):
        return functools.reduce(jnp.minimum, [jnp.min(carry) for _, carry in st]) <= dead_at

    def cond(c):
        step, live, _ = c
        return (step < i) & live

    def body(c):
        step, _, st = c
        r0 = pl.multiple_of((i - 1 - step) * T, T)
        pv, carries = tiles(r0, [carry for _, carry in st], None)
        new = tuple((st[p][0] + pv[p], carries[p]) for p in range(n_pairs))
        return step + 1, alive(new), new

    _, _, state = lax.while_loop(cond, body, (jnp.int32(0), alive(state), state))
    y = jnp.concatenate([jnp.where(lane < HEAD_DIM, acc[:T], acc[T:]) for acc, _ in state], axis=1)
    o_ref[0] = _group_rms(y, gn_ref[...]).astype(o_ref.dtype)


def _sb_attention(zmax, q3, k3, v3, uu, gn):
    b, s, w = q3.shape
    t = min(ATT_T, s)
    return pl.pallas_call(
        _sb_attn_kernel,
        grid=(b, s // t),
        in_specs=[
            pl.BlockSpec(memory_space=pltpu.SMEM),
            pl.BlockSpec((1, t, w), lambda bi, i: (bi, i, 0)),
            pl.BlockSpec((1, s, w), lambda bi, i: (bi, 0, 0)),
            pl.BlockSpec((1, s, w), lambda bi, i: (bi, 0, 0)),
            pl.BlockSpec((t, t), lambda bi, i: (0, 0)),
            pl.BlockSpec((1, w), lambda bi, i: (0, 0)),
        ],
        out_specs=pl.BlockSpec((1, t, w), lambda bi, i: (bi, i, 0)),
        out_shape=jax.ShapeDtypeStruct((b, s, w), BF16),
        compiler_params=_params("arbitrary", "arbitrary"),
        name="sb_attn",
    )(zmax, q3, k3, v3, uu, gn)


def _conformer_rows(val, gate, w_ref, b_ref, lng_ref, lnb_ref, gn, o_ref, row0, hs_ref, sh_ref):
    T = val.shape[0]
    H = CONF_HALO
    SUB = 8
    hs_ref[H:H + T, :] = val * jax.nn.sigmoid(gate)
    off = H - (CONF_KERNEL - 1)
    L = sh_ref.shape[1]
    for s in range(1, SUB):
        sh_ref[s - 1] = hs_ref[s:s + L, :]
    R = min(CONF_R, T)
    taps_w = [jnp.broadcast_to(w_ref[k:k + 1, :], (R, GROUP_WIDTH)) for k in range(CONF_KERNEL)]
    for r0 in range(0, T, R):
        acc = jnp.broadcast_to(b_ref[...], (R, GROUP_WIDTH))
        for k in range(CONF_KERNEL):
            s = (off + k) % SUB
            a = off + k - s + r0
            tap = hs_ref[a:a + R, :] if s == 0 else sh_ref[s - 1, a:a + R, :]
            acc = acc + taps_w[k] * tap
        mu = jnp.mean(acc, axis=-1, keepdims=True)
        xc = acc - mu
        var = jnp.mean(xc * xc, axis=-1, keepdims=True)
        y = xc * lax.rsqrt(var + EPS) * lng_ref[...] + lnb_ref[...]
        y = y * jax.nn.sigmoid(y)
        o_ref[row0 + r0:row0 + r0 + R, :] = _group_rms(y, gn).astype(o_ref.dtype)
    hs_ref[0:H, :] = hs_ref[T:T + H, :]


def _gmlp_rows(u_in, v_in, lng_ref, lnb_ref, ws_ref, bs_ref, gn, o_ref, row0):
    C = GMLP_CHUNK
    lane = lax.broadcasted_iota(jnp.int32, (C, GROUP_WIDTH), 1)
    u = _gelu_tanh(u_in)
    v = _gelu_tanh(v_in)
    mu = jnp.mean(v, axis=-1, keepdims=True)
    xc = v - mu
    var = jnp.mean(xc * xc, axis=-1, keepdims=True)
    vb = (xc * lax.rsqrt(var + EPS) * lng_ref[...] + lnb_ref[...]).astype(BF16)
    zero = jnp.zeros_like(vb)
    vstack = jnp.concatenate(
        [jnp.where((lane >= h * HEAD_DIM) & (lane < (h + 1) * HEAD_DIM), vb, zero) for h in range(GROUP_HEADS)],
        axis=0)
    mixed = bs_ref[...] + jnp.dot(ws_ref[...], vstack, preferred_element_type=F32)
    o_ref[row0:row0 + C, :] = _group_rms(u * mixed, gn).astype(o_ref.dtype)


def _lru_rows(x_in, gate, cw_ref, cb_ref, wax_ref, ba_ref, bx_ref, lam_ref, gn, o_ref, row0, xs_ref, hprev_ref):
    T = x_in.shape[0]
    W = GROUP_WIDTH
    H = LRU_HALO
    xs_ref[H:H + T, :] = x_in
    off = H - (LRU_CONV - 1)
    xb = jnp.broadcast_to(cb_ref[...], (T, W))
    for k in range(LRU_CONV):
        xb = xb + cw_ref[k:k + 1, :] * xs_ref[off + k:off + k + T, :]
    xs_ref[0:H, :] = xs_ref[T:T + H, :]

    pre = jnp.dot(xb.astype(BF16), wax_ref[...], preferred_element_type=F32)
    r = jax.nn.sigmoid(pre[:, :W] + ba_ref[...])
    ig = jax.nn.sigmoid(pre[:, W:] + bx_ref[...])
    nlam = -lam_ref[...]
    log_a = (-LRU_C) * r * (jnp.maximum(nlam, 0.0) + jnp.log1p(jnp.exp(-jnp.abs(nlam))))
    a = jnp.exp(log_a)
    th = jnp.tanh(log_a)
    one_minus_a2 = 2.0 * th / (th - 1.0)
    b = jnp.sqrt(one_minus_a2) * (ig * xb)

    SUB = 8
    a = a.reshape(T // SUB, SUB, W)
    b = b.reshape(T // SUB, SUB, W)
    sub = lax.broadcasted_iota(jnp.int32, a.shape, 1)
    d = 1
    while d < SUB:
        keep = sub >= d
        a_sh = jnp.where(keep, pltpu.roll(a, d, 1), 1.0)
        b_sh = jnp.where(keep, pltpu.roll(b, d, 1), 0.0)
        b = a * b_sh + b
        a = a * a_sh
        d *= 2
    carry = hprev_ref[0:1, :]
    tiles = []
    for g in range(T // SUB):
        hg = b[g] + a[g] * carry
        carry = hg[SUB - 1:SUB]
        tiles.append(hg)
    h = jnp.concatenate(tiles, axis=0)
    hprev_ref[...] = jnp.broadcast_to(carry, hprev_ref.shape)
    y = h * _gelu_tanh(gate)
    o_ref[row0:row0 + T, :] = _group_rms(y, gn).astype(o_ref.dtype)


def _in_mix_kernel(blocks_per_seq, x_ref, g_ref, w_ref, qg_ref, kg_ref, bd_ref,
                   cw_ref, cb_ref, clg_ref, clb_ref,
                   glg_ref, glb_ref, ws_ref, bs_ref,
                   lw_ref, lb_ref, wax_ref, ba_ref, bx_ref, lam_ref, gn_ref,
                   q_ref, k_ref, v_ref, yb_ref, yc_ref, yd_ref,
                   rest_ref, hs_ref, sh_ref, xs_ref, hprev_ref):
    tm = x_ref.shape[0]
    W = GROUP_WIDTH

    @pl.when(pl.program_id(0) % blocks_per_seq == 0)
    def _():
        hs_ref[0:CONF_HALO, :] = jnp.zeros((CONF_HALO, W), F32)
        xs_ref[0:LRU_HALO, :] = jnp.zeros((LRU_HALO, W), F32)
        hprev_ref[...] = jnp.zeros(hprev_ref.shape, F32)

    def step(rest_w, rest_r):
        xf = x_ref[...]
        ms = jnp.mean(xf * xf, axis=-1, keepdims=True)
        h = (xf * lax.rsqrt(ms + EPS) * g_ref[...]).astype(BF16)

        def proj(c):
            return jnp.dot(h, w_ref[:, c * W:(c + 1) * W], preferred_element_type=F32)

        def head_norm(p, gain):
            hi, lo = _split_hi_lo(p * p)
            ss = jnp.dot(jnp.concatenate([hi, lo], axis=1), bd_ref[...], preferred_element_type=F32)
            return p * lax.rsqrt(ss * (1.0 / HEAD_DIM) + EPS) * gain

        def project(c):
            if c == 0:
                q_ref[...] = head_norm(proj(0), qg_ref[...]).astype(BF16)
            elif c == 1:
                k_ref[...] = head_norm(proj(1), kg_ref[...]).astype(BF16)
            elif c == 2:
                v_ref[...] = proj(2).astype(BF16)
            else:
                rest_w[:, (c - 3) * W:(c - 2) * W] = proj(c)

        def cols(r0, n, c):
            return rest_r[r0:r0 + n, c * W:(c + 1) * W]

        T = min(MIX_T, tm)

        def conformer(r0):
            _conformer_rows(cols(r0, T, 0), cols(r0, T, 1), cw_ref, cb_ref, clg_ref, clb_ref, gn_ref[1:2, :],
                            yb_ref, r0, hs_ref, sh_ref)

        def lru(r0):
            _lru_rows(cols(r0, T, 4), cols(r0, T, 5), lw_ref, lb_ref, wax_ref, ba_ref, bx_ref, lam_ref,
                      gn_ref[3:4, :], yd_ref, r0, xs_ref, hprev_ref)

        def gmlp(r0):
            _gmlp_rows(cols(r0, GMLP_CHUNK, 2), cols(r0, GMLP_CHUNK, 3), glg_ref, glb_ref, ws_ref, bs_ref,
                       gn_ref[2:3, :], yc_ref, r0)

        mix = []
        for r0 in range(0, tm, T):
            mix.extend(functools.partial(gmlp, c0) for c0 in range(r0, r0 + T, GMLP_CHUNK))
            mix.append(functools.partial(lru, r0))
            mix.append(functools.partial(conformer, r0))
        n_proj = w_ref.shape[1] // W
        for c in range(3, n_proj):
            project(c)
        per = -(-len(mix) // 4)
        for c in range(4):
            if c > 0:
                project(c - 1)
            for task in mix[c * per:(c + 1) * per]:
                task()

    step(rest_ref, rest_ref)


def _in_mix(x2, seq_len, g, w_bf, qg, kg, bd2, conf, gmlp, lru, gn4):
    n, d = x2.shape
    cols = w_bf.shape[1]
    W = GROUP_WIDTH
    rest_cols = cols - 3 * W
    tm = min(IN_TM, seq_len)
    assert seq_len % tm == 0 and tm % min(MIX_T, tm) == 0 and tm % GMLP_CHUNK == 0
    t = min(MIX_T, tm)
    row = lambda i: (i, 0)
    whole = lambda a: pl.BlockSpec(a.shape, lambda i: (0,) * a.ndim)
    params = (g, w_bf, qg, kg, bd2) + tuple(conf) + tuple(gmlp) + tuple(lru) + (gn4,)
    out_shape = jax.ShapeDtypeStruct((n, W), BF16)
    return pl.pallas_call(
        functools.partial(_in_mix_kernel, seq_len // tm),
        grid=(n // tm,),
        in_specs=[pl.BlockSpec((tm, d), row)] + [whole(a) for a in params],
        out_specs=[pl.BlockSpec((tm, W), row)] * 6,
        out_shape=[out_shape] * 6,
        scratch_shapes=[pltpu.VMEM((tm, rest_cols), F32),
                        pltpu.VMEM((t + CONF_HALO, W), F32),
                        pltpu.VMEM((7, t + CONF_HALO - 8, W), F32),
                        pltpu.VMEM((t + LRU_HALO, W), F32),
                        pltpu.VMEM((8, W), F32)],
        compiler_params=_params("arbitrary"),
        name="in_mix",
    )(x2, *params)


def _out_proj_kernel(ya_ref, yb_ref, yc_ref, yd_ref, w_ref, x_ref, g_ref, rh_ref, rl_ref, tri_ref,
                     xo_ref, h_ref, route_ref, idx_ref, cnt_ref, base_ref):
    W = GROUP_WIDTH

    @pl.when(pl.program_id(0) == 0)
    def _():
        base_ref[...] = jnp.zeros(base_ref.shape, F32)

    acc = x_ref[...]
    for gi, y_ref in enumerate((ya_ref, yb_ref, yc_ref, yd_ref)):
        acc = acc + jnp.dot(y_ref[...], w_ref[gi * W:(gi + 1) * W, :], preferred_element_type=F32)
    xo_ref[...] = acc
    ms = jnp.mean(acc * acc, axis=-1, keepdims=True)
    h = acc * lax.rsqrt(ms + EPS) * g_ref[...]
    h_ref[...] = _pack_bf16_pairs(h)
    _route_rows(h, rh_ref, rl_ref, tri_ref, route_ref, idx_ref, cnt_ref, base_ref)


def _out_proj_route(ys, w_bf, x2, g, rh, rl, tri):
    n, d = x2.shape
    W = GROUP_WIDTH
    tm = tri.shape[0]
    row = lambda i: (i, 0)
    const = lambda i: (0, 0)
    ysp = pl.BlockSpec((tm, W), row)
    return pl.pallas_call(
        _out_proj_kernel,
        grid=(n // tm,),
        in_specs=[ysp, ysp, ysp, ysp,
                  pl.BlockSpec(w_bf.shape, const),
                  pl.BlockSpec((tm, d), row),
                  pl.BlockSpec((1, d), const),
                  pl.BlockSpec((d, LANES), const),
                  pl.BlockSpec((d, LANES), const),
                  pl.BlockSpec((tm, tm), const)],
        out_specs=[pl.BlockSpec((tm, d), row),
                   pl.BlockSpec((tm, d // 2), row),
                   pl.BlockSpec((tm, LANES), row),
                   pl.BlockSpec((ROUTE_IDX_COLS, tm), lambda i: (0, i)),
                   pl.BlockSpec((8, LANES), const)],
        out_shape=[jax.ShapeDtypeStruct((n, d), F32),
                   jax.ShapeDtypeStruct((n, d // 2), jnp.int32),
                   jax.ShapeDtypeStruct((n, LANES), F32),
                   jax.ShapeDtypeStruct((ROUTE_IDX_COLS, n), jnp.int32),
                   jax.ShapeDtypeStruct((8, LANES), F32)],
        scratch_shapes=[pltpu.VMEM((8, LANES), F32)],
        compiler_params=_params("arbitrary"),
        name="out_proj_route",
    )(*ys, w_bf, x2, g, rh, rl, tri)


def _out_ffn_kernel(ya_ref, yb_ref, yc_ref, yd_ref, wo_ref, x_ref, g_ref, wg_ref, wu_ref, wd_ref, o_ref,
                    xn_ref, act_ref):
    W = GROUP_WIDTH
    f = wg_ref.shape[1]
    acc = x_ref[...]
    for gi, y_ref in enumerate((ya_ref, yb_ref, yc_ref, yd_ref)):
        acc = acc + jnp.dot(y_ref[...], wo_ref[gi * W:(gi + 1) * W, :], preferred_element_type=F32)
    xn_ref[...] = acc
    ms = jnp.mean(acc * acc, axis=-1, keepdims=True)
    h = (acc * lax.rsqrt(ms + EPS) * g_ref[...]).astype(BF16)
    for c0 in range(0, f, FFN_TF):
        g = jnp.dot(h, wg_ref[:, c0:c0 + FFN_TF], preferred_element_type=F32)
        u = jnp.dot(h, wu_ref[:, c0:c0 + FFN_TF], preferred_element_type=F32)
        act_ref[:, c0:c0 + FFN_TF] = (g * jax.nn.sigmoid(g) * u).astype(BF16)
    o_ref[...] = xn_ref[...] + jnp.dot(act_ref[...], wd_ref[...], preferred_element_type=F32)


def _out_proj_dense_ffn(ys, wo, x2, g, wg, wu, wd):
    n, d = x2.shape
    f = wg.shape[1]
    W = GROUP_WIDTH
    assert f % FFN_TF == 0
    tm = min(FFN_TM, n)
    row = lambda i: (i, 0)
    resident = dict(pipeline_mode=pl.Buffered(1))
    ysp = pl.BlockSpec((tm, W), row)
    return pl.pallas_call(
        _out_ffn_kernel,
        grid=(n // tm,),
        in_specs=[
            ysp, ysp, ysp, ysp,
            pl.BlockSpec(wo.shape, lambda i: (0, 0), **resident),
            pl.BlockSpec((tm, d), row),
            pl.BlockSpec((1, d), lambda i: (0, 0)),
            pl.BlockSpec((d, f), lambda i: (0, 0), **resident),
            pl.BlockSpec((d, f), lambda i: (0, 0), **resident),
            pl.BlockSpec((f, d), lambda i: (0, 0), **resident),
        ],
        out_specs=pl.BlockSpec((tm, d), row),
        out_shape=jax.ShapeDtypeStruct((n, d), F32),
        scratch_shapes=[pltpu.VMEM((tm, d), F32), pltpu.VMEM((tm, f), BF16)],
        compiler_params=_params("arbitrary"),
        name="out_proj_dense_ffn",
    )(*ys, wo, x2, g, wg, wu, wd)


def _route_rows(h, rh_ref, rl_ref, tri_ref, route_ref, idx_ref, cnt_ref, base_ref):
    tm = h.shape[0]
    hh, hl = _split_hi_lo(h)
    rh = rh_ref[...]
    logits = (jnp.dot(hh, rh, preferred_element_type=F32)
              + jnp.dot(hl, rh, preferred_element_type=F32)
              + jnp.dot(hh, rl_ref[...], preferred_element_type=F32))
    lane = lax.broadcasted_iota(jnp.int32, (tm, LANES), 1).astype(F32)
    neg = jnp.float32(-jnp.inf)
    logits = jnp.where(lane < N_EXPERTS, logits, neg)
    m1 = jnp.max(logits, axis=1, keepdims=True)
    i1 = jnp.min(jnp.where(logits == m1, lane, float(LANES)), axis=1, keepdims=True)
    l2 = jnp.where(lane == i1, neg, logits)
    m2 = jnp.max(l2, axis=1, keepdims=True)
    i2 = jnp.min(jnp.where(l2 == m2, lane, float(LANES)), axis=1, keepdims=True)
    e = jnp.exp(m2 - m1)
    g1 = 1.0 / (1.0 + e)
    g2 = e / (1.0 + e)
    oh1 = jnp.where(lane == i1, 1.0, 0.0)
    oh2 = jnp.where(lane == i2, 1.0, 0.0)
    oh = oh1 + oh2
    before = jnp.dot(tri_ref[...], oh.astype(BF16), preferred_element_type=F32) + base_ref[0:1, :]
    r1 = jnp.sum(oh1 * before, axis=1, keepdims=True)
    r2 = jnp.sum(oh2 * before, axis=1, keepdims=True)
    base = base_ref[0:1, :] + jnp.sum(oh, axis=0, keepdims=True)
    base_ref[...] = jnp.broadcast_to(base, base_ref.shape)
    cnt_ref[...] = jnp.broadcast_to(base, cnt_ref.shape)
    out = jnp.where(lane == 0, i1, 0.0)
    out = jnp.where(lane == 1, i2, out)
    out = jnp.where(lane == 2, g1, out)
    out = jnp.where(lane == 3, g2, out)
    out = jnp.where(lane == 4, r1, out)
    out = jnp.where(lane == 5, r2, out)
    route_ref[...] = out
    hi = jnp.floor(out * (1.0 / 256.0))
    lo = out - 256.0 * hi
    sel = (lax.broadcasted_iota(jnp.int32, (2 * ROUTE_IDX_COLS, LANES), 0)
           == lax.broadcasted_iota(jnp.int32, (2 * ROUTE_IDX_COLS, LANES), 1)).astype(BF16)
    nt = (((1,), (1,)), ((), ()))
    hi_t = lax.dot_general(sel, hi.astype(BF16), nt, preferred_element_type=F32)
    lo_t = lax.dot_general(sel, lo.astype(BF16), nt, preferred_element_type=F32)
    idx_ref[...] = (256.0 * hi_t + lo_t)[:ROUTE_IDX_COLS].astype(jnp.int32)


def _sc_window_rows(table):
    row_bytes = table.shape[1] * table.dtype.itemsize
    return min(SC_MAX_INDEX_VECTOR, SC_WINDOW_BYTES // row_bytes)


def _sc_gather_rows(table, idx):
    _, d = table.shape
    b = idx.shape[0]
    win = _sc_window_rows(table)
    sc = plsc.get_sparse_core_info()
    n_workers = sc.num_cores * sc.num_subcores
    per_w = b // n_workers
    n_win = per_w // win
    assert per_w * n_workers == b and n_win * win == per_w and n_win % 2 == 0, (b, n_workers, win)
    mesh = plsc.VectorSubcoreMesh(core_axis_name="c", subcore_axis_name="s")
    dma = pltpu.SemaphoreType.DMA

    @functools.partial(
        pl.kernel, mesh=mesh, out_type=jax.ShapeDtypeStruct((b, d), table.dtype), name="sc_gather_rows",
        scratch_types=[pltpu.VMEM((win,), jnp.int32), pltpu.VMEM((win,), jnp.int32),
                       pltpu.VMEM((win, d), table.dtype), pltpu.VMEM((win, d), table.dtype),
                       dma, dma, dma, dma])
    def gather_kernel(table_hbm, idx_hbm, out_hbm, i0, i1, r0, r1, g0, g1, w0, w1):
        wid = lax.axis_index("s") * sc.num_cores + lax.axis_index("c")
        base = wid * per_w
        idxb, rows, gsem, wsem = (i0, i1), (r0, r1), (g0, g1), (w0, w1)

        def off(c):
            return pl.multiple_of(base + c * win, win)

        def gather(s):
            return pltpu.make_async_copy(table_hbm.at[idxb[s]], rows[s], gsem[s])

        def write(c, s):
            return pltpu.make_async_copy(rows[s], out_hbm.at[pl.ds(off(c), win)], wsem[s])

        pltpu.sync_copy(idx_hbm.at[pl.ds(off(0), win)], idxb[0])
        gather(0).start()

        @pl.loop(0, n_win, step=2)
        def _(c):
            for s in (0, 1):
                cc = c + s

                @pl.when(cc + 1 < n_win)
                def _():
                    @pl.when(cc >= 1)
                    def _():
                        write(cc - 1, 1 - s).wait()

                    pltpu.sync_copy(idx_hbm.at[pl.ds(off(cc + 1), win)], idxb[1 - s])
                    gather(1 - s).start()

                gather(s).wait()
                write(cc, s).start()

        write(n_win - 2, 0).wait()
        write(n_win - 1, 1).wait()

    return gather_kernel(table, idx)


def _sc_dispatch_rows(src, dest1, dest2, pad_idx, n_rows):
    n, d = src.shape
    win = _sc_window_rows(src)
    sc = plsc.get_sparse_core_info()
    n_workers = sc.num_cores * sc.num_subcores
    per_w = n // n_workers
    n_win = per_w // win
    assert per_w * n_workers == n and n_win * win == per_w and n_win % 2 == 0, (n, n_workers, win)
    n_pad = pad_idx.shape[0]
    assert n_pad + 2 * n == n_rows and n_pad % (n_workers * win) == 0, (n_pad, n, n_rows)
    pad_per_w = n_pad // n_workers
    zero_rows = jnp.zeros((win, d), src.dtype)
    mesh = plsc.VectorSubcoreMesh(core_axis_name="c", subcore_axis_name="s")
    dma = pltpu.SemaphoreType.DMA
    ivec = pltpu.VMEM((win,), jnp.int32)
    rbuf = pltpu.VMEM((win, d), src.dtype)

    @functools.partial(
        pl.kernel, mesh=mesh, out_type=jax.ShapeDtypeStruct((n_rows, d), src.dtype), name="sc_dispatch_rows",
        scratch_types=[ivec, ivec, ivec, ivec, rbuf, rbuf, dma, dma, dma, dma, dma, dma])
    def dispatch_kernel(src_hbm, d1_hbm, d2_hbm, pad_hbm, zero_hbm, out_hbm,
                        a0, a1, b0, b1, r0, r1, l0, l1, p0, p1, q0, q1):
        wid = lax.axis_index("s") * sc.num_cores + lax.axis_index("c")
        base = wid * per_w
        ia, ib, rows, lsem, psem, qsem = (a0, a1), (b0, b1), (r0, r1), (l0, l1), (p0, p1), (q0, q1)

        def off(c):
            return pl.multiple_of(base + c * win, win)

        def load(c, s):
            return pltpu.make_async_copy(src_hbm.at[pl.ds(off(c), win)], rows[s], lsem[s])

        def scatters(s):
            return (pltpu.make_async_copy(rows[s], out_hbm.at[ia[s]], psem[s]),
                    pltpu.make_async_copy(rows[s], out_hbm.at[ib[s]], qsem[s]))

        def fetch(c, s):
            pltpu.sync_copy(d1_hbm.at[pl.ds(off(c), win)], ia[s])
            pltpu.sync_copy(d2_hbm.at[pl.ds(off(c), win)], ib[s])
            load(c, s).start()

        fetch(0, 0)

        @pl.loop(0, n_win, step=2)
        def _(c):
            for s in (0, 1):
                cc = c + s

                @pl.when(cc + 1 < n_win)
                def _():
                    @pl.when(cc >= 1)
                    def _():
                        for cp in scatters(1 - s):
                            cp.wait()

                    fetch(cc + 1, 1 - s)

                load(cc, s).wait()
                for cp in scatters(s):
                    cp.start()

        for s in (0, 1):
            for cp in scatters(s):
                cp.wait()

        pltpu.sync_copy(zero_hbm, rows[0])

        @pl.loop(0, pad_per_w // win)
        def _(c):
            o = pl.multiple_of(wid * pad_per_w + c * win, win)
            pltpu.sync_copy(pad_hbm.at[pl.ds(o, win)], ia[0])
            pltpu.sync_copy(rows[0], out_hbm.at[ia[0]])

    return dispatch_kernel(src, dest1, dest2, pad_idx, zero_rows)


def _moe_kernel(bexp_ref, nused_ref, x_ref, wg_ref, wu_ref, wd_ref, o_ref, xb, acc):
    i = pl.program_id(0)
    j = pl.program_id(1)
    nj = pl.num_programs(1)
    used = i < nused_ref[0]

    @pl.when(used & (j == 0))
    def _():
        xb[...] = _unpack_bf16_pairs(x_ref[...]).astype(BF16)
        acc[...] = jnp.zeros(acc.shape, F32)

    @pl.when(used)
    def _():
        x = xb[...]
        g = jnp.dot(x, wg_ref[0].astype(BF16), preferred_element_type=F32)
        u = jnp.dot(x, wu_ref[0].astype(BF16), preferred_element_type=F32)
        act = (g * jax.nn.sigmoid(g) * u).astype(BF16)
        acc[...] += jnp.dot(act, wd_ref[0].astype(BF16), preferred_element_type=F32)

        @pl.when(j == nj - 1)
        def _():
            o_ref[...] = _pack_bf16_pairs(acc[...])

    @pl.when(jnp.logical_not(used) & (j == 0))
    def _():
        o_ref[...] = jnp.zeros(o_ref.shape, o_ref.dtype)


def _moe_experts(block_exp, n_used, xs, wg, wu, wd, tm):
    n_rows, dp = xs.shape
    d = 2 * dp
    n_blocks = n_rows // tm
    f = wg.shape[2]
    tf = MOE_TF if f % MOE_TF == 0 else f
    nj = f // tf
    assert nj >= 2

    def live(i, j, be, nu):
        u = i < nu[0]
        return jnp.where(u, i, nu[0] - 1), jnp.where(u, j, nj - 1)

    def w_in_map(i, j, be, nu):
        ii, jj = live(i, j, be, nu)
        return (be[ii], 0, jj)

    def w_down_map(i, j, be, nu):
        ii, jj = live(i, j, be, nu)
        return (be[ii], jj, 0)

    def x_map(i, j, be, nu):
        return (live(i, j, be, nu)[0], 0)

    def out_map(i, j, be, nu):
        return (i, 0)

    grid_spec = pltpu.PrefetchScalarGridSpec(
        num_scalar_prefetch=2,
        grid=(n_blocks, nj),
        in_specs=[
            pl.BlockSpec((tm, dp), x_map),
            pl.BlockSpec((1, d, tf), w_in_map),
            pl.BlockSpec((1, d, tf), w_in_map),
            pl.BlockSpec((1, tf, d), w_down_map),
        ],
        out_specs=pl.BlockSpec((tm, dp), out_map),
        scratch_shapes=[pltpu.VMEM((tm, d), BF16), pltpu.VMEM((tm, d), F32)],
    )
    return pl.pallas_call(
        _moe_kernel,
        grid_spec=grid_spec,
        out_shape=jax.ShapeDtypeStruct((n_rows, dp), jnp.int32),
        compiler_params=_params("arbitrary", "arbitrary"),
        name="moe_experts",
    )(block_exp, n_used, xs, wg, wu, wd)


def _combine_kernel(x_ref, route_ref, a_ref, b_ref, o_ref):
    route = route_ref[...]
    ya = _unpack_bf16_pairs(a_ref[...])
    yb = _unpack_bf16_pairs(b_ref[...])
    o_ref[...] = x_ref[...] + route[:, 2:3] * ya + route[:, 3:4] * yb


def _combine(yg, x2, route, part, n_parts):
    n, d = x2.shape
    dp = yg.shape[1]
    tm = min(COMB_TM, n // n_parts)
    nb = n // n_parts // tm
    first = part * nb
    return pl.pallas_call(
        _combine_kernel,
        grid=(nb,),
        in_specs=[pl.BlockSpec((tm, d), lambda i: (i + first, 0)),
                  pl.BlockSpec((tm, LANES), lambda i: (i + first, 0)),
                  pl.BlockSpec((tm, dp), lambda i: (i, 0)),
                  pl.BlockSpec((tm, dp), lambda i: (i + nb, 0))],
        out_specs=pl.BlockSpec((tm, d), lambda i: (i + first, 0)),
        out_shape=jax.ShapeDtypeStruct((n, d), F32),
        input_output_aliases={0: 0},
        compiler_params=_params("arbitrary"),
        name="moe_combine",
    )(x2, route, yg, yg)


def _top2_moe(ys_mix, w_out_bf, x2, g2, router, wg, wu, wd):
    n, d = x2.shape
    a = 2 * n
    tm = min(MOE_TM, n)
    r_pad = jnp.zeros((d, LANES), F32).at[:, :N_EXPERTS].set(router)
    rh, rl = _split_hi_lo(r_pad)
    rt = min(ROUTE_TM, n)
    tri = (lax.broadcasted_iota(jnp.int32, (rt, rt), 1) < lax.broadcasted_iota(jnp.int32, (rt, rt), 0)).astype(BF16)
    x2, hn, route, cols, cnt = _out_proj_route(ys_mix, w_out_bf, x2, g2, rh, rl, tri)

    counts = cnt[0, :N_EXPERTS].astype(jnp.int32)
    padded = (counts + tm - 1) // tm * tm
    pad_ends = jnp.cumsum(padded)
    pad_starts = pad_ends - padded
    def lookup(table, idx):
        hit = idx[None, :] == jnp.arange(N_EXPERTS, dtype=jnp.int32)[:, None]
        return jnp.sum(jnp.where(hit, table[:, None], 0), axis=0)

    dest1 = lookup(pad_starts, cols[0]) + cols[4]
    dest2 = lookup(pad_starts, cols[1]) + cols[5]
    n_blocks = a // tm + N_EXPERTS
    n_rows = n_blocks * tm
    block_start = jnp.arange(n_blocks, dtype=jnp.int32) * tm
    block_exp = jnp.minimum(jnp.sum(block_start[None, :] >= pad_ends[:, None], axis=0), N_EXPERTS - 1).astype(jnp.int32)
    n_used = (pad_ends[-1] // tm).astype(jnp.int32).reshape(1)
    pad_cnt = padded - counts
    pad_cum = jnp.cumsum(pad_cnt)
    k = jnp.arange(n_rows - a, dtype=jnp.int32)
    seg = jnp.sum(k[None, :] >= pad_cum[:, None], axis=0).astype(jnp.int32)
    in_expert = lookup(pad_starts + counts - (pad_cum - pad_cnt), jnp.minimum(seg, N_EXPERTS - 1)) + k
    pad_idx = jnp.where(seg < N_EXPERTS, in_expert, pad_ends[-1] + k - pad_cum[-1])

    xs = _sc_dispatch_rows(hn, dest1, dest2, pad_idx, n_rows)
    ys = _moe_experts(block_exp, n_used, xs, wg, wu, wd, tm)
    n_parts = COMBINE_PARTS if n % (COMBINE_PARTS * COMB_TM) == 0 else 1
    step = n // n_parts
    gathered = [_sc_gather_rows(ys, jnp.concatenate([dest1[p * step:(p + 1) * step], dest2[p * step:(p + 1) * step]]))
                for p in range(n_parts)]
    for p in range(n_parts):
        x2 = _combine(gathered[p], x2, route, p, n_parts)
    return x2


def _block_diag(w):
    h, dh, _ = w.shape
    eye = jnp.eye(h, dtype=w.dtype)
    return jnp.einsum("hij,hg->higj", w, eye).reshape(h * dh, h * dh)


def kernel(x, norm1_g, w_in, q_norm_g, k_norm_g, conf_dw_w, conf_dw_b, conf_ln_g, conf_ln_b, gmlp_ln_g, gmlp_ln_b, gmlp_ws, gmlp_bs, lru_conv_w, lru_conv_b, lru_wa, lru_ba, lru_wx, lru_bx, lru_lambda, group_norm_g, w_out, norm2_g, ffn_w_gate, ffn_w_up, ffn_w_down, moe_router, moe_w_gate, moe_w_up, moe_w_down):
    bsz, s, d = x.shape
    n = bsz * s
    depth = w_in.shape[0]
    W = GROUP_WIDTH
    row = lambda v: v.reshape(1, -1).astype(F32)

    head_id = jnp.arange(W) // HEAD_DIM
    bd = (head_id[:, None] == head_id[None, :]).astype(BF16)
    bd2 = jnp.concatenate([bd, bd], axis=0)
    t_att = min(ATT_T, s)
    uu = (jnp.arange(t_att)[:, None] >= jnp.arange(t_att)[None, :]).astype(BF16)
    tril = jnp.tril(jnp.ones((GMLP_CHUNK, GMLP_CHUNK), dtype=bool))

    x2 = x.reshape(n, d)
    for l in range(depth):
        qg = row(jnp.tile(q_norm_g[l], GROUP_HEADS) * (HEAD_DIM ** -0.5))
        kg = row(jnp.tile(k_norm_g[l], GROUP_HEADS))
        gn4 = group_norm_g[l].reshape(N_GROUPS, W).astype(F32)
        cw = jnp.zeros((CONF_HALO, W), F32).at[:CONF_KERNEL].set(conf_dw_w[l])
        conf = (cw, row(conf_dw_b[l]), row(conf_ln_g[l]), row(conf_ln_b[l]))
        ws = jnp.concatenate(list(jnp.where(tril, gmlp_ws[l], 0.0).astype(BF16)), axis=1)
        bs_mat = jnp.repeat(gmlp_bs[l].T, HEAD_DIM, axis=1)
        gmlp = (row(gmlp_ln_g[l]), row(gmlp_ln_b[l]), ws, bs_mat)
        lw = jnp.zeros((8, W), F32).at[:LRU_CONV].set(lru_conv_w[l])
        wax = jnp.concatenate([_block_diag(lru_wa[l]), _block_diag(lru_wx[l])], axis=1).astype(BF16)
        lru = (lw, row(lru_conv_b[l]), wax, row(lru_ba[l]), row(lru_bx[l]), row(lru_lambda[l]))
        q, k, v, y_b, y_c, y_d = _in_mix(x2, s, row(norm1_g[l]), w_in[l].astype(BF16), qg, kg, bd2,
                                         conf, gmlp, lru, gn4)
        to3 = lambda t: t.reshape(bsz, s, t.shape[-1])
        zmax = (ATT_ZMAX_SLACK * HEAD_DIM ** 0.5 * jnp.max(jnp.abs(q_norm_g[l] * k_norm_g[l]))).reshape(1)
        y_a = _sb_attention(zmax.astype(F32), to3(q), to3(k), to3(v), uu, gn4[0:1]).reshape(n, W)
        ys = [y_a, y_b, y_c, y_d]
        j = l // 2
        if l % 2 == 0:
            x2 = _out_proj_dense_ffn(ys, w_out[l].astype(BF16), x2, row(norm2_g[l]), ffn_w_gate[j].astype(BF16),
                                     ffn_w_up[j].astype(BF16), ffn_w_down[j].astype(BF16))
        else:
            x2 = _top2_moe(ys, w_out[l].astype(BF16), x2, row(norm2_g[l]), moe_router[j], moe_w_gate[j],
                           moe_w_up[j], moe_w_down[j])
    return x2.reshape(bsz, s, d)
```

```python
import functools

import jax
import jax.numpy as jnp
from jax import lax
from jax.experimental import pallas as pl
from jax.experimental.pallas import tpu as pltpu
from jax.experimental.pallas import tpu_sc as plsc

F32 = jnp.float32
BF16 = jnp.bfloat16

HEAD_DIM = 64
GROUP_HEADS = 4
GROUP_WIDTH = GROUP_HEADS * HEAD_DIM
N_GROUPS = 4
CONF_KERNEL = 31
GMLP_CHUNK = 128
LRU_CONV = 4
LRU_C = 8.0
N_EXPERTS = 8
EPS = 1e-6

LANES = 128
VMEM_LIMIT = 56 * 1024 * 1024

IN_TM = 1024
MIX_T = 256
ATT_T = 256
ATT_UNDERFLOW = 110.0
ATT_ZMAX_SLACK = 1.05
CONF_R = 32
CONF_HALO = 32
LRU_HALO = 8
FFN_TM = 1024
FFN_TF = 256
ROUTE_TM = 1024
ROUTE_SPLIT = 2
ROUTE_IDX_COLS = 8
MOE_TM = 1024
MOE_TF = 512
COMB_TM = 1024
COMBINE_PARTS = 2
SC_WINDOW_BYTES = 128 * 1024
SC_MAX_INDEX_VECTOR = 128


def _params(*sem):
    return pltpu.CompilerParams(dimension_semantics=sem, vmem_limit_bytes=VMEM_LIMIT)


def _split_hi_lo(x):
    hi = x.astype(BF16)
    lo = (x - hi.astype(F32)).astype(BF16)
    return hi, lo


SOFTPLUS_CLAMP = 60.0


def _softplus(z):
    return jnp.maximum(jnp.log(1.0 + jnp.exp(jnp.minimum(z, SOFTPLUS_CLAMP))), z)


def _gelu_tanh(x):
    c = 0.7978845608028654
    return 0.5 * x * (1.0 + jnp.tanh(c * (x + 0.044715 * (x * x * x))))


def _pack_bf16_pairs(y):
    c = y.shape[1] // 2
    bits = lax.bitcast_convert_type(y.astype(BF16).astype(F32), jnp.uint32)
    word = (bits[:, :c] >> 16) | bits[:, c:]
    return lax.bitcast_convert_type(word, jnp.int32)


def _unpack_bf16_pairs(w):
    bits = lax.bitcast_convert_type(w, jnp.uint32)
    lo = lax.bitcast_convert_type(bits << 16, F32)
    hi = lax.bitcast_convert_type(bits & jnp.uint32(0xFFFF0000), F32)
    return jnp.concatenate([lo, hi], axis=1)


def _group_rms(y, g):
    ms = jnp.mean(y * y, axis=-1, keepdims=True)
    return y * lax.rsqrt(ms + EPS) * g


def _sb_attn_kernel(zmax_ref, q_ref, k_ref, v_ref, uu_ref, gn_ref, o_ref):
    T = q_ref.shape[1]
    n_pairs = GROUP_WIDTH // LANES
    i = pl.program_id(1)
    lane = lax.broadcasted_iota(jnp.int32, (T, LANES), 1)
    rows = lax.broadcasted_iota(jnp.int32, (2 * T, T), 0)
    cols = lax.broadcasted_iota(jnp.int32, (2 * T, T), 1)
    causal = cols < jnp.where(rows >= T, rows - T, rows)
    uu = uu_ref[...]

    def tiles(r0, carries, mask):
        ps = range(n_pairs)
        cs = [slice(p * LANES, (p + 1) * LANES) for p in ps]
        z = [lax.dot_general(qs[p], k_ref[0, pl.ds(r0, T), cs[p]], (((1,), (1,)), ((), ())),
                             preferred_element_type=F32) for p in ps]
        sp = [_softplus(z[p]) for p in ps]
        if mask is not None:
            sp = [jnp.where(mask, s, 0.0) for s in sp]
        cum = [jnp.dot(sp[p].astype(BF16), uu, preferred_element_type=F32) for p in ps]
        w = [jnp.exp(z[p] - cum[p] - carries[p]) for p in ps]
        if mask is not None:
            w = [jnp.where(mask, x, 0.0) for x in w]
        pv = [jnp.dot(w[p].astype(BF16), v_ref[0, pl.ds(r0, T), cs[p]], preferred_element_type=F32)
              for p in ps]
        return pv, [carries[p] + jnp.sum(sp[p], axis=1, keepdims=True) for p in ps]

    qs = []
    for pair in range(n_pairs):
        q2 = q_ref[0, :, pair * LANES:(pair + 1) * LANES]
        zero = jnp.zeros_like(q2)
        qs.append(jnp.concatenate([jnp.where(lane < HEAD_DIM, q2, zero),
                                   jnp.where(lane >= HEAD_DIM, q2, zero)], axis=0))
    row0 = pl.multiple_of(i * T, T)
    state = tuple(zip(*tiles(row0, [jnp.zeros((2 * T, 1), F32)] * n_pairs, causal)))

    dead_at = zmax_ref[0] + ATT_UNDERFLOW

    def alive(st):
        return functools.reduce(jnp.minimum, [jnp.min(carry) for _, carry in st]) <= dead_at

    def cond(c):
        step, live, _ = c
        return (step < i) & live

    def body(c):
        step, _, st = c
        r0 = pl.multiple_of((i - 1 - step) * T, T)
        pv, carries = tiles(r0, [carry for _, carry in st], None)
        new = tuple((st[p][0] + pv[p], carries[p]) for p in range(n_pairs))
        return step + 1, alive(new), new

    _, _, state = lax.while_loop(cond, body, (jnp.int32(0), alive(state), state))
    y = jnp.concatenate([jnp.where(lane < HEAD_DIM, acc[:T], acc[T:]) for acc, _ in state], axis=1)
    o_ref[0] = _group_rms(y, gn_ref[...]).astype(o_ref.dtype)


def _sb_attention(zmax, q3, k3, v3, uu, gn):
    b, s, w = q3.shape
    t = min(ATT_T, s)
    return pl.pallas_call(
        _sb_attn_kernel,
        grid=(b, s // t),
        in_specs=[
            pl.BlockSpec(memory_space=pltpu.SMEM),
            pl.BlockSpec((1, t, w), lambda bi, i: (bi, i, 0)),
            pl.BlockSpec((1, s, w), lambda bi, i: (bi, 0, 0)),
            pl.BlockSpec((1, s, w), lambda bi, i: (bi, 0, 0)),
            pl.BlockSpec((t, t), lambda bi, i: (0, 0)),
            pl.BlockSpec((1, w), lambda bi, i: (0, 0)),
        ],
        out_specs=pl.BlockSpec((1, t, w), lambda bi, i: (bi, i, 0)),
        out_shape=jax.ShapeDtypeStruct((b, s, w), BF16),
        compiler_params=_params("arbitrary", "arbitrary"),
        name="sb_attn",
    )(zmax, q3, k3, v3, uu, gn)


def _conformer_rows(val, gate, w_ref, b_ref, lng_ref, lnb_ref, gn, o_ref, row0, hs_ref, sh_ref):
    T = val.shape[0]
    H = CONF_HALO
    SUB = 8
    hs_ref[H:H + T, :] = val * jax.nn.sigmoid(gate)
    off = H - (CONF_KERNEL - 1)
    L = sh_ref.shape[1]
    for s in range(1, SUB):
        sh_ref[s - 1] = hs_ref[s:s + L, :]
    R = min(CONF_R, T)
    taps_w = [jnp.broadcast_to(w_ref[k:k + 1, :], (R, GROUP_WIDTH)) for k in range(CONF_KERNEL)]
    for r0 in range(0, T, R):
        acc = jnp.broadcast_to(b_ref[...], (R, GROUP_WIDTH))
        for k in range(CONF_KERNEL):
            s = (off + k) % SUB
            a = off + k - s + r0
            tap = hs_ref[a:a + R, :] if s == 0 else sh_ref[s - 1, a:a + R, :]
            acc = acc + taps_w[k] * tap
        mu = jnp.mean(acc, axis=-1, keepdims=True)
        xc = acc - mu
        var = jnp.mean(xc * xc, axis=-1, keepdims=True)
        y = xc * lax.rsqrt(var + EPS) * lng_ref[...] + lnb_ref[...]
        y = y * jax.nn.sigmoid(y)
        o_ref[row0 + r0:row0 + r0 + R, :] = _group_rms(y, gn).astype(o_ref.dtype)
    hs_ref[0:H, :] = hs_ref[T:T + H, :]


def _gmlp_rows(u_in, v_in, lng_ref, lnb_ref, ws_ref, bs_ref, gn, o_ref, row0):
    C = GMLP_CHUNK
    lane = lax.broadcasted_iota(jnp.int32, (C, GROUP_WIDTH), 1)
    u = _gelu_tanh(u_in)
    v = _gelu_tanh(v_in)
    mu = jnp.mean(v, axis=-1, keepdims=True)
    xc = v - mu
    var = jnp.mean(xc * xc, axis=-1, keepdims=True)
    vb = (xc * lax.rsqrt(var + EPS) * lng_ref[...] + lnb_ref[...]).astype(BF16)
    zero = jnp.zeros_like(vb)
    vstack = jnp.concatenate(
        [jnp.where((lane >= h * HEAD_DIM) & (lane < (h + 1) * HEAD_DIM), vb, zero) for h in range(GROUP_HEADS)],
        axis=0)
    mixed = bs_ref[...] + jnp.dot(ws_ref[...], vstack, preferred_element_type=F32)
    o_ref[row0:row0 + C, :] = _group_rms(u * mixed, gn).astype(o_ref.dtype)


def _lru_rows(x_in, gate, cw_ref, cb_ref, wax_ref, ba_ref, bx_ref, lam_ref, gn, o_ref, row0, xs_ref, hprev_ref):
    T = x_in.shape[0]
    W = GROUP_WIDTH
    H = LRU_HALO
    xs_ref[H:H + T, :] = x_in
    off = H - (LRU_CONV - 1)
    xb = jnp.broadcast_to(cb_ref[...], (T, W))
    for k in range(LRU_CONV):
        xb = xb + cw_ref[k:k + 1, :] * xs_ref[off + k:off + k + T, :]
    xs_ref[0:H, :] = xs_ref[T:T + H, :]

    pre = jnp.dot(xb.astype(BF16), wax_ref[...], preferred_element_type=F32)
    r = jax.nn.sigmoid(pre[:, :W] + ba_ref[...])
    ig = jax.nn.sigmoid(pre[:, W:] + bx_ref[...])
    nlam = -lam_ref[...]
    log_a = (-LRU_C) * r * (jnp.maximum(nlam, 0.0) + jnp.log1p(jnp.exp(-jnp.abs(nlam))))
    a = jnp.exp(log_a)
    th = jnp.tanh(log_a)
    one_minus_a2 = 2.0 * th / (th - 1.0)
    b = jnp.sqrt(one_minus_a2) * (ig * xb)

    SUB = 8
    a = a.reshape(T // SUB, SUB, W)
    b = b.reshape(T // SUB, SUB, W)
    sub = lax.broadcasted_iota(jnp.int32, a.shape, 1)
    d = 1
    while d < SUB:
        keep = sub >= d
        a_sh = jnp.where(keep, pltpu.roll(a, d, 1), 1.0)
        b_sh = jnp.where(keep, pltpu.roll(b, d, 1), 0.0)
        b = a * b_sh + b
        a = a * a_sh
        d *= 2
    carry = hprev_ref[0:1, :]
    tiles = []
    for g in range(T // SUB):
        hg = b[g] + a[g] * carry
        carry = hg[SUB - 1:SUB]
        tiles.append(hg)
    h = jnp.concatenate(tiles, axis=0)
    hprev_ref[...] = jnp.broadcast_to(carry, hprev_ref.shape)
    y = h * _gelu_tanh(gate)
    o_ref[row0:row0 + T, :] = _group_rms(y, gn).astype(o_ref.dtype)


def _in_mix_kernel(blocks_per_seq, x_ref, g_ref, w_ref, qg_ref, kg_ref, bd_ref,
                   cw_ref, cb_ref, clg_ref, clb_ref,
                   glg_ref, glb_ref, ws_ref, bs_ref,
                   lw_ref, lb_ref, wax_ref, ba_ref, bx_ref, lam_ref, gn_ref,
                   q_ref, k_ref, v_ref, yb_ref, yc_ref, yd_ref,
                   rest_ref, hs_ref, sh_ref, xs_ref, hprev_ref):
    tm = x_ref.shape[0]
    W = GROUP_WIDTH

    @pl.when(pl.program_id(0) % blocks_per_seq == 0)
    def _():
        hs_ref[0:CONF_HALO, :] = jnp.zeros((CONF_HALO, W), F32)
        xs_ref[0:LRU_HALO, :] = jnp.zeros((LRU_HALO, W), F32)
        hprev_ref[...] = jnp.zeros(hprev_ref.shape, F32)

    def step(rest_w, rest_r):
        xf = x_ref[...]
        ms = jnp.mean(xf * xf, axis=-1, keepdims=True)
        h = (xf * lax.rsqrt(ms + EPS) * g_ref[...]).astype(BF16)

        def proj(c):
            return jnp.dot(h, w_ref[:, c * W:(c + 1) * W], preferred_element_type=F32)

        def head_norm(p, gain):
            hi, lo = _split_hi_lo(p * p)
            ss = jnp.dot(jnp.concatenate([hi, lo], axis=1), bd_ref[...], preferred_element_type=F32)
            return p * lax.rsqrt(ss * (1.0 / HEAD_DIM) + EPS) * gain

        def project(c):
            if c == 0:
                q_ref[...] = head_norm(proj(0), qg_ref[...]).astype(BF16)
            elif c == 1:
                k_ref[...] = head_norm(proj(1), kg_ref[...]).astype(BF16)
            elif c == 2:
                v_ref[...] = proj(2).astype(BF16)
            else:
                rest_w[:, (c - 3) * W:(c - 2) * W] = proj(c)

        def cols(r0, n, c):
            return rest_r[r0:r0 + n, c * W:(c + 1) * W]

        T = min(MIX_T, tm)

        def conformer(r0):
            _conformer_rows(cols(r0, T, 0), cols(r0, T, 1), cw_ref, cb_ref, clg_ref, clb_ref, gn_ref[1:2, :],
                            yb_ref, r0, hs_ref, sh_ref)

        def lru(r0):
            _lru_rows(cols(r0, T, 4), cols(r0, T, 5), lw_ref, lb_ref, wax_ref, ba_ref, bx_ref, lam_ref,
                      gn_ref[3:4, :], yd_ref, r0, xs_ref, hprev_ref)

        def gmlp(r0):
            _gmlp_rows(cols(r0, GMLP_CHUNK, 2), cols(r0, GMLP_CHUNK, 3), glg_ref, glb_ref, ws_ref, bs_ref,
                       gn_ref[2:3, :], yc_ref, r0)

        mix = []
        for r0 in range(0, tm, T):
            mix.extend(functools.partial(gmlp, c0) for c0 in range(r0, r0 + T, GMLP_CHUNK))
            mix.append(functools.partial(lru, r0))
            mix.append(functools.partial(conformer, r0))
        n_proj = w_ref.shape[1] // W
        for c in range(3, n_proj):
            project(c)
        per = -(-len(mix) // 4)
        for c in range(4):
            if c > 0:
                project(c - 1)
            for task in mix[c * per:(c + 1) * per]:
                task()

    step(rest_ref, rest_ref)


def _in_mix(x2, seq_len, g, w_bf, qg, kg, bd2, conf, gmlp, lru, gn4):
    n, d = x2.shape
    cols = w_bf.shape[1]
    W = GROUP_WIDTH
    rest_cols = cols - 3 * W
    tm = min(IN_TM, seq_len)
    assert seq_len % tm == 0 and tm % min(MIX_T, tm) == 0 and tm % GMLP_CHUNK == 0
    t = min(MIX_T, tm)
    row = lambda i: (i, 0)
    whole = lambda a: pl.BlockSpec(a.shape, lambda i: (0,) * a.ndim)
    params = (g, w_bf, qg, kg, bd2) + tuple(conf) + tuple(gmlp) + tuple(lru) + (gn4,)
    out_shape = jax.ShapeDtypeStruct((n, W), BF16)
    return pl.pallas_call(
        functools.partial(_in_mix_kernel, seq_len // tm),
        grid=(n // tm,),
        in_specs=[pl.BlockSpec((tm, d), row)] + [whole(a) for a in params],
        out_specs=[pl.BlockSpec((tm, W), row)] * 6,
        out_shape=[out_shape] * 6,
        scratch_shapes=[pltpu.VMEM((tm, rest_cols), F32),
                        pltpu.VMEM((t + CONF_HALO, W), F32),
                        pltpu.VMEM((7, t + CONF_HALO - 8, W), F32),
                        pltpu.VMEM((t + LRU_HALO, W), F32),
                        pltpu.VMEM((8, W), F32)],
        compiler_params=_params("arbitrary"),
        name="in_mix",
    )(x2, *params)


def _out_proj_kernel(ya_ref, yb_ref, yc_ref, yd_ref, w_ref, x_ref, g_ref, rh_ref, rl_ref, tri_ref,
                     xo_ref, h_ref, route_ref, idx_ref, cnt_ref, base_ref):
    W = GROUP_WIDTH

    @pl.when(pl.program_id(0) == 0)
    def _():
        base_ref[...] = jnp.zeros(base_ref.shape, F32)

    acc = x_ref[...]
    for gi, y_ref in enumerate((ya_ref, yb_ref, yc_ref, yd_ref)):
        acc = acc + jnp.dot(y_ref[...], w_ref[gi * W:(gi + 1) * W, :], preferred_element_type=F32)
    xo_ref[...] = acc
    ms = jnp.mean(acc * acc, axis=-1, keepdims=True)
    h = acc * lax.rsqrt(ms + EPS) * g_ref[...]
    h_ref[...] = _pack_bf16_pairs(h)
    _route_rows(h, rh_ref, rl_ref, tri_ref, route_ref, idx_ref, cnt_ref, base_ref)


def _out_proj_route(ys, w_bf, x2, g, rh, rl, tri):
    n, d = x2.shape
    W = GROUP_WIDTH
    tm = tri.shape[0]
    row = lambda i: (i, 0)
    const = lambda i: (0, 0)
    ysp = pl.BlockSpec((tm, W), row)
    return pl.pallas_call(
        _out_proj_kernel,
        grid=(n // tm,),
        in_specs=[ysp, ysp, ysp, ysp,
                  pl.BlockSpec(w_bf.shape, const),
                  pl.BlockSpec((tm, d), row),
                  pl.BlockSpec((1, d), const),
                  pl.BlockSpec((d, LANES), const),
                  pl.BlockSpec((d, LANES), const),
                  pl.BlockSpec((tm, tm), const)],
        out_specs=[pl.BlockSpec((tm, d), row),
                   pl.BlockSpec((tm, d // 2), row),
                   pl.BlockSpec((tm, LANES), row),
                   pl.BlockSpec((ROUTE_IDX_COLS, tm), lambda i: (0, i)),
                   pl.BlockSpec((8, LANES), const)],
        out_shape=[jax.ShapeDtypeStruct((n, d), F32),
                   jax.ShapeDtypeStruct((n, d // 2), jnp.int32),
                   jax.ShapeDtypeStruct((n, LANES), F32),
                   jax.ShapeDtypeStruct((ROUTE_IDX_COLS, n), jnp.int32),
                   jax.ShapeDtypeStruct((8, LANES), F32)],
        scratch_shapes=[pltpu.VMEM((8, LANES), F32)],
        compiler_params=_params("arbitrary"),
        name="out_proj_route",
    )(*ys, w_bf, x2, g, rh, rl, tri)


def _out_ffn_kernel(ya_ref, yb_ref, yc_ref, yd_ref, wo_ref, x_ref, g_ref, wg_ref, wu_ref, wd_ref, o_ref,
                    xn_ref, act_ref):
    W = GROUP_WIDTH
    f = wg_ref.shape[1]
    acc = x_ref[...]
    for gi, y_ref in enumerate((ya_ref, yb_ref, yc_ref, yd_ref)):
        acc = acc + jnp.dot(y_ref[...], wo_ref[gi * W:(gi + 1) * W, :], preferred_element_type=F32)
    xn_ref[...] = acc
    ms = jnp.mean(acc * acc, axis=-1, keepdims=True)
    h = (acc * lax.rsqrt(ms + EPS) * g_ref[...]).astype(BF16)
    for c0 in range(0, f, FFN_TF):
        g = jnp.dot(h, wg_ref[:, c0:c0 + FFN_TF], preferred_element_type=F32)
        u = jnp.dot(h, wu_ref[:, c0:c0 + FFN_TF], preferred_element_type=F32)
        act_ref[:, c0:c0 + FFN_TF] = (g * jax.nn.sigmoid(g) * u).astype(BF16)
    o_ref[...] = xn_ref[...] + jnp.dot(act_ref[...], wd_ref[...], preferred_element_type=F32)


def _out_proj_dense_ffn(ys, wo, x2, g, wg, wu, wd):
    n, d = x2.shape
    f = wg.shape[1]
    W = GROUP_WIDTH
    assert f % FFN_TF == 0
    tm = min(FFN_TM, n)
    row = lambda i: (i, 0)
    resident = dict(pipeline_mode=pl.Buffered(1))
    ysp = pl.BlockSpec((tm, W), row)
    return pl.pallas_call(
        _out_ffn_kernel,
        grid=(n // tm,),
        in_specs=[
            ysp, ysp, ysp, ysp,
            pl.BlockSpec(wo.shape, lambda i: (0, 0), **resident),
            pl.BlockSpec((tm, d), row),
            pl.BlockSpec((1, d), lambda i: (0, 0)),
            pl.BlockSpec((d, f), lambda i: (0, 0), **resident),
            pl.BlockSpec((d, f), lambda i: (0, 0), **resident),
            pl.BlockSpec((f, d), lambda i: (0, 0), **resident),
        ],
        out_specs=pl.BlockSpec((tm, d), row),
        out_shape=jax.ShapeDtypeStruct((n, d), F32),
        scratch_shapes=[pltpu.VMEM((tm, d), F32), pltpu.VMEM((tm, f), BF16)],
        compiler_params=_params("arbitrary"),
        name="out_proj_dense_ffn",
    )(*ys, wo, x2, g, wg, wu, wd)


def _route_rows(h, rh_ref, rl_ref, tri_ref, route_ref, idx_ref, cnt_ref, base_ref):
    tm = h.shape[0]
    G = ROUTE_SPLIT if tm % (ROUTE_SPLIT * 16) == 0 else 1
    hm = tm // G
    gs = range(G)
    rows = [slice(g * hm, (g + 1) * hm) for g in gs]
    hh, hl = _split_hi_lo(h)
    rh = rh_ref[...]
    rl = rl_ref[...]
    lane = lax.broadcasted_iota(jnp.int32, (hm, LANES), 1).astype(F32)
    neg = jnp.float32(-jnp.inf)
    logits = [jnp.dot(hh[r], rh, preferred_element_type=F32) + jnp.dot(hl[r], rh, preferred_element_type=F32)
              + jnp.dot(hh[r], rl, preferred_element_type=F32) for r in rows]
    logits = [jnp.where(lane < N_EXPERTS, x, neg) for x in logits]
    m1 = [jnp.max(x, axis=1, keepdims=True) for x in logits]
    i1 = [jnp.min(jnp.where(logits[g] == m1[g], lane, float(LANES)), axis=1, keepdims=True) for g in gs]
    l2 = [jnp.where(lane == i1[g], neg, logits[g]) for g in gs]
    m2 = [jnp.max(x, axis=1, keepdims=True) for x in l2]
    i2 = [jnp.min(jnp.where(l2[g] == m2[g], lane, float(LANES)), axis=1, keepdims=True) for g in gs]
    e = [jnp.exp(m2[g] - m1[g]) for g in gs]
    g1 = [1.0 / (1.0 + x) for x in e]
    g2 = [x / (1.0 + x) for x in e]
    oh1 = [jnp.where(lane == x, 1.0, 0.0) for x in i1]
    oh2 = [jnp.where(lane == x, 1.0, 0.0) for x in i2]
    oh = [oh1[g] + oh2[g] for g in gs]
    tri = tri_ref[0:hm, 0:hm]
    within = [jnp.dot(tri, x.astype(BF16), preferred_element_type=F32) for x in oh]
    totals = [jnp.sum(x, axis=0, keepdims=True) for x in oh]
    bases = [base_ref[0:1, :]]
    for g in gs:
        bases.append(bases[g] + totals[g])
    before = [within[g] + bases[g] for g in gs]
    r1 = [jnp.sum(oh1[g] * before[g], axis=1, keepdims=True) for g in gs]
    r2 = [jnp.sum(oh2[g] * before[g], axis=1, keepdims=True) for g in gs]
    base_ref[...] = jnp.broadcast_to(bases[G], base_ref.shape)
    cnt_ref[...] = jnp.broadcast_to(bases[G], cnt_ref.shape)
    sel = (lax.broadcasted_iota(jnp.int32, (2 * ROUTE_IDX_COLS, LANES), 0)
           == lax.broadcasted_iota(jnp.int32, (2 * ROUTE_IDX_COLS, LANES), 1)).astype(BF16)
    nt = (((1,), (1,)), ((), ()))
    for g in gs:
        out = jnp.where(lane == 0, i1[g], 0.0)
        out = jnp.where(lane == 1, i2[g], out)
        out = jnp.where(lane == 2, g1[g], out)
        out = jnp.where(lane == 3, g2[g], out)
        out = jnp.where(lane == 4, r1[g], out)
        out = jnp.where(lane == 5, r2[g], out)
        route_ref[rows[g], :] = out
        hi = jnp.floor(out * (1.0 / 256.0))
        lo = out - 256.0 * hi
        hi_t = lax.dot_general(sel, hi.astype(BF16), nt, preferred_element_type=F32)
        lo_t = lax.dot_general(sel, lo.astype(BF16), nt, preferred_element_type=F32)
        idx_ref[:, rows[g]] = (256.0 * hi_t + lo_t)[:ROUTE_IDX_COLS].astype(jnp.int32)


def _sc_window_rows(table):
    row_bytes = table.shape[1] * table.dtype.itemsize
    return min(SC_MAX_INDEX_VECTOR, SC_WINDOW_BYTES // row_bytes)


def _sc_gather_rows(table, idx):
    _, d = table.shape
    b = idx.shape[0]
    win = _sc_window_rows(table)
    sc = plsc.get_sparse_core_info()
    n_workers = sc.num_cores * sc.num_subcores
    per_w = b // n_workers
    n_win = per_w // win
    assert per_w * n_workers == b and n_win * win == per_w and n_win % 2 == 0, (b, n_workers, win)
    mesh = plsc.VectorSubcoreMesh(core_axis_name="c", subcore_axis_name="s")
    dma = pltpu.SemaphoreType.DMA

    @functools.partial(
        pl.kernel, mesh=mesh, out_type=jax.ShapeDtypeStruct((b, d), table.dtype), name="sc_gather_rows",
        scratch_types=[pltpu.VMEM((win,), jnp.int32), pltpu.VMEM((win,), jnp.int32),
                       pltpu.VMEM((win, d), table.dtype), pltpu.VMEM((win, d), table.dtype),
                       dma, dma, dma, dma])
    def gather_kernel(table_hbm, idx_hbm, out_hbm, i0, i1, r0, r1, g0, g1, w0, w1):
        wid = lax.axis_index("s") * sc.num_cores + lax.axis_index("c")
        base = wid * per_w
        idxb, rows, gsem, wsem = (i0, i1), (r0, r1), (g0, g1), (w0, w1)

        def off(c):
            return pl.multiple_of(base + c * win, win)

        def gather(s):
            return pltpu.make_async_copy(table_hbm.at[idxb[s]], rows[s], gsem[s])

        def write(c, s):
            return pltpu.make_async_copy(rows[s], out_hbm.at[pl.ds(off(c), win)], wsem[s])

        pltpu.sync_copy(idx_hbm.at[pl.ds(off(0), win)], idxb[0])
        gather(0).start()

        @pl.loop(0, n_win, step=2)
        def _(c):
            for s in (0, 1):
                cc = c + s

                @pl.when(cc + 1 < n_win)
                def _():
                    @pl.when(cc >= 1)
                    def _():
                        write(cc - 1, 1 - s).wait()

                    pltpu.sync_copy(idx_hbm.at[pl.ds(off(cc + 1), win)], idxb[1 - s])
                    gather(1 - s).start()

                gather(s).wait()
                write(cc, s).start()

        write(n_win - 2, 0).wait()
        write(n_win - 1, 1).wait()

    return gather_kernel(table, idx)


def _sc_dispatch_rows(src, dest1, dest2, pad_idx, n_rows):
    n, d = src.shape
    win = _sc_window_rows(src)
    sc = plsc.get_sparse_core_info()
    n_workers = sc.num_cores * sc.num_subcores
    per_w = n // n_workers
    n_win = per_w // win
    assert per_w * n_workers == n and n_win * win == per_w and n_win % 2 == 0, (n, n_workers, win)
    n_pad = pad_idx.shape[0]
    assert n_pad + 2 * n == n_rows and n_pad % (n_workers * win) == 0, (n_pad, n, n_rows)
    pad_per_w = n_pad // n_workers
    zero_rows = jnp.zeros((win, d), src.dtype)
    mesh = plsc.VectorSubcoreMesh(core_axis_name="c", subcore_axis_name="s")
    dma = pltpu.SemaphoreType.DMA
    ivec = pltpu.VMEM((win,), jnp.int32)
    rbuf = pltpu.VMEM((win, d), src.dtype)

    @functools.partial(
        pl.kernel, mesh=mesh, out_type=jax.ShapeDtypeStruct((n_rows, d), src.dtype), name="sc_dispatch_rows",
        scratch_types=[ivec, ivec, ivec, ivec, rbuf, rbuf, dma, dma, dma, dma, dma, dma])
    def dispatch_kernel(src_hbm, d1_hbm, d2_hbm, pad_hbm, zero_hbm, out_hbm,
                        a0, a1, b0, b1, r0, r1, l0, l1, p0, p1, q0, q1):
        wid = lax.axis_index("s") * sc.num_cores + lax.axis_index("c")
        base = wid * per_w
        ia, ib, rows, lsem, psem, qsem = (a0, a1), (b0, b1), (r0, r1), (l0, l1), (p0, p1), (q0, q1)

        def off(c):
            return pl.multiple_of(base + c * win, win)

        def load(c, s):
            return pltpu.make_async_copy(src_hbm.at[pl.ds(off(c), win)], rows[s], lsem[s])

        def scatters(s):
            return (pltpu.make_async_copy(rows[s], out_hbm.at[ia[s]], psem[s]),
                    pltpu.make_async_copy(rows[s], out_hbm.at[ib[s]], qsem[s]))

        def fetch(c, s):
            pltpu.sync_copy(d1_hbm.at[pl.ds(off(c), win)], ia[s])
            pltpu.sync_copy(d2_hbm.at[pl.ds(off(c), win)], ib[s])
            load(c, s).start()

        fetch(0, 0)

        @pl.loop(0, n_win, step=2)
        def _(c):
            for s in (0, 1):
                cc = c + s

                @pl.when(cc + 1 < n_win)
                def _():
                    @pl.when(cc >= 1)
                    def _():
                        for cp in scatters(1 - s):
                            cp.wait()

                    fetch(cc + 1, 1 - s)

                load(cc, s).wait()
                for cp in scatters(s):
                    cp.start()

        for s in (0, 1):
            for cp in scatters(s):
                cp.wait()

        pltpu.sync_copy(zero_hbm, rows[0])

        @pl.loop(0, pad_per_w // win)
        def _(c):
            o = pl.multiple_of(wid * pad_per_w + c * win, win)
            pltpu.sync_copy(pad_hbm.at[pl.ds(o, win)], ia[0])
            pltpu.sync_copy(rows[0], out_hbm.at[ia[0]])

    return dispatch_kernel(src, dest1, dest2, pad_idx, zero_rows)


def _moe_kernel(bexp_ref, nused_ref, x_ref, wg_ref, wu_ref, wd_ref, o_ref, xb, acc):
    i = pl.program_id(0)
    j = pl.program_id(1)
    nj = pl.num_programs(1)
    used = i < nused_ref[0]

    @pl.when(used & (j == 0))
    def _():
        xb[...] = _unpack_bf16_pairs(x_ref[...]).astype(BF16)
        acc[...] = jnp.zeros(acc.shape, F32)

    @pl.when(used)
    def _():
        x = xb[...]
        g = jnp.dot(x, wg_ref[0].astype(BF16), preferred_element_type=F32)
        u = jnp.dot(x, wu_ref[0].astype(BF16), preferred_element_type=F32)
        act = (g * jax.nn.sigmoid(g) * u).astype(BF16)
        acc[...] += jnp.dot(act, wd_ref[0].astype(BF16), preferred_element_type=F32)

        @pl.when(j == nj - 1)
        def _():
            o_ref[...] = _pack_bf16_pairs(acc[...])

    @pl.when(jnp.logical_not(used) & (j == 0))
    def _():
        o_ref[...] = jnp.zeros(o_ref.shape, o_ref.dtype)


def _moe_experts(block_exp, n_used, xs, wg, wu, wd, tm):
    n_rows, dp = xs.shape
    d = 2 * dp
    n_blocks = n_rows // tm
    f = wg.shape[2]
    tf = MOE_TF if f % MOE_TF == 0 else f
    nj = f // tf
    assert nj >= 2

    def live(i, j, be, nu):
        u = i < nu[0]
        return jnp.where(u, i, nu[0] - 1), jnp.where(u, j, nj - 1)

    def w_in_map(i, j, be, nu):
        ii, jj = live(i, j, be, nu)
        return (be[ii], 0, jj)

    def w_down_map(i, j, be, nu):
        ii, jj = live(i, j, be, nu)
        return (be[ii], jj, 0)

    def x_map(i, j, be, nu):
        return (live(i, j, be, nu)[0], 0)

    def out_map(i, j, be, nu):
        return (i, 0)

    grid_spec = pltpu.PrefetchScalarGridSpec(
        num_scalar_prefetch=2,
        grid=(n_blocks, nj),
        in_specs=[
            pl.BlockSpec((tm, dp), x_map),
            pl.BlockSpec((1, d, tf), w_in_map),
            pl.BlockSpec((1, d, tf), w_in_map),
            pl.BlockSpec((1, tf, d), w_down_map),
        ],
        out_specs=pl.BlockSpec((tm, dp), out_map),
        scratch_shapes=[pltpu.VMEM((tm, d), BF16), pltpu.VMEM((tm, d), F32)],
    )
    return pl.pallas_call(
        _moe_kernel,
        grid_spec=grid_spec,
        out_shape=jax.ShapeDtypeStruct((n_rows, dp), jnp.int32),
        compiler_params=_params("arbitrary", "arbitrary"),
        name="moe_experts",
    )(block_exp, n_used, xs, wg, wu, wd)


def _combine_kernel(x_ref, route_ref, a_ref, b_ref, o_ref):
    route = route_ref[...]
    ya = _unpack_bf16_pairs(a_ref[...])
    yb = _unpack_bf16_pairs(b_ref[...])
    o_ref[...] = x_ref[...] + route[:, 2:3] * ya + route[:, 3:4] * yb


def _combine(yg, x2, route, part, n_parts):
    n, d = x2.shape
    dp = yg.shape[1]
    tm = min(COMB_TM, n // n_parts)
    nb = n // n_parts // tm
    first = part * nb
    return pl.pallas_call(
        _combine_kernel,
        grid=(nb,),
        in_specs=[pl.BlockSpec((tm, d), lambda i: (i + first, 0)),
                  pl.BlockSpec((tm, LANES), lambda i: (i + first, 0)),
                  pl.BlockSpec((tm, dp), lambda i: (i, 0)),
                  pl.BlockSpec((tm, dp), lambda i: (i + nb, 0))],
        out_specs=pl.BlockSpec((tm, d), lambda i: (i + first, 0)),
        out_shape=jax.ShapeDtypeStruct((n, d), F32),
        input_output_aliases={0: 0},
        compiler_params=_params("arbitrary"),
        name="moe_combine",
    )(x2, route, yg, yg)


def _top2_moe(ys_mix, w_out_bf, x2, g2, router, wg, wu, wd):
    n, d = x2.shape
    a = 2 * n
    tm = min(MOE_TM, n)
    r_pad = jnp.zeros((d, LANES), F32).at[:, :N_EXPERTS].set(router)
    rh, rl = _split_hi_lo(r_pad)
    rt = min(ROUTE_TM, n)
    tri = (lax.broadcasted_iota(jnp.int32, (rt, rt), 1) < lax.broadcasted_iota(jnp.int32, (rt, rt), 0)).astype(BF16)
    x2, hn, route, cols, cnt = _out_proj_route(ys_mix, w_out_bf, x2, g2, rh, rl, tri)

    counts = cnt[0, :N_EXPERTS].astype(jnp.int32)
    padded = (counts + tm - 1) // tm * tm
    pad_ends = jnp.cumsum(padded)
    pad_starts = pad_ends - padded
    def lookup(table, idx):
        hit = idx[None, :] == jnp.arange(N_EXPERTS, dtype=jnp.int32)[:, None]
        return jnp.sum(jnp.where(hit, table[:, None], 0), axis=0)

    dest1 = lookup(pad_starts, cols[0]) + cols[4]
    dest2 = lookup(pad_starts, cols[1]) + cols[5]
    n_blocks = a // tm + N_EXPERTS
    n_rows = n_blocks * tm
    block_start = jnp.arange(n_blocks, dtype=jnp.int32) * tm
    block_exp = jnp.minimum(jnp.sum(block_start[None, :] >= pad_ends[:, None], axis=0), N_EXPERTS - 1).astype(jnp.int32)
    n_used = (pad_ends[-1] // tm).astype(jnp.int32).reshape(1)
    pad_cnt = padded - counts
    pad_cum = jnp.cumsum(pad_cnt)
    k = jnp.arange(n_rows - a, dtype=jnp.int32)
    seg = jnp.sum(k[None, :] >= pad_cum[:, None], axis=0).astype(jnp.int32)
    in_expert = lookup(pad_starts + counts - (pad_cum - pad_cnt), jnp.minimum(seg, N_EXPERTS - 1)) + k
    pad_idx = jnp.where(seg < N_EXPERTS, in_expert, pad_ends[-1] + k - pad_cum[-1])

    xs = _sc_dispatch_rows(hn, dest1, dest2, pad_idx, n_rows)
    ys = _moe_experts(block_exp, n_used, xs, wg, wu, wd, tm)
    n_parts = COMBINE_PARTS if n % (COMBINE_PARTS * COMB_TM) == 0 else 1
    step = n // n_parts
    gathered = [_sc_gather_rows(ys, jnp.concatenate([dest1[p * step:(p + 1) * step], dest2[p * step:(p + 1) * step]]))
                for p in range(n_parts)]
    for p in range(n_parts):
        x2 = _combine(gathered[p], x2, route, p, n_parts)
    return x2


def _block_diag(w):
    h, dh, _ = w.shape
    eye = jnp.eye(h, dtype=w.dtype)
    return jnp.einsum("hij,hg->higj", w, eye).reshape(h * dh, h * dh)


def kernel(x, norm1_g, w_in, q_norm_g, k_norm_g, conf_dw_w, conf_dw_b, conf_ln_g, conf_ln_b, gmlp_ln_g, gmlp_ln_b, gmlp_ws, gmlp_bs, lru_conv_w, lru_conv_b, lru_wa, lru_ba, lru_wx, lru_bx, lru_lambda, group_norm_g, w_out, norm2_g, ffn_w_gate, ffn_w_up, ffn_w_down, moe_router, moe_w_gate, moe_w_up, moe_w_down):
    bsz, s, d = x.shape
    n = bsz * s
    depth = w_in.shape[0]
    W = GROUP_WIDTH
    row = lambda v: v.reshape(1, -1).astype(F32)

    head_id = jnp.arange(W) // HEAD_DIM
    bd = (head_id[:, None] == head_id[None, :]).astype(BF16)
    bd2 = jnp.concatenate([bd, bd], axis=0)
    t_att = min(ATT_T, s)
    uu = (jnp.arange(t_att)[:, None] >= jnp.arange(t_att)[None, :]).astype(BF16)
    tril = jnp.tril(jnp.ones((GMLP_CHUNK, GMLP_CHUNK), dtype=bool))

    x2 = x.reshape(n, d)
    for l in range(depth):
        qg = row(jnp.tile(q_norm_g[l], GROUP_HEADS) * (HEAD_DIM ** -0.5))
        kg = row(jnp.tile(k_norm_g[l], GROUP_HEADS))
        gn4 = group_norm_g[l].reshape(N_GROUPS, W).astype(F32)
        cw = jnp.zeros((CONF_HALO, W), F32).at[:CONF_KERNEL].set(conf_dw_w[l])
        conf = (cw, row(conf_dw_b[l]), row(conf_ln_g[l]), row(conf_ln_b[l]))
        ws = jnp.concatenate(list(jnp.where(tril, gmlp_ws[l], 0.0).astype(BF16)), axis=1)
        bs_mat = jnp.repeat(gmlp_bs[l].T, HEAD_DIM, axis=1)
        gmlp = (row(gmlp_ln_g[l]), row(gmlp_ln_b[l]), ws, bs_mat)
        lw = jnp.zeros((8, W), F32).at[:LRU_CONV].set(lru_conv_w[l])
        wax = jnp.concatenate([_block_diag(lru_wa[l]), _block_diag(lru_wx[l])], axis=1).astype(BF16)
        lru = (lw, row(lru_conv_b[l]), wax, row(lru_ba[l]), row(lru_bx[l]), row(lru_lambda[l]))
        q, k, v, y_b, y_c, y_d = _in_mix(x2, s, row(norm1_g[l]), w_in[l].astype(BF16), qg, kg, bd2,
                                         conf, gmlp, lru, gn4)
        to3 = lambda t: t.reshape(bsz, s, t.shape[-1])
        zmax = (ATT_ZMAX_SLACK * HEAD_DIM ** 0.5 * jnp.max(jnp.abs(q_norm_g[l] * k_norm_g[l]))).reshape(1)
        y_a = _sb_attention(zmax.astype(F32), to3(q), to3(k), to3(v), uu, gn4[0:1]).reshape(n, W)
        ys = [y_a, y_b, y_c, y_d]
        j = l // 2
        if l % 2 == 0:
            x2 = _out_proj_dense_ffn(ys, w_out[l].astype(BF16), x2, row(norm2_g[l]), ffn_w_gate[j].astype(BF16),
                                     ffn_w_up[j].astype(BF16), ffn_w_down[j].astype(BF16))
        else:
            x2 = _top2_moe(ys, w_out[l].astype(BF16), x2, row(norm2_g[l]), moe_router[j], moe_w_gate[j],
                           moe_w_up[j], moe_w_down[j])
    return x2.reshape(bsz, s, d)
```

```python
import functools

import jax
import jax.numpy as jnp
from jax import lax
from jax.experimental import pallas as pl
from jax.experimental.pallas import tpu as pltpu
from jax.experimental.pallas import tpu_sc as plsc

F32 = jnp.float32
BF16 = jnp.bfloat16

HEAD_DIM = 64
GROUP_HEADS = 4
GROUP_WIDTH = GROUP_HEADS * HEAD_DIM
N_GROUPS = 4
CONF_KERNEL = 31
GMLP_CHUNK = 128
LRU_CONV = 4
LRU_C = 8.0
N_EXPERTS = 8
EPS = 1e-6

LANES = 128
VMEM_LIMIT = 56 * 1024 * 1024

IN_TM = 1024
MIX_T = 256
ATT_T = 256
ATT_UNDERFLOW = 110.0
ATT_ZMAX_SLACK = 1.05
CONF_R = 32
CONF_HALO = 32
LRU_HALO = 8
FFN_TM = 1024
FFN_TF = 256
ROUTE_TM = 1024
ROUTE_SPLIT = 4
ROUTE_IDX_COLS = 8
MOE_TM = 1024
MOE_TF = 512
COMB_TM = 1024
COMBINE_PARTS = 2
SC_WINDOW_BYTES = 128 * 1024
SC_MAX_INDEX_VECTOR = 128


def _params(*sem):
    return pltpu.CompilerParams(dimension_semantics=sem, vmem_limit_bytes=VMEM_LIMIT)


def _split_hi_lo(x):
    hi = x.astype(BF16)
    lo = (x - hi.astype(F32)).astype(BF16)
    return hi, lo


SOFTPLUS_CLAMP = 60.0


def _softplus(z):
    return jnp.maximum(jnp.log(1.0 + jnp.exp(jnp.minimum(z, SOFTPLUS_CLAMP))), z)


def _gelu_tanh(x):
    c = 0.7978845608028654
    return 0.5 * x * (1.0 + jnp.tanh(c * (x + 0.044715 * (x * x * x))))


def _pack_bf16_pairs(y):
    c = y.shape[1] // 2
    bits = lax.bitcast_convert_type(y.astype(BF16).astype(F32), jnp.uint32)
    word = (bits[:, :c] >> 16) | bits[:, c:]
    return lax.bitcast_convert_type(word, jnp.int32)


def _unpack_bf16_pairs(w):
    bits = lax.bitcast_convert_type(w, jnp.uint32)
    lo = lax.bitcast_convert_type(bits << 16, F32)
    hi = lax.bitcast_convert_type(bits & jnp.uint32(0xFFFF0000), F32)
    return jnp.concatenate([lo, hi], axis=1)


def _group_rms(y, g):
    ms = jnp.mean(y * y, axis=-1, keepdims=True)
    return y * lax.rsqrt(ms + EPS) * g


def _sb_attn_kernel(zmax_ref, q_ref, k_ref, v_ref, uu_ref, gn_ref, o_ref):
    T = q_ref.shape[1]
    n_pairs = GROUP_WIDTH // LANES
    i = pl.program_id(1)
    lane = lax.broadcasted_iota(jnp.int32, (T, LANES), 1)
    rows = lax.broadcasted_iota(jnp.int32, (2 * T, T), 0)
    cols = lax.broadcasted_iota(jnp.int32, (2 * T, T), 1)
    causal = cols < jnp.where(rows >= T, rows - T, rows)
    uu = uu_ref[...]

    def tiles(r0, carries, mask):
        ps = range(n_pairs)
        cs = [slice(p * LANES, (p + 1) * LANES) for p in ps]
        z = [lax.dot_general(qs[p], k_ref[0, pl.ds(r0, T), cs[p]], (((1,), (1,)), ((), ())),
                             preferred_element_type=F32) for p in ps]
        sp = [_softplus(z[p]) for p in ps]
        if mask is not None:
            sp = [jnp.where(mask, s, 0.0) for s in sp]
        cum = [jnp.dot(sp[p].astype(BF16), uu, preferred_element_type=F32) for p in ps]
        w = [jnp.exp(z[p] - cum[p] - carries[p]) for p in ps]
        if mask is not None:
            w = [jnp.where(mask, x, 0.0) for x in w]
        pv = [jnp.dot(w[p].astype(BF16), v_ref[0, pl.ds(r0, T), cs[p]], preferred_element_type=F32)
              for p in ps]
        return pv, [carries[p] + jnp.sum(sp[p], axis=1, keepdims=True) for p in ps]

    qs = []
    for pair in range(n_pairs):
        q2 = q_ref[0, :, pair * LANES:(pair + 1) * LANES]
        zero = jnp.zeros_like(q2)
        qs.append(jnp.concatenate([jnp.where(lane < HEAD_DIM, q2, zero),
                                   jnp.where(lane >= HEAD_DIM, q2, zero)], axis=0))
    row0 = pl.multiple_of(i * T, T)
    state = tuple(zip(*tiles(row0, [jnp.zeros((2 * T, 1), F32)] * n_pairs, causal)))

    dead_at = zmax_ref[0] + ATT_UNDERFLOW

    def alive(st):
        return functools.reduce(jnp.minimum, [jnp.min(carry) for _, carry in st]) <= dead_at

    def cond(c):
        step, live, _ = c
        return (step < i) & live

    def body(c):
        step, _, st = c
        r0 = pl.multiple_of((i - 1 - step) * T, T)
        pv, carries = tiles(r0, [carry for _, carry in st], None)
        new = tuple((st[p][0] + pv[p], carries[p]) for p in range(n_pairs))
        return step + 1, alive(new), new

    _, _, state = lax.while_loop(cond, body, (jnp.int32(0), alive(state), state))
    y = jnp.concatenate([jnp.where(lane < HEAD_DIM, acc[:T], acc[T:]) for acc, _ in state], axis=1)
    o_ref[0] = _group_rms(y, gn_ref[...]).astype(o_ref.dtype)


def _sb_attention(zmax, q3, k3, v3, uu, gn):
    b, s, w = q3.shape
    t = min(ATT_T, s)
    return pl.pallas_call(
        _sb_attn_kernel,
        grid=(b, s // t),
        in_specs=[
            pl.BlockSpec(memory_space=pltpu.SMEM),
            pl.BlockSpec((1, t, w), lambda bi, i: (bi, i, 0)),
            pl.BlockSpec((1, s, w), lambda bi, i: (bi, 0, 0)),
            pl.BlockSpec((1, s, w), lambda bi, i: (bi, 0, 0)),
            pl.BlockSpec((t, t), lambda bi, i: (0, 0)),
            pl.BlockSpec((1, w), lambda bi, i: (0, 0)),
        ],
        out_specs=pl.BlockSpec((1, t, w), lambda bi, i: (bi, i, 0)),
        out_shape=jax.ShapeDtypeStruct((b, s, w), BF16),
        compiler_params=_params("arbitrary", "arbitrary"),
        name="sb_attn",
    )(zmax, q3, k3, v3, uu, gn)


def _conformer_rows(val, gate, w_ref, b_ref, lng_ref, lnb_ref, gn, o_ref, row0, hs_ref, sh_ref):
    T = val.shape[0]
    H = CONF_HALO
    SUB = 8
    hs_ref[H:H + T, :] = val * jax.nn.sigmoid(gate)
    off = H - (CONF_KERNEL - 1)
    L = sh_ref.shape[1]
    for s in range(1, SUB):
        sh_ref[s - 1] = hs_ref[s:s + L, :]
    R = min(CONF_R, T)
    taps_w = [jnp.broadcast_to(w_ref[k:k + 1, :], (R, GROUP_WIDTH)) for k in range(CONF_KERNEL)]
    for r0 in range(0, T, R):
        acc = jnp.broadcast_to(b_ref[...], (R, GROUP_WIDTH))
        for k in range(CONF_KERNEL):
            s = (off + k) % SUB
            a = off + k - s + r0
            tap = hs_ref[a:a + R, :] if s == 0 else sh_ref[s - 1, a:a + R, :]
            acc = acc + taps_w[k] * tap
        mu = jnp.mean(acc, axis=-1, keepdims=True)
        xc = acc - mu
        var = jnp.mean(xc * xc, axis=-1, keepdims=True)
        y = xc * lax.rsqrt(var + EPS) * lng_ref[...] + lnb_ref[...]
        y = y * jax.nn.sigmoid(y)
        o_ref[row0 + r0:row0 + r0 + R, :] = _group_rms(y, gn).astype(o_ref.dtype)
    hs_ref[0:H, :] = hs_ref[T:T + H, :]


def _gmlp_rows(u_in, v_in, lng_ref, lnb_ref, ws_ref, bs_ref, gn, o_ref, row0):
    C = GMLP_CHUNK
    lane = lax.broadcasted_iota(jnp.int32, (C, GROUP_WIDTH), 1)
    u = _gelu_tanh(u_in)
    v = _gelu_tanh(v_in)
    mu = jnp.mean(v, axis=-1, keepdims=True)
    xc = v - mu
    var = jnp.mean(xc * xc, axis=-1, keepdims=True)
    vb = (xc * lax.rsqrt(var + EPS) * lng_ref[...] + lnb_ref[...]).astype(BF16)
    zero = jnp.zeros_like(vb)
    vstack = jnp.concatenate(
        [jnp.where((lane >= h * HEAD_DIM) & (lane < (h + 1) * HEAD_DIM), vb, zero) for h in range(GROUP_HEADS)],
        axis=0)
    mixed = bs_ref[...] + jnp.dot(ws_ref[...], vstack, preferred_element_type=F32)
    o_ref[row0:row0 + C, :] = _group_rms(u * mixed, gn).astype(o_ref.dtype)


def _lru_rows(x_in, gate, cw_ref, cb_ref, wax_ref, ba_ref, bx_ref, lam_ref, gn, o_ref, row0, xs_ref, hprev_ref):
    T = x_in.shape[0]
    W = GROUP_WIDTH
    H = LRU_HALO
    xs_ref[H:H + T, :] = x_in
    off = H - (LRU_CONV - 1)
    xb = jnp.broadcast_to(cb_ref[...], (T, W))
    for k in range(LRU_CONV):
        xb = xb + cw_ref[k:k + 1, :] * xs_ref[off + k:off + k + T, :]
    xs_ref[0:H, :] = xs_ref[T:T + H, :]

    pre = jnp.dot(xb.astype(BF16), wax_ref[...], preferred_element_type=F32)
    r = jax.nn.sigmoid(pre[:, :W] + ba_ref[...])
    ig = jax.nn.sigmoid(pre[:, W:] + bx_ref[...])
    nlam = -lam_ref[...]
    log_a = (-LRU_C) * r * (jnp.maximum(nlam, 0.0) + jnp.log1p(jnp.exp(-jnp.abs(nlam))))
    a = jnp.exp(log_a)
    th = jnp.tanh(log_a)
    one_minus_a2 = 2.0 * th / (th - 1.0)
    b = jnp.sqrt(one_minus_a2) * (ig * xb)

    SUB = 8
    a = a.reshape(T // SUB, SUB, W)
    b = b.reshape(T // SUB, SUB, W)
    sub = lax.broadcasted_iota(jnp.int32, a.shape, 1)
    d = 1
    while d < SUB:
        keep = sub >= d
        a_sh = jnp.where(keep, pltpu.roll(a, d, 1), 1.0)
        b_sh = jnp.where(keep, pltpu.roll(b, d, 1), 0.0)
        b = a * b_sh + b
        a = a * a_sh
        d *= 2
    carry = hprev_ref[0:1, :]
    tiles = []
    for g in range(T // SUB):
        hg = b[g] + a[g] * carry
        carry = hg[SUB - 1:SUB]
        tiles.append(hg)
    h = jnp.concatenate(tiles, axis=0)
    hprev_ref[...] = jnp.broadcast_to(carry, hprev_ref.shape)
    y = h * _gelu_tanh(gate)
    o_ref[row0:row0 + T, :] = _group_rms(y, gn).astype(o_ref.dtype)


def _in_mix_kernel(blocks_per_seq, x_ref, g_ref, w_ref, qg_ref, kg_ref, bd_ref,
                   cw_ref, cb_ref, clg_ref, clb_ref,
                   glg_ref, glb_ref, ws_ref, bs_ref,
                   lw_ref, lb_ref, wax_ref, ba_ref, bx_ref, lam_ref, gn_ref,
                   q_ref, k_ref, v_ref, yb_ref, yc_ref, yd_ref,
                   rest_ref, hs_ref, sh_ref, xs_ref, hprev_ref):
    tm = x_ref.shape[0]
    W = GROUP_WIDTH

    @pl.when(pl.program_id(0) % blocks_per_seq == 0)
    def _():
        hs_ref[0:CONF_HALO, :] = jnp.zeros((CONF_HALO, W), F32)
        xs_ref[0:LRU_HALO, :] = jnp.zeros((LRU_HALO, W), F32)
        hprev_ref[...] = jnp.zeros(hprev_ref.shape, F32)

    def step(rest_w, rest_r):
        xf = x_ref[...]
        ms = jnp.mean(xf * xf, axis=-1, keepdims=True)
        h = (xf * lax.rsqrt(ms + EPS) * g_ref[...]).astype(BF16)

        def proj(c):
            return jnp.dot(h, w_ref[:, c * W:(c + 1) * W], preferred_element_type=F32)

        def head_norm(p, gain):
            hi, lo = _split_hi_lo(p * p)
            ss = jnp.dot(jnp.concatenate([hi, lo], axis=1), bd_ref[...], preferred_element_type=F32)
            return p * lax.rsqrt(ss * (1.0 / HEAD_DIM) + EPS) * gain

        def project(c):
            if c == 0:
                q_ref[...] = head_norm(proj(0), qg_ref[...]).astype(BF16)
            elif c == 1:
                k_ref[...] = head_norm(proj(1), kg_ref[...]).astype(BF16)
            elif c == 2:
                v_ref[...] = proj(2).astype(BF16)
            else:
                rest_w[:, (c - 3) * W:(c - 2) * W] = proj(c)

        def cols(r0, n, c):
            return rest_r[r0:r0 + n, c * W:(c + 1) * W]

        T = min(MIX_T, tm)

        def conformer(r0):
            _conformer_rows(cols(r0, T, 0), cols(r0, T, 1), cw_ref, cb_ref, clg_ref, clb_ref, gn_ref[1:2, :],
                            yb_ref, r0, hs_ref, sh_ref)

        def lru(r0):
            _lru_rows(cols(r0, T, 4), cols(r0, T, 5), lw_ref, lb_ref, wax_ref, ba_ref, bx_ref, lam_ref,
                      gn_ref[3:4, :], yd_ref, r0, xs_ref, hprev_ref)

        def gmlp(r0):
            _gmlp_rows(cols(r0, GMLP_CHUNK, 2), cols(r0, GMLP_CHUNK, 3), glg_ref, glb_ref, ws_ref, bs_ref,
                       gn_ref[2:3, :], yc_ref, r0)

        mix = []
        for r0 in range(0, tm, T):
            mix.extend(functools.partial(gmlp, c0) for c0 in range(r0, r0 + T, GMLP_CHUNK))
            mix.append(functools.partial(lru, r0))
            mix.append(functools.partial(conformer, r0))
        n_proj = w_ref.shape[1] // W
        for c in range(3, n_proj):
            project(c)
        per = -(-len(mix) // 4)
        for c in range(4):
            if c > 0:
                project(c - 1)
            for task in mix[c * per:(c + 1) * per]:
                task()

    step(rest_ref, rest_ref)


def _in_mix(x2, seq_len, g, w_bf, qg, kg, bd2, conf, gmlp, lru, gn4):
    n, d = x2.shape
    cols = w_bf.shape[1]
    W = GROUP_WIDTH
    rest_cols = cols - 3 * W
    tm = min(IN_TM, seq_len)
    assert seq_len % tm == 0 and tm % min(MIX_T, tm) == 0 and tm % GMLP_CHUNK == 0
    t = min(MIX_T, tm)
    row = lambda i: (i, 0)
    whole = lambda a: pl.BlockSpec(a.shape, lambda i: (0,) * a.ndim)
    params = (g, w_bf, qg, kg, bd2) + tuple(conf) + tuple(gmlp) + tuple(lru) + (gn4,)
    out_shape = jax.ShapeDtypeStruct((n, W), BF16)
    return pl.pallas_call(
        functools.partial(_in_mix_kernel, seq_len // tm),
        grid=(n // tm,),
        in_specs=[pl.BlockSpec((tm, d), row)] + [whole(a) for a in params],
        out_specs=[pl.BlockSpec((tm, W), row)] * 6,
        out_shape=[out_shape] * 6,
        scratch_shapes=[pltpu.VMEM((tm, rest_cols), F32),
                        pltpu.VMEM((t + CONF_HALO, W), F32),
                        pltpu.VMEM((7, t + CONF_HALO - 8, W), F32),
                        pltpu.VMEM((t + LRU_HALO, W), F32),
                        pltpu.VMEM((8, W), F32)],
        compiler_params=_params("arbitrary"),
        name="in_mix",
    )(x2, *params)


def _out_proj_kernel(ya_ref, yb_ref, yc_ref, yd_ref, w_ref, x_ref, g_ref, rh_ref, rl_ref, tri_ref,
                     xo_ref, h_ref, route_ref, idx_ref, cnt_ref, base_ref):
    W = GROUP_WIDTH

    @pl.when(pl.program_id(0) == 0)
    def _():
        base_ref[...] = jnp.zeros(base_ref.shape, F32)

    acc = x_ref[...]
    for gi, y_ref in enumerate((ya_ref, yb_ref, yc_ref, yd_ref)):
        acc = acc + jnp.dot(y_ref[...], w_ref[gi * W:(gi + 1) * W, :], preferred_element_type=F32)
    xo_ref[...] = acc
    ms = jnp.mean(acc * acc, axis=-1, keepdims=True)
    h = acc * lax.rsqrt(ms + EPS) * g_ref[...]
    h_ref[...] = _pack_bf16_pairs(h)
    _route_rows(h, rh_ref, rl_ref, tri_ref, route_ref, idx_ref, cnt_ref, base_ref)


def _out_proj_route(ys, w_bf, x2, g, rh, rl, tri):
    n, d = x2.shape
    W = GROUP_WIDTH
    tm = tri.shape[0]
    row = lambda i: (i, 0)
    const = lambda i: (0, 0)
    ysp = pl.BlockSpec((tm, W), row)
    return pl.pallas_call(
        _out_proj_kernel,
        grid=(n // tm,),
        in_specs=[ysp, ysp, ysp, ysp,
                  pl.BlockSpec(w_bf.shape, const),
                  pl.BlockSpec((tm, d), row),
                  pl.BlockSpec((1, d), const),
                  pl.BlockSpec((d, LANES), const),
                  pl.BlockSpec((d, LANES), const),
                  pl.BlockSpec((tm, tm), const)],
        out_specs=[pl.BlockSpec((tm, d), row),
                   pl.BlockSpec((tm, d // 2), row),
                   pl.BlockSpec((tm, LANES), row),
                   pl.BlockSpec((ROUTE_IDX_COLS, tm), lambda i: (0, i)),
                   pl.BlockSpec((8, LANES), const)],
        out_shape=[jax.ShapeDtypeStruct((n, d), F32),
                   jax.ShapeDtypeStruct((n, d // 2), jnp.int32),
                   jax.ShapeDtypeStruct((n, LANES), F32),
                   jax.ShapeDtypeStruct((ROUTE_IDX_COLS, n), jnp.int32),
                   jax.ShapeDtypeStruct((8, LANES), F32)],
        scratch_shapes=[pltpu.VMEM((8, LANES), F32)],
        compiler_params=_params("arbitrary"),
        name="out_proj_route",
    )(*ys, w_bf, x2, g, rh, rl, tri)


def _out_ffn_kernel(ya_ref, yb_ref, yc_ref, yd_ref, wo_ref, x_ref, g_ref, wg_ref, wu_ref, wd_ref, o_ref,
                    xn_ref, act_ref):
    W = GROUP_WIDTH
    f = wg_ref.shape[1]
    acc = x_ref[...]
    for gi, y_ref in enumerate((ya_ref, yb_ref, yc_ref, yd_ref)):
        acc = acc + jnp.dot(y_ref[...], wo_ref[gi * W:(gi + 1) * W, :], preferred_element_type=F32)
    xn_ref[...] = acc
    ms = jnp.mean(acc * acc, axis=-1, keepdims=True)
    h = (acc * lax.rsqrt(ms + EPS) * g_ref[...]).astype(BF16)
    for c0 in range(0, f, FFN_TF):
        g = jnp.dot(h, wg_ref[:, c0:c0 + FFN_TF], preferred_element_type=F32)
        u = jnp.dot(h, wu_ref[:, c0:c0 + FFN_TF], preferred_element_type=F32)
        act_ref[:, c0:c0 + FFN_TF] = (g * jax.nn.sigmoid(g) * u).astype(BF16)
    o_ref[...] = xn_ref[...] + jnp.dot(act_ref[...], wd_ref[...], preferred_element_type=F32)


def _out_proj_dense_ffn(ys, wo, x2, g, wg, wu, wd):
    n, d = x2.shape
    f = wg.shape[1]
    W = GROUP_WIDTH
    assert f % FFN_TF == 0
    tm = min(FFN_TM, n)
    row = lambda i: (i, 0)
    resident = dict(pipeline_mode=pl.Buffered(1))
    ysp = pl.BlockSpec((tm, W), row)
    return pl.pallas_call(
        _out_ffn_kernel,
        grid=(n // tm,),
        in_specs=[
            ysp, ysp, ysp, ysp,
            pl.BlockSpec(wo.shape, lambda i: (0, 0), **resident),
            pl.BlockSpec((tm, d), row),
            pl.BlockSpec((1, d), lambda i: (0, 0)),
            pl.BlockSpec((d, f), lambda i: (0, 0), **resident),
            pl.BlockSpec((d, f), lambda i: (0, 0), **resident),
            pl.BlockSpec((f, d), lambda i: (0, 0), **resident),
        ],
        out_specs=pl.BlockSpec((tm, d), row),
        out_shape=jax.ShapeDtypeStruct((n, d), F32),
        scratch_shapes=[pltpu.VMEM((tm, d), F32), pltpu.VMEM((tm, f), BF16)],
        compiler_params=_params("arbitrary"),
        name="out_proj_dense_ffn",
    )(*ys, wo, x2, g, wg, wu, wd)


def _route_rows(h, rh_ref, rl_ref, tri_ref, route_ref, idx_ref, cnt_ref, base_ref):
    tm = h.shape[0]
    G = ROUTE_SPLIT if tm % (ROUTE_SPLIT * 16) == 0 else 1
    hm = tm // G
    gs = range(G)
    rows = [slice(g * hm, (g + 1) * hm) for g in gs]
    hh, hl = _split_hi_lo(h)
    rh = rh_ref[...]
    rl = rl_ref[...]
    lane = lax.broadcasted_iota(jnp.int32, (hm, LANES), 1).astype(F32)
    neg = jnp.float32(-jnp.inf)
    logits = [jnp.dot(hh[r], rh, preferred_element_type=F32) + jnp.dot(hl[r], rh, preferred_element_type=F32)
              + jnp.dot(hh[r], rl, preferred_element_type=F32) for r in rows]
    logits = [jnp.where(lane < N_EXPERTS, x, neg) for x in logits]
    m1 = [jnp.max(x, axis=1, keepdims=True) for x in logits]
    i1 = [jnp.min(jnp.where(logits[g] == m1[g], lane, float(LANES)), axis=1, keepdims=True) for g in gs]
    l2 = [jnp.where(lane == i1[g], neg, logits[g]) for g in gs]
    m2 = [jnp.max(x, axis=1, keepdims=True) for x in l2]
    i2 = [jnp.min(jnp.where(l2[g] == m2[g], lane, float(LANES)), axis=1, keepdims=True) for g in gs]
    e = [jnp.exp(m2[g] - m1[g]) for g in gs]
    g1 = [1.0 / (1.0 + x) for x in e]
    g2 = [x / (1.0 + x) for x in e]
    oh1 = [jnp.where(lane == x, 1.0, 0.0) for x in i1]
    oh2 = [jnp.where(lane == x, 1.0, 0.0) for x in i2]
    oh = [oh1[g] + oh2[g] for g in gs]
    tri = tri_ref[0:hm, 0:hm]
    within = [jnp.dot(tri, x.astype(BF16), preferred_element_type=F32) for x in oh]
    totals = [jnp.sum(x, axis=0, keepdims=True) for x in oh]
    bases = [base_ref[0:1, :]]
    for g in gs:
        bases.append(bases[g] + totals[g])
    before = [within[g] + bases[g] for g in gs]
    r1 = [jnp.sum(oh1[g] * before[g], axis=1, keepdims=True) for g in gs]
    r2 = [jnp.sum(oh2[g] * before[g], axis=1, keepdims=True) for g in gs]
    base_ref[...] = jnp.broadcast_to(bases[G], base_ref.shape)
    cnt_ref[...] = jnp.broadcast_to(bases[G], cnt_ref.shape)
    sel = (lax.broadcasted_iota(jnp.int32, (2 * ROUTE_IDX_COLS, LANES), 0)
           == lax.broadcasted_iota(jnp.int32, (2 * ROUTE_IDX_COLS, LANES), 1)).astype(BF16)
    nt = (((1,), (1,)), ((), ()))
    for g in gs:
        out = jnp.where(lane == 0, i1[g], 0.0)
        out = jnp.where(lane == 1, i2[g], out)
        out = jnp.where(lane == 2, g1[g], out)
        out = jnp.where(lane == 3, g2[g], out)
        out = jnp.where(lane == 4, r1[g], out)
        out = jnp.where(lane == 5, r2[g], out)
        route_ref[rows[g], :] = out
        hi = jnp.floor(out * (1.0 / 256.0))
        lo = out - 256.0 * hi
        hi_t = lax.dot_general(sel, hi.astype(BF16), nt, preferred_element_type=F32)
        lo_t = lax.dot_general(sel, lo.astype(BF16), nt, preferred_element_type=F32)
        idx_ref[:, rows[g]] = (256.0 * hi_t + lo_t)[:ROUTE_IDX_COLS].astype(jnp.int32)


def _sc_window_rows(table):
    row_bytes = table.shape[1] * table.dtype.itemsize
    return min(SC_MAX_INDEX_VECTOR, SC_WINDOW_BYTES // row_bytes)


def _sc_gather_rows(table, idx):
    _, d = table.shape
    b = idx.shape[0]
    win = _sc_window_rows(table)
    sc = plsc.get_sparse_core_info()
    n_workers = sc.num_cores * sc.num_subcores
    per_w = b // n_workers
    n_win = per_w // win
    assert per_w * n_workers == b and n_win * win == per_w and n_win % 2 == 0, (b, n_workers, win)
    mesh = plsc.VectorSubcoreMesh(core_axis_name="c", subcore_axis_name="s")
    dma = pltpu.SemaphoreType.DMA

    @functools.partial(
        pl.kernel, mesh=mesh, out_type=jax.ShapeDtypeStruct((b, d), table.dtype), name="sc_gather_rows",
        scratch_types=[pltpu.VMEM((win,), jnp.int32), pltpu.VMEM((win,), jnp.int32),
                       pltpu.VMEM((win, d), table.dtype), pltpu.VMEM((win, d), table.dtype),
                       dma, dma, dma, dma])
    def gather_kernel(table_hbm, idx_hbm, out_hbm, i0, i1, r0, r1, g0, g1, w0, w1):
        wid = lax.axis_index("s") * sc.num_cores + lax.axis_index("c")
        base = wid * per_w
        idxb, rows, gsem, wsem = (i0, i1), (r0, r1), (g0, g1), (w0, w1)

        def off(c):
            return pl.multiple_of(base + c * win, win)

        def gather(s):
            return pltpu.make_async_copy(table_hbm.at[idxb[s]], rows[s], gsem[s])

        def write(c, s):
            return pltpu.make_async_copy(rows[s], out_hbm.at[pl.ds(off(c), win)], wsem[s])

        pltpu.sync_copy(idx_hbm.at[pl.ds(off(0), win)], idxb[0])
        gather(0).start()

        @pl.loop(0, n_win, step=2)
        def _(c):
            for s in (0, 1):
                cc = c + s

                @pl.when(cc + 1 < n_win)
                def _():
                    @pl.when(cc >= 1)
                    def _():
                        write(cc - 1, 1 - s).wait()

                    pltpu.sync_copy(idx_hbm.at[pl.ds(off(cc + 1), win)], idxb[1 - s])
                    gather(1 - s).start()

                gather(s).wait()
                write(cc, s).start()

        write(n_win - 2, 0).wait()
        write(n_win - 1, 1).wait()

    return gather_kernel(table, idx)


def _sc_dispatch_rows(src, dest1, dest2, pad_idx, n_rows):
    n, d = src.shape
    win = _sc_window_rows(src)
    sc = plsc.get_sparse_core_info()
    n_workers = sc.num_cores * sc.num_subcores
    per_w = n // n_workers
    n_win = per_w // win
    assert per_w * n_workers == n and n_win * win == per_w and n_win % 2 == 0, (n, n_workers, win)
    n_pad = pad_idx.shape[0]
    assert n_pad + 2 * n == n_rows and n_pad % (n_workers * win) == 0, (n_pad, n, n_rows)
    pad_per_w = n_pad // n_workers
    zero_rows = jnp.zeros((win, d), src.dtype)
    mesh = plsc.VectorSubcoreMesh(core_axis_name="c", subcore_axis_name="s")
    dma = pltpu.SemaphoreType.DMA
    ivec = pltpu.VMEM((win,), jnp.int32)
    rbuf = pltpu.VMEM((win, d), src.dtype)

    @functools.partial(
        pl.kernel, mesh=mesh, out_type=jax.ShapeDtypeStruct((n_rows, d), src.dtype), name="sc_dispatch_rows",
        scratch_types=[ivec, ivec, ivec, ivec, rbuf, rbuf, dma, dma, dma, dma, dma, dma])
    def dispatch_kernel(src_hbm, d1_hbm, d2_hbm, pad_hbm, zero_hbm, out_hbm,
                        a0, a1, b0, b1, r0, r1, l0, l1, p0, p1, q0, q1):
        wid = lax.axis_index("s") * sc.num_cores + lax.axis_index("c")
        base = wid * per_w
        ia, ib, rows, lsem, psem, qsem = (a0, a1), (b0, b1), (r0, r1), (l0, l1), (p0, p1), (q0, q1)

        def off(c):
            return pl.multiple_of(base + c * win, win)

        def load(c, s):
            return pltpu.make_async_copy(src_hbm.at[pl.ds(off(c), win)], rows[s], lsem[s])

        def scatters(s):
            return (pltpu.make_async_copy(rows[s], out_hbm.at[ia[s]], psem[s]),
                    pltpu.make_async_copy(rows[s], out_hbm.at[ib[s]], qsem[s]))

        def fetch(c, s):
            pltpu.sync_copy(d1_hbm.at[pl.ds(off(c), win)], ia[s])
            pltpu.sync_copy(d2_hbm.at[pl.ds(off(c), win)], ib[s])
            load(c, s).start()

        fetch(0, 0)

        @pl.loop(0, n_win, step=2)
        def _(c):
            for s in (0, 1):
                cc = c + s

                @pl.when(cc + 1 < n_win)
                def _():
                    @pl.when(cc >= 1)
                    def _():
                        for cp in scatters(1 - s):
                            cp.wait()

                    fetch(cc + 1, 1 - s)

                load(cc, s).wait()
                for cp in scatters(s):
                    cp.start()

        for s in (0, 1):
            for cp in scatters(s):
                cp.wait()

        pltpu.sync_copy(zero_hbm, rows[0])

        @pl.loop(0, pad_per_w // win)
        def _(c):
            o = pl.multiple_of(wid * pad_per_w + c * win, win)
            pltpu.sync_copy(pad_hbm.at[pl.ds(o, win)], ia[0])
            pltpu.sync_copy(rows[0], out_hbm.at[ia[0]])

    return dispatch_kernel(src, dest1, dest2, pad_idx, zero_rows)


def _moe_kernel(bexp_ref, nused_ref, x_ref, wg_ref, wu_ref, wd_ref, o_ref, xb, acc):
    i = pl.program_id(0)
    j = pl.program_id(1)
    nj = pl.num_programs(1)
    used = i < nused_ref[0]

    @pl.when(used & (j == 0))
    def _():
        xb[...] = _unpack_bf16_pairs(x_ref[...]).astype(BF16)
        acc[...] = jnp.zeros(acc.shape, F32)

    @pl.when(used)
    def _():
        x = xb[...]
        g = jnp.dot(x, wg_ref[0].astype(BF16), preferred_element_type=F32)
        u = jnp.dot(x, wu_ref[0].astype(BF16), preferred_element_type=F32)
        act = (g * jax.nn.sigmoid(g) * u).astype(BF16)
        acc[...] += jnp.dot(act, wd_ref[0].astype(BF16), preferred_element_type=F32)

        @pl.when(j == nj - 1)
        def _():
            o_ref[...] = _pack_bf16_pairs(acc[...])

    @pl.when(jnp.logical_not(used) & (j == 0))
    def _():
        o_ref[...] = jnp.zeros(o_ref.shape, o_ref.dtype)


def _moe_experts(block_exp, n_used, xs, wg, wu, wd, tm):
    n_rows, dp = xs.shape
    d = 2 * dp
    n_blocks = n_rows // tm
    f = wg.shape[2]
    tf = MOE_TF if f % MOE_TF == 0 else f
    nj = f // tf
    assert nj >= 2

    def live(i, j, be, nu):
        u = i < nu[0]
        return jnp.where(u, i, nu[0] - 1), jnp.where(u, j, nj - 1)

    def w_in_map(i, j, be, nu):
        ii, jj = live(i, j, be, nu)
        return (be[ii], 0, jj)

    def w_down_map(i, j, be, nu):
        ii, jj = live(i, j, be, nu)
        return (be[ii], jj, 0)

    def x_map(i, j, be, nu):
        return (live(i, j, be, nu)[0], 0)

    def out_map(i, j, be, nu):
        return (i, 0)

    grid_spec = pltpu.PrefetchScalarGridSpec(
        num_scalar_prefetch=2,
        grid=(n_blocks, nj),
        in_specs=[
            pl.BlockSpec((tm, dp), x_map),
            pl.BlockSpec((1, d, tf), w_in_map),
            pl.BlockSpec((1, d, tf), w_in_map),
            pl.BlockSpec((1, tf, d), w_down_map),
        ],
        out_specs=pl.BlockSpec((tm, dp), out_map),
        scratch_shapes=[pltpu.VMEM((tm, d), BF16), pltpu.VMEM((tm, d), F32)],
    )
    return pl.pallas_call(
        _moe_kernel,
        grid_spec=grid_spec,
        out_shape=jax.ShapeDtypeStruct((n_rows, dp), jnp.int32),
        compiler_params=_params("arbitrary", "arbitrary"),
        name="moe_experts",
    )(block_exp, n_used, xs, wg, wu, wd)


def _combine_kernel(x_ref, route_ref, a_ref, b_ref, o_ref):
    route = route_ref[...]
    ya = _unpack_bf16_pairs(a_ref[...])
    yb = _unpack_bf16_pairs(b_ref[...])
    o_ref[...] = x_ref[...] + route[:, 2:3] * ya + route[:, 3:4] * yb


def _combine(yg, x2, route, part, n_parts):
    n, d = x2.shape
    dp = yg.shape[1]
    tm = min(COMB_TM, n // n_parts)
    nb = n // n_parts // tm
    first = part * nb
    return pl.pallas_call(
        _combine_kernel,
        grid=(nb,),
        in_specs=[pl.BlockSpec((tm, d), lambda i: (i + first, 0)),
                  pl.BlockSpec((tm, LANES), lambda i: (i + first, 0)),
                  pl.BlockSpec((tm, dp), lambda i: (i, 0)),
                  pl.BlockSpec((tm, dp), lambda i: (i + nb, 0))],
        out_specs=pl.BlockSpec((tm, d), lambda i: (i + first, 0)),
        out_shape=jax.ShapeDtypeStruct((n, d), F32),
        input_output_aliases={0: 0},
        compiler_params=_params("arbitrary"),
        name="moe_combine",
    )(x2, route, yg, yg)


def _top2_moe(ys_mix, w_out_bf, x2, g2, router, wg, wu, wd):
    n, d = x2.shape
    a = 2 * n
    tm = min(MOE_TM, n)
    r_pad = jnp.zeros((d, LANES), F32).at[:, :N_EXPERTS].set(router)
    rh, rl = _split_hi_lo(r_pad)
    rt = min(ROUTE_TM, n)
    tri = (lax.broadcasted_iota(jnp.int32, (rt, rt), 1) < lax.broadcasted_iota(jnp.int32, (rt, rt), 0)).astype(BF16)
    x2, hn, route, cols, cnt = _out_proj_route(ys_mix, w_out_bf, x2, g2, rh, rl, tri)

    counts = cnt[0, :N_EXPERTS].astype(jnp.int32)
    padded = (counts + tm - 1) // tm * tm
    pad_ends = jnp.cumsum(padded)
    pad_starts = pad_ends - padded
    def lookup(table, idx):
        hit = idx[None, :] == jnp.arange(N_EXPERTS, dtype=jnp.int32)[:, None]
        return jnp.sum(jnp.where(hit, table[:, None], 0), axis=0)

    dest1 = lookup(pad_starts, cols[0]) + cols[4]
    dest2 = lookup(pad_starts, cols[1]) + cols[5]
    n_blocks = a // tm + N_EXPERTS
    n_rows = n_blocks * tm
    block_start = jnp.arange(n_blocks, dtype=jnp.int32) * tm
    block_exp = jnp.minimum(jnp.sum(block_start[None, :] >= pad_ends[:, None], axis=0), N_EXPERTS - 1).astype(jnp.int32)
    n_used = (pad_ends[-1] // tm).astype(jnp.int32).reshape(1)
    pad_cnt = padded - counts
    pad_cum = jnp.cumsum(pad_cnt)
    k = jnp.arange(n_rows - a, dtype=jnp.int32)
    seg = jnp.sum(k[None, :] >= pad_cum[:, None], axis=0).astype(jnp.int32)
    in_expert = lookup(pad_starts + counts - (pad_cum - pad_cnt), jnp.minimum(seg, N_EXPERTS - 1)) + k
    pad_idx = jnp.where(seg < N_EXPERTS, in_expert, pad_ends[-1] + k - pad_cum[-1])

    xs = _sc_dispatch_rows(hn, dest1, dest2, pad_idx, n_rows)
    ys = _moe_experts(block_exp, n_used, xs, wg, wu, wd, tm)
    n_parts = COMBINE_PARTS if n % (COMBINE_PARTS * COMB_TM) == 0 else 1
    step = n // n_parts
    gathered = [_sc_gather_rows(ys, jnp.concatenate([dest1[p * step:(p + 1) * step], dest2[p * step:(p + 1) * step]]))
                for p in range(n_parts)]
    for p in range(n_parts):
        x2 = _combine(gathered[p], x2, route, p, n_parts)
    return x2


def _block_diag(w):
    h, dh, _ = w.shape
    eye = jnp.eye(h, dtype=w.dtype)
    return jnp.einsum("hij,hg->higj", w, eye).reshape(h * dh, h * dh)


def kernel(x, norm1_g, w_in, q_norm_g, k_norm_g, conf_dw_w, conf_dw_b, conf_ln_g, conf_ln_b, gmlp_ln_g, gmlp_ln_b, gmlp_ws, gmlp_bs, lru_conv_w, lru_conv_b, lru_wa, lru_ba, lru_wx, lru_bx, lru_lambda, group_norm_g, w_out, norm2_g, ffn_w_gate, ffn_w_up, ffn_w_down, moe_router, moe_w_gate, moe_w_up, moe_w_down):
    bsz, s, d = x.shape
    n = bsz * s
    depth = w_in.shape[0]
    W = GROUP_WIDTH
    row = lambda v: v.reshape(1, -1).astype(F32)

    head_id = jnp.arange(W) // HEAD_DIM
    bd = (head_id[:, None] == head_id[None, :]).astype(BF16)
    bd2 = jnp.concatenate([bd, bd], axis=0)
    t_att = min(ATT_T, s)
    uu = (jnp.arange(t_att)[:, None] >= jnp.arange(t_att)[None, :]).astype(BF16)
    tril = jnp.tril(jnp.ones((GMLP_CHUNK, GMLP_CHUNK), dtype=bool))

    x2 = x.reshape(n, d)
    for l in range(depth):
        qg = row(jnp.tile(q_norm_g[l], GROUP_HEADS) * (HEAD_DIM ** -0.5))
        kg = row(jnp.tile(k_norm_g[l], GROUP_HEADS))
        gn4 = group_norm_g[l].reshape(N_GROUPS, W).astype(F32)
        cw = jnp.zeros((CONF_HALO, W), F32).at[:CONF_KERNEL].set(conf_dw_w[l])
        conf = (cw, row(conf_dw_b[l]), row(conf_ln_g[l]), row(conf_ln_b[l]))
        ws = jnp.concatenate(list(jnp.where(tril, gmlp_ws[l], 0.0).astype(BF16)), axis=1)
        bs_mat = jnp.repeat(gmlp_bs[l].T, HEAD_DIM, axis=1)
        gmlp = (row(gmlp_ln_g[l]), row(gmlp_ln_b[l]), ws, bs_mat)
        lw = jnp.zeros((8, W), F32).at[:LRU_CONV].set(lru_conv_w[l])
        wax = jnp.concatenate([_block_diag(lru_wa[l]), _block_diag(lru_wx[l])], axis=1).astype(BF16)
        lru = (lw, row(lru_conv_b[l]), wax, row(lru_ba[l]), row(lru_bx[l]), row(lru_lambda[l]))
        q, k, v, y_b, y_c, y_d = _in_mix(x2, s, row(norm1_g[l]), w_in[l].astype(BF16), qg, kg, bd2,
                                         conf, gmlp, lru, gn4)
        to3 = lambda t: t.reshape(bsz, s, t.shape[-1])
        zmax = (ATT_ZMAX_SLACK * HEAD_DIM ** 0.5 * jnp.max(jnp.abs(q_norm_g[l] * k_norm_g[l]))).reshape(1)
        y_a = _sb_attention(zmax.astype(F32), to3(q), to3(k), to3(v), uu, gn4[0:1]).reshape(n, W)
        ys = [y_a, y_b, y_c, y_d]
        j = l // 2
        if l % 2 == 0:
            x2 = _out_proj_dense_ffn(ys, w_out[l].astype(BF16), x2, row(norm2_g[l]), ffn_w_gate[j].astype(BF16),
                                     ffn_w_up[j].astype(BF16), ffn_w_down[j].astype(BF16))
        else:
            x2 = _top2_moe(ys, w_out[l].astype(BF16), x2, row(norm2_g[l]), moe_router[j], moe_w_gate[j],
                           moe_w_up[j], moe_w_down[j])
    return x2.reshape(bsz, s, d)
```
